```python
import math
import jax, jax.numpy as jnp
from jax import lax
import numpy as np

D_MODEL = 1024
BATCH = 16
SEQ = 256
DEPTH = 1
DEC_BATCH = 2
DEC_SEQ = 2048
PAST_LEN = 256

GRID_W = 64
N_HEADS = 16
QK_NOPE_DIM = 64
QK_ROPE_DIM = 32
V_DIM = 64
Q_LORA = 384
KV_LORA = 256
ROPE_BASE = 10000.0
Q_BLOCK = 128
ATTN_SCALE = 1.0 / math.sqrt(QK_NOPE_DIM + QK_ROPE_DIM)
CONV_C = D_MODEL
CONV_K = 31
CONV_PAD = CONV_K // 2
N_EXPERTS = 64
TOP_K = 8
N_GROUPS = 8
TOPK_GROUPS = 4
EXPERT_FF = 256
ROUTED_SCALE = 2.5
EPS = 1e-6
N_MOD = 6
IN_COLS = Q_LORA + KV_LORA + QK_ROPE_DIM + 2 * CONV_C + 2 * D_MODEL
SPLITS = [Q_LORA, Q_LORA + KV_LORA, Q_LORA + KV_LORA + QK_ROPE_DIM,
          Q_LORA + KV_LORA + QK_ROPE_DIM + 2 * CONV_C]

kernel_name = "hybrid_mla_conformer_moe_dit_step"


def rmsnorm(x, g):
    xf = x.astype(jnp.float32)
    y = xf * lax.rsqrt(jnp.mean(xf * xf, axis=-1, keepdims=True) + EPS)
    return (y * g.astype(jnp.float32)).astype(x.dtype)


def layernorm(x, g, b):
    xf = x.astype(jnp.float32)
    mu = jnp.mean(xf, axis=-1, keepdims=True)
    var = jnp.mean(jnp.square(xf - mu), axis=-1, keepdims=True)
    y = (xf - mu) * lax.rsqrt(var + EPS)
    return (y * g.astype(jnp.float32) + b.astype(jnp.float32)).astype(x.dtype)


def axial_rope(seq_len):
    pos = jnp.arange(seq_len, dtype=jnp.int32)
    row = (pos // GRID_W).astype(jnp.float32)
    col = (pos % GRID_W).astype(jnp.float32)
    n_freq = QK_ROPE_DIM // 4
    inv_freq = ROPE_BASE ** (-jnp.arange(n_freq, dtype=jnp.float32) / n_freq)
    ang = jnp.stack([row[:, None] * inv_freq, col[:, None] * inv_freq], axis=1)
    return jnp.cos(ang), jnp.sin(ang)


def apply_rope(x, cos, sin):
    shp = x.shape
    xr = x.reshape(*shp[:-1], 2, 2, shp[-1] // 4)
    x1, x2 = xr[..., 0, :], xr[..., 1, :]
    c = cos[:, None].astype(x.dtype)
    s = sin[:, None].astype(x.dtype)
    out = jnp.stack([x1 * c - x2 * s, x1 * s + x2 * c], axis=-2)
    return out.reshape(shp)


def adaln(cond, w, b):
    m = (jax.nn.silu(cond) @ w + b)[:, None, :]
    return jnp.split(m, N_MOD, axis=-1)


def decompress_kv(ckv, w_ukv):
    B, S, _ = ckv.shape
    kv = (ckv @ w_ukv).reshape(B, S, N_HEADS, QK_NOPE_DIM + V_DIM)
    return kv[..., :QK_NOPE_DIM], kv[..., QK_NOPE_DIM:]


def mla_attend(q_nope, q_rope, k_nope, k_rope, v):
    B, S, H, _ = q_nope.shape
    nb = S // Q_BLOCK

    def block(args):
        qn, qr = args
        s = (jnp.einsum('bqhd,bkhd->bhqk', qn, k_nope)
             + jnp.einsum('bqhr,bkr->bhqk', qr, k_rope)).astype(jnp.float32)
        p = jax.nn.softmax(s * ATTN_SCALE, axis=-1).astype(v.dtype)
        return jnp.einsum('bhqk,bkhd->bqhd', p, v)

    qn_b = q_nope.reshape(B, nb, Q_BLOCK, H, QK_NOPE_DIM).transpose(1, 0, 2, 3, 4)
    qr_b = q_rope.reshape(B, nb, Q_BLOCK, H, QK_ROPE_DIM).transpose(1, 0, 2, 3, 4)
    out = lax.map(block, (qn_b, qr_b))
    return out.transpose(1, 0, 2, 3, 4).reshape(B, S, H * V_DIM)


def conformer_conv(u, conv_w, conv_b, ln_g, ln_b, w_pw2):
    a, g = jnp.split(u, 2, axis=-1)
    z = a * jax.nn.sigmoid(g)
    z = lax.conv_general_dilated(z, conv_w[:, None, :].astype(z.dtype), window_strides=(1,),
                                 padding=[(CONV_PAD, CONV_PAD)],
                                 dimension_numbers=('NWC', 'WIO', 'NWC'),
                                 feature_group_count=CONV_C) + conv_b
    z = jax.nn.silu(layernorm(z, ln_g, ln_b))
    return z @ w_pw2


def moe(h, router_w, router_bias, exp_w_gate, exp_w_up, exp_w_down, sh_w_gate, sh_w_up, sh_w_down):
    B, S, D = h.shape
    t = h.reshape(B * S, D)
    T = t.shape[0]
    scores = jax.nn.sigmoid((t @ router_w).astype(jnp.float32))
    sel = scores + router_bias.astype(jnp.float32)
    grp = sel.reshape(T, N_GROUPS, N_EXPERTS // N_GROUPS)
    grp_score = jnp.sum(lax.top_k(grp, 2)[0], axis=-1)
    _, gidx = lax.top_k(grp_score, TOPK_GROUPS)
    gmask = jnp.sum(jax.nn.one_hot(gidx, N_GROUPS, dtype=jnp.float32), axis=1)
    emask = jnp.repeat(gmask, N_EXPERTS // N_GROUPS, axis=1)
    masked = jnp.where(emask > 0, sel, -jnp.inf)
    _, idx = lax.top_k(masked, TOP_K)
    w = jnp.take_along_axis(scores, idx, axis=-1)
    w = w / jnp.sum(w, axis=-1, keepdims=True) * ROUTED_SCALE
    gates = jnp.sum(jax.nn.one_hot(idx, N_EXPERTS, dtype=jnp.float32) * w[..., None], axis=1)
    hid = jax.nn.silu(jnp.einsum('td,edf->tef', t, exp_w_gate)) * jnp.einsum('td,edf->tef', t, exp_w_up)
    hid = hid * gates[:, :, None].astype(hid.dtype)
    routed = jnp.einsum('tef,efd->td', hid, exp_w_down)
    shared = (jax.nn.silu(t @ sh_w_gate) * (t @ sh_w_up)) @ sh_w_down
    return (routed + shared).reshape(B, S, D)


def trunk_layer(x, mods, lp, rope, ctx_kv):
    shift1, scale1, gate1, shift2, scale2, gate2 = mods
    B, S, _ = x.shape
    h = rmsnorm(x, lp['norm1_g']) * (1.0 + scale1) + shift1
    u = h @ lp['w_in']
    q_lat, ckv, k_rope, conv_in, gate_logits = jnp.split(u, SPLITS, axis=-1)
    q = (rmsnorm(q_lat, lp['q_norm_g']) @ lp['w_uq']).reshape(B, S, N_HEADS, QK_NOPE_DIM + QK_ROPE_DIM)
    q_nope, q_rope = q[..., :QK_NOPE_DIM], q[..., QK_NOPE_DIM:]
    ckv = rmsnorm(ckv, lp['kv_norm_g'])
    k_nope, v = decompress_kv(ckv, lp['w_ukv'])
    k_rope_used = k_rope
    if rope is not None:
        cos, sin = rope
        q_rope = apply_rope(q_rope, cos, sin)
        k_rope_used = apply_rope(k_rope[:, :, None, :], cos, sin)[:, :, 0, :]
    if ctx_kv is not None:
        ckv_ctx, krope_ctx = ctx_kv
        kn_ctx, v_ctx = decompress_kv(ckv_ctx, lp['w_ukv'])
        k_nope = jnp.concatenate([k_nope, kn_ctx], axis=1)
        v = jnp.concatenate([v, v_ctx], axis=1)
        k_rope_used = jnp.concatenate([k_rope_used, krope_ctx], axis=1)
    attn_out = mla_attend(q_nope, q_rope, k_nope, k_rope_used, v) @ lp['w_o_attn']
    conv_out = conformer_conv(conv_in, lp['conv_w'], lp['conv_b'], lp['conv_ln_g'], lp['conv_ln_b'], lp['w_pw2'])
    g_attn, g_conv = jnp.split(jax.nn.sigmoid(gate_logits), 2, axis=-1)
    mix = (g_attn * attn_out + g_conv * conv_out) @ lp['w_out']
    x = x + gate1 * mix
    h2 = rmsnorm(x, lp['norm2_g']) * (1.0 + scale2) + shift2
    ff = moe(h2, lp['router_w'], lp['router_bias'], lp['exp_w_gate'], lp['exp_w_up'], lp['exp_w_down'],
             lp['sh_w_gate'], lp['sh_w_up'], lp['sh_w_down'])
    x = x + gate2 * ff
    return x, ckv, k_rope


def setup_inputs(seed: int = 0) -> dict:
    key = jax.random.key(seed)
    ks = iter(jax.random.split(key, 40))

    def nrm(shape, scale=1.0):
        return jax.random.normal(next(ks), shape, jnp.float32) * scale

    def gain(shape):
        return 1.0 + 0.05 * jax.random.normal(next(ks), shape, jnp.float32)

    L, D = DEPTH, D_MODEL
    return {
        'x_prompt': nrm((BATCH, SEQ, D)),
        'x_sample': nrm((DEC_BATCH, DEC_SEQ, D)),
        'cache_ckv': nrm((DEC_BATCH, L, PAST_LEN, KV_LORA)),
        'cache_krope': nrm((DEC_BATCH, L, PAST_LEN, QK_ROPE_DIM)),
        'c': nrm((DEC_BATCH, D)),
        'c_ctx': nrm((D,)),
        'mod_w': nrm((L, D, N_MOD * D), 0.5 * D ** -0.5),
        'mod_b': nrm((L, N_MOD * D), 0.02),
        'norm1_g': gain((L, D)),
        'w_in': nrm((L, D, IN_COLS), D ** -0.5),
        'q_norm_g': gain((L, Q_LORA)),
        'w_uq': nrm((L, Q_LORA, N_HEADS * (QK_NOPE_DIM + QK_ROPE_DIM)), Q_LORA ** -0.5),
        'kv_norm_g': gain((L, KV_LORA)),
        'w_ukv': nrm((L, KV_LORA, N_HEADS * (QK_NOPE_DIM + V_DIM)), KV_LORA ** -0.5),
        'w_o_attn': nrm((L, N_HEADS * V_DIM, D), (N_HEADS * V_DIM) ** -0.5),
        'conv_w': nrm((L, CONV_K, CONV_C), CONV_K ** -0.5),
        'conv_b': nrm((L, CONV_C), 0.02),
        'conv_ln_g': gain((L, CONV_C)),
        'conv_ln_b': nrm((L, CONV_C), 0.02),
        'w_pw2': nrm((L, CONV_C, D), CONV_C ** -0.5),
        'w_out': nrm((L, D, D), D ** -0.5),
        'norm2_g': gain((L, D)),
        'router_w': nrm((L, D, N_EXPERTS), D ** -0.5),
        'router_bias': nrm((L, N_EXPERTS), 0.01),
        'exp_w_gate': nrm((L, N_EXPERTS, D, EXPERT_FF), D ** -0.5),
        'exp_w_up': nrm((L, N_EXPERTS, D, EXPERT_FF), D ** -0.5),
        'exp_w_down': nrm((L, N_EXPERTS, EXPERT_FF, D), EXPERT_FF ** -0.5),
        'sh_w_gate': nrm((L, D, EXPERT_FF), D ** -0.5),
        'sh_w_up': nrm((L, D, EXPERT_FF), D ** -0.5),
        'sh_w_down': nrm((L, EXPERT_FF, D), EXPERT_FF ** -0.5),
        'final_g': gain((D,)),
    }


def reference(x_prompt, x_sample, cache_ckv, cache_krope, c, c_ctx, mod_w, mod_b, norm1_g, w_in,
              q_norm_g, w_uq, kv_norm_g, w_ukv, w_o_attn, conv_w, conv_b, conv_ln_g, conv_ln_b, w_pw2,
              w_out, norm2_g, router_w, router_bias, exp_w_gate, exp_w_up, exp_w_down,
              sh_w_gate, sh_w_up, sh_w_down, final_g):
    def layer_params(l):
        return {
            'norm1_g': norm1_g[l], 'w_in': w_in[l], 'q_norm_g': q_norm_g[l], 'w_uq': w_uq[l],
            'kv_norm_g': kv_norm_g[l], 'w_ukv': w_ukv[l], 'w_o_attn': w_o_attn[l],
            'conv_w': conv_w[l], 'conv_b': conv_b[l], 'conv_ln_g': conv_ln_g[l], 'conv_ln_b': conv_ln_b[l],
            'w_pw2': w_pw2[l], 'w_out': w_out[l], 'norm2_g': norm2_g[l], 'router_w': router_w[l],
            'router_bias': router_bias[l], 'exp_w_gate': exp_w_gate[l], 'exp_w_up': exp_w_up[l],
            'exp_w_down': exp_w_down[l], 'sh_w_gate': sh_w_gate[l], 'sh_w_up': sh_w_up[l],
            'sh_w_down': sh_w_down[l],
        }

    x = x_prompt
    ckv_list, krope_list = [], []
    for l in range(DEPTH):
        mods = adaln(c_ctx[None, :], mod_w[l], mod_b[l])
        x, ckv_l, krope_l = trunk_layer(x, mods, layer_params(l), None, None)
        ckv_list.append(ckv_l)
        krope_list.append(krope_l)
    y_prompt = rmsnorm(x, final_g)
    new_cache_ckv = jnp.stack(ckv_list, axis=1)
    new_cache_krope = jnp.stack(krope_list, axis=1)

    rope = axial_rope(x_sample.shape[1])
    x = x_sample
    for l in range(DEPTH):
        mods = adaln(c, mod_w[l], mod_b[l])
        x, _, _ = trunk_layer(x, mods, layer_params(l), rope, (cache_ckv[:, l], cache_krope[:, l]))
    y_sample = rmsnorm(x, final_g)
    return (y_prompt, y_sample, new_cache_ckv, new_cache_krope)
```

```python
import functools
import math

import jax
import jax.numpy as jnp
from jax import lax
from jax.experimental import pallas as pl
from jax.experimental.pallas import tpu as pltpu

F32 = jnp.float32
BF16 = jnp.bfloat16

D_MODEL = 1024
BATCH = 16
SEQ = 256
DEC_BATCH = 2
DEC_SEQ = 2048
PAST_LEN = 256
GRID_W = 64
N_HEADS = 16
QK_NOPE_DIM = 64
QK_ROPE_DIM = 32
V_DIM = 64
Q_LORA = 384
KV_LORA = 256
ROPE_BASE = 10000.0
ATTN_SCALE = 1.0 / math.sqrt(QK_NOPE_DIM + QK_ROPE_DIM)
CONV_K = 31
CONV_PAD = CONV_K // 2
N_EXPERTS = 64
TOP_K = 8
N_GROUPS = 8
GROUP_SIZE = N_EXPERTS // N_GROUPS
TOPK_GROUPS = 4
EXPERT_FF = 256
ROUTED_SCALE = 2.5
EPS = 1e-6
N_MOD = 6

LANE = 128
HEAD_PAD = LANE
QK_COLS = N_HEADS * HEAD_PAD
N_PROMPT = BATCH * SEQ
N_SAMPLE = DEC_BATCH * DEC_SEQ
N_TOK = N_PROMPT + N_SAMPLE

TM = 256
NP_TILES = N_PROMPT // TM
NS_TILES = N_SAMPLE // TM
N_TILES = NP_TILES + NS_TILES
TILES_PER_DEC = DEC_SEQ // TM
HALO = 16

C_Q = 0
C_KV = C_Q + Q_LORA
C_KR = C_KV + KV_LORA
C_CONV = C_KR + 2 * LANE
C_GATE = C_CONV + 2 * D_MODEL
IN_COLS_PACKED = C_GATE + 2 * D_MODEL

MOE_TM = 1024
MOE_EPS = 2
VMEM_CAP = 56 * 1024 * 1024


def _cparams(sem, vmem_mb):
    return pltpu.CompilerParams(dimension_semantics=sem,
                                vmem_limit_bytes=min(vmem_mb * 1024 * 1024, VMEM_CAP))


def _const_spec(shape):
    n = len(shape)
    return pl.BlockSpec(shape, lambda *_: (0,) * n, pipeline_mode=pl.Buffered(1))


def _seg_of_tile(i):
    return jnp.where(i < NP_TILES, 0, 1 + (i - NP_TILES) // TILES_PER_DEC)


def _rms(x, g):
    return x * lax.rsqrt(jnp.mean(x * x, axis=-1, keepdims=True) + EPS) * g


def _mods_kernel(cond_ref, w_ref, b_ref, o_ref):
    c = cond_ref[...]
    s = c * jax.nn.sigmoid(c)
    o_ref[...] = jnp.dot(s.astype(BF16), w_ref[...].astype(BF16),
                         preferred_element_type=F32) + b_ref[...]


def _mods(cond8, mod_w, mod_b):
    n = N_MOD * D_MODEL
    bn = 512
    return pl.pallas_call(
        _mods_kernel,
        grid=(n // bn,),
        in_specs=[pl.BlockSpec((8, D_MODEL), lambda j: (0, 0)),
                  pl.BlockSpec((D_MODEL, bn), lambda j: (0, j)),
                  pl.BlockSpec((1, bn), lambda j: (0, j))],
        out_specs=pl.BlockSpec((8, bn), lambda j: (0, j)),
        out_shape=jax.ShapeDtypeStruct((8, n), F32),
        compiler_params=_cparams(("arbitrary",), 16),
        name="mods",
    )(cond8, mod_w, mod_b)


def _inproj_kernel(xp_ref, xs_ref, mods_ref, g1_ref, qg_ref, kvg_ref, win_ref, wuqp_ref, wuqs_ref,
                   wukp_ref, wuv_ref, tab_ref,
                   q_ref, k_ref, v_ref, z_ref, gt_ref, ckv_ref, kr_ref):
    i = pl.program_id(0)
    x = jnp.where(i < NP_TILES, xp_ref[...], xs_ref[...])
    m = mods_ref[0]
    h = _rms(x, g1_ref[...]) * (1.0 + m[1:2]) + m[0:1]
    hb = h.astype(BF16)

    def proj(c0, c1):
        return jnp.dot(hb, win_ref[:, c0:c1], preferred_element_type=F32)

    def lanes16(t):
        return jnp.concatenate([t] * N_HEADS, axis=1)

    qn = _rms(proj(C_Q, C_KV), qg_ref[...]).astype(BF16)
    q = (jnp.dot(qn, wuqp_ref[...], preferred_element_type=F32) * lanes16(tab_ref[0])
         + jnp.dot(qn, wuqs_ref[...], preferred_element_type=F32) * lanes16(tab_ref[1]))
    q_ref[...] = q.astype(BF16)

    ckv = _rms(proj(C_KV, C_KR), kvg_ref[...])
    ckvb = ckv.astype(BF16)
    ukr = proj(C_KR, C_CONV)
    ua = ukr[:, :LANE]
    krot = ua * tab_ref[2] + ukr[:, LANE:] * tab_ref[3]
    k = jnp.dot(ckvb, wukp_ref[...], preferred_element_type=F32) + lanes16(krot)
    k_ref[...] = k.astype(BF16)
    v_ref[...] = jnp.dot(ckvb, wuv_ref[...], preferred_element_type=F32).astype(BF16)

    @pl.when(i < NP_TILES)
    def _():
        ckv_ref[...] = ckv
        kr_ref[...] = ua[:, QK_NOPE_DIM:QK_NOPE_DIM + QK_ROPE_DIM]

    a = proj(C_CONV, C_CONV + D_MODEL)
    g = proj(C_CONV + D_MODEL, C_GATE)
    z_ref[...] = (a * jax.nn.sigmoid(g)).astype(BF16)
    gt_ref[...] = jax.nn.sigmoid(proj(C_GATE, IN_COLS_PACKED)).astype(BF16)


def _inproj(xp, xs, mods3, g1, qg, kvg, win, wuqp, wuqs, wukp, wuv, tab):
    tile = lambda i: (i, 0)
    ptile = lambda i: (jnp.minimum(i, NP_TILES - 1), 0)
    stile = lambda i: (jnp.maximum(i - NP_TILES, 0), 0)
    tab_blk = lambda i: (0, jnp.where(i < NP_TILES, 0, 1 + (i - NP_TILES) % TILES_PER_DEC), 0)
    return pl.pallas_call(
        _inproj_kernel,
        grid=(N_TILES,),
        in_specs=[pl.BlockSpec((TM, D_MODEL), ptile),
                  pl.BlockSpec((TM, D_MODEL), stile),
                  pl.BlockSpec((1, N_MOD, D_MODEL), lambda i: (_seg_of_tile(i), 0, 0)),
                  _const_spec((1, D_MODEL)),
                  _const_spec((1, Q_LORA)),
                  _const_spec((1, KV_LORA)),
                  _const_spec((D_MODEL, IN_COLS_PACKED)),
                  _const_spec((Q_LORA, QK_COLS)),
                  _const_spec((Q_LORA, QK_COLS)),
                  _const_spec((KV_LORA, QK_COLS)),
                  _const_spec((KV_LORA, N_HEADS * V_DIM)),
                  pl.BlockSpec((4, TM, LANE), tab_blk)],
        out_specs=[pl.BlockSpec((TM, QK_COLS), tile),
                   pl.BlockSpec((TM, QK_COLS), tile),
                   pl.BlockSpec((TM, N_HEADS * V_DIM), tile),
                   pl.BlockSpec((TM, D_MODEL), tile),
                   pl.BlockSpec((TM, 2 * D_MODEL), tile),
                   pl.BlockSpec((TM, KV_LORA), ptile),
                   pl.BlockSpec((TM, QK_ROPE_DIM), ptile)],
        out_shape=[jax.ShapeDtypeStruct((N_TOK, QK_COLS), BF16),
                   jax.ShapeDtypeStruct((N_TOK, QK_COLS), BF16),
                   jax.ShapeDtypeStruct((N_TOK, N_HEADS * V_DIM), BF16),
                   jax.ShapeDtypeStruct((N_TOK, D_MODEL), BF16),
                   jax.ShapeDtypeStruct((N_TOK, 2 * D_MODEL), BF16),
                   jax.ShapeDtypeStruct((N_PROMPT, KV_LORA), F32),
                   jax.ShapeDtypeStruct((N_PROMPT, QK_ROPE_DIM), F32)],
        compiler_params=_cparams(("arbitrary",), 48),
        name="inproj",
    )(xp, xs, mods3, g1, qg, kvg, win, wuqp, wuqs, wukp, wuv, tab)


def _ctxkv_kernel(ckv_ref, kr_ref, wukp_ref, wuv_ref, k_ref, v_ref):
    c = ckv_ref[0].astype(BF16)
    kr = jnp.concatenate([kr_ref[0]] * N_HEADS, axis=1)
    k_ref[0] = (jnp.dot(c, wukp_ref[...], preferred_element_type=F32) + kr).astype(BF16)
    v_ref[0] = jnp.dot(c, wuv_ref[...], preferred_element_type=F32).astype(BF16)


def _ctxkv(cache_ckv, krope_pad, wukp, wuv):
    return pl.pallas_call(
        _ctxkv_kernel,
        grid=(DEC_BATCH,),
        in_specs=[pl.BlockSpec((1, PAST_LEN, KV_LORA), lambda b: (b, 0, 0)),
                  pl.BlockSpec((1, PAST_LEN, LANE), lambda b: (b, 0, 0)),
                  _const_spec((KV_LORA, QK_COLS)),
                  _const_spec((KV_LORA, N_HEADS * V_DIM))],
        out_specs=[pl.BlockSpec((1, PAST_LEN, QK_COLS), lambda b: (b, 0, 0)),
                   pl.BlockSpec((1, PAST_LEN, N_HEADS * V_DIM), lambda b: (b, 0, 0))],
        out_shape=[jax.ShapeDtypeStruct((DEC_BATCH, PAST_LEN, QK_COLS), BF16),
                   jax.ShapeDtypeStruct((DEC_BATCH, PAST_LEN, N_HEADS * V_DIM), BF16)],
        compiler_params=_cparams(("arbitrary",), 16),
        name="ctxkv",
    )(cache_ckv, krope_pad, wukp, wuv)


_NT = (((1,), (1,)), ((), ()))


def _attn_kernel(*refs, has_ctx):
    if has_ctx:
        q_ref, k_ref, v_ref, kc_ref, vc_ref, o_ref = refs
    else:
        q_ref, k_ref, v_ref, o_ref = refs
    tq = q_ref.shape[0]
    lane = lax.broadcasted_iota(jnp.int32, (tq, LANE), 1)
    for j in range(N_HEADS // 2):
        pair = slice(j * LANE, (j + 1) * LANE)
        outs = []
        for h in (2 * j, 2 * j + 1):
            hs = slice(h * HEAD_PAD, (h + 1) * HEAD_PAD)
            qh = q_ref[:, hs]
            s = lax.dot_general(qh, k_ref[:, hs], _NT, preferred_element_type=F32)
            mx = jnp.max(s, axis=-1, keepdims=True)
            if has_ctx:
                sc = lax.dot_general(qh, kc_ref[0, :, hs], _NT, preferred_element_type=F32)
                mx = jnp.maximum(mx, jnp.max(sc, axis=-1, keepdims=True))
            p = jnp.exp(s - mx)
            den = jnp.sum(p, axis=-1, keepdims=True)
            o = jnp.dot(p.astype(BF16), v_ref[:, pair], preferred_element_type=F32)
            if has_ctx:
                pc = jnp.exp(sc - mx)
                den = den + jnp.sum(pc, axis=-1, keepdims=True)
                o = o + jnp.dot(pc.astype(BF16), vc_ref[0, :, pair], preferred_element_type=F32)
            outs.append(o / den)
        o_ref[:, pair] = jnp.where(lane < V_DIM, outs[0], outs[1]).astype(BF16)


def _attn_prompt(q, k, v):
    blk = lambda b: (b, 0)
    return pl.pallas_call(
        functools.partial(_attn_kernel, has_ctx=False),
        grid=(BATCH,),
        in_specs=[pl.BlockSpec((SEQ, QK_COLS), blk),
                  pl.BlockSpec((SEQ, QK_COLS), blk),
                  pl.BlockSpec((SEQ, N_HEADS * V_DIM), blk)],
        out_specs=pl.BlockSpec((SEQ, N_HEADS * V_DIM), blk),
        out_shape=jax.ShapeDtypeStruct((N_PROMPT, N_HEADS * V_DIM), BF16),
        compiler_params=_cparams(("arbitrary",), 24),
        name="attn_prompt",
    )(q, k, v)


ATT_TQ = 256


def _attn_sample(q, k, v, kc, vc):
    nq = DEC_SEQ // ATT_TQ
    q0 = N_PROMPT // ATT_TQ
    s0 = N_PROMPT // DEC_SEQ
    once = pl.Buffered(1)
    return pl.pallas_call(
        functools.partial(_attn_kernel, has_ctx=True),
        grid=(DEC_BATCH, nq),
        in_specs=[pl.BlockSpec((ATT_TQ, QK_COLS), lambda b, t: (q0 + b * nq + t, 0)),
                  pl.BlockSpec((DEC_SEQ, QK_COLS), lambda b, t: (s0 + b, 0), pipeline_mode=once),
                  pl.BlockSpec((DEC_SEQ, N_HEADS * V_DIM), lambda b, t: (s0 + b, 0), pipeline_mode=once),
                  pl.BlockSpec((1, PAST_LEN, QK_COLS), lambda b, t: (b, 0, 0), pipeline_mode=once),
                  pl.BlockSpec((1, PAST_LEN, N_HEADS * V_DIM), lambda b, t: (b, 0, 0),
                               pipeline_mode=once)],
        out_specs=pl.BlockSpec((ATT_TQ, N_HEADS * V_DIM), lambda b, t: (b * nq + t, 0)),
        out_shape=jax.ShapeDtypeStruct((N_SAMPLE, N_HEADS * V_DIM), BF16),
        compiler_params=_cparams(("arbitrary", "arbitrary"), 48),
        name="attn_sample",
    )(q, k, v, kc, vc)


def _topk_rows(v, k, n):
    rows = lax.broadcasted_iota(jnp.int32, v.shape, 0).astype(F32)
    sel = jnp.zeros(v.shape, F32)
    for _ in range(k):
        mx = jnp.max(v, axis=0, keepdims=True)
        first = jnp.min(jnp.where(v == mx, rows, float(n)), axis=0, keepdims=True)
        hit = rows == first
        sel = jnp.where(hit, 1.0, sel)
        v = jnp.where(hit, -jnp.inf, v)
    return sel > 0.0


def _route(logits_t, bias):
    scores = jax.nn.sigmoid(logits_t)
    sel = scores + bias
    gscore = []
    for g in range(N_GROUPS):
        blk = sel[g * GROUP_SIZE:(g + 1) * GROUP_SIZE]
        top2 = _topk_rows(blk, 2, GROUP_SIZE)
        gscore.append(jnp.sum(jnp.where(top2, blk, 0.0), axis=0, keepdims=True))
    masked = []
    for g in range(N_GROUPS):
        beaten = jnp.zeros(gscore[g].shape, F32)
        for o in range(N_GROUPS):
            if o < g:
                beaten = beaten + jnp.where(gscore[o] >= gscore[g], 1.0, 0.0)
            elif o > g:
                beaten = beaten + jnp.where(gscore[o] > gscore[g], 1.0, 0.0)
        keep = beaten < float(TOPK_GROUPS)
        masked.append(jnp.where(keep, sel[g * GROUP_SIZE:(g + 1) * GROUP_SIZE], -jnp.inf))
    chosen = _topk_rows(jnp.concatenate(masked, axis=0), TOP_K, N_EXPERTS)
    w = jnp.where(chosen, scores, 0.0)
    return w / jnp.sum(w, axis=0, keepdims=True) * ROUTED_SCALE


def _mix_kernel(ap_ref, as_ref, z_ref, zp_ref, zn_ref, gt_ref, xp_ref, xs_ref, mods_ref,
                wo_ref, wpw_ref, wout_ref, cw_ref, cb_ref, lng_ref, lnb_ref, g2_ref,
                rwh_ref, rwl_ref, rb_ref,
                x1_ref, h2_ref, gates_ref, win_ref, cz_ref):
    i = pl.program_id(0)
    is_prompt = i < NP_TILES
    pos = (i - NP_TILES) % TILES_PER_DEC
    first = jnp.logical_or(is_prompt, pos == 0)
    last = jnp.logical_or(is_prompt, pos == TILES_PER_DEC - 1)

    nchunk = D_MODEL // LANE
    zprev = jnp.where(first, 0.0, zp_ref[...].astype(F32))
    znext = jnp.where(last, 0.0, zn_ref[...].astype(F32))
    zc = z_ref[...].astype(F32)
    for c in range(nchunk):
        cs = slice(c * LANE, (c + 1) * LANE)
        win_ref[c, 0:HALO, :] = zprev[:, cs]
        win_ref[c, HALO:HALO + TM, :] = zc[:, cs]
        win_ref[c, HALO + TM:, :] = znext[:, cs]

    def conv_chunk(c, carry):
        w = cw_ref[c]
        acc = jnp.zeros((TM, LANE), F32)
        for kk in range(CONV_K):
            off = HALO - CONV_PAD + kk
            acc = acc + win_ref[c, off:off + TM, :] * w[kk:kk + 1, :]
        cz_ref[c] = acc
        return carry

    lax.fori_loop(0, nchunk, conv_chunk, 0)
    conv = jnp.concatenate([cz_ref[c] for c in range(nchunk)], axis=1) + cb_ref[...]
    mu = jnp.mean(conv, axis=-1, keepdims=True)
    cen = conv - mu
    var = jnp.mean(cen * cen, axis=-1, keepdims=True)
    ln = cen * lax.rsqrt(var + EPS) * lng_ref[...] + lnb_ref[...]
    act = (ln * jax.nn.sigmoid(ln)).astype(BF16)
    conv_out = jnp.dot(act, wpw_ref[...], preferred_element_type=F32)

    attn = jnp.where(is_prompt, ap_ref[...], as_ref[...])
    attn_out = jnp.dot(attn, wo_ref[...], preferred_element_type=F32)
    gt = gt_ref[...].astype(F32)
    merged = gt[:, :D_MODEL] * attn_out + gt[:, D_MODEL:] * conv_out
    mix = jnp.dot(merged.astype(BF16), wout_ref[...], preferred_element_type=F32)

    m = mods_ref[0]
    x = jnp.where(is_prompt, xp_ref[...], xs_ref[...])
    x1 = x + m[2:3] * mix
    x1_ref[...] = x1
    h2 = _rms(x1, g2_ref[...]) * (1.0 + m[4:5]) + m[3:4]
    hh = h2.astype(BF16)
    h2_ref[...] = hh

    hl = (h2 - hh.astype(F32)).astype(BF16)
    rwh = rwh_ref[...]
    logits_t = (lax.dot_general(rwh, hh, _NT, preferred_element_type=F32)
                + lax.dot_general(rwh, hl, _NT, preferred_element_type=F32)
                + lax.dot_general(rwl_ref[...], hh, _NT, preferred_element_type=F32))
    gates_ref[...] = _route(logits_t, rb_ref[...])


def _mix(attn_p, attn_s, z, gt, xp, xs, mods3, wo, wpw, wout, cw, cb, lng, lnb, g2, rwh, rwl, rb):
    tile = lambda i: (i, 0)
    ptile = lambda i: (jnp.minimum(i, NP_TILES - 1), 0)
    stile = lambda i: (jnp.maximum(i - NP_TILES, 0), 0)
    hpt = TM // HALO
    nhb = N_TOK // HALO
    return pl.pallas_call(
        _mix_kernel,
        grid=(N_TILES,),
        in_specs=[pl.BlockSpec((TM, D_MODEL), ptile),
                  pl.BlockSpec((TM, D_MODEL), stile),
                  pl.BlockSpec((TM, D_MODEL), tile),
                  pl.BlockSpec((HALO, D_MODEL), lambda i: (jnp.maximum(i * hpt - 1, 0), 0)),
                  pl.BlockSpec((HALO, D_MODEL), lambda i: (jnp.minimum((i + 1) * hpt, nhb - 1), 0)),
                  pl.BlockSpec((TM, 2 * D_MODEL), tile),
                  pl.BlockSpec((TM, D_MODEL), ptile),
                  pl.BlockSpec((TM, D_MODEL), stile),
                  pl.BlockSpec((1, N_MOD, D_MODEL), lambda i: (_seg_of_tile(i), 0, 0)),
                  _const_spec((D_MODEL, D_MODEL)),
                  _const_spec((D_MODEL, D_MODEL)),
                  _const_spec((D_MODEL, D_MODEL)),
                  _const_spec((D_MODEL // LANE, CONV_K, LANE)),
                  _const_spec((1, D_MODEL)),
                  _const_spec((1, D_MODEL)),
                  _const_spec((1, D_MODEL)),
                  _const_spec((1, D_MODEL)),
                  _const_spec((N_EXPERTS, D_MODEL)),
                  _const_spec((N_EXPERTS, D_MODEL)),
                  _const_spec((N_EXPERTS, 1))],
        out_specs=[pl.BlockSpec((TM, D_MODEL), tile),
                   pl.BlockSpec((TM, D_MODEL), tile),
                   pl.BlockSpec((N_EXPERTS, TM), lambda i: (0, i))],
        out_shape=[jax.ShapeDtypeStruct((N_TOK, D_MODEL), F32),
                   jax.ShapeDtypeStruct((N_TOK, D_MODEL), BF16),
                   jax.ShapeDtypeStruct((N_EXPERTS, N_TOK), F32)],
        scratch_shapes=[pltpu.VMEM((D_MODEL // LANE, TM + 2 * HALO, LANE), F32),
                        pltpu.VMEM((D_MODEL // LANE, TM, LANE), F32)],
        compiler_params=_cparams(("arbitrary",), 40),
        name="mix",
    )(attn_p, attn_s, z, z, z, gt, xp, xs, mods3, wo, wpw, wout, cw, cb, lng, lnb, g2, rwh, rwl, rb)


def _moe_kernel(h_ref, g_ref, wg_ref, wu_ref, wd_ref, sg_ref, su_ref, sd_ref, o_ref):
    e = pl.program_id(1)
    h = h_ref[...]

    def ffn(wg, wu):
        a = jnp.dot(h, wg, preferred_element_type=F32)
        b = jnp.dot(h, wu, preferred_element_type=F32)
        return a * jax.nn.sigmoid(a) * b

    @pl.when(e == 0)
    def _():
        hid = ffn(sg_ref[...], su_ref[...])
        o_ref[...] = jnp.dot(hid.astype(BF16), sd_ref[...], preferred_element_type=F32)

    gates = g_ref[...]
    lane = lax.broadcasted_iota(jnp.int32, gates.shape, 1)
    acc = o_ref[...]
    for s in range(MOE_EPS):
        col = jnp.sum(jnp.where(lane == e * MOE_EPS + s, gates, 0.0), axis=1, keepdims=True)
        hid = ffn(wg_ref[s].astype(BF16), wu_ref[s].astype(BF16)) * col
        acc = acc + jnp.dot(hid.astype(BF16), wd_ref[s].astype(BF16), preferred_element_type=F32)
    o_ref[...] = acc


def _moe(h2, gates, wg, wu, wd, sg, su, sd):
    return pl.pallas_call(
        _moe_kernel,
        grid=(N_TOK // MOE_TM, N_EXPERTS // MOE_EPS),
        in_specs=[pl.BlockSpec((MOE_TM, D_MODEL), lambda i, e: (i, 0)),
                  pl.BlockSpec((MOE_TM, N_EXPERTS), lambda i, e: (i, 0)),
                  pl.BlockSpec((MOE_EPS, D_MODEL, EXPERT_FF), lambda i, e: (e, 0, 0)),
                  pl.BlockSpec((MOE_EPS, D_MODEL, EXPERT_FF), lambda i, e: (e, 0, 0)),
                  pl.BlockSpec((MOE_EPS, EXPERT_FF, D_MODEL), lambda i, e: (e, 0, 0)),
                  _const_spec((D_MODEL, EXPERT_FF)),
                  _const_spec((D_MODEL, EXPERT_FF)),
                  _const_spec((EXPERT_FF, D_MODEL))],
        out_specs=pl.BlockSpec((MOE_TM, D_MODEL), lambda i, e: (i, 0)),
        out_shape=jax.ShapeDtypeStruct((N_TOK, D_MODEL), F32),
        compiler_params=_cparams(("arbitrary", "arbitrary"), 48),
        name="moe",
    )(h2, gates, wg, wu, wd, sg, su, sd)


FIN_TM = 512
FIN_NP = N_PROMPT // FIN_TM


def _final_kernel(x1_ref, ff_ref, mods_ref, fg_ref, yp_ref, ys_ref):
    i = pl.program_id(0)
    m = mods_ref[0]
    y = _rms(x1_ref[...] + m[5:6] * ff_ref[...], fg_ref[...])

    @pl.when(i < FIN_NP)
    def _():
        yp_ref[...] = y

    @pl.when(i >= FIN_NP)
    def _():
        ys_ref[...] = y


def _final(x1, ff, mods3, fg):
    tile = lambda i: (i, 0)
    per_dec = DEC_SEQ // FIN_TM
    seg = lambda i: (jnp.where(i < FIN_NP, 0, 1 + (i - FIN_NP) // per_dec), 0, 0)
    return pl.pallas_call(
        _final_kernel,
        grid=(N_TOK // FIN_TM,),
        in_specs=[pl.BlockSpec((FIN_TM, D_MODEL), tile),
                  pl.BlockSpec((FIN_TM, D_MODEL), tile),
                  pl.BlockSpec((1, N_MOD, D_MODEL), seg),
                  _const_spec((1, D_MODEL))],
        out_specs=[pl.BlockSpec((FIN_TM, D_MODEL), lambda i: (jnp.minimum(i, FIN_NP - 1), 0)),
                   pl.BlockSpec((FIN_TM, D_MODEL), lambda i: (jnp.maximum(i - FIN_NP, 0), 0))],
        out_shape=[jax.ShapeDtypeStruct((N_PROMPT, D_MODEL), F32),
                   jax.ShapeDtypeStruct((N_SAMPLE, D_MODEL), F32)],
        compiler_params=_cparams(("arbitrary",), 32),
        name="final",
    )(x1, ff, mods3, fg)


_ROPE_SWAP = tuple(list(range(8, 16)) + list(range(0, 8)) + list(range(24, 32)) + list(range(16, 24)))


def _rope_tables():
    pos = jnp.arange(DEC_SEQ, dtype=jnp.int32)
    row = (pos // GRID_W).astype(F32)
    col = (pos % GRID_W).astype(F32)
    n_freq = QK_ROPE_DIM // 4
    inv_freq = ROPE_BASE ** (-jnp.arange(n_freq, dtype=F32) / n_freq)
    ar, ac = row[:, None] * inv_freq, col[:, None] * inv_freq
    cos32 = jnp.concatenate([jnp.cos(ar), jnp.cos(ar), jnp.cos(ac), jnp.cos(ac)], axis=1)
    sin32 = jnp.concatenate([-jnp.sin(ar), jnp.sin(ar), -jnp.sin(ac), jnp.sin(ac)], axis=1)
    cos32 = jnp.concatenate([jnp.ones((TM, QK_ROPE_DIM), F32), cos32], axis=0)
    sin32 = jnp.concatenate([jnp.zeros((TM, QK_ROPE_DIM), F32), sin32], axis=0)
    n = cos32.shape[0]
    ones = jnp.ones((n, QK_NOPE_DIM), F32)
    zeros = jnp.zeros((n, QK_NOPE_DIM), F32)
    tail = jnp.zeros((n, HEAD_PAD - QK_NOPE_DIM - QK_ROPE_DIM), F32)
    qc = jnp.concatenate([ones, cos32, tail], axis=1) * ATTN_SCALE
    qs = jnp.concatenate([zeros, sin32, tail], axis=1) * ATTN_SCALE
    kc = jnp.concatenate([zeros, cos32, tail], axis=1)
    ks = jnp.concatenate([zeros, sin32, tail], axis=1)
    return jnp.stack([qc, qs, kc, ks], axis=0)


def _pack_weights(w_in, w_uq, w_ukv):
    swap = jnp.array(_ROPE_SWAP, dtype=jnp.int32)
    s0, s1, s2, s3 = Q_LORA, Q_LORA + KV_LORA, Q_LORA + KV_LORA + QK_ROPE_DIM, \
        Q_LORA + KV_LORA + QK_ROPE_DIM + 2 * D_MODEL
    wkr = w_in[:, s1:s2]
    pad_l = jnp.zeros((D_MODEL, QK_NOPE_DIM), F32)
    pad_r = jnp.zeros((D_MODEL, HEAD_PAD - QK_NOPE_DIM - QK_ROPE_DIM), F32)
    win = jnp.concatenate([w_in[:, :s0], w_in[:, s0:s1],
                           pad_l, wkr, pad_r, pad_l, wkr[:, swap], pad_r,
                           w_in[:, s2:s3], w_in[:, s3:]], axis=1).astype(BF16)
    uq = w_uq.reshape(Q_LORA, N_HEADS, QK_NOPE_DIM + QK_ROPE_DIM)
    nope, rope = uq[:, :, :QK_NOPE_DIM], uq[:, :, QK_NOPE_DIM:]
    zt = jnp.zeros((Q_LORA, N_HEADS, HEAD_PAD - QK_NOPE_DIM - QK_ROPE_DIM), F32)
    wuqp = jnp.concatenate([nope, rope, zt], axis=2).reshape(Q_LORA, QK_COLS).astype(BF16)
    wuqs = jnp.concatenate([jnp.zeros_like(nope), rope[:, :, swap], zt], axis=2)
    wuqs = wuqs.reshape(Q_LORA, QK_COLS).astype(BF16)
    ukv = w_ukv.reshape(KV_LORA, N_HEADS, QK_NOPE_DIM + V_DIM)
    zk = jnp.zeros((KV_LORA, N_HEADS, HEAD_PAD - QK_NOPE_DIM), F32)
    wukp = jnp.concatenate([ukv[:, :, :QK_NOPE_DIM], zk], axis=2).reshape(KV_LORA, QK_COLS).astype(BF16)
    wuv = ukv[:, :, QK_NOPE_DIM:].reshape(KV_LORA, N_HEADS * V_DIM).astype(BF16)
    return win, wuqp, wuqs, wukp, wuv


def kernel(x_prompt, x_sample, cache_ckv, cache_krope, c, c_ctx, mod_w, mod_b, norm1_g, w_in, q_norm_g, w_uq, kv_norm_g, w_ukv, w_o_attn, conv_w, conv_b, conv_ln_g, conv_ln_b, w_pw2, w_out, norm2_g, router_w, router_bias, exp_w_gate, exp_w_up, exp_w_down, sh_w_gate, sh_w_up, sh_w_down, final_g):
    xp = x_prompt.reshape(N_PROMPT, D_MODEL)
    xs = x_sample.reshape(N_SAMPLE, D_MODEL)
    cond8 = jnp.concatenate([c_ctx[None, :], c, jnp.zeros((8 - 1 - DEC_BATCH, D_MODEL), F32)], axis=0)
    mods3 = _mods(cond8, mod_w[0], mod_b).reshape(8, N_MOD, D_MODEL)

    win, wuqp, wuqs, wukp, wuv = _pack_weights(w_in[0], w_uq[0], w_ukv[0])
    q, k, v, z, gt, ckv_new, kr_new = _inproj(
        xp, xs, mods3, norm1_g, q_norm_g, kv_norm_g, win, wuqp, wuqs, wukp, wuv, _rope_tables())

    krope_pad = jnp.pad(cache_krope[:, 0], ((0, 0), (0, 0),
                                            (QK_NOPE_DIM, HEAD_PAD - QK_NOPE_DIM - QK_ROPE_DIM)))
    kc, vc = _ctxkv(cache_ckv[:, 0], krope_pad, wukp, wuv)
    attn_p = _attn_prompt(q, k, v)
    attn_s = _attn_sample(q, k, v, kc, vc)

    rwt = router_w[0].T
    rwh = rwt.astype(BF16)
    rwl = (rwt - rwh.astype(F32)).astype(BF16)
    cw = conv_w[0].reshape(CONV_K, D_MODEL // LANE, LANE).transpose(1, 0, 2)
    x1, h2, gates_t = _mix(
        attn_p, attn_s, z, gt, xp, xs, mods3,
        w_o_attn[0].astype(BF16), w_pw2[0].astype(BF16), w_out[0].astype(BF16),
        cw, conv_b, conv_ln_g, conv_ln_b, norm2_g, rwh, rwl, router_bias.reshape(N_EXPERTS, 1))

    ff = _moe(h2, gates_t.T, exp_w_gate[0], exp_w_up[0], exp_w_down[0],
              sh_w_gate[0].astype(BF16), sh_w_up[0].astype(BF16), sh_w_down[0].astype(BF16))
    y_p, y_s = _final(x1, ff, mods3, final_g[None, :])
    return (y_p.reshape(BATCH, SEQ, D_MODEL),
            y_s.reshape(DEC_BATCH, DEC_SEQ, D_MODEL),
            ckv_new.reshape(BATCH, 1, SEQ, KV_LORA),
            kr_new.reshape(BATCH, 1, SEQ, QK_ROPE_DIM))
```

```python
import functools
import math

import jax
import jax.numpy as jnp
from jax import lax
from jax.experimental import pallas as pl
from jax.experimental.pallas import tpu as pltpu

F32 = jnp.float32
BF16 = jnp.bfloat16

D_MODEL = 1024
BATCH = 16
SEQ = 256
DEC_BATCH = 2
DEC_SEQ = 2048
PAST_LEN = 256
GRID_W = 64
N_HEADS = 16
QK_NOPE_DIM = 64
QK_ROPE_DIM = 32
V_DIM = 64
Q_LORA = 384
KV_LORA = 256
ROPE_BASE = 10000.0
ATTN_SCALE = 1.0 / math.sqrt(QK_NOPE_DIM + QK_ROPE_DIM)
CONV_K = 31
CONV_PAD = CONV_K // 2
N_EXPERTS = 64
TOP_K = 8
N_GROUPS = 8
GROUP_SIZE = N_EXPERTS // N_GROUPS
TOPK_GROUPS = 4
EXPERT_FF = 256
ROUTED_SCALE = 2.5
EPS = 1e-6
N_MOD = 6

LANE = 128
HEAD_PAD = LANE
QK_COLS = N_HEADS * HEAD_PAD
N_PROMPT = BATCH * SEQ
N_SAMPLE = DEC_BATCH * DEC_SEQ
N_TOK = N_PROMPT + N_SAMPLE

TM = 256
NP_TILES = N_PROMPT // TM
NS_TILES = N_SAMPLE // TM
N_TILES = NP_TILES + NS_TILES
TILES_PER_DEC = DEC_SEQ // TM
HALO = 16

C_Q = 0
C_KV = C_Q + Q_LORA
C_KR = C_KV + KV_LORA
C_CONV = C_KR + 2 * LANE
C_GATE = C_CONV + 2 * D_MODEL
IN_COLS_PACKED = C_GATE + 2 * D_MODEL

VMEM_CAP = 56 * 1024 * 1024


def _cparams(sem, vmem_mb):
    return pltpu.CompilerParams(dimension_semantics=sem,
                                vmem_limit_bytes=min(vmem_mb * 1024 * 1024, VMEM_CAP))


def _const_spec(shape):
    n = len(shape)
    return pl.BlockSpec(shape, lambda *_: (0,) * n, pipeline_mode=pl.Buffered(1))


def _seg_of_tile(i):
    return jnp.where(i < NP_TILES, 0, 1 + (i - NP_TILES) // TILES_PER_DEC)


def _rms(x, g):
    return x * lax.rsqrt(jnp.mean(x * x, axis=-1, keepdims=True) + EPS) * g


def _mods_kernel(cond_ref, w_ref, b_ref, o_ref):
    c = cond_ref[...]
    s = c * jax.nn.sigmoid(c)
    o_ref[...] = jnp.dot(s.astype(BF16), w_ref[...].astype(BF16),
                         preferred_element_type=F32) + b_ref[...]


def _mods(cond8, mod_w, mod_b):
    n = N_MOD * D_MODEL
    bn = 512
    return pl.pallas_call(
        _mods_kernel,
        grid=(n // bn,),
        in_specs=[pl.BlockSpec((8, D_MODEL), lambda j: (0, 0)),
                  pl.BlockSpec((D_MODEL, bn), lambda j: (0, j)),
                  pl.BlockSpec((1, bn), lambda j: (0, j))],
        out_specs=pl.BlockSpec((8, bn), lambda j: (0, j)),
        out_shape=jax.ShapeDtypeStruct((8, n), F32),
        compiler_params=_cparams(("arbitrary",), 16),
        name="mods",
    )(cond8, mod_w, mod_b)


def _inproj_kernel(xp_ref, xs_ref, mods_ref, g1_ref, qg_ref, kvg_ref, win_ref, wuqp_ref, wuqs_ref,
                   wukp_ref, wuv_ref, tab_ref,
                   q_ref, k_ref, v_ref, z_ref, gt_ref, ckv_ref, kr_ref):
    i = pl.program_id(0)
    x = jnp.where(i < NP_TILES, xp_ref[...], xs_ref[...])
    m = mods_ref[0]
    h = _rms(x, g1_ref[...]) * (1.0 + m[1:2]) + m[0:1]
    hb = h.astype(BF16)

    def proj(c0, c1):
        return jnp.dot(hb, win_ref[:, c0:c1], preferred_element_type=F32)

    def lanes16(t):
        return jnp.concatenate([t] * N_HEADS, axis=1)

    qn = _rms(proj(C_Q, C_KV), qg_ref[...]).astype(BF16)
    q = (jnp.dot(qn, wuqp_ref[...], preferred_element_type=F32) * lanes16(tab_ref[0])
         + jnp.dot(qn, wuqs_ref[...], preferred_element_type=F32) * lanes16(tab_ref[1]))
    q_ref[...] = q.astype(BF16)

    ckv = _rms(proj(C_KV, C_KR), kvg_ref[...])
    ckvb = ckv.astype(BF16)
    ukr = proj(C_KR, C_CONV)
    ua = ukr[:, :LANE]
    krot = ua * tab_ref[2] + ukr[:, LANE:] * tab_ref[3]
    k = jnp.dot(ckvb, wukp_ref[...], preferred_element_type=F32) + lanes16(krot)
    k_ref[...] = k.astype(BF16)
    v_ref[...] = jnp.dot(ckvb, wuv_ref[...], preferred_element_type=F32).astype(BF16)

    @pl.when(i < NP_TILES)
    def _():
        ckv_ref[...] = ckv
        kr_ref[...] = ua[:, QK_NOPE_DIM:QK_NOPE_DIM + QK_ROPE_DIM]

    a = proj(C_CONV, C_CONV + D_MODEL)
    g = proj(C_CONV + D_MODEL, C_GATE)
    z_ref[...] = (a * jax.nn.sigmoid(g)).astype(BF16)
    gt_ref[...] = jax.nn.sigmoid(proj(C_GATE, IN_COLS_PACKED)).astype(BF16)


def _inproj(xp, xs, mods3, g1, qg, kvg, win, wuqp, wuqs, wukp, wuv, tab):
    tile = lambda i: (i, 0)
    ptile = lambda i: (jnp.minimum(i, NP_TILES - 1), 0)
    stile = lambda i: (jnp.maximum(i - NP_TILES, 0), 0)
    tab_blk = lambda i: (0, jnp.where(i < NP_TILES, 0, 1 + (i - NP_TILES) % TILES_PER_DEC), 0)
    return pl.pallas_call(
        _inproj_kernel,
        grid=(N_TILES,),
        in_specs=[pl.BlockSpec((TM, D_MODEL), ptile),
                  pl.BlockSpec((TM, D_MODEL), stile),
                  pl.BlockSpec((1, N_MOD, D_MODEL), lambda i: (_seg_of_tile(i), 0, 0)),
                  _const_spec((1, D_MODEL)),
                  _const_spec((1, Q_LORA)),
                  _const_spec((1, KV_LORA)),
                  _const_spec((D_MODEL, IN_COLS_PACKED)),
                  _const_spec((Q_LORA, QK_COLS)),
                  _const_spec((Q_LORA, QK_COLS)),
                  _const_spec((KV_LORA, QK_COLS)),
                  _const_spec((KV_LORA, N_HEADS * V_DIM)),
                  pl.BlockSpec((4, TM, LANE), tab_blk)],
        out_specs=[pl.BlockSpec((TM, QK_COLS), tile),
                   pl.BlockSpec((TM, QK_COLS), tile),
                   pl.BlockSpec((TM, N_HEADS * V_DIM), tile),
                   pl.BlockSpec((TM, D_MODEL), tile),
                   pl.BlockSpec((TM, 2 * D_MODEL), tile),
                   pl.BlockSpec((TM, KV_LORA), ptile),
                   pl.BlockSpec((TM, QK_ROPE_DIM), ptile)],
        out_shape=[jax.ShapeDtypeStruct((N_TOK, QK_COLS), BF16),
                   jax.ShapeDtypeStruct((N_TOK, QK_COLS), BF16),
                   jax.ShapeDtypeStruct((N_TOK, N_HEADS * V_DIM), BF16),
                   jax.ShapeDtypeStruct((N_TOK, D_MODEL), BF16),
                   jax.ShapeDtypeStruct((N_TOK, 2 * D_MODEL), BF16),
                   jax.ShapeDtypeStruct((N_PROMPT, KV_LORA), F32),
                   jax.ShapeDtypeStruct((N_PROMPT, QK_ROPE_DIM), F32)],
        compiler_params=_cparams(("arbitrary",), 48),
        name="inproj",
    )(xp, xs, mods3, g1, qg, kvg, win, wuqp, wuqs, wukp, wuv, tab)


def _ctxkv_kernel(ckv_ref, kr_ref, wukp_ref, wuv_ref, k_ref, v_ref):
    c = ckv_ref[0].astype(BF16)
    kr = jnp.concatenate([kr_ref[0]] * N_HEADS, axis=1)
    k_ref[0] = (jnp.dot(c, wukp_ref[...], preferred_element_type=F32) + kr).astype(BF16)
    v_ref[0] = jnp.dot(c, wuv_ref[...], preferred_element_type=F32).astype(BF16)


def _ctxkv(cache_ckv, krope_pad, wukp, wuv):
    return pl.pallas_call(
        _ctxkv_kernel,
        grid=(DEC_BATCH,),
        in_specs=[pl.BlockSpec((1, PAST_LEN, KV_LORA), lambda b: (b, 0, 0)),
                  pl.BlockSpec((1, PAST_LEN, LANE), lambda b: (b, 0, 0)),
                  _const_spec((KV_LORA, QK_COLS)),
                  _const_spec((KV_LORA, N_HEADS * V_DIM))],
        out_specs=[pl.BlockSpec((1, PAST_LEN, QK_COLS), lambda b: (b, 0, 0)),
                   pl.BlockSpec((1, PAST_LEN, N_HEADS * V_DIM), lambda b: (b, 0, 0))],
        out_shape=[jax.ShapeDtypeStruct((DEC_BATCH, PAST_LEN, QK_COLS), BF16),
                   jax.ShapeDtypeStruct((DEC_BATCH, PAST_LEN, N_HEADS * V_DIM), BF16)],
        compiler_params=_cparams(("arbitrary",), 16),
        name="ctxkv",
    )(cache_ckv, krope_pad, wukp, wuv)


_NT = (((1,), (1,)), ((), ()))


def _attn_kernel(*refs, has_ctx):
    if has_ctx:
        q_ref, k_ref, v_ref, kc_ref, vc_ref, o_ref = refs
    else:
        q_ref, k_ref, v_ref, o_ref = refs
    tq = q_ref.shape[0]
    lane = lax.broadcasted_iota(jnp.int32, (tq, LANE), 1)
    for j in range(N_HEADS // 2):
        pair = slice(j * LANE, (j + 1) * LANE)
        outs = []
        for h in (2 * j, 2 * j + 1):
            hs = slice(h * HEAD_PAD, (h + 1) * HEAD_PAD)
            qh = q_ref[:, hs]
            s = lax.dot_general(qh, k_ref[:, hs], _NT, preferred_element_type=F32)
            mx = jnp.max(s, axis=-1, keepdims=True)
            if has_ctx:
                sc = lax.dot_general(qh, kc_ref[0, :, hs], _NT, preferred_element_type=F32)
                mx = jnp.maximum(mx, jnp.max(sc, axis=-1, keepdims=True))
            p = jnp.exp(s - mx)
            den = jnp.sum(p, axis=-1, keepdims=True)
            o = jnp.dot(p.astype(BF16), v_ref[:, pair], preferred_element_type=F32)
            if has_ctx:
                pc = jnp.exp(sc - mx)
                den = den + jnp.sum(pc, axis=-1, keepdims=True)
                o = o + jnp.dot(pc.astype(BF16), vc_ref[0, :, pair], preferred_element_type=F32)
            outs.append(o / den)
        o_ref[:, pair] = jnp.where(lane < V_DIM, outs[0], outs[1]).astype(BF16)


def _attn_prompt(q, k, v):
    blk = lambda b: (b, 0)
    return pl.pallas_call(
        functools.partial(_attn_kernel, has_ctx=False),
        grid=(BATCH,),
        in_specs=[pl.BlockSpec((SEQ, QK_COLS), blk),
                  pl.BlockSpec((SEQ, QK_COLS), blk),
                  pl.BlockSpec((SEQ, N_HEADS * V_DIM), blk)],
        out_specs=pl.BlockSpec((SEQ, N_HEADS * V_DIM), blk),
        out_shape=jax.ShapeDtypeStruct((N_PROMPT, N_HEADS * V_DIM), BF16),
        compiler_params=_cparams(("arbitrary",), 24),
        name="attn_prompt",
    )(q, k, v)


ATT_TQ = 256


def _attn_sample(q, k, v, kc, vc):
    nq = DEC_SEQ // ATT_TQ
    q0 = N_PROMPT // ATT_TQ
    s0 = N_PROMPT // DEC_SEQ
    once = pl.Buffered(1)
    return pl.pallas_call(
        functools.partial(_attn_kernel, has_ctx=True),
        grid=(DEC_BATCH, nq),
        in_specs=[pl.BlockSpec((ATT_TQ, QK_COLS), lambda b, t: (q0 + b * nq + t, 0)),
                  pl.BlockSpec((DEC_SEQ, QK_COLS), lambda b, t: (s0 + b, 0), pipeline_mode=once),
                  pl.BlockSpec((DEC_SEQ, N_HEADS * V_DIM), lambda b, t: (s0 + b, 0), pipeline_mode=once),
                  pl.BlockSpec((1, PAST_LEN, QK_COLS), lambda b, t: (b, 0, 0), pipeline_mode=once),
                  pl.BlockSpec((1, PAST_LEN, N_HEADS * V_DIM), lambda b, t: (b, 0, 0),
                               pipeline_mode=once)],
        out_specs=pl.BlockSpec((ATT_TQ, N_HEADS * V_DIM), lambda b, t: (b * nq + t, 0)),
        out_shape=jax.ShapeDtypeStruct((N_SAMPLE, N_HEADS * V_DIM), BF16),
        compiler_params=_cparams(("arbitrary", "arbitrary"), 48),
        name="attn_sample",
    )(q, k, v, kc, vc)


def _topk_rows(v, k, n):
    rows = lax.broadcasted_iota(jnp.int32, v.shape, 0).astype(F32)
    sel = jnp.zeros(v.shape, F32)
    for _ in range(k):
        mx = jnp.max(v, axis=0, keepdims=True)
        first = jnp.min(jnp.where(v == mx, rows, float(n)), axis=0, keepdims=True)
        hit = rows == first
        sel = jnp.where(hit, 1.0, sel)
        v = jnp.where(hit, -jnp.inf, v)
    return sel > 0.0


def _route(logits_t, bias):
    scores = jax.nn.sigmoid(logits_t)
    sel = scores + bias
    gscore = []
    for g in range(N_GROUPS):
        blk = sel[g * GROUP_SIZE:(g + 1) * GROUP_SIZE]
        top2 = _topk_rows(blk, 2, GROUP_SIZE)
        gscore.append(jnp.sum(jnp.where(top2, blk, 0.0), axis=0, keepdims=True))
    masked = []
    for g in range(N_GROUPS):
        beaten = jnp.zeros(gscore[g].shape, F32)
        for o in range(N_GROUPS):
            if o < g:
                beaten = beaten + jnp.where(gscore[o] >= gscore[g], 1.0, 0.0)
            elif o > g:
                beaten = beaten + jnp.where(gscore[o] > gscore[g], 1.0, 0.0)
        keep = beaten < float(TOPK_GROUPS)
        masked.append(jnp.where(keep, sel[g * GROUP_SIZE:(g + 1) * GROUP_SIZE], -jnp.inf))
    chosen = _topk_rows(jnp.concatenate(masked, axis=0), TOP_K, N_EXPERTS)
    w = jnp.where(chosen, scores, 0.0)
    return w / jnp.sum(w, axis=0, keepdims=True) * ROUTED_SCALE


def _mix_kernel(ap_ref, as_ref, z_ref, zp_ref, zn_ref, gt_ref, xp_ref, xs_ref, mods_ref,
                wo_ref, wpw_ref, wout_ref, cw_ref, cb_ref, lng_ref, lnb_ref, g2_ref,
                rwh_ref, rwl_ref, rb_ref,
                x1_ref, h2_ref, gates_ref, win_ref, cz_ref):
    i = pl.program_id(0)
    is_prompt = i < NP_TILES
    pos = (i - NP_TILES) % TILES_PER_DEC
    first = jnp.logical_or(is_prompt, pos == 0)
    last = jnp.logical_or(is_prompt, pos == TILES_PER_DEC - 1)

    nchunk = D_MODEL // LANE
    zprev = jnp.where(first, 0.0, zp_ref[...].astype(F32))
    znext = jnp.where(last, 0.0, zn_ref[...].astype(F32))
    zc = z_ref[...].astype(F32)
    for c in range(nchunk):
        cs = slice(c * LANE, (c + 1) * LANE)
        win_ref[c, 0:HALO, :] = zprev[:, cs]
        win_ref[c, HALO:HALO + TM, :] = zc[:, cs]
        win_ref[c, HALO + TM:, :] = znext[:, cs]

    def conv_chunk(c, carry):
        w = cw_ref[c]
        acc = jnp.zeros((TM, LANE), F32)
        for kk in range(CONV_K):
            off = HALO - CONV_PAD + kk
            acc = acc + win_ref[c, off:off + TM, :] * w[kk:kk + 1, :]
        cz_ref[c] = acc
        return carry

    lax.fori_loop(0, nchunk, conv_chunk, 0)
    conv = jnp.concatenate([cz_ref[c] for c in range(nchunk)], axis=1) + cb_ref[...]
    mu = jnp.mean(conv, axis=-1, keepdims=True)
    cen = conv - mu
    var = jnp.mean(cen * cen, axis=-1, keepdims=True)
    ln = cen * lax.rsqrt(var + EPS) * lng_ref[...] + lnb_ref[...]
    act = (ln * jax.nn.sigmoid(ln)).astype(BF16)
    conv_out = jnp.dot(act, wpw_ref[...], preferred_element_type=F32)

    attn = jnp.where(is_prompt, ap_ref[...], as_ref[...])
    attn_out = jnp.dot(attn, wo_ref[...], preferred_element_type=F32)
    gt = gt_ref[...].astype(F32)
    merged = gt[:, :D_MODEL] * attn_out + gt[:, D_MODEL:] * conv_out
    mix = jnp.dot(merged.astype(BF16), wout_ref[...], preferred_element_type=F32)

    m = mods_ref[0]
    x = jnp.where(is_prompt, xp_ref[...], xs_ref[...])
    x1 = x + m[2:3] * mix
    x1_ref[...] = x1
    h2 = _rms(x1, g2_ref[...]) * (1.0 + m[4:5]) + m[3:4]
    hh = h2.astype(BF16)
    h2_ref[...] = hh

    hl = (h2 - hh.astype(F32)).astype(BF16)
    rwh = rwh_ref[...]
    logits_t = (lax.dot_general(rwh, hh, _NT, preferred_element_type=F32)
                + lax.dot_general(rwh, hl, _NT, preferred_element_type=F32)
                + lax.dot_general(rwl_ref[...], hh, _NT, preferred_element_type=F32))
    gates_ref[...] = _route(logits_t, rb_ref[...])


def _mix(attn_p, attn_s, z, gt, xp, xs, mods3, wo, wpw, wout, cw, cb, lng, lnb, g2, rwh, rwl, rb):
    tile = lambda i: (i, 0)
    ptile = lambda i: (jnp.minimum(i, NP_TILES - 1), 0)
    stile = lambda i: (jnp.maximum(i - NP_TILES, 0), 0)
    hpt = TM // HALO
    nhb = N_TOK // HALO
    return pl.pallas_call(
        _mix_kernel,
        grid=(N_TILES,),
        in_specs=[pl.BlockSpec((TM, D_MODEL), ptile),
                  pl.BlockSpec((TM, D_MODEL), stile),
                  pl.BlockSpec((TM, D_MODEL), tile),
                  pl.BlockSpec((HALO, D_MODEL), lambda i: (jnp.maximum(i * hpt - 1, 0), 0)),
                  pl.BlockSpec((HALO, D_MODEL), lambda i: (jnp.minimum((i + 1) * hpt, nhb - 1), 0)),
                  pl.BlockSpec((TM, 2 * D_MODEL), tile),
                  pl.BlockSpec((TM, D_MODEL), ptile),
                  pl.BlockSpec((TM, D_MODEL), stile),
                  pl.BlockSpec((1, N_MOD, D_MODEL), lambda i: (_seg_of_tile(i), 0, 0)),
                  _const_spec((D_MODEL, D_MODEL)),
                  _const_spec((D_MODEL, D_MODEL)),
                  _const_spec((D_MODEL, D_MODEL)),
                  _const_spec((D_MODEL // LANE, CONV_K, LANE)),
                  _const_spec((1, D_MODEL)),
                  _const_spec((1, D_MODEL)),
                  _const_spec((1, D_MODEL)),
                  _const_spec((1, D_MODEL)),
                  _const_spec((N_EXPERTS, D_MODEL)),
                  _const_spec((N_EXPERTS, D_MODEL)),
                  _const_spec((N_EXPERTS, 1))],
        out_specs=[pl.BlockSpec((TM, D_MODEL), tile),
                   pl.BlockSpec((TM, D_MODEL), tile),
                   pl.BlockSpec((N_EXPERTS, TM), lambda i: (0, i))],
        out_shape=[jax.ShapeDtypeStruct((N_TOK, D_MODEL), F32),
                   jax.ShapeDtypeStruct((N_TOK, D_MODEL), BF16),
                   jax.ShapeDtypeStruct((N_EXPERTS, N_TOK), F32)],
        scratch_shapes=[pltpu.VMEM((D_MODEL // LANE, TM + 2 * HALO, LANE), F32),
                        pltpu.VMEM((D_MODEL // LANE, TM, LANE), F32)],
        compiler_params=_cparams(("arbitrary",), 40),
        name="mix",
    )(attn_p, attn_s, z, z, z, gt, xp, xs, mods3, wo, wpw, wout, cw, cb, lng, lnb, g2, rwh, rwl, rb)


CHUNK = 16
BLK_ROWS = TM * TOP_K + N_EXPERTS * CHUNK
BLK_CHUNKS = BLK_ROWS // CHUNK
SUB = 256
N_SUB = BLK_ROWS // SUB
N_BLK = N_TOK // TM
TILE_CHUNKS = 16
TILE_ROWS = TILE_CHUNKS * CHUNK
MAX_CHUNKS = N_TOK * TOP_K // CHUNK + N_BLK * N_EXPERTS
MAX_TILES = MAX_CHUNKS // TILE_CHUNKS + N_EXPERTS
ZERO_CHUNK = N_BLK * BLK_CHUNKS
TRASH_CHUNK = ZERO_CHUNK + 1


def _plan_kernel(g_ref, dk_ref, wk_ref, cnt_ref):
    g = g_ref[...]
    sel = g > 0.0
    picked = jnp.where(sel, 1.0, 0.0)
    selb = picked.astype(BF16)
    r = lax.broadcasted_iota(jnp.int32, (TM, TM), 0)
    c = lax.broadcasted_iota(jnp.int32, (TM, TM), 1)
    before = jnp.where(r < c, 1.0, 0.0).astype(BF16)
    rank = jnp.dot(selb, before, preferred_element_type=F32)
    n = jnp.sum(picked, axis=1, keepdims=True)
    nchunk = jnp.floor((n + float(CHUNK - 1)) * (1.0 / CHUNK))
    er = lax.broadcasted_iota(jnp.int32, (N_EXPERTS, N_EXPERTS), 0)
    ec = lax.broadcasted_iota(jnp.int32, (N_EXPERTS, N_EXPERTS), 1)
    lower = jnp.where(ec < er, 1.0, 0.0).astype(BF16)
    lower_eq = jnp.where(ec <= er, 1.0, 0.0).astype(BF16)
    nchunk_b = jnp.broadcast_to(nchunk, (N_EXPERTS, TM))
    seg_chunk = jnp.dot(lower, nchunk_b.astype(BF16), preferred_element_type=F32)
    dest = seg_chunk * float(CHUNK) + rank
    order = jnp.where(sel, jnp.dot(lower_eq, selb, preferred_element_type=F32), 0.0)
    dks, wks = [], []
    for j in range(TOP_K):
        m = order == float(j + 1)
        dks.append(jnp.sum(jnp.where(m, dest + 1.0, 0.0), axis=0, keepdims=True) - 1.0)
        wks.append(jnp.sum(jnp.where(m, g, 0.0), axis=0, keepdims=True))
    dk_ref[...] = jnp.concatenate(dks, axis=0)
    wk_ref[...] = jnp.concatenate(wks, axis=0)
    cnt_ref[0] = jnp.broadcast_to(nchunk, (N_EXPERTS, LANE))


def _plan(gates_t):
    return pl.pallas_call(
        _plan_kernel,
        grid=(N_BLK,),
        in_specs=[pl.BlockSpec((N_EXPERTS, TM), lambda b: (0, b))],
        out_specs=[pl.BlockSpec((TOP_K, TM), lambda b: (0, b)),
                   pl.BlockSpec((TOP_K, TM), lambda b: (0, b)),
                   pl.BlockSpec((1, N_EXPERTS, LANE), lambda b: (b, 0, 0))],
        out_shape=[jax.ShapeDtypeStruct((TOP_K, N_TOK), F32),
                   jax.ShapeDtypeStruct((TOP_K, N_TOK), F32),
                   jax.ShapeDtypeStruct((N_BLK, N_EXPERTS, LANE), F32)],
        compiler_params=_cparams(("arbitrary",), 16),
        name="plan",
    )(gates_t)


def _tile_tables(cnt):
    c = cnt[:, :, 0].astype(jnp.int32)
    seg_chunk = jnp.cumsum(c, axis=1) - c
    ct = c.T
    tiles_e = (ct.sum(axis=1) + TILE_CHUNKS - 1) // TILE_CHUNKS
    tile_end = jnp.cumsum(tiles_e)
    tile_start = tile_end - tiles_e
    seg_slot = (tile_start[:, None] * TILE_CHUNKS + jnp.cumsum(ct, axis=1) - ct).reshape(-1)
    seg_len = ct.reshape(-1)
    seg_src = (jnp.arange(N_BLK, dtype=jnp.int32)[None, :] * BLK_CHUNKS + seg_chunk.T).reshape(-1)
    s = jnp.arange(MAX_TILES * TILE_CHUNKS, dtype=jnp.int32)
    seg = jnp.clip(jnp.searchsorted(seg_slot, s, side="right") - 1, 0, seg_slot.shape[0] - 1)
    within = s - seg_slot[seg]
    valid = within < seg_len[seg]
    src = seg_src[seg] + within
    trash = TRASH_CHUNK + ((s // TILE_CHUNKS) % 2) * TILE_CHUNKS + s % TILE_CHUNKS
    in_src = jnp.where(valid, src, ZERO_CHUNK).astype(jnp.int32)
    out_dst = jnp.where(valid, src, trash).astype(jnp.int32)
    tile_start = jnp.concatenate([tile_start, tile_end[-1:]]).astype(jnp.int32)
    return tile_start, tiles_e.astype(jnp.int32), in_src, out_dst


def _dispatch_kernel(h_ref, dk_ref, s_ref):
    b = pl.program_id(0)

    @pl.when(b == N_BLK)
    def _():
        s_ref[...] = jnp.zeros(s_ref.shape, BF16)

    @pl.when(b < N_BLK)
    def _():
        h = h_ref[...]
        dk = dk_ref[...]
        sub_of = jnp.floor(dk * (1.0 / SUB))
        row_in = dk - sub_of * float(SUB)
        rows = lax.broadcasted_iota(jnp.int32, (SUB, TM), 0).astype(F32).astype(BF16)
        one = jnp.ones((SUB, TM), BF16)
        for sb in range(N_SUB):
            local = jnp.where(sub_of == float(sb), row_in, -1.0).astype(BF16)
            p = jnp.zeros((SUB, TM), BF16)
            for j in range(TOP_K):
                p = jnp.where(rows == local[j:j + 1, :], one, p)
            s_ref[sb * SUB:(sb + 1) * SUB, :] = jnp.dot(
                p, h, preferred_element_type=F32).astype(BF16)


def _dispatch(h2, dk):
    last = N_BLK - 1
    return pl.pallas_call(
        _dispatch_kernel,
        grid=(N_BLK + 1,),
        in_specs=[pl.BlockSpec((TM, D_MODEL), lambda b: (jnp.minimum(b, last), 0)),
                  pl.BlockSpec((TOP_K, TM), lambda b: (0, jnp.minimum(b, last)))],
        out_specs=pl.BlockSpec((BLK_ROWS, D_MODEL), lambda b: (b, 0)),
        out_shape=jax.ShapeDtypeStruct(((N_BLK + 1) * BLK_ROWS, D_MODEL), BF16),
        compiler_params=_cparams(("arbitrary",), 40),
        name="dispatch",
    )(h2, dk)


def _experts_kernel(tstart_ref, tcount_ref, insrc_ref, outdst_ref,
                    s_hbm, wg_ref, wu_ref, wd_ref, o_hbm,
                    xbuf, obuf, wgb, wub, wdb, insem, outsem):
    e = pl.program_id(0)
    n_tiles = tstart_ref[N_EXPERTS]
    t0 = tstart_ref[e]

    def in_copy(t, par, i):
        src = insrc_ref[t * TILE_CHUNKS + i] * CHUNK
        return pltpu.make_async_copy(s_hbm.at[pl.ds(pl.multiple_of(src, CHUNK), CHUNK), :],
                                     xbuf.at[par, pl.ds(i * CHUNK, CHUNK), :], insem.at[par])

    def out_copy(t, par, i):
        dst = outdst_ref[t * TILE_CHUNKS + i] * CHUNK
        return pltpu.make_async_copy(obuf.at[par, pl.ds(i * CHUNK, CHUNK), :],
                                     o_hbm.at[pl.ds(pl.multiple_of(dst, CHUNK), CHUNK), :],
                                     outsem.at[par])

    def start_in(t, par):
        for i in range(TILE_CHUNKS):
            in_copy(t, par, i).start()

    def wait_in(t, par):
        for i in range(TILE_CHUNKS):
            in_copy(t, par, i).wait()

    def start_out(t, par):
        for i in range(TILE_CHUNKS):
            out_copy(t, par, i).start()

    def wait_out(t, par):
        for i in range(TILE_CHUNKS):
            out_copy(t, par, i).wait()

    wgb[...] = wg_ref[0].astype(BF16)
    wub[...] = wu_ref[0].astype(BF16)
    wdb[...] = wd_ref[0].astype(BF16)

    @pl.when(jnp.logical_and(e == 0, n_tiles > 0))
    def _():
        start_in(0, 0)

    def tile_body(j, carry):
        t = t0 + j
        par = t % 2

        @pl.when(t + 1 < n_tiles)
        def _():
            start_in(t + 1, 1 - par)

        @pl.when(t >= 2)
        def _():
            wait_out(t - 2, par)

        wait_in(t, par)
        x = xbuf[par]
        a = jnp.dot(x, wgb[...], preferred_element_type=F32)
        u = jnp.dot(x, wub[...], preferred_element_type=F32)
        hid = (a * jax.nn.sigmoid(a) * u).astype(BF16)
        obuf[par] = jnp.dot(hid, wdb[...], preferred_element_type=F32).astype(BF16)
        start_out(t, par)
        return carry

    lax.fori_loop(0, tcount_ref[e], tile_body, 0)

    @pl.when(e == N_EXPERTS - 1)
    def _():
        @pl.when(n_tiles >= 2)
        def _():
            wait_out(n_tiles - 2, n_tiles % 2)

        @pl.when(n_tiles >= 1)
        def _():
            wait_out(n_tiles - 1, (n_tiles - 1) % 2)


def _experts(tile_start, tile_count, in_src, out_dst, sorted_rows, wg, wu, wd):
    wspec = lambda shape: pl.BlockSpec(shape, lambda e, *_: (e, 0, 0))
    grid_spec = pltpu.PrefetchScalarGridSpec(
        num_scalar_prefetch=4,
        grid=(N_EXPERTS,),
        in_specs=[pl.BlockSpec(memory_space=pl.ANY),
                  wspec((1, D_MODEL, EXPERT_FF)),
                  wspec((1, D_MODEL, EXPERT_FF)),
                  wspec((1, EXPERT_FF, D_MODEL))],
        out_specs=pl.BlockSpec(memory_space=pl.ANY),
        scratch_shapes=[pltpu.VMEM((2, TILE_ROWS, D_MODEL), BF16),
                        pltpu.VMEM((2, TILE_ROWS, D_MODEL), BF16),
                        pltpu.VMEM((D_MODEL, EXPERT_FF), BF16),
                        pltpu.VMEM((D_MODEL, EXPERT_FF), BF16),
                        pltpu.VMEM((EXPERT_FF, D_MODEL), BF16),
                        pltpu.SemaphoreType.DMA((2,)),
                        pltpu.SemaphoreType.DMA((2,))])
    return pl.pallas_call(
        _experts_kernel,
        grid_spec=grid_spec,
        out_shape=jax.ShapeDtypeStruct(sorted_rows.shape, BF16),
        input_output_aliases={4: 0},
        compiler_params=_cparams(("arbitrary",), 32),
        name="experts",
    )(tile_start, tile_count, in_src, out_dst, sorted_rows, wg, wu, wd)


def _combine_kernel(o_ref, dkt_ref, wkt_ref, h_ref, x1_ref, mods_ref, sg_ref, su_ref, sd_ref, fg_ref,
                    yp_ref, ys_ref, acc_ref):
    i = pl.program_id(0)
    dkt = dkt_ref[...]
    wkt = wkt_ref[...]
    sub_of = jnp.floor(dkt * (1.0 / SUB))
    row_in = dkt - sub_of * float(SUB)
    lanes = lax.broadcasted_iota(jnp.int32, (TM, SUB), 1).astype(F32).astype(BF16)
    wcols = [jnp.broadcast_to(wkt[:, j:j + 1], (TM, SUB)).astype(BF16) for j in range(TOP_K)]

    h = h_ref[...]
    a = jnp.dot(h, sg_ref[...], preferred_element_type=F32)
    u = jnp.dot(h, su_ref[...], preferred_element_type=F32)
    acc_ref[...] = jnp.dot((a * jax.nn.sigmoid(a) * u).astype(BF16), sd_ref[...],
                           preferred_element_type=F32)
    for sb in range(N_SUB):
        local = jnp.where(sub_of == float(sb), row_in, -1.0)
        pw = jnp.zeros((TM, SUB), BF16)
        for j in range(TOP_K):
            col = jnp.broadcast_to(local[:, j:j + 1], (TM, SUB)).astype(BF16)
            pw = jnp.where(lanes == col, wcols[j], pw)
        acc_ref[...] += jnp.dot(pw, o_ref[sb * SUB:(sb + 1) * SUB, :], preferred_element_type=F32)

    m = mods_ref[0]
    y = _rms(x1_ref[...] + m[5:6] * acc_ref[...], fg_ref[...])

    @pl.when(i < NP_TILES)
    def _():
        yp_ref[...] = y

    @pl.when(i >= NP_TILES)
    def _():
        ys_ref[...] = y


def _combine(expert_rows, dkt, wkt, h2, x1, mods3, sg, su, sd, fg):
    tile = lambda i: (i, 0)
    return pl.pallas_call(
        _combine_kernel,
        grid=(N_BLK,),
        in_specs=[pl.BlockSpec((BLK_ROWS, D_MODEL), tile),
                  pl.BlockSpec((TM, TOP_K), tile),
                  pl.BlockSpec((TM, TOP_K), tile),
                  pl.BlockSpec((TM, D_MODEL), tile),
                  pl.BlockSpec((TM, D_MODEL), tile),
                  pl.BlockSpec((1, N_MOD, D_MODEL), lambda i: (_seg_of_tile(i), 0, 0)),
                  _const_spec((D_MODEL, EXPERT_FF)),
                  _const_spec((D_MODEL, EXPERT_FF)),
                  _const_spec((EXPERT_FF, D_MODEL)),
                  _const_spec((1, D_MODEL))],
        out_specs=[pl.BlockSpec((TM, D_MODEL), lambda i: (jnp.minimum(i, NP_TILES - 1), 0)),
                   pl.BlockSpec((TM, D_MODEL), lambda i: (jnp.maximum(i - NP_TILES, 0), 0))],
        out_shape=[jax.ShapeDtypeStruct((N_PROMPT, D_MODEL), F32),
                   jax.ShapeDtypeStruct((N_SAMPLE, D_MODEL), F32)],
        scratch_shapes=[pltpu.VMEM((TM, D_MODEL), F32)],
        compiler_params=_cparams(("arbitrary",), 40),
        name="combine",
    )(expert_rows, dkt, wkt, h2, x1, mods3, sg, su, sd, fg)


_ROPE_SWAP = tuple(list(range(8, 16)) + list(range(0, 8)) + list(range(24, 32)) + list(range(16, 24)))


def _rope_tables():
    pos = jnp.arange(DEC_SEQ, dtype=jnp.int32)
    row = (pos // GRID_W).astype(F32)
    col = (pos % GRID_W).astype(F32)
    n_freq = QK_ROPE_DIM // 4
    inv_freq = ROPE_BASE ** (-jnp.arange(n_freq, dtype=F32) / n_freq)
    ar, ac = row[:, None] * inv_freq, col[:, None] * inv_freq
    cos32 = jnp.concatenate([jnp.cos(ar), jnp.cos(ar), jnp.cos(ac), jnp.cos(ac)], axis=1)
    sin32 = jnp.concatenate([-jnp.sin(ar), jnp.sin(ar), -jnp.sin(ac), jnp.sin(ac)], axis=1)
    cos32 = jnp.concatenate([jnp.ones((TM, QK_ROPE_DIM), F32), cos32], axis=0)
    sin32 = jnp.concatenate([jnp.zeros((TM, QK_ROPE_DIM), F32), sin32], axis=0)
    n = cos32.shape[0]
    ones = jnp.ones((n, QK_NOPE_DIM), F32)
    zeros = jnp.zeros((n, QK_NOPE_DIM), F32)
    tail = jnp.zeros((n, HEAD_PAD - QK_NOPE_DIM - QK_ROPE_DIM), F32)
    qc = jnp.concatenate([ones, cos32, tail], axis=1) * ATTN_SCALE
    qs = jnp.concatenate([zeros, sin32, tail], axis=1) * ATTN_SCALE
    kc = jnp.concatenate([zeros, cos32, tail], axis=1)
    ks = jnp.concatenate([zeros, sin32, tail], axis=1)
    return jnp.stack([qc, qs, kc, ks], axis=0)


def _pack_weights(w_in, w_uq, w_ukv):
    swap = jnp.array(_ROPE_SWAP, dtype=jnp.int32)
    s0, s1, s2, s3 = Q_LORA, Q_LORA + KV_LORA, Q_LORA + KV_LORA + QK_ROPE_DIM, \
        Q_LORA + KV_LORA + QK_ROPE_DIM + 2 * D_MODEL
    wkr = w_in[:, s1:s2]
    pad_l = jnp.zeros((D_MODEL, QK_NOPE_DIM), F32)
    pad_r = jnp.zeros((D_MODEL, HEAD_PAD - QK_NOPE_DIM - QK_ROPE_DIM), F32)
    win = jnp.concatenate([w_in[:, :s0], w_in[:, s0:s1],
                           pad_l, wkr, pad_r, pad_l, wkr[:, swap], pad_r,
                           w_in[:, s2:s3], w_in[:, s3:]], axis=1).astype(BF16)
    uq = w_uq.reshape(Q_LORA, N_HEADS, QK_NOPE_DIM + QK_ROPE_DIM)
    nope, rope = uq[:, :, :QK_NOPE_DIM], uq[:, :, QK_NOPE_DIM:]
    zt = jnp.zeros((Q_LORA, N_HEADS, HEAD_PAD - QK_NOPE_DIM - QK_ROPE_DIM), F32)
    wuqp = jnp.concatenate([nope, rope, zt], axis=2).reshape(Q_LORA, QK_COLS).astype(BF16)
    wuqs = jnp.concatenate([jnp.zeros_like(nope), rope[:, :, swap], zt], axis=2)
    wuqs = wuqs.reshape(Q_LORA, QK_COLS).astype(BF16)
    ukv = w_ukv.reshape(KV_LORA, N_HEADS, QK_NOPE_DIM + V_DIM)
    zk = jnp.zeros((KV_LORA, N_HEADS, HEAD_PAD - QK_NOPE_DIM), F32)
    wukp = jnp.concatenate([ukv[:, :, :QK_NOPE_DIM], zk], axis=2).reshape(KV_LORA, QK_COLS).astype(BF16)
    wuv = ukv[:, :, QK_NOPE_DIM:].reshape(KV_LORA, N_HEADS * V_DIM).astype(BF16)
    return win, wuqp, wuqs, wukp, wuv


def kernel(x_prompt, x_sample, cache_ckv, cache_krope, c, c_ctx, mod_w, mod_b, norm1_g, w_in, q_norm_g, w_uq, kv_norm_g, w_ukv, w_o_attn, conv_w, conv_b, conv_ln_g, conv_ln_b, w_pw2, w_out, norm2_g, router_w, router_bias, exp_w_gate, exp_w_up, exp_w_down, sh_w_gate, sh_w_up, sh_w_down, final_g):
    xp = x_prompt.reshape(N_PROMPT, D_MODEL)
    xs = x_sample.reshape(N_SAMPLE, D_MODEL)
    cond8 = jnp.concatenate([c_ctx[None, :], c, jnp.zeros((8 - 1 - DEC_BATCH, D_MODEL), F32)], axis=0)
    mods3 = _mods(cond8, mod_w[0], mod_b).reshape(8, N_MOD, D_MODEL)

    win, wuqp, wuqs, wukp, wuv = _pack_weights(w_in[0], w_uq[0], w_ukv[0])
    q, k, v, z, gt, ckv_new, kr_new = _inproj(
        xp, xs, mods3, norm1_g, q_norm_g, kv_norm_g, win, wuqp, wuqs, wukp, wuv, _rope_tables())

    krope_pad = jnp.pad(cache_krope[:, 0], ((0, 0), (0, 0),
                                            (QK_NOPE_DIM, HEAD_PAD - QK_NOPE_DIM - QK_ROPE_DIM)))
    kc, vc = _ctxkv(cache_ckv[:, 0], krope_pad, wukp, wuv)
    attn_p = _attn_prompt(q, k, v)
    attn_s = _attn_sample(q, k, v, kc, vc)

    rwt = router_w[0].T
    rwh = rwt.astype(BF16)
    rwl = (rwt - rwh.astype(F32)).astype(BF16)
    cw = conv_w[0].reshape(CONV_K, D_MODEL // LANE, LANE).transpose(1, 0, 2)
    x1, h2, gates_t = _mix(
        attn_p, attn_s, z, gt, xp, xs, mods3,
        w_o_attn[0].astype(BF16), w_pw2[0].astype(BF16), w_out[0].astype(BF16),
        cw, conv_b, conv_ln_g, conv_ln_b, norm2_g, rwh, rwl, router_bias.reshape(N_EXPERTS, 1))

    dk, wk, cnt = _plan(gates_t)
    tile_start, tile_count, in_src, out_dst = _tile_tables(cnt)
    expert_rows = _experts(tile_start, tile_count, in_src, out_dst, _dispatch(h2, dk),
                           exp_w_gate[0], exp_w_up[0], exp_w_down[0])
    y_p, y_s = _combine(expert_rows, dk.T, wk.T, h2, x1, mods3,
                        sh_w_gate[0].astype(BF16), sh_w_up[0].astype(BF16),
                        sh_w_down[0].astype(BF16), final_g[None, :])
    return (y_p.reshape(BATCH, SEQ, D_MODEL),
            y_s.reshape(DEC_BATCH, DEC_SEQ, D_MODEL),
            ckv_new.reshape(BATCH, 1, SEQ, KV_LORA),
            kr_new.reshape(BATCH, 1, SEQ, QK_ROPE_DIM))
```

```python
import functools
import math

import jax
import jax.numpy as jnp
from jax import lax
from jax.experimental import pallas as pl
from jax.experimental.pallas import tpu as pltpu

F32 = jnp.float32
BF16 = jnp.bfloat16

D_MODEL = 1024
BATCH = 16
SEQ = 256
DEC_BATCH = 2
DEC_SEQ = 2048
PAST_LEN = 256
GRID_W = 64
N_HEADS = 16
QK_NOPE_DIM = 64
QK_ROPE_DIM = 32
V_DIM = 64
Q_LORA = 384
KV_LORA = 256
ROPE_BASE = 10000.0
ATTN_SCALE = 1.0 / math.sqrt(QK_NOPE_DIM + QK_ROPE_DIM)
CONV_K = 31
CONV_PAD = CONV_K // 2
N_EXPERTS = 64
TOP_K = 8
N_GROUPS = 8
GROUP_SIZE = N_EXPERTS // N_GROUPS
TOPK_GROUPS = 4
EXPERT_FF = 256
ROUTED_SCALE = 2.5
EPS = 1e-6
N_MOD = 6

LANE = 128
HEAD_PAD = LANE
QK_COLS = N_HEADS * HEAD_PAD
N_PROMPT = BATCH * SEQ
N_SAMPLE = DEC_BATCH * DEC_SEQ
N_TOK = N_PROMPT + N_SAMPLE

TM = 256
NP_TILES = N_PROMPT // TM
NS_TILES = N_SAMPLE // TM
N_TILES = NP_TILES + NS_TILES
TILES_PER_DEC = DEC_SEQ // TM
HALO = 16

C_Q = 0
C_KV = C_Q + Q_LORA
C_KR = C_KV + KV_LORA
C_CONV = C_KR + 2 * LANE
C_GATE = C_CONV + 2 * D_MODEL
IN_COLS_PACKED = C_GATE + 2 * D_MODEL

VMEM_CAP = 56 * 1024 * 1024


def _cparams(sem, vmem_mb):
    return pltpu.CompilerParams(dimension_semantics=sem,
                                vmem_limit_bytes=min(vmem_mb * 1024 * 1024, VMEM_CAP))


def _const_spec(shape):
    n = len(shape)
    return pl.BlockSpec(shape, lambda *_: (0,) * n, pipeline_mode=pl.Buffered(1))


def _seg_of_tile(i):
    return jnp.where(i < NP_TILES, 0, 1 + (i - NP_TILES) // TILES_PER_DEC)


def _rms(x, g):
    return x * lax.rsqrt(jnp.mean(x * x, axis=-1, keepdims=True) + EPS) * g


def _mods_kernel(cond_ref, w_ref, b_ref, o_ref):
    c = cond_ref[...]
    s = c * jax.nn.sigmoid(c)
    o_ref[...] = jnp.dot(s.astype(BF16), w_ref[...].astype(BF16),
                         preferred_element_type=F32) + b_ref[...]


def _mods(cond8, mod_w, mod_b):
    n = N_MOD * D_MODEL
    bn = 512
    return pl.pallas_call(
        _mods_kernel,
        grid=(n // bn,),
        in_specs=[pl.BlockSpec((8, D_MODEL), lambda j: (0, 0)),
                  pl.BlockSpec((D_MODEL, bn), lambda j: (0, j)),
                  pl.BlockSpec((1, bn), lambda j: (0, j))],
        out_specs=pl.BlockSpec((8, bn), lambda j: (0, j)),
        out_shape=jax.ShapeDtypeStruct((8, n), F32),
        compiler_params=_cparams(("arbitrary",), 16),
        name="mods",
    )(cond8, mod_w, mod_b)


def _inproj_kernel(xp_ref, xs_ref, mods_ref, g1_ref, qg_ref, kvg_ref, win_ref, wuqp_ref, wuqs_ref,
                   wukp_ref, wuv_ref, tab_ref,
                   q_ref, k_ref, v_ref, z_ref, gt_ref, ckv_ref, kr_ref):
    i = pl.program_id(0)
    x = jnp.where(i < NP_TILES, xp_ref[...], xs_ref[...])
    m = mods_ref[0]
    h = _rms(x, g1_ref[...]) * (1.0 + m[1:2]) + m[0:1]
    hb = h.astype(BF16)

    def proj(c0, c1):
        return jnp.dot(hb, win_ref[:, c0:c1], preferred_element_type=F32)

    def lanes16(t):
        return jnp.concatenate([t] * N_HEADS, axis=1)

    qn = _rms(proj(C_Q, C_KV), qg_ref[...]).astype(BF16)
    q = (jnp.dot(qn, wuqp_ref[...], preferred_element_type=F32) * lanes16(tab_ref[0])
         + jnp.dot(qn, wuqs_ref[...], preferred_element_type=F32) * lanes16(tab_ref[1]))
    q_ref[...] = q.astype(BF16)

    ckv = _rms(proj(C_KV, C_KR), kvg_ref[...])
    ckvb = ckv.astype(BF16)
    ukr = proj(C_KR, C_CONV)
    ua = ukr[:, :LANE]
    krot = ua * tab_ref[2] + ukr[:, LANE:] * tab_ref[3]
    k = jnp.dot(ckvb, wukp_ref[...], preferred_element_type=F32) + lanes16(krot)
    k_ref[...] = k.astype(BF16)
    v_ref[...] = jnp.dot(ckvb, wuv_ref[...], preferred_element_type=F32).astype(BF16)

    @pl.when(i < NP_TILES)
    def _():
        ckv_ref[...] = ckv
        kr_ref[...] = ua[:, QK_NOPE_DIM:QK_NOPE_DIM + QK_ROPE_DIM]

    a = proj(C_CONV, C_CONV + D_MODEL)
    g = proj(C_CONV + D_MODEL, C_GATE)
    z_ref[...] = (a * jax.nn.sigmoid(g)).astype(BF16)
    gt_ref[...] = jax.nn.sigmoid(proj(C_GATE, IN_COLS_PACKED)).astype(BF16)


def _inproj(xp, xs, mods3, g1, qg, kvg, win, wuqp, wuqs, wukp, wuv, tab):
    tile = lambda i: (i, 0)
    ptile = lambda i: (jnp.minimum(i, NP_TILES - 1), 0)
    stile = lambda i: (jnp.maximum(i - NP_TILES, 0), 0)
    tab_blk = lambda i: (0, jnp.where(i < NP_TILES, 0, 1 + (i - NP_TILES) % TILES_PER_DEC), 0)
    return pl.pallas_call(
        _inproj_kernel,
        grid=(N_TILES,),
        in_specs=[pl.BlockSpec((TM, D_MODEL), ptile),
                  pl.BlockSpec((TM, D_MODEL), stile),
                  pl.BlockSpec((1, N_MOD, D_MODEL), lambda i: (_seg_of_tile(i), 0, 0)),
                  _const_spec((1, D_MODEL)),
                  _const_spec((1, Q_LORA)),
                  _const_spec((1, KV_LORA)),
                  _const_spec((D_MODEL, IN_COLS_PACKED)),
                  _const_spec((Q_LORA, QK_COLS)),
                  _const_spec((Q_LORA, QK_COLS)),
                  _const_spec((KV_LORA, QK_COLS)),
                  _const_spec((KV_LORA, N_HEADS * V_DIM)),
                  pl.BlockSpec((4, TM, LANE), tab_blk)],
        out_specs=[pl.BlockSpec((TM, QK_COLS), tile),
                   pl.BlockSpec((TM, QK_COLS), tile),
                   pl.BlockSpec((TM, N_HEADS * V_DIM), tile),
                   pl.BlockSpec((TM, D_MODEL), tile),
                   pl.BlockSpec((TM, 2 * D_MODEL), tile),
                   pl.BlockSpec((TM, KV_LORA), ptile),
                   pl.BlockSpec((TM, QK_ROPE_DIM), ptile)],
        out_shape=[jax.ShapeDtypeStruct((N_TOK, QK_COLS), BF16),
                   jax.ShapeDtypeStruct((N_TOK, QK_COLS), BF16),
                   jax.ShapeDtypeStruct((N_TOK, N_HEADS * V_DIM), BF16),
                   jax.ShapeDtypeStruct((N_TOK, D_MODEL), BF16),
                   jax.ShapeDtypeStruct((N_TOK, 2 * D_MODEL), BF16),
                   jax.ShapeDtypeStruct((N_PROMPT, KV_LORA), F32),
                   jax.ShapeDtypeStruct((N_PROMPT, QK_ROPE_DIM), F32)],
        compiler_params=_cparams(("arbitrary",), 48),
        name="inproj",
    )(xp, xs, mods3, g1, qg, kvg, win, wuqp, wuqs, wukp, wuv, tab)


def _ctxkv_kernel(ckv_ref, kr_ref, wukp_ref, wuv_ref, k_ref, v_ref):
    c = ckv_ref[0].astype(BF16)
    kr = jnp.concatenate([kr_ref[0]] * N_HEADS, axis=1)
    k_ref[0] = (jnp.dot(c, wukp_ref[...], preferred_element_type=F32) + kr).astype(BF16)
    v_ref[0] = jnp.dot(c, wuv_ref[...], preferred_element_type=F32).astype(BF16)


def _ctxkv(cache_ckv, krope_pad, wukp, wuv):
    return pl.pallas_call(
        _ctxkv_kernel,
        grid=(DEC_BATCH,),
        in_specs=[pl.BlockSpec((1, PAST_LEN, KV_LORA), lambda b: (b, 0, 0)),
                  pl.BlockSpec((1, PAST_LEN, LANE), lambda b: (b, 0, 0)),
                  _const_spec((KV_LORA, QK_COLS)),
                  _const_spec((KV_LORA, N_HEADS * V_DIM))],
        out_specs=[pl.BlockSpec((1, PAST_LEN, QK_COLS), lambda b: (b, 0, 0)),
                   pl.BlockSpec((1, PAST_LEN, N_HEADS * V_DIM), lambda b: (b, 0, 0))],
        out_shape=[jax.ShapeDtypeStruct((DEC_BATCH, PAST_LEN, QK_COLS), BF16),
                   jax.ShapeDtypeStruct((DEC_BATCH, PAST_LEN, N_HEADS * V_DIM), BF16)],
        compiler_params=_cparams(("arbitrary",), 16),
        name="ctxkv",
    )(cache_ckv, krope_pad, wukp, wuv)


_NT = (((1,), (1,)), ((), ()))


def _attn_kernel(*refs, has_ctx):
    if has_ctx:
        q_ref, k_ref, v_ref, kc_ref, vc_ref, o_ref = refs
    else:
        q_ref, k_ref, v_ref, o_ref = refs
    tq = q_ref.shape[0]
    lane = lax.broadcasted_iota(jnp.int32, (tq, LANE), 1)
    for j in range(N_HEADS // 2):
        pair = slice(j * LANE, (j + 1) * LANE)
        outs = []
        for h in (2 * j, 2 * j + 1):
            hs = slice(h * HEAD_PAD, (h + 1) * HEAD_PAD)
            qh = q_ref[:, hs]
            s = lax.dot_general(qh, k_ref[:, hs], _NT, preferred_element_type=F32)
            mx = jnp.max(s, axis=-1, keepdims=True)
            if has_ctx:
                sc = lax.dot_general(qh, kc_ref[0, :, hs], _NT, preferred_element_type=F32)
                mx = jnp.maximum(mx, jnp.max(sc, axis=-1, keepdims=True))
            p = jnp.exp(s - mx)
            den = jnp.sum(p, axis=-1, keepdims=True)
            o = jnp.dot(p.astype(BF16), v_ref[:, pair], preferred_element_type=F32)
            if has_ctx:
                pc = jnp.exp(sc - mx)
                den = den + jnp.sum(pc, axis=-1, keepdims=True)
                o = o + jnp.dot(pc.astype(BF16), vc_ref[0, :, pair], preferred_element_type=F32)
            outs.append(o / den)
        o_ref[:, pair] = jnp.where(lane < V_DIM, outs[0], outs[1]).astype(BF16)


def _attn_prompt(q, k, v):
    blk = lambda b: (b, 0)
    return pl.pallas_call(
        functools.partial(_attn_kernel, has_ctx=False),
        grid=(BATCH,),
        in_specs=[pl.BlockSpec((SEQ, QK_COLS), blk),
                  pl.BlockSpec((SEQ, QK_COLS), blk),
                  pl.BlockSpec((SEQ, N_HEADS * V_DIM), blk)],
        out_specs=pl.BlockSpec((SEQ, N_HEADS * V_DIM), blk),
        out_shape=jax.ShapeDtypeStruct((N_PROMPT, N_HEADS * V_DIM), BF16),
        compiler_params=_cparams(("arbitrary",), 24),
        name="attn_prompt",
    )(q, k, v)


ATT_TQ = 256


def _attn_sample(q, k, v, kc, vc):
    nq = DEC_SEQ // ATT_TQ
    q0 = N_PROMPT // ATT_TQ
    s0 = N_PROMPT // DEC_SEQ
    once = pl.Buffered(1)
    return pl.pallas_call(
        functools.partial(_attn_kernel, has_ctx=True),
        grid=(DEC_BATCH, nq),
        in_specs=[pl.BlockSpec((ATT_TQ, QK_COLS), lambda b, t: (q0 + b * nq + t, 0)),
                  pl.BlockSpec((DEC_SEQ, QK_COLS), lambda b, t: (s0 + b, 0), pipeline_mode=once),
                  pl.BlockSpec((DEC_SEQ, N_HEADS * V_DIM), lambda b, t: (s0 + b, 0), pipeline_mode=once),
                  pl.BlockSpec((1, PAST_LEN, QK_COLS), lambda b, t: (b, 0, 0), pipeline_mode=once),
                  pl.BlockSpec((1, PAST_LEN, N_HEADS * V_DIM), lambda b, t: (b, 0, 0),
                               pipeline_mode=once)],
        out_specs=pl.BlockSpec((ATT_TQ, N_HEADS * V_DIM), lambda b, t: (b * nq + t, 0)),
        out_shape=jax.ShapeDtypeStruct((N_SAMPLE, N_HEADS * V_DIM), BF16),
        compiler_params=_cparams(("arbitrary", "arbitrary"), 48),
        name="attn_sample",
    )(q, k, v, kc, vc)


def _topk_rows(v, k, n):
    rows = lax.broadcasted_iota(jnp.int32, v.shape, 0).astype(F32)
    sel = jnp.zeros(v.shape, F32)
    for _ in range(k):
        mx = jnp.max(v, axis=0, keepdims=True)
        first = jnp.min(jnp.where(v == mx, rows, float(n)), axis=0, keepdims=True)
        hit = rows == first
        sel = jnp.where(hit, 1.0, sel)
        v = jnp.where(hit, -jnp.inf, v)
    return sel > 0.0


def _route(logits_t, bias):
    scores = jax.nn.sigmoid(logits_t)
    sel = scores + bias
    gscore = []
    for g in range(N_GROUPS):
        blk = sel[g * GROUP_SIZE:(g + 1) * GROUP_SIZE]
        top2 = _topk_rows(blk, 2, GROUP_SIZE)
        gscore.append(jnp.sum(jnp.where(top2, blk, 0.0), axis=0, keepdims=True))
    masked = []
    for g in range(N_GROUPS):
        beaten = jnp.zeros(gscore[g].shape, F32)
        for o in range(N_GROUPS):
            if o < g:
                beaten = beaten + jnp.where(gscore[o] >= gscore[g], 1.0, 0.0)
            elif o > g:
                beaten = beaten + jnp.where(gscore[o] > gscore[g], 1.0, 0.0)
        keep = beaten < float(TOPK_GROUPS)
        masked.append(jnp.where(keep, sel[g * GROUP_SIZE:(g + 1) * GROUP_SIZE], -jnp.inf))
    chosen = _topk_rows(jnp.concatenate(masked, axis=0), TOP_K, N_EXPERTS)
    w = jnp.where(chosen, scores, 0.0)
    return w / jnp.sum(w, axis=0, keepdims=True) * ROUTED_SCALE


def _mix_kernel(ap_ref, as_ref, z_ref, zp_ref, zn_ref, gt_ref, xp_ref, xs_ref, mods_ref,
                wo_ref, wpw_ref, wout_ref, cw_ref, cb_ref, lng_ref, lnb_ref, g2_ref,
                rwh_ref, rwl_ref, rb_ref,
                x1_ref, h2_ref, gates_ref, win_ref, cz_ref):
    i = pl.program_id(0)
    is_prompt = i < NP_TILES
    pos = (i - NP_TILES) % TILES_PER_DEC
    first = jnp.logical_or(is_prompt, pos == 0)
    last = jnp.logical_or(is_prompt, pos == TILES_PER_DEC - 1)

    nchunk = D_MODEL // LANE
    zprev = jnp.where(first, 0.0, zp_ref[...].astype(F32))
    znext = jnp.where(last, 0.0, zn_ref[...].astype(F32))
    zc = z_ref[...].astype(F32)
    for c in range(nchunk):
        cs = slice(c * LANE, (c + 1) * LANE)
        win_ref[c, 0:HALO, :] = zprev[:, cs]
        win_ref[c, HALO:HALO + TM, :] = zc[:, cs]
        win_ref[c, HALO + TM:, :] = znext[:, cs]

    def conv_chunk(c, carry):
        w = cw_ref[c]
        acc = jnp.zeros((TM, LANE), F32)
        for kk in range(CONV_K):
            off = HALO - CONV_PAD + kk
            acc = acc + win_ref[c, off:off + TM, :] * w[kk:kk + 1, :]
        cz_ref[c] = acc
        return carry

    lax.fori_loop(0, nchunk, conv_chunk, 0)
    conv = jnp.concatenate([cz_ref[c] for c in range(nchunk)], axis=1) + cb_ref[...]
    mu = jnp.mean(conv, axis=-1, keepdims=True)
    cen = conv - mu
    var = jnp.mean(cen * cen, axis=-1, keepdims=True)
    ln = cen * lax.rsqrt(var + EPS) * lng_ref[...] + lnb_ref[...]
    act = (ln * jax.nn.sigmoid(ln)).astype(BF16)
    conv_out = jnp.dot(act, wpw_ref[...], preferred_element_type=F32)

    attn = jnp.where(is_prompt, ap_ref[...], as_ref[...])
    attn_out = jnp.dot(attn, wo_ref[...], preferred_element_type=F32)
    gt = gt_ref[...].astype(F32)
    merged = gt[:, :D_MODEL] * attn_out + gt[:, D_MODEL:] * conv_out
    mix = jnp.dot(merged.astype(BF16), wout_ref[...], preferred_element_type=F32)

    m = mods_ref[0]
    x = jnp.where(is_prompt, xp_ref[...], xs_ref[...])
    x1 = x + m[2:3] * mix
    x1_ref[...] = x1
    h2 = _rms(x1, g2_ref[...]) * (1.0 + m[4:5]) + m[3:4]
    hh = h2.astype(BF16)
    h2_ref[...] = hh

    hl = (h2 - hh.astype(F32)).astype(BF16)
    rwh = rwh_ref[...]
    logits_t = (lax.dot_general(rwh, hh, _NT, preferred_element_type=F32)
                + lax.dot_general(rwh, hl, _NT, preferred_element_type=F32)
                + lax.dot_general(rwl_ref[...], hh, _NT, preferred_element_type=F32))
    gates_ref[...] = _route(logits_t, rb_ref[...])


def _mix(attn_p, attn_s, z, gt, xp, xs, mods3, wo, wpw, wout, cw, cb, lng, lnb, g2, rwh, rwl, rb):
    tile = lambda i: (i, 0)
    ptile = lambda i: (jnp.minimum(i, NP_TILES - 1), 0)
    stile = lambda i: (jnp.maximum(i - NP_TILES, 0), 0)
    hpt = TM // HALO
    nhb = N_TOK // HALO
    return pl.pallas_call(
        _mix_kernel,
        grid=(N_TILES,),
        in_specs=[pl.BlockSpec((TM, D_MODEL), ptile),
                  pl.BlockSpec((TM, D_MODEL), stile),
                  pl.BlockSpec((TM, D_MODEL), tile),
                  pl.BlockSpec((HALO, D_MODEL), lambda i: (jnp.maximum(i * hpt - 1, 0), 0)),
                  pl.BlockSpec((HALO, D_MODEL), lambda i: (jnp.minimum((i + 1) * hpt, nhb - 1), 0)),
                  pl.BlockSpec((TM, 2 * D_MODEL), tile),
                  pl.BlockSpec((TM, D_MODEL), ptile),
                  pl.BlockSpec((TM, D_MODEL), stile),
                  pl.BlockSpec((1, N_MOD, D_MODEL), lambda i: (_seg_of_tile(i), 0, 0)),
                  _const_spec((D_MODEL, D_MODEL)),
                  _const_spec((D_MODEL, D_MODEL)),
                  _const_spec((D_MODEL, D_MODEL)),
                  _const_spec((D_MODEL // LANE, CONV_K, LANE)),
                  _const_spec((1, D_MODEL)),
                  _const_spec((1, D_MODEL)),
                  _const_spec((1, D_MODEL)),
                  _const_spec((1, D_MODEL)),
                  _const_spec((N_EXPERTS, D_MODEL)),
                  _const_spec((N_EXPERTS, D_MODEL)),
                  _const_spec((N_EXPERTS, 1))],
        out_specs=[pl.BlockSpec((TM, D_MODEL), tile),
                   pl.BlockSpec((TM, D_MODEL), tile),
                   pl.BlockSpec((N_EXPERTS, TM), lambda i: (0, i))],
        out_shape=[jax.ShapeDtypeStruct((N_TOK, D_MODEL), F32),
                   jax.ShapeDtypeStruct((N_TOK, D_MODEL), BF16),
                   jax.ShapeDtypeStruct((N_EXPERTS, N_TOK), F32)],
        scratch_shapes=[pltpu.VMEM((D_MODEL // LANE, TM + 2 * HALO, LANE), F32),
                        pltpu.VMEM((D_MODEL // LANE, TM, LANE), F32)],
        compiler_params=_cparams(("arbitrary",), 40),
        name="mix",
    )(attn_p, attn_s, z, z, z, gt, xp, xs, mods3, wo, wpw, wout, cw, cb, lng, lnb, g2, rwh, rwl, rb)


CHUNK = 16
BLK_ROWS = TM * TOP_K + N_EXPERTS * CHUNK
BLK_CHUNKS = BLK_ROWS // CHUNK
SUB = 256
N_SUB = BLK_ROWS // SUB
N_BLK = N_TOK // TM
TILE_CHUNKS = 16
TILE_ROWS = TILE_CHUNKS * CHUNK
MAX_CHUNKS = N_TOK * TOP_K // CHUNK + N_BLK * N_EXPERTS
MAX_TILES = MAX_CHUNKS // TILE_CHUNKS + N_EXPERTS
EXP_TILES = N_TOK // TILE_ROWS + 3
EXP_SLOTS = EXP_TILES * TILE_CHUNKS
N_BUF = 3
ZERO_CHUNK = N_BLK * BLK_CHUNKS
TRASH_CHUNK = ZERO_CHUNK + 1


def _plan_kernel(g_ref, dk_ref, wk_ref, cnt_ref):
    g = g_ref[...]
    sel = g > 0.0
    picked = jnp.where(sel, 1.0, 0.0)
    selb = picked.astype(BF16)
    r = lax.broadcasted_iota(jnp.int32, (TM, TM), 0)
    c = lax.broadcasted_iota(jnp.int32, (TM, TM), 1)
    before = jnp.where(r < c, 1.0, 0.0).astype(BF16)
    rank = jnp.dot(selb, before, preferred_element_type=F32)
    n = jnp.sum(picked, axis=1, keepdims=True)
    nchunk = jnp.floor((n + float(CHUNK - 1)) * (1.0 / CHUNK))
    er = lax.broadcasted_iota(jnp.int32, (N_EXPERTS, N_EXPERTS), 0)
    ec = lax.broadcasted_iota(jnp.int32, (N_EXPERTS, N_EXPERTS), 1)
    lower = jnp.where(ec < er, 1.0, 0.0).astype(BF16)
    lower_eq = jnp.where(ec <= er, 1.0, 0.0).astype(BF16)
    nchunk_b = jnp.broadcast_to(nchunk, (N_EXPERTS, TM))
    seg_chunk = jnp.dot(lower, nchunk_b.astype(BF16), preferred_element_type=F32)
    dest = seg_chunk * float(CHUNK) + rank
    order = jnp.where(sel, jnp.dot(lower_eq, selb, preferred_element_type=F32), 0.0)
    dks, wks = [], []
    for j in range(TOP_K):
        m = order == float(j + 1)
        dks.append(jnp.sum(jnp.where(m, dest + 1.0, 0.0), axis=0, keepdims=True) - 1.0)
        wks.append(jnp.sum(jnp.where(m, g, 0.0), axis=0, keepdims=True))
    dk_ref[...] = jnp.concatenate(dks, axis=0)
    wk_ref[...] = jnp.concatenate(wks, axis=0)
    cnt_ref[0] = jnp.broadcast_to(nchunk, (N_EXPERTS, LANE))


def _plan(gates_t):
    return pl.pallas_call(
        _plan_kernel,
        grid=(N_BLK,),
        in_specs=[pl.BlockSpec((N_EXPERTS, TM), lambda b: (0, b))],
        out_specs=[pl.BlockSpec((TOP_K, TM), lambda b: (0, b)),
                   pl.BlockSpec((TOP_K, TM), lambda b: (0, b)),
                   pl.BlockSpec((1, N_EXPERTS, LANE), lambda b: (b, 0, 0))],
        out_shape=[jax.ShapeDtypeStruct((TOP_K, N_TOK), F32),
                   jax.ShapeDtypeStruct((TOP_K, N_TOK), F32),
                   jax.ShapeDtypeStruct((N_BLK, N_EXPERTS, LANE), F32)],
        compiler_params=_cparams(("arbitrary",), 16),
        name="plan",
    )(gates_t)


def _tile_tables(cnt):
    c = cnt[:, :, 0].astype(jnp.int32)
    seg_chunk = jnp.cumsum(c, axis=1) - c
    ct = c.T
    seg_q = jnp.cumsum(ct, axis=1) - ct
    tiles_e = (ct.sum(axis=1) + TILE_CHUNKS - 1) // TILE_CHUNKS
    tile_end = jnp.cumsum(tiles_e)
    tile_start = tile_end - tiles_e
    q = jnp.arange(EXP_SLOTS, dtype=jnp.int32)
    inside = jnp.logical_and(q[None, :, None] >= seg_q[:, None, :],
                             q[None, :, None] < (seg_q + ct)[:, None, :])
    base = jnp.arange(N_BLK, dtype=jnp.int32)[None, :] * BLK_CHUNKS + seg_chunk.T - seg_q
    src = q[None, :] + jnp.sum(jnp.where(inside, base[:, None, :], 0), axis=2)
    valid = jnp.any(inside, axis=2)
    buf = (tile_start[:, None] + q[None, :] // TILE_CHUNKS) % N_BUF
    trash = TRASH_CHUNK + buf * TILE_CHUNKS + q[None, :] % TILE_CHUNKS
    in_src = jnp.where(valid, src, ZERO_CHUNK).astype(jnp.int32).reshape(-1)
    out_dst = jnp.where(valid, src, trash).astype(jnp.int32).reshape(-1)
    t = jnp.arange(MAX_TILES + N_BUF, dtype=jnp.int32)
    tile_expert = jnp.minimum(jnp.sum((t[:, None] >= tile_end[None, :]).astype(jnp.int32), axis=1),
                              N_EXPERTS - 1)
    tile_start = jnp.concatenate([tile_start, tile_end[-1:]]).astype(jnp.int32)
    sub_used = (jnp.sum(c, axis=1) * CHUNK + SUB - 1) // SUB
    return tile_start, tiles_e.astype(jnp.int32), tile_expert, in_src, out_dst, sub_used.astype(jnp.int32)


def _dispatch_kernel(sub_ref, h_ref, dk_ref, s_ref):
    b = pl.program_id(0)
    n_sub = jnp.where(b < N_BLK, sub_ref[jnp.minimum(b, N_BLK - 1)], 0)
    dk = dk_ref[...]
    sub_of = jnp.floor(dk * (1.0 / SUB))
    row_in = dk - sub_of * float(SUB)
    for sb in range(N_SUB):
        @pl.when(sb < n_sub)
        def _():
            rows = lax.broadcasted_iota(jnp.int32, (SUB, TM), 0).astype(F32).astype(BF16)
            one = jnp.ones((SUB, TM), BF16)
            local = jnp.where(sub_of == float(sb), row_in, -1.0).astype(BF16)
            p = jnp.zeros((SUB, TM), BF16)
            for j in range(TOP_K):
                p = jnp.where(rows == local[j:j + 1, :], one, p)
            s_ref[sb * SUB:(sb + 1) * SUB, :] = jnp.dot(
                p, h_ref[...], preferred_element_type=F32).astype(BF16)

        @pl.when(sb >= n_sub)
        def _():
            s_ref[sb * SUB:(sb + 1) * SUB, :] = jnp.zeros((SUB, D_MODEL), BF16)


def _dispatch(sub_used, h2, dk):
    last = N_BLK - 1
    grid_spec = pltpu.PrefetchScalarGridSpec(
        num_scalar_prefetch=1,
        grid=(N_BLK + 1,),
        in_specs=[pl.BlockSpec((TM, D_MODEL), lambda b, *_: (jnp.minimum(b, last), 0)),
                  pl.BlockSpec((TOP_K, TM), lambda b, *_: (0, jnp.minimum(b, last)))],
        out_specs=pl.BlockSpec((BLK_ROWS, D_MODEL), lambda b, *_: (b, 0)))
    return pl.pallas_call(
        _dispatch_kernel,
        grid_spec=grid_spec,
        out_shape=jax.ShapeDtypeStruct(((N_BLK + 1) * BLK_ROWS, D_MODEL), BF16),
        compiler_params=_cparams(("arbitrary",), 40),
        name="dispatch",
    )(sub_used, h2, dk)


def _experts_kernel(tstart_ref, tcount_ref, texp_ref, insrc_ref, outdst_ref,
                    s_hbm, wg_ref, wu_ref, wd_ref, o_hbm,
                    xb0, xb1, xb2, ob0, ob1, ob2, wgb, wub, wdb, insem, outsem):
    xbufs = (xb0, xb1, xb2)
    obufs = (ob0, ob1, ob2)
    e = pl.program_id(0)
    n_tiles = tstart_ref[N_EXPERTS]
    t0 = tstart_ref[e]

    def slot_base(t):
        te = texp_ref[t]
        j = jnp.minimum(t - tstart_ref[te], EXP_TILES - 1)
        return te * EXP_SLOTS + j * TILE_CHUNKS

    def in_copy(src_chunk, buf, i, sem):
        return pltpu.make_async_copy(s_hbm.at[src_chunk], xbufs[buf].at[i], sem)

    def out_copy(dst_chunk, buf, i, sem):
        return pltpu.make_async_copy(obufs[buf].at[i], o_hbm.at[dst_chunk], sem)

    def start_in(t, buf):
        base = slot_base(t)
        for i in range(TILE_CHUNKS):
            in_copy(insrc_ref[base + i], buf, i, insem.at[buf]).start()

    def start_out(t, buf, real):
        base = slot_base(jnp.maximum(t, 0))
        for i in range(TILE_CHUNKS):
            dst = jnp.where(real, outdst_ref[base + i], TRASH_CHUNK + buf * TILE_CHUNKS + i)
            out_copy(dst, buf, i, outsem.at[buf]).start()

    def wait_in(sem_idx):
        for i in range(TILE_CHUNKS):
            in_copy(0, 0, i, insem.at[sem_idx]).wait()

    def wait_out(sem_idx):
        for i in range(TILE_CHUNKS):
            out_copy(0, 0, i, outsem.at[sem_idx]).wait()

    def on_buffer(buf_dyn, fn):
        for k in range(N_BUF):
            pl.when(buf_dyn == k)(functools.partial(fn, k))

    wgb[...] = wg_ref[0].astype(BF16)
    wub[...] = wu_ref[0].astype(BF16)
    wdb[...] = wd_ref[0].astype(BF16)

    @pl.when(e == 0)
    def _():
        for ob in obufs:
            ob[...] = jnp.zeros(ob.shape, BF16)
        start_out(0, 0, False)
        start_out(0, 1, False)
        start_in(0, 0)
        start_in(1, 1)

    def tile_body(j, carry):
        t = t0 + j

        def run(k):
            nxt = (k + 2) % N_BUF
            wait_in(k)
            wait_out(k)
            start_in(t + 2, nxt)
            start_out(t - 1, nxt, t >= 1)
            x = xbufs[k][...].reshape(TILE_ROWS, D_MODEL)
            a = jnp.dot(x, wgb[...], preferred_element_type=F32)
            u = jnp.dot(x, wub[...], preferred_element_type=F32)
            hid = (a * jax.nn.sigmoid(a) * u).astype(BF16)
            out = jnp.dot(hid, wdb[...], preferred_element_type=F32).astype(BF16)
            obufs[k][...] = out.reshape(TILE_CHUNKS, CHUNK, D_MODEL)

        on_buffer(t % N_BUF, run)
        return carry

    lax.fori_loop(0, tcount_ref[e], tile_body, 0)

    @pl.when(e == N_EXPERTS - 1)
    def _():
        on_buffer((n_tiles + 2) % N_BUF, lambda k: start_out(n_tiles - 1, k, n_tiles >= 1))
        wait_in(n_tiles % N_BUF)
        wait_in((n_tiles + 1) % N_BUF)
        for k in range(N_BUF):
            wait_out(k)


def _experts(tile_start, tile_count, tile_expert, in_src, out_dst, sorted_rows, wg, wu, wd):
    wspec = lambda shape: pl.BlockSpec(shape, lambda e, *_: (e, 0, 0))
    grid_spec = pltpu.PrefetchScalarGridSpec(
        num_scalar_prefetch=5,
        grid=(N_EXPERTS,),
        in_specs=[pl.BlockSpec(memory_space=pl.ANY),
                  wspec((1, D_MODEL, EXPERT_FF)),
                  wspec((1, D_MODEL, EXPERT_FF)),
                  wspec((1, EXPERT_FF, D_MODEL))],
        out_specs=pl.BlockSpec(memory_space=pl.ANY),
        scratch_shapes=[pltpu.VMEM((TILE_CHUNKS, CHUNK, D_MODEL), BF16)] * (2 * N_BUF) + [
                        pltpu.VMEM((D_MODEL, EXPERT_FF), BF16),
                        pltpu.VMEM((D_MODEL, EXPERT_FF), BF16),
                        pltpu.VMEM((EXPERT_FF, D_MODEL), BF16),
                        pltpu.SemaphoreType.DMA((N_BUF,)),
                        pltpu.SemaphoreType.DMA((N_BUF,))])
    return pl.pallas_call(
        _experts_kernel,
        grid_spec=grid_spec,
        out_shape=jax.ShapeDtypeStruct(sorted_rows.shape, BF16),
        input_output_aliases={5: 0},
        compiler_params=_cparams(("arbitrary",), 32),
        name="experts",
    )(tile_start, tile_count, tile_expert, in_src, out_dst, sorted_rows, wg, wu, wd)


def _combine_kernel(sub_ref, o_ref, dkt_ref, wkt_ref, h_ref, x1_ref, mods_ref, sg_ref, su_ref, sd_ref,
                    fg_ref, yp_ref, ys_ref, acc_ref):
    i = pl.program_id(0)
    n_sub = sub_ref[i]
    dkt = dkt_ref[...]
    sub_of = jnp.floor(dkt * (1.0 / SUB))
    row_in = dkt - sub_of * float(SUB)

    h = h_ref[...]
    a = jnp.dot(h, sg_ref[...], preferred_element_type=F32)
    u = jnp.dot(h, su_ref[...], preferred_element_type=F32)
    acc_ref[...] = jnp.dot((a * jax.nn.sigmoid(a) * u).astype(BF16), sd_ref[...],
                           preferred_element_type=F32)
    for sb in range(N_SUB):
        @pl.when(sb < n_sub)
        def _():
            lanes = lax.broadcasted_iota(jnp.int32, (TM, SUB), 1).astype(F32).astype(BF16)
            wkt = wkt_ref[...]
            local = jnp.where(sub_of == float(sb), row_in, -1.0)
            pw = jnp.zeros((TM, SUB), BF16)
            for j in range(TOP_K):
                col = jnp.broadcast_to(local[:, j:j + 1], (TM, SUB)).astype(BF16)
                wcol = jnp.broadcast_to(wkt[:, j:j + 1], (TM, SUB)).astype(BF16)
                pw = jnp.where(lanes == col, wcol, pw)
            acc_ref[...] += jnp.dot(pw, o_ref[sb * SUB:(sb + 1) * SUB, :],
                                    preferred_element_type=F32)

    m = mods_ref[0]
    y = _rms(x1_ref[...] + m[5:6] * acc_ref[...], fg_ref[...])

    @pl.when(i < NP_TILES)
    def _():
        yp_ref[...] = y

    @pl.when(i >= NP_TILES)
    def _():
        ys_ref[...] = y


def _combine(sub_used, expert_rows, dkt, wkt, h2, x1, mods3, sg, su, sd, fg):
    tile = lambda i, *_: (i, 0)
    grid_spec = pltpu.PrefetchScalarGridSpec(
        num_scalar_prefetch=1,
        grid=(N_BLK,),
        in_specs=[pl.BlockSpec((BLK_ROWS, D_MODEL), tile),
                  pl.BlockSpec((TM, TOP_K), tile),
                  pl.BlockSpec((TM, TOP_K), tile),
                  pl.BlockSpec((TM, D_MODEL), tile),
                  pl.BlockSpec((TM, D_MODEL), tile),
                  pl.BlockSpec((1, N_MOD, D_MODEL), lambda i, *_: (_seg_of_tile(i), 0, 0)),
                  _const_spec((D_MODEL, EXPERT_FF)),
                  _const_spec((D_MODEL, EXPERT_FF)),
                  _const_spec((EXPERT_FF, D_MODEL)),
                  _const_spec((1, D_MODEL))],
        out_specs=[pl.BlockSpec((TM, D_MODEL), lambda i, *_: (jnp.minimum(i, NP_TILES - 1), 0)),
                   pl.BlockSpec((TM, D_MODEL), lambda i, *_: (jnp.maximum(i - NP_TILES, 0), 0))],
        scratch_shapes=[pltpu.VMEM((TM, D_MODEL), F32)])
    return pl.pallas_call(
        _combine_kernel,
        grid_spec=grid_spec,
        out_shape=[jax.ShapeDtypeStruct((N_PROMPT, D_MODEL), F32),
                   jax.ShapeDtypeStruct((N_SAMPLE, D_MODEL), F32)],
        compiler_params=_cparams(("arbitrary",), 40),
        name="combine",
    )(sub_used, expert_rows, dkt, wkt, h2, x1, mods3, sg, su, sd, fg)


_ROPE_SWAP = tuple(list(range(8, 16)) + list(range(0, 8)) + list(range(24, 32)) + list(range(16, 24)))


def _rope_tables():
    pos = jnp.arange(DEC_SEQ, dtype=jnp.int32)
    row = (pos // GRID_W).astype(F32)
    col = (pos % GRID_W).astype(F32)
    n_freq = QK_ROPE_DIM // 4
    inv_freq = ROPE_BASE ** (-jnp.arange(n_freq, dtype=F32) / n_freq)
    ar, ac = row[:, None] * inv_freq, col[:, None] * inv_freq
    cos32 = jnp.concatenate([jnp.cos(ar), jnp.cos(ar), jnp.cos(ac), jnp.cos(ac)], axis=1)
    sin32 = jnp.concatenate([-jnp.sin(ar), jnp.sin(ar), -jnp.sin(ac), jnp.sin(ac)], axis=1)
    cos32 = jnp.concatenate([jnp.ones((TM, QK_ROPE_DIM), F32), cos32], axis=0)
    sin32 = jnp.concatenate([jnp.zeros((TM, QK_ROPE_DIM), F32), sin32], axis=0)
    n = cos32.shape[0]
    ones = jnp.ones((n, QK_NOPE_DIM), F32)
    zeros = jnp.zeros((n, QK_NOPE_DIM), F32)
    tail = jnp.zeros((n, HEAD_PAD - QK_NOPE_DIM - QK_ROPE_DIM), F32)
    qc = jnp.concatenate([ones, cos32, tail], axis=1) * ATTN_SCALE
    qs = jnp.concatenate([zeros, sin32, tail], axis=1) * ATTN_SCALE
    kc = jnp.concatenate([zeros, cos32, tail], axis=1)
    ks = jnp.concatenate([zeros, sin32, tail], axis=1)
    return jnp.stack([qc, qs, kc, ks], axis=0)


def _pack_weights(w_in, w_uq, w_ukv):
    swap = jnp.array(_ROPE_SWAP, dtype=jnp.int32)
    s0, s1, s2, s3 = Q_LORA, Q_LORA + KV_LORA, Q_LORA + KV_LORA + QK_ROPE_DIM, \
        Q_LORA + KV_LORA + QK_ROPE_DIM + 2 * D_MODEL
    wkr = w_in[:, s1:s2]
    pad_l = jnp.zeros((D_MODEL, QK_NOPE_DIM), F32)
    pad_r = jnp.zeros((D_MODEL, HEAD_PAD - QK_NOPE_DIM - QK_ROPE_DIM), F32)
    win = jnp.concatenate([w_in[:, :s0], w_in[:, s0:s1],
                           pad_l, wkr, pad_r, pad_l, wkr[:, swap], pad_r,
                           w_in[:, s2:s3], w_in[:, s3:]], axis=1).astype(BF16)
    uq = w_uq.reshape(Q_LORA, N_HEADS, QK_NOPE_DIM + QK_ROPE_DIM)
    nope, rope = uq[:, :, :QK_NOPE_DIM], uq[:, :, QK_NOPE_DIM:]
    zt = jnp.zeros((Q_LORA, N_HEADS, HEAD_PAD - QK_NOPE_DIM - QK_ROPE_DIM), F32)
    wuqp = jnp.concatenate([nope, rope, zt], axis=2).reshape(Q_LORA, QK_COLS).astype(BF16)
    wuqs = jnp.concatenate([jnp.zeros_like(nope), rope[:, :, swap], zt], axis=2)
    wuqs = wuqs.reshape(Q_LORA, QK_COLS).astype(BF16)
    ukv = w_ukv.reshape(KV_LORA, N_HEADS, QK_NOPE_DIM + V_DIM)
    zk = jnp.zeros((KV_LORA, N_HEADS, HEAD_PAD - QK_NOPE_DIM), F32)
    wukp = jnp.concatenate([ukv[:, :, :QK_NOPE_DIM], zk], axis=2).reshape(KV_LORA, QK_COLS).astype(BF16)
    wuv = ukv[:, :, QK_NOPE_DIM:].reshape(KV_LORA, N_HEADS * V_DIM).astype(BF16)
    return win, wuqp, wuqs, wukp, wuv


def kernel(x_prompt, x_sample, cache_ckv, cache_krope, c, c_ctx, mod_w, mod_b, norm1_g, w_in, q_norm_g, w_uq, kv_norm_g, w_ukv, w_o_attn, conv_w, conv_b, conv_ln_g, conv_ln_b, w_pw2, w_out, norm2_g, router_w, router_bias, exp_w_gate, exp_w_up, exp_w_down, sh_w_gate, sh_w_up, sh_w_down, final_g):
    xp = x_prompt.reshape(N_PROMPT, D_MODEL)
    xs = x_sample.reshape(N_SAMPLE, D_MODEL)
    cond8 = jnp.concatenate([c_ctx[None, :], c, jnp.zeros((8 - 1 - DEC_BATCH, D_MODEL), F32)], axis=0)
    mods3 = _mods(cond8, mod_w[0], mod_b).reshape(8, N_MOD, D_MODEL)

    win, wuqp, wuqs, wukp, wuv = _pack_weights(w_in[0], w_uq[0], w_ukv[0])
    q, k, v, z, gt, ckv_new, kr_new = _inproj(
        xp, xs, mods3, norm1_g, q_norm_g, kv_norm_g, win, wuqp, wuqs, wukp, wuv, _rope_tables())

    krope_pad = jnp.pad(cache_krope[:, 0], ((0, 0), (0, 0),
                                            (QK_NOPE_DIM, HEAD_PAD - QK_NOPE_DIM - QK_ROPE_DIM)))
    kc, vc = _ctxkv(cache_ckv[:, 0], krope_pad, wukp, wuv)
    attn_p = _attn_prompt(q, k, v)
    attn_s = _attn_sample(q, k, v, kc, vc)

    rwt = router_w[0].T
    rwh = rwt.astype(BF16)
    rwl = (rwt - rwh.astype(F32)).astype(BF16)
    cw = conv_w[0].reshape(CONV_K, D_MODEL // LANE, LANE).transpose(1, 0, 2)
    x1, h2, gates_t = _mix(
        attn_p, attn_s, z, gt, xp, xs, mods3,
        w_o_attn[0].astype(BF16), w_pw2[0].astype(BF16), w_out[0].astype(BF16),
        cw, conv_b, conv_ln_g, conv_ln_b, norm2_g, rwh, rwl, router_bias.reshape(N_EXPERTS, 1))

    dk, wk, cnt = _plan(gates_t)
    tile_start, tile_count, tile_expert, in_src, out_dst, sub_used = _tile_tables(cnt)
    expert_rows = _experts(tile_start, tile_count, tile_expert, in_src, out_dst,
                           _dispatch(sub_used, h2, dk).reshape(-1, CHUNK, D_MODEL),
                           exp_w_gate[0], exp_w_up[0], exp_w_down[0])
    y_p, y_s = _combine(sub_used, expert_rows.reshape(-1, D_MODEL), dk.T, wk.T, h2, x1, mods3,
                        sh_w_gate[0].astype(BF16), sh_w_up[0].astype(BF16),
                        sh_w_down[0].astype(BF16), final_g[None, :])
    return (y_p.reshape(BATCH, SEQ, D_MODEL),
            y_s.reshape(DEC_BATCH, DEC_SEQ, D_MODEL),
            ckv_new.reshape(BATCH, 1, SEQ, KV_LORA),
            kr_new.reshape(BATCH, 1, SEQ, QK_ROPE_DIM))
```

```python
import functools
import math

import jax
import jax.numpy as jnp
from jax import lax
from jax.experimental import pallas as pl
from jax.experimental.pallas import tpu as pltpu

F32 = jnp.float32
BF16 = jnp.bfloat16

D_MODEL = 1024
BATCH = 16
SEQ = 256
DEC_BATCH = 2
DEC_SEQ = 2048
PAST_LEN = 256
GRID_W = 64
N_HEADS = 16
QK_NOPE_DIM = 64
QK_ROPE_DIM = 32
V_DIM = 64
Q_LORA = 384
KV_LORA = 256
ROPE_BASE = 10000.0
ATTN_SCALE = 1.0 / math.sqrt(QK_NOPE_DIM + QK_ROPE_DIM)
CONV_K = 31
CONV_PAD = CONV_K // 2
N_EXPERTS = 64
TOP_K = 8
N_GROUPS = 8
GROUP_SIZE = N_EXPERTS // N_GROUPS
TOPK_GROUPS = 4
EXPERT_FF = 256
ROUTED_SCALE = 2.5
EPS = 1e-6
N_MOD = 6

LANE = 128
HEAD_PAD = LANE
QK_COLS = N_HEADS * HEAD_PAD
N_PROMPT = BATCH * SEQ
N_SAMPLE = DEC_BATCH * DEC_SEQ
N_TOK = N_PROMPT + N_SAMPLE

TM = 256
NP_TILES = N_PROMPT // TM
NS_TILES = N_SAMPLE // TM
N_TILES = NP_TILES + NS_TILES
TILES_PER_DEC = DEC_SEQ // TM
HALO = 16

C_Q = 0
C_KV = C_Q + Q_LORA
C_KR = C_KV + KV_LORA
C_CONV = C_KR + 2 * LANE
C_GATE = C_CONV + 2 * D_MODEL
IN_COLS_PACKED = C_GATE + 2 * D_MODEL

VMEM_CAP = 56 * 1024 * 1024


def _cparams(sem, vmem_mb):
    return pltpu.CompilerParams(dimension_semantics=sem,
                                vmem_limit_bytes=min(vmem_mb * 1024 * 1024, VMEM_CAP))


def _const_spec(shape):
    n = len(shape)
    return pl.BlockSpec(shape, lambda *_: (0,) * n, pipeline_mode=pl.Buffered(1))


def _seg_of_tile(i):
    return jnp.where(i < NP_TILES, 0, 1 + (i - NP_TILES) // TILES_PER_DEC)


def _rms(x, g):
    return x * lax.rsqrt(jnp.mean(x * x, axis=-1, keepdims=True) + EPS) * g


def _mods_kernel(cond_ref, w_ref, b_ref, o_ref):
    c = cond_ref[...]
    s = c * jax.nn.sigmoid(c)
    o_ref[...] = jnp.dot(s.astype(BF16), w_ref[...].astype(BF16),
                         preferred_element_type=F32) + b_ref[...]


def _mods(cond8, mod_w, mod_b):
    n = N_MOD * D_MODEL
    bn = 512
    return pl.pallas_call(
        _mods_kernel,
        grid=(n // bn,),
        in_specs=[pl.BlockSpec((8, D_MODEL), lambda j: (0, 0)),
                  pl.BlockSpec((D_MODEL, bn), lambda j: (0, j)),
                  pl.BlockSpec((1, bn), lambda j: (0, j))],
        out_specs=pl.BlockSpec((8, bn), lambda j: (0, j)),
        out_shape=jax.ShapeDtypeStruct((8, n), F32),
        compiler_params=_cparams(("arbitrary",), 16),
        name="mods",
    )(cond8, mod_w, mod_b)


def _inproj_kernel(xp_ref, xs_ref, mods_ref, g1_ref, qg_ref, kvg_ref, win_ref, wuqp_ref, wuqs_ref,
                   wukp_ref, wuv_ref, tab_ref,
                   q_ref, k_ref, v_ref, z_ref, gt_ref, ckv_ref, kr_ref):
    i = pl.program_id(0)
    x = jnp.where(i < NP_TILES, xp_ref[...], xs_ref[...])
    m = mods_ref[0]
    h = _rms(x, g1_ref[...]) * (1.0 + m[1:2]) + m[0:1]
    hb = h.astype(BF16)

    def proj(c0, c1):
        return jnp.dot(hb, win_ref[:, c0:c1], preferred_element_type=F32)

    def lanes16(t):
        return jnp.concatenate([t] * N_HEADS, axis=1)

    qn = _rms(proj(C_Q, C_KV), qg_ref[...]).astype(BF16)
    q = (jnp.dot(qn, wuqp_ref[...], preferred_element_type=F32) * lanes16(tab_ref[0])
         + jnp.dot(qn, wuqs_ref[...], preferred_element_type=F32) * lanes16(tab_ref[1]))
    q_ref[...] = q.astype(BF16)

    ckv = _rms(proj(C_KV, C_KR), kvg_ref[...])
    ckvb = ckv.astype(BF16)
    ukr = proj(C_KR, C_CONV)
    ua = ukr[:, :LANE]
    krot = ua * tab_ref[2] + ukr[:, LANE:] * tab_ref[3]
    k = jnp.dot(ckvb, wukp_ref[...], preferred_element_type=F32) + lanes16(krot)
    k_ref[...] = k.astype(BF16)
    v_ref[...] = jnp.dot(ckvb, wuv_ref[...], preferred_element_type=F32).astype(BF16)

    @pl.when(i < NP_TILES)
    def _():
        ckv_ref[...] = ckv
        kr_ref[...] = ua[:, QK_NOPE_DIM:QK_NOPE_DIM + QK_ROPE_DIM]

    a = proj(C_CONV, C_CONV + D_MODEL)
    g = proj(C_CONV + D_MODEL, C_GATE)
    z_ref[...] = (a * jax.nn.sigmoid(g)).astype(BF16)
    gt_ref[...] = jax.nn.sigmoid(proj(C_GATE, IN_COLS_PACKED)).astype(BF16)


def _inproj(xp, xs, mods3, g1, qg, kvg, win, wuqp, wuqs, wukp, wuv, tab):
    tile = lambda i: (i, 0)
    ptile = lambda i: (jnp.minimum(i, NP_TILES - 1), 0)
    stile = lambda i: (jnp.maximum(i - NP_TILES, 0), 0)
    tab_blk = lambda i: (0, jnp.where(i < NP_TILES, 0, 1 + (i - NP_TILES) % TILES_PER_DEC), 0)
    return pl.pallas_call(
        _inproj_kernel,
        grid=(N_TILES,),
        in_specs=[pl.BlockSpec((TM, D_MODEL), ptile),
                  pl.BlockSpec((TM, D_MODEL), stile),
                  pl.BlockSpec((1, N_MOD, D_MODEL), lambda i: (_seg_of_tile(i), 0, 0)),
                  _const_spec((1, D_MODEL)),
                  _const_spec((1, Q_LORA)),
                  _const_spec((1, KV_LORA)),
                  _const_spec((D_MODEL, IN_COLS_PACKED)),
                  _const_spec((Q_LORA, QK_COLS)),
                  _const_spec((Q_LORA, QK_COLS)),
                  _const_spec((KV_LORA, QK_COLS)),
                  _const_spec((KV_LORA, N_HEADS * V_DIM)),
                  pl.BlockSpec((4, TM, LANE), tab_blk)],
        out_specs=[pl.BlockSpec((TM, QK_COLS), tile),
                   pl.BlockSpec((TM, QK_COLS), tile),
                   pl.BlockSpec((TM, N_HEADS * V_DIM), tile),
                   pl.BlockSpec((TM, D_MODEL), tile),
                   pl.BlockSpec((TM, 2 * D_MODEL), tile),
                   pl.BlockSpec((TM, KV_LORA), ptile),
                   pl.BlockSpec((TM, QK_ROPE_DIM), ptile)],
        out_shape=[jax.ShapeDtypeStruct((N_TOK, QK_COLS), BF16),
                   jax.ShapeDtypeStruct((N_TOK, QK_COLS), BF16),
                   jax.ShapeDtypeStruct((N_TOK, N_HEADS * V_DIM), BF16),
                   jax.ShapeDtypeStruct((N_TOK, D_MODEL), BF16),
                   jax.ShapeDtypeStruct((N_TOK, 2 * D_MODEL), BF16),
                   jax.ShapeDtypeStruct((N_PROMPT, KV_LORA), F32),
                   jax.ShapeDtypeStruct((N_PROMPT, QK_ROPE_DIM), F32)],
        compiler_params=_cparams(("arbitrary",), 48),
        name="inproj",
    )(xp, xs, mods3, g1, qg, kvg, win, wuqp, wuqs, wukp, wuv, tab)


def _ctxkv_kernel(ckv_ref, kr_ref, wukp_ref, wuv_ref, k_ref, v_ref):
    c = ckv_ref[0].astype(BF16)
    kr = jnp.concatenate([kr_ref[0]] * N_HEADS, axis=1)
    k_ref[0] = (jnp.dot(c, wukp_ref[...], preferred_element_type=F32) + kr).astype(BF16)
    v_ref[0] = jnp.dot(c, wuv_ref[...], preferred_element_type=F32).astype(BF16)


def _ctxkv(cache_ckv, krope_pad, wukp, wuv):
    return pl.pallas_call(
        _ctxkv_kernel,
        grid=(DEC_BATCH,),
        in_specs=[pl.BlockSpec((1, PAST_LEN, KV_LORA), lambda b: (b, 0, 0)),
                  pl.BlockSpec((1, PAST_LEN, LANE), lambda b: (b, 0, 0)),
                  _const_spec((KV_LORA, QK_COLS)),
                  _const_spec((KV_LORA, N_HEADS * V_DIM))],
        out_specs=[pl.BlockSpec((1, PAST_LEN, QK_COLS), lambda b: (b, 0, 0)),
                   pl.BlockSpec((1, PAST_LEN, N_HEADS * V_DIM), lambda b: (b, 0, 0))],
        out_shape=[jax.ShapeDtypeStruct((DEC_BATCH, PAST_LEN, QK_COLS), BF16),
                   jax.ShapeDtypeStruct((DEC_BATCH, PAST_LEN, N_HEADS * V_DIM), BF16)],
        compiler_params=_cparams(("arbitrary",), 16),
        name="ctxkv",
    )(cache_ckv, krope_pad, wukp, wuv)


_NT = (((1,), (1,)), ((), ()))


def _attn_kernel(*refs, has_ctx):
    if has_ctx:
        q_ref, k_ref, v_ref, kc_ref, vc_ref, o_ref = refs
    else:
        q_ref, k_ref, v_ref, o_ref = refs
    tq = q_ref.shape[0]
    lane = lax.broadcasted_iota(jnp.int32, (tq, LANE), 1)
    for j in range(N_HEADS // 2):
        pair = slice(j * LANE, (j + 1) * LANE)
        outs = []
        for h in (2 * j, 2 * j + 1):
            hs = slice(h * HEAD_PAD, (h + 1) * HEAD_PAD)
            qh = q_ref[:, hs]
            s = lax.dot_general(qh, k_ref[:, hs], _NT, preferred_element_type=F32)
            mx = jnp.max(s, axis=-1, keepdims=True)
            if has_ctx:
                sc = lax.dot_general(qh, kc_ref[0, :, hs], _NT, preferred_element_type=F32)
                mx = jnp.maximum(mx, jnp.max(sc, axis=-1, keepdims=True))
            p = jnp.exp(s - mx)
            den = jnp.sum(p, axis=-1, keepdims=True)
            o = jnp.dot(p.astype(BF16), v_ref[:, pair], preferred_element_type=F32)
            if has_ctx:
                pc = jnp.exp(sc - mx)
                den = den + jnp.sum(pc, axis=-1, keepdims=True)
                o = o + jnp.dot(pc.astype(BF16), vc_ref[0, :, pair], preferred_element_type=F32)
            outs.append(o / den)
        o_ref[:, pair] = jnp.where(lane < V_DIM, outs[0], outs[1]).astype(BF16)


def _attn_prompt(q, k, v):
    blk = lambda b: (b, 0)
    return pl.pallas_call(
        functools.partial(_attn_kernel, has_ctx=False),
        grid=(BATCH,),
        in_specs=[pl.BlockSpec((SEQ, QK_COLS), blk),
                  pl.BlockSpec((SEQ, QK_COLS), blk),
                  pl.BlockSpec((SEQ, N_HEADS * V_DIM), blk)],
        out_specs=pl.BlockSpec((SEQ, N_HEADS * V_DIM), blk),
        out_shape=jax.ShapeDtypeStruct((N_PROMPT, N_HEADS * V_DIM), BF16),
        compiler_params=_cparams(("arbitrary",), 24),
        name="attn_prompt",
    )(q, k, v)


ATT_TQ = 256


def _attn_sample(q, k, v, kc, vc):
    nq = DEC_SEQ // ATT_TQ
    q0 = N_PROMPT // ATT_TQ
    s0 = N_PROMPT // DEC_SEQ
    once = pl.Buffered(1)
    return pl.pallas_call(
        functools.partial(_attn_kernel, has_ctx=True),
        grid=(DEC_BATCH, nq),
        in_specs=[pl.BlockSpec((ATT_TQ, QK_COLS), lambda b, t: (q0 + b * nq + t, 0)),
                  pl.BlockSpec((DEC_SEQ, QK_COLS), lambda b, t: (s0 + b, 0), pipeline_mode=once),
                  pl.BlockSpec((DEC_SEQ, N_HEADS * V_DIM), lambda b, t: (s0 + b, 0), pipeline_mode=once),
                  pl.BlockSpec((1, PAST_LEN, QK_COLS), lambda b, t: (b, 0, 0), pipeline_mode=once),
                  pl.BlockSpec((1, PAST_LEN, N_HEADS * V_DIM), lambda b, t: (b, 0, 0),
                               pipeline_mode=once)],
        out_specs=pl.BlockSpec((ATT_TQ, N_HEADS * V_DIM), lambda b, t: (b * nq + t, 0)),
        out_shape=jax.ShapeDtypeStruct((N_SAMPLE, N_HEADS * V_DIM), BF16),
        compiler_params=_cparams(("arbitrary", "arbitrary"), 48),
        name="attn_sample",
    )(q, k, v, kc, vc)


def _topk_rows(v, k, n):
    rows = lax.broadcasted_iota(jnp.int32, v.shape, 0).astype(F32)
    sel = jnp.zeros(v.shape, F32)
    for _ in range(k):
        mx = jnp.max(v, axis=0, keepdims=True)
        first = jnp.min(jnp.where(v == mx, rows, float(n)), axis=0, keepdims=True)
        hit = rows == first
        sel = jnp.where(hit, 1.0, sel)
        v = jnp.where(hit, -jnp.inf, v)
    return sel > 0.0


def _route(logits_t, bias):
    scores = jax.nn.sigmoid(logits_t)
    sel = scores + bias
    gscore = []
    for g in range(N_GROUPS):
        blk = sel[g * GROUP_SIZE:(g + 1) * GROUP_SIZE]
        top2 = _topk_rows(blk, 2, GROUP_SIZE)
        gscore.append(jnp.sum(jnp.where(top2, blk, 0.0), axis=0, keepdims=True))
    masked = []
    for g in range(N_GROUPS):
        beaten = jnp.zeros(gscore[g].shape, F32)
        for o in range(N_GROUPS):
            if o < g:
                beaten = beaten + jnp.where(gscore[o] >= gscore[g], 1.0, 0.0)
            elif o > g:
                beaten = beaten + jnp.where(gscore[o] > gscore[g], 1.0, 0.0)
        keep = beaten < float(TOPK_GROUPS)
        masked.append(jnp.where(keep, sel[g * GROUP_SIZE:(g + 1) * GROUP_SIZE], -jnp.inf))
    chosen = _topk_rows(jnp.concatenate(masked, axis=0), TOP_K, N_EXPERTS)
    w = jnp.where(chosen, scores, 0.0)
    return w / jnp.sum(w, axis=0, keepdims=True) * ROUTED_SCALE


def _mix_kernel(ap_ref, as_ref, z_ref, zp_ref, zn_ref, gt_ref, xp_ref, xs_ref, mods_ref,
                wo_ref, wpw_ref, wout_ref, cw_ref, cb_ref, lng_ref, lnb_ref, g2_ref,
                rwh_ref, rwl_ref, rb_ref,
                x1_ref, h2_ref, gates_ref, win_ref, cz_ref):
    i = pl.program_id(0)
    is_prompt = i < NP_TILES
    pos = (i - NP_TILES) % TILES_PER_DEC
    first = jnp.logical_or(is_prompt, pos == 0)
    last = jnp.logical_or(is_prompt, pos == TILES_PER_DEC - 1)

    nchunk = D_MODEL // LANE
    zprev = jnp.where(first, 0.0, zp_ref[...].astype(F32))
    znext = jnp.where(last, 0.0, zn_ref[...].astype(F32))
    zc = z_ref[...].astype(F32)
    for c in range(nchunk):
        cs = slice(c * LANE, (c + 1) * LANE)
        win_ref[c, 0:HALO, :] = zprev[:, cs]
        win_ref[c, HALO:HALO + TM, :] = zc[:, cs]
        win_ref[c, HALO + TM:, :] = znext[:, cs]

    def conv_chunk(c, carry):
        w = cw_ref[c]
        acc = jnp.zeros((TM, LANE), F32)
        for kk in range(CONV_K):
            off = HALO - CONV_PAD + kk
            acc = acc + win_ref[c, off:off + TM, :] * w[kk:kk + 1, :]
        cz_ref[c] = acc
        return carry

    lax.fori_loop(0, nchunk, conv_chunk, 0)
    conv = jnp.concatenate([cz_ref[c] for c in range(nchunk)], axis=1) + cb_ref[...]
    mu = jnp.mean(conv, axis=-1, keepdims=True)
    cen = conv - mu
    var = jnp.mean(cen * cen, axis=-1, keepdims=True)
    ln = cen * lax.rsqrt(var + EPS) * lng_ref[...] + lnb_ref[...]
    act = (ln * jax.nn.sigmoid(ln)).astype(BF16)
    conv_out = jnp.dot(act, wpw_ref[...], preferred_element_type=F32)

    attn = jnp.where(is_prompt, ap_ref[...], as_ref[...])
    attn_out = jnp.dot(attn, wo_ref[...], preferred_element_type=F32)
    gt = gt_ref[...].astype(F32)
    merged = gt[:, :D_MODEL] * attn_out + gt[:, D_MODEL:] * conv_out
    mix = jnp.dot(merged.astype(BF16), wout_ref[...], preferred_element_type=F32)

    m = mods_ref[0]
    x = jnp.where(is_prompt, xp_ref[...], xs_ref[...])
    x1 = x + m[2:3] * mix
    x1_ref[...] = x1
    h2 = _rms(x1, g2_ref[...]) * (1.0 + m[4:5]) + m[3:4]
    hh = h2.astype(BF16)
    h2_ref[...] = hh

    hl = (h2 - hh.astype(F32)).astype(BF16)
    rwh = rwh_ref[...]
    logits_t = (lax.dot_general(rwh, hh, _NT, preferred_element_type=F32)
                + lax.dot_general(rwh, hl, _NT, preferred_element_type=F32)
                + lax.dot_general(rwl_ref[...], hh, _NT, preferred_element_type=F32))
    gates_ref[...] = _route(logits_t, rb_ref[...])


def _mix(attn_p, attn_s, z, gt, xp, xs, mods3, wo, wpw, wout, cw, cb, lng, lnb, g2, rwh, rwl, rb):
    tile = lambda i: (i, 0)
    ptile = lambda i: (jnp.minimum(i, NP_TILES - 1), 0)
    stile = lambda i: (jnp.maximum(i - NP_TILES, 0), 0)
    hpt = TM // HALO
    nhb = N_TOK // HALO
    return pl.pallas_call(
        _mix_kernel,
        grid=(N_TILES,),
        in_specs=[pl.BlockSpec((TM, D_MODEL), ptile),
                  pl.BlockSpec((TM, D_MODEL), stile),
                  pl.BlockSpec((TM, D_MODEL), tile),
                  pl.BlockSpec((HALO, D_MODEL), lambda i: (jnp.maximum(i * hpt - 1, 0), 0)),
                  pl.BlockSpec((HALO, D_MODEL), lambda i: (jnp.minimum((i + 1) * hpt, nhb - 1), 0)),
                  pl.BlockSpec((TM, 2 * D_MODEL), tile),
                  pl.BlockSpec((TM, D_MODEL), ptile),
                  pl.BlockSpec((TM, D_MODEL), stile),
                  pl.BlockSpec((1, N_MOD, D_MODEL), lambda i: (_seg_of_tile(i), 0, 0)),
                  _const_spec((D_MODEL, D_MODEL)),
                  _const_spec((D_MODEL, D_MODEL)),
                  _const_spec((D_MODEL, D_MODEL)),
                  _const_spec((D_MODEL // LANE, CONV_K, LANE)),
                  _const_spec((1, D_MODEL)),
                  _const_spec((1, D_MODEL)),
                  _const_spec((1, D_MODEL)),
                  _const_spec((1, D_MODEL)),
                  _const_spec((N_EXPERTS, D_MODEL)),
                  _const_spec((N_EXPERTS, D_MODEL)),
                  _const_spec((N_EXPERTS, 1))],
        out_specs=[pl.BlockSpec((TM, D_MODEL), tile),
                   pl.BlockSpec((TM, D_MODEL), tile),
                   pl.BlockSpec((N_EXPERTS, TM), lambda i: (0, i))],
        out_shape=[jax.ShapeDtypeStruct((N_TOK, D_MODEL), F32),
                   jax.ShapeDtypeStruct((N_TOK, D_MODEL), BF16),
                   jax.ShapeDtypeStruct((N_EXPERTS, N_TOK), F32)],
        scratch_shapes=[pltpu.VMEM((D_MODEL // LANE, TM + 2 * HALO, LANE), F32),
                        pltpu.VMEM((D_MODEL // LANE, TM, LANE), F32)],
        compiler_params=_cparams(("arbitrary",), 40),
        name="mix",
    )(attn_p, attn_s, z, z, z, gt, xp, xs, mods3, wo, wpw, wout, cw, cb, lng, lnb, g2, rwh, rwl, rb)


CHUNK = 16
BLK_ROWS = TM * TOP_K + N_EXPERTS * CHUNK
BLK_CHUNKS = BLK_ROWS // CHUNK
SUB = 256
N_SUB = BLK_ROWS // SUB
SUB_ALWAYS = TM * TOP_K // SUB
N_BLK = N_TOK // TM
TILE_CHUNKS = 32
TILE_ROWS = TILE_CHUNKS * CHUNK
MAX_CHUNKS = N_TOK * TOP_K // CHUNK + N_BLK * N_EXPERTS
MAX_TILES = MAX_CHUNKS // TILE_CHUNKS + N_EXPERTS
EXP_TILES = N_TOK // TILE_ROWS + 3
EXP_SLOTS = EXP_TILES * TILE_CHUNKS
N_BUF = 3
ZERO_CHUNK = N_BLK * BLK_CHUNKS
TRASH_CHUNK = ZERO_CHUNK + 1


def _plan_kernel(g_ref, pos_ref, cnt_ref):
    g = g_ref[...]
    picked = jnp.where(g > 0.0, 1.0, 0.0)
    r = lax.broadcasted_iota(jnp.int32, (TM, TM), 0)
    c = lax.broadcasted_iota(jnp.int32, (TM, TM), 1)
    upto = jnp.where(r <= c, 1.0, 0.0).astype(BF16)
    pos = picked * jnp.dot(picked.astype(BF16), upto, preferred_element_type=F32)
    pos_ref[0:N_EXPERTS, :] = pos.astype(BF16)
    pos_ref[N_EXPERTS:, :] = jnp.zeros((LANE - N_EXPERTS, TM), BF16)
    n = jnp.sum(picked, axis=1, keepdims=True)
    nchunk = jnp.floor((n + float(CHUNK - 1)) * (1.0 / CHUNK))
    cnt_ref[0] = jnp.broadcast_to(nchunk, (N_EXPERTS, LANE))


def _plan(gates_t):
    return pl.pallas_call(
        _plan_kernel,
        grid=(N_BLK,),
        in_specs=[pl.BlockSpec((N_EXPERTS, TM), lambda b: (0, b))],
        out_specs=[pl.BlockSpec((LANE, TM), lambda b: (0, b)),
                   pl.BlockSpec((1, N_EXPERTS, LANE), lambda b: (b, 0, 0))],
        out_shape=[jax.ShapeDtypeStruct((LANE, N_TOK), BF16),
                   jax.ShapeDtypeStruct((N_BLK, N_EXPERTS, LANE), F32)],
        compiler_params=_cparams(("arbitrary",), 16),
        name="plan",
    )(gates_t)


def _tile_tables(cnt):
    c = cnt[:, :, 0].astype(jnp.int32)
    seg_chunk = jnp.cumsum(c, axis=1) - c
    ct = c.T
    seg_q = jnp.cumsum(ct, axis=1) - ct
    tiles_e = (ct.sum(axis=1) + TILE_CHUNKS - 1) // TILE_CHUNKS
    tile_end = jnp.cumsum(tiles_e)
    tile_start = tile_end - tiles_e
    q = jnp.arange(EXP_SLOTS, dtype=jnp.int32)
    inside = jnp.logical_and(q[None, :, None] >= seg_q[:, None, :],
                             q[None, :, None] < (seg_q + ct)[:, None, :])
    base = jnp.arange(N_BLK, dtype=jnp.int32)[None, :] * BLK_CHUNKS + seg_chunk.T - seg_q
    src = q[None, :] + jnp.sum(jnp.where(inside, base[:, None, :], 0), axis=2)
    valid = jnp.any(inside, axis=2)
    buf = (tile_start[:, None] + q[None, :] // TILE_CHUNKS) % N_BUF
    trash = TRASH_CHUNK + buf * TILE_CHUNKS + q[None, :] % TILE_CHUNKS
    in_src = jnp.where(valid, src, ZERO_CHUNK).astype(jnp.int32).reshape(-1)
    out_dst = jnp.where(valid, src, trash).astype(jnp.int32).reshape(-1)
    t = jnp.arange(MAX_TILES + N_BUF, dtype=jnp.int32)
    tile_expert = jnp.minimum(jnp.sum((t[:, None] >= tile_end[None, :]).astype(jnp.int32), axis=1),
                              N_EXPERTS - 1)
    tile_start = jnp.concatenate([tile_start, tile_end[-1:]]).astype(jnp.int32)
    sub_used = (jnp.sum(c, axis=1) * CHUNK + SUB - 1) // SUB
    pad = jnp.full((N_BLK, LANE - N_EXPERTS), float(2 * BLK_ROWS), F32)
    first = jnp.concatenate([(seg_chunk * CHUNK).astype(F32), pad], axis=1)
    last = jnp.concatenate([((seg_chunk + c) * CHUNK).astype(F32), pad], axis=1)
    first_chunk = jnp.concatenate([seg_chunk.astype(F32), jnp.zeros_like(pad)], axis=1)
    seg = dict(
        first_row=jnp.broadcast_to(first[:, None, :], (N_BLK, 8, LANE)),
        last_row=jnp.broadcast_to(last[:, None, :], (N_BLK, 8, LANE)),
        first_col=jnp.broadcast_to(first[:, :, None], (N_BLK, LANE, LANE)),
        last_col=jnp.broadcast_to(last[:, :, None], (N_BLK, LANE, LANE)),
        first_chunk_col=jnp.broadcast_to(first_chunk[:, :, None], (N_BLK, LANE, LANE)).astype(BF16))
    return (tile_start, tiles_e.astype(jnp.int32), tile_expert, in_src, out_dst,
            sub_used.astype(jnp.int32), seg)


def _dispatch_kernel(sub_ref, h_ref, pos_ref, first_ref, last_ref, fchunk_ref, s_ref):
    b = pl.program_id(0)
    n_sub = jnp.where(b < N_BLK, sub_ref[jnp.minimum(b, N_BLK - 1)], 0)

    def fill(sb):
        rows = (lax.broadcasted_iota(jnp.int32, (SUB, LANE), 0) + sb * SUB).astype(F32)
        own = (jnp.where(rows >= first_ref[0, 0:1, :], 1.0, 0.0)
               - jnp.where(rows >= last_ref[0, 0:1, :], 1.0, 0.0)).astype(BF16)
        want = rows + 1.0 - float(CHUNK) * jnp.dot(own, fchunk_ref[0], preferred_element_type=F32)
        have = jnp.dot(own, pos_ref[...], preferred_element_type=F32)
        p = jnp.where(have == jnp.concatenate([want] * (TM // LANE), axis=1), 1.0, 0.0)
        s_ref[sb * SUB:(sb + 1) * SUB, :] = jnp.dot(
            p.astype(BF16), h_ref[...], preferred_element_type=F32).astype(BF16)

    def clear(sb):
        s_ref[sb * SUB:(sb + 1) * SUB, :] = jnp.zeros((SUB, D_MODEL), BF16)

    @pl.when(b < N_BLK)
    def _():
        for sb in range(SUB_ALWAYS):
            fill(sb)

    @pl.when(b == N_BLK)
    def _():
        for sb in range(SUB_ALWAYS):
            clear(sb)

    for sb in range(SUB_ALWAYS, N_SUB):
        pl.when(sb < n_sub)(functools.partial(fill, sb))
        pl.when(sb >= n_sub)(functools.partial(clear, sb))


def _dispatch(sub_used, h2, pos, seg):
    last = N_BLK - 1
    blk3 = lambda b, *_: (jnp.minimum(b, last), 0, 0)
    grid_spec = pltpu.PrefetchScalarGridSpec(
        num_scalar_prefetch=1,
        grid=(N_BLK + 1,),
        in_specs=[pl.BlockSpec((TM, D_MODEL), lambda b, *_: (jnp.minimum(b, last), 0)),
                  pl.BlockSpec((LANE, TM), lambda b, *_: (0, jnp.minimum(b, last))),
                  pl.BlockSpec((1, 8, LANE), blk3),
                  pl.BlockSpec((1, 8, LANE), blk3),
                  pl.BlockSpec((1, LANE, LANE), blk3)],
        out_specs=pl.BlockSpec((BLK_ROWS, D_MODEL), lambda b, *_: (b, 0)))
    return pl.pallas_call(
        _dispatch_kernel,
        grid_spec=grid_spec,
        out_shape=jax.ShapeDtypeStruct(((N_BLK + 1) * BLK_ROWS, D_MODEL), BF16),
        compiler_params=_cparams(("arbitrary",), 40),
        name="dispatch",
    )(sub_used, h2, pos, seg["first_row"], seg["last_row"], seg["first_chunk_col"])


def _experts_kernel(tstart_ref, tcount_ref, texp_ref, insrc_ref, outdst_ref,
                    s_hbm, wg_ref, wu_ref, wd_ref, o_hbm,
                    xb0, xb1, xb2, ob0, ob1, ob2, wgub, wdb, insem, outsem):
    xbufs = (xb0, xb1, xb2)
    obufs = (ob0, ob1, ob2)
    e = pl.program_id(0)
    n_tiles = tstart_ref[N_EXPERTS]
    t0 = tstart_ref[e]

    def slot_base(t):
        te = texp_ref[t]
        j = jnp.minimum(t - tstart_ref[te], EXP_TILES - 1)
        return te * EXP_SLOTS + j * TILE_CHUNKS

    def in_copy(src_chunk, buf, i, sem):
        return pltpu.make_async_copy(s_hbm.at[src_chunk], xbufs[buf].at[i], sem)

    def out_copy(dst_chunk, buf, i, sem):
        return pltpu.make_async_copy(obufs[buf].at[i], o_hbm.at[dst_chunk], sem)

    def start_in(t, buf):
        base = slot_base(t)
        for i in range(TILE_CHUNKS):
            in_copy(insrc_ref[base + i], buf, i, insem.at[buf]).start()

    def start_out(t, buf, real):
        base = slot_base(jnp.maximum(t, 0))
        for i in range(TILE_CHUNKS):
            dst = jnp.where(real, outdst_ref[base + i], TRASH_CHUNK + buf * TILE_CHUNKS + i)
            out_copy(dst, buf, i, outsem.at[buf]).start()

    def wait_in(sem_idx):
        for i in range(TILE_CHUNKS):
            in_copy(0, 0, i, insem.at[sem_idx]).wait()

    def wait_out(sem_idx):
        for i in range(TILE_CHUNKS):
            out_copy(0, 0, i, outsem.at[sem_idx]).wait()

    def on_buffer(buf_dyn, fn):
        for k in range(N_BUF):
            pl.when(buf_dyn == k)(functools.partial(fn, k))

    wgub[:, :EXPERT_FF] = wg_ref[0].astype(BF16)
    wgub[:, EXPERT_FF:] = wu_ref[0].astype(BF16)
    wdb[...] = wd_ref[0].astype(BF16)

    @pl.when(e == 0)
    def _():
        for ob in obufs:
            ob[...] = jnp.zeros(ob.shape, BF16)
        start_out(0, 0, False)
        start_out(0, 1, False)
        start_in(0, 0)
        start_in(1, 1)

    def tile_body(j, carry):
        t = t0 + j

        def run(k):
            nxt = (k + 2) % N_BUF
            wait_in(k)
            wait_out(k)
            start_in(t + 2, nxt)
            start_out(t - 1, nxt, t >= 1)
            x = xbufs[k][...].reshape(TILE_ROWS, D_MODEL)
            au = jnp.dot(x, wgub[...], preferred_element_type=F32)
            a, u = au[:, :EXPERT_FF], au[:, EXPERT_FF:]
            hid = (a * jax.nn.sigmoid(a) * u).astype(BF16)
            out = jnp.dot(hid, wdb[...], preferred_element_type=F32).astype(BF16)
            obufs[k][...] = out.reshape(TILE_CHUNKS, CHUNK, D_MODEL)

        on_buffer(t % N_BUF, run)
        return carry

    lax.fori_loop(0, tcount_ref[e], tile_body, 0)

    @pl.when(e == N_EXPERTS - 1)
    def _():
        on_buffer((n_tiles + 2) % N_BUF, lambda k: start_out(n_tiles - 1, k, n_tiles >= 1))
        wait_in(n_tiles % N_BUF)
        wait_in((n_tiles + 1) % N_BUF)
        for k in range(N_BUF):
            wait_out(k)


def _experts(tile_start, tile_count, tile_expert, in_src, out_dst, sorted_rows, wg, wu, wd):
    wspec = lambda shape: pl.BlockSpec(shape, lambda e, *_: (e, 0, 0))
    grid_spec = pltpu.PrefetchScalarGridSpec(
        num_scalar_prefetch=5,
        grid=(N_EXPERTS,),
        in_specs=[pl.BlockSpec(memory_space=pl.ANY),
                  wspec((1, D_MODEL, EXPERT_FF)),
                  wspec((1, D_MODEL, EXPERT_FF)),
                  wspec((1, EXPERT_FF, D_MODEL))],
        out_specs=pl.BlockSpec(memory_space=pl.ANY),
        scratch_shapes=[pltpu.VMEM((TILE_CHUNKS, CHUNK, D_MODEL), BF16)] * (2 * N_BUF) + [
                        pltpu.VMEM((D_MODEL, 2 * EXPERT_FF), BF16),
                        pltpu.VMEM((EXPERT_FF, D_MODEL), BF16),
                        pltpu.SemaphoreType.DMA((N_BUF,)),
                        pltpu.SemaphoreType.DMA((N_BUF,))])
    return pl.pallas_call(
        _experts_kernel,
        grid_spec=grid_spec,
        out_shape=jax.ShapeDtypeStruct(sorted_rows.shape, BF16),
        input_output_aliases={5: 0},
        compiler_params=_cparams(("arbitrary",), 32),
        name="experts",
    )(tile_start, tile_count, tile_expert, in_src, out_dst, sorted_rows, wg, wu, wd)


def _combine_kernel(sub_ref, o_ref, post_ref, gate_ref, first_ref, last_ref, h_ref, x1_ref, mods_ref,
                    sg_ref, su_ref, sd_ref, fg_ref, yp_ref, ys_ref, acc_ref):
    i = pl.program_id(0)
    n_sub = sub_ref[i]

    def weights_of(sb):
        rows = (lax.broadcasted_iota(jnp.int32, (LANE, SUB), 1) + sb * SUB).astype(F32)
        first = jnp.concatenate([first_ref[0]] * (SUB // LANE), axis=1)
        last = jnp.concatenate([last_ref[0]] * (SUB // LANE), axis=1)
        own = jnp.where(rows >= first, 1.0, 0.0) - jnp.where(rows >= last, 1.0, 0.0)
        start = jnp.sum(jnp.where(own > 0.0, first, 0.0), axis=0, keepdims=True)
        want = rows[0:1, :] + 1.0 - start
        ownb = own.astype(BF16)
        have = jnp.dot(post_ref[...], ownb, preferred_element_type=F32)
        own2 = jnp.concatenate([ownb[:N_EXPERTS], ownb[:N_EXPERTS]], axis=0)
        gate = jnp.dot(gate_ref[...], own2, preferred_element_type=F32)
        return jnp.where(have == want, gate, 0.0).astype(BF16)

    h = h_ref[...]
    a = jnp.dot(h, sg_ref[...], preferred_element_type=F32)
    u = jnp.dot(h, su_ref[...], preferred_element_type=F32)
    shared = jnp.dot((a * jax.nn.sigmoid(a) * u).astype(BF16), sd_ref[...],
                     preferred_element_type=F32)
    pw_main = jnp.concatenate([weights_of(sb) for sb in range(SUB_ALWAYS)], axis=1)
    acc_ref[...] = shared + jnp.dot(pw_main, o_ref[0:SUB_ALWAYS * SUB, :],
                                    preferred_element_type=F32)
    for sb in range(SUB_ALWAYS, N_SUB):
        @pl.when(sb < n_sub)
        def _():
            acc_ref[...] += jnp.dot(weights_of(sb), o_ref[sb * SUB:(sb + 1) * SUB, :],
                                    preferred_element_type=F32)

    m = mods_ref[0]
    y = _rms(x1_ref[...] + m[5:6] * acc_ref[...], fg_ref[...])

    @pl.when(i < NP_TILES)
    def _():
        yp_ref[...] = y

    @pl.when(i >= NP_TILES)
    def _():
        ys_ref[...] = y


def _combine(sub_used, expert_rows, pos_t, gate_t, seg, h2, x1, mods3, sg, su, sd, fg):
    tile = lambda i, *_: (i, 0)
    blk3 = lambda i, *_: (i, 0, 0)
    grid_spec = pltpu.PrefetchScalarGridSpec(
        num_scalar_prefetch=1,
        grid=(N_BLK,),
        in_specs=[pl.BlockSpec((BLK_ROWS, D_MODEL), tile),
                  pl.BlockSpec((TM, LANE), tile),
                  pl.BlockSpec((TM, LANE), tile),
                  pl.BlockSpec((1, LANE, LANE), blk3),
                  pl.BlockSpec((1, LANE, LANE), blk3),
                  pl.BlockSpec((TM, D_MODEL), tile),
                  pl.BlockSpec((TM, D_MODEL), tile),
                  pl.BlockSpec((1, N_MOD, D_MODEL), lambda i, *_: (_seg_of_tile(i), 0, 0)),
                  _const_spec((D_MODEL, EXPERT_FF)),
                  _const_spec((D_MODEL, EXPERT_FF)),
                  _const_spec((EXPERT_FF, D_MODEL)),
                  _const_spec((1, D_MODEL))],
        out_specs=[pl.BlockSpec((TM, D_MODEL), lambda i, *_: (jnp.minimum(i, NP_TILES - 1), 0)),
                   pl.BlockSpec((TM, D_MODEL), lambda i, *_: (jnp.maximum(i - NP_TILES, 0), 0))],
        scratch_shapes=[pltpu.VMEM((TM, D_MODEL), F32)])
    return pl.pallas_call(
        _combine_kernel,
        grid_spec=grid_spec,
        out_shape=[jax.ShapeDtypeStruct((N_PROMPT, D_MODEL), F32),
                   jax.ShapeDtypeStruct((N_SAMPLE, D_MODEL), F32)],
        compiler_params=_cparams(("arbitrary",), 40),
        name="combine",
    )(sub_used, expert_rows, pos_t, gate_t, seg["first_col"], seg["last_col"], h2, x1, mods3,
      sg, su, sd, fg)


_ROPE_SWAP = tuple(list(range(8, 16)) + list(range(0, 8)) + list(range(24, 32)) + list(range(16, 24)))


def _rope_tables():
    pos = jnp.arange(DEC_SEQ, dtype=jnp.int32)
    row = (pos // GRID_W).astype(F32)
    col = (pos % GRID_W).astype(F32)
    n_freq = QK_ROPE_DIM // 4
    inv_freq = ROPE_BASE ** (-jnp.arange(n_freq, dtype=F32) / n_freq)
    ar, ac = row[:, None] * inv_freq, col[:, None] * inv_freq
    cos32 = jnp.concatenate([jnp.cos(ar), jnp.cos(ar), jnp.cos(ac), jnp.cos(ac)], axis=1)
    sin32 = jnp.concatenate([-jnp.sin(ar), jnp.sin(ar), -jnp.sin(ac), jnp.sin(ac)], axis=1)
    cos32 = jnp.concatenate([jnp.ones((TM, QK_ROPE_DIM), F32), cos32], axis=0)
    sin32 = jnp.concatenate([jnp.zeros((TM, QK_ROPE_DIM), F32), sin32], axis=0)
    n = cos32.shape[0]
    ones = jnp.ones((n, QK_NOPE_DIM), F32)
    zeros = jnp.zeros((n, QK_NOPE_DIM), F32)
    tail = jnp.zeros((n, HEAD_PAD - QK_NOPE_DIM - QK_ROPE_DIM), F32)
    qc = jnp.concatenate([ones, cos32, tail], axis=1) * ATTN_SCALE
    qs = jnp.concatenate([zeros, sin32, tail], axis=1) * ATTN_SCALE
    kc = jnp.concatenate([zeros, cos32, tail], axis=1)
    ks = jnp.concatenate([zeros, sin32, tail], axis=1)
    return jnp.stack([qc, qs, kc, ks], axis=0)


def _pack_weights(w_in, w_uq, w_ukv):
    swap = jnp.array(_ROPE_SWAP, dtype=jnp.int32)
    s0, s1, s2, s3 = Q_LORA, Q_LORA + KV_LORA, Q_LORA + KV_LORA + QK_ROPE_DIM, \
        Q_LORA + KV_LORA + QK_ROPE_DIM + 2 * D_MODEL
    wkr = w_in[:, s1:s2]
    pad_l = jnp.zeros((D_MODEL, QK_NOPE_DIM), F32)
    pad_r = jnp.zeros((D_MODEL, HEAD_PAD - QK_NOPE_DIM - QK_ROPE_DIM), F32)
    win = jnp.concatenate([w_in[:, :s0], w_in[:, s0:s1],
                           pad_l, wkr, pad_r, pad_l, wkr[:, swap], pad_r,
                           w_in[:, s2:s3], w_in[:, s3:]], axis=1).astype(BF16)
    uq = w_uq.reshape(Q_LORA, N_HEADS, QK_NOPE_DIM + QK_ROPE_DIM)
    nope, rope = uq[:, :, :QK_NOPE_DIM], uq[:, :, QK_NOPE_DIM:]
    zt = jnp.zeros((Q_LORA, N_HEADS, HEAD_PAD - QK_NOPE_DIM - QK_ROPE_DIM), F32)
    wuqp = jnp.concatenate([nope, rope, zt], axis=2).reshape(Q_LORA, QK_COLS).astype(BF16)
    wuqs = jnp.concatenate([jnp.zeros_like(nope), rope[:, :, swap], zt], axis=2)
    wuqs = wuqs.reshape(Q_LORA, QK_COLS).astype(BF16)
    ukv = w_ukv.reshape(KV_LORA, N_HEADS, QK_NOPE_DIM + V_DIM)
    zk = jnp.zeros((KV_LORA, N_HEADS, HEAD_PAD - QK_NOPE_DIM), F32)
    wukp = jnp.concatenate([ukv[:, :, :QK_NOPE_DIM], zk], axis=2).reshape(KV_LORA, QK_COLS).astype(BF16)
    wuv = ukv[:, :, QK_NOPE_DIM:].reshape(KV_LORA, N_HEADS * V_DIM).astype(BF16)
    return win, wuqp, wuqs, wukp, wuv


def kernel(x_prompt, x_sample, cache_ckv, cache_krope, c, c_ctx, mod_w, mod_b, norm1_g, w_in, q_norm_g, w_uq, kv_norm_g, w_ukv, w_o_attn, conv_w, conv_b, conv_ln_g, conv_ln_b, w_pw2, w_out, norm2_g, router_w, router_bias, exp_w_gate, exp_w_up, exp_w_down, sh_w_gate, sh_w_up, sh_w_down, final_g):
    xp = x_prompt.reshape(N_PROMPT, D_MODEL)
    xs = x_sample.reshape(N_SAMPLE, D_MODEL)
    cond8 = jnp.concatenate([c_ctx[None, :], c, jnp.zeros((8 - 1 - DEC_BATCH, D_MODEL), F32)], axis=0)
    mods3 = _mods(cond8, mod_w[0], mod_b).reshape(8, N_MOD, D_MODEL)

    win, wuqp, wuqs, wukp, wuv = _pack_weights(w_in[0], w_uq[0], w_ukv[0])
    q, k, v, z, gt, ckv_new, kr_new = _inproj(
        xp, xs, mods3, norm1_g, q_norm_g, kv_norm_g, win, wuqp, wuqs, wukp, wuv, _rope_tables())

    krope_pad = jnp.pad(cache_krope[:, 0], ((0, 0), (0, 0),
                                            (QK_NOPE_DIM, HEAD_PAD - QK_NOPE_DIM - QK_ROPE_DIM)))
    kc, vc = _ctxkv(cache_ckv[:, 0], krope_pad, wukp, wuv)
    attn_p = _attn_prompt(q, k, v)
    attn_s = _attn_sample(q, k, v, kc, vc)

    rwt = router_w[0].T
    rwh = rwt.astype(BF16)
    rwl = (rwt - rwh.astype(F32)).astype(BF16)
    cw = conv_w[0].reshape(CONV_K, D_MODEL // LANE, LANE).transpose(1, 0, 2)
    x1, h2, gates_t = _mix(
        attn_p, attn_s, z, gt, xp, xs, mods3,
        w_o_attn[0].astype(BF16), w_pw2[0].astype(BF16), w_out[0].astype(BF16),
        cw, conv_b, conv_ln_g, conv_ln_b, norm2_g, rwh, rwl, router_bias.reshape(N_EXPERTS, 1))

    pos, cnt = _plan(gates_t)
    tile_start, tile_count, tile_expert, in_src, out_dst, sub_used, seg = _tile_tables(cnt)
    expert_rows = _experts(tile_start, tile_count, tile_expert, in_src, out_dst,
                           _dispatch(sub_used, h2, pos, seg).reshape(-1, CHUNK, D_MODEL),
                           exp_w_gate[0], exp_w_up[0], exp_w_down[0])
    gates = gates_t.T
    gates_hi = gates.astype(BF16)
    gates_lo = (gates - gates_hi.astype(F32)).astype(BF16)
    y_p, y_s = _combine(sub_used, expert_rows.reshape(-1, D_MODEL), pos.T,
                        jnp.concatenate([gates_hi, gates_lo], axis=1), seg, h2, x1, mods3,
                        sh_w_gate[0].astype(BF16), sh_w_up[0].astype(BF16),
                        sh_w_down[0].astype(BF16), final_g[None, :])
    return (y_p.reshape(BATCH, SEQ, D_MODEL),
            y_s.reshape(DEC_BATCH, DEC_SEQ, D_MODEL),
            ckv_new.reshape(BATCH, 1, SEQ, KV_LORA),
            kr_new.reshape(BATCH, 1, SEQ, QK_ROPE_DIM))
```

```python
import functools
import math

import jax
import jax.numpy as jnp
import numpy as np
from jax import lax
from jax.experimental import pallas as pl
from jax.experimental.pallas import tpu as pltpu

F32 = jnp.float32
BF16 = jnp.bfloat16

D_MODEL = 1024
BATCH = 16
SEQ = 256
DEC_BATCH = 2
DEC_SEQ = 2048
PAST_LEN = 256
GRID_W = 64
N_HEADS = 16
QK_NOPE_DIM = 64
QK_ROPE_DIM = 32
V_DIM = 64
Q_LORA = 384
KV_LORA = 256
ROPE_BASE = 10000.0
ATTN_SCALE = 1.0 / math.sqrt(QK_NOPE_DIM + QK_ROPE_DIM)
CONV_K = 31
CONV_PAD = CONV_K // 2
N_EXPERTS = 64
TOP_K = 8
N_GROUPS = 8
GROUP_SIZE = N_EXPERTS // N_GROUPS
TOPK_GROUPS = 4
EXPERT_FF = 256
ROUTED_SCALE = 2.5
EPS = 1e-6
N_MOD = 6

LANE = 128
HEAD_PAD = LANE
QK_COLS = N_HEADS * HEAD_PAD
N_PROMPT = BATCH * SEQ
N_SAMPLE = DEC_BATCH * DEC_SEQ
N_TOK = N_PROMPT + N_SAMPLE

TM = 256
NP_TILES = N_PROMPT // TM
NS_TILES = N_SAMPLE // TM
N_TILES = NP_TILES + NS_TILES
TILES_PER_DEC = DEC_SEQ // TM
HALO = 16

S_Q = 0
S_KV = S_Q + Q_LORA
S_KR = S_KV + KV_LORA
S_CONV = S_KR + QK_ROPE_DIM
S_GATE = S_CONV + 2 * D_MODEL
PHASE = S_CONV % LANE
CONV_WIN = 2 * D_MODEL + LANE
GATE_WIN = 2 * D_MODEL + LANE
IN_COLS_PADDED = S_GATE - PHASE + GATE_WIN
V_COLS = N_HEADS * V_DIM

VMEM_CAP = 56 * 1024 * 1024


def _cparams(sem, vmem_mb):
    return pltpu.CompilerParams(dimension_semantics=sem,
                                vmem_limit_bytes=min(vmem_mb * 1024 * 1024, VMEM_CAP))


def _const_spec(shape):
    n = len(shape)
    return pl.BlockSpec(shape, lambda *_: (0,) * n, pipeline_mode=pl.Buffered(1))


def _seg_of_tile(i):
    return jnp.where(i < NP_TILES, 0, 1 + (i - NP_TILES) // TILES_PER_DEC)


def _rms(x, g):
    return x * lax.rsqrt(jnp.mean(x * x, axis=-1, keepdims=True) + EPS) * g


_NT = (((1,), (1,)), ((), ()))


def _mods_kernel(cond_ref, w_ref, b_ref, o_ref):
    c = cond_ref[...]
    s = c * jax.nn.sigmoid(c)
    o_ref[...] = jnp.dot(s.astype(BF16), w_ref[...].astype(BF16),
                         preferred_element_type=F32) + b_ref[...]


def _mods(cond8, mod_w, mod_b):
    n = N_MOD * D_MODEL
    bn = 512
    return pl.pallas_call(
        _mods_kernel,
        grid=(n // bn,),
        in_specs=[pl.BlockSpec((8, D_MODEL), lambda j: (0, 0)),
                  pl.BlockSpec((D_MODEL, bn), lambda j: (0, j)),
                  pl.BlockSpec((1, bn), lambda j: (0, j))],
        out_specs=pl.BlockSpec((8, bn), lambda j: (0, j)),
        out_shape=jax.ShapeDtypeStruct((8, n), F32),
        compiler_params=_cparams(("arbitrary",), 16),
        name="mods",
    )(cond8, mod_w, mod_b)


def _keys_t(wukp_t, ckv, krope):
    eye = (lax.broadcasted_iota(jnp.int32, (HEAD_PAD, HEAD_PAD), 0)
           == lax.broadcasted_iota(jnp.int32, (HEAD_PAD, HEAD_PAD), 1))
    rope_t = lax.dot_general(jnp.where(eye, 1.0, 0.0).astype(BF16), krope, _NT,
                             preferred_element_type=F32)
    nope_t = lax.dot_general(wukp_t, ckv, _NT, preferred_element_type=F32)
    return (nope_t + jnp.concatenate([rope_t] * N_HEADS, axis=0)).astype(BF16)


def _inproj_kernel(xp_ref, xs_ref, mods_ref, g1_ref, qg_ref, kvg_ref, win_ref, wkrs_ref, wuqp_ref,
                   wuqs_ref, wukp_ref, wuv_ref, tab_ref,
                   q_ref, k_ref, v_ref, z_ref, gt_ref, ckv_ref, kr_ref):
    i = pl.program_id(0)
    x = jnp.where(i < NP_TILES, xp_ref[...], xs_ref[...])
    m = mods_ref[0]
    h = _rms(x, g1_ref[...]) * (1.0 + m[1:2]) + m[0:1]
    hb = h.astype(BF16)

    def proj(c0, c1):
        return jnp.dot(hb, win_ref[:, c0:c1], preferred_element_type=F32)

    def lanes16(t):
        return jnp.concatenate([t] * N_HEADS, axis=1)

    qn = _rms(proj(S_Q, S_KV), qg_ref[...]).astype(BF16)
    q = (jnp.dot(qn, wuqp_ref[...], preferred_element_type=F32) * lanes16(tab_ref[0])
         + jnp.dot(qn, wuqs_ref[...], preferred_element_type=F32) * lanes16(tab_ref[1]))
    q_ref[...] = q.astype(BF16)

    ckv = _rms(proj(S_KV, S_KR), kvg_ref[...])
    ckvb = ckv.astype(BF16)
    u = proj(S_KR, S_KR + CONV_WIN)
    ua = u[:, :LANE]
    ub = jnp.dot(hb, wkrs_ref[...], preferred_element_type=F32)
    krot = ua * tab_ref[2] + ub * tab_ref[3]
    k_ref[...] = _keys_t(wukp_ref[...], ckvb, krot.astype(BF16))
    v_ref[...] = jnp.dot(ckvb, wuv_ref[...], preferred_element_type=F32).astype(BF16)

    @pl.when(i < NP_TILES)
    def _():
        ckv_ref[...] = ckv
        kr_ref[...] = ua[:, :QK_ROPE_DIM]

    glu = u[:, :D_MODEL + LANE] * jax.nn.sigmoid(u[:, D_MODEL:])
    z_ref[...] = pltpu.roll(glu, D_MODEL + LANE - PHASE, axis=1)[:, :D_MODEL].astype(BF16)
    gates = jax.nn.sigmoid(proj(S_GATE - PHASE, S_GATE - PHASE + GATE_WIN))
    gt_ref[...] = pltpu.roll(gates, GATE_WIN - PHASE, axis=1)[:, :2 * D_MODEL].astype(BF16)


def _inproj(xp, xs, mods3, g1, qg, kvg, win, wkrs, wuqp, wuqs, wukp, wuv, tab):
    tile = lambda i: (i, 0)
    ptile = lambda i: (jnp.minimum(i, NP_TILES - 1), 0)
    stile = lambda i: (jnp.maximum(i - NP_TILES, 0), 0)
    tab_blk = lambda i: (0, jnp.where(i < NP_TILES, 0, 1 + (i - NP_TILES) % TILES_PER_DEC), 0)
    return pl.pallas_call(
        _inproj_kernel,
        grid=(N_TILES,),
        in_specs=[pl.BlockSpec((TM, D_MODEL), ptile),
                  pl.BlockSpec((TM, D_MODEL), stile),
                  pl.BlockSpec((1, N_MOD, D_MODEL), lambda i: (_seg_of_tile(i), 0, 0)),
                  _const_spec((1, D_MODEL)),
                  _const_spec((1, Q_LORA)),
                  _const_spec((1, KV_LORA)),
                  _const_spec((D_MODEL, IN_COLS_PADDED)),
                  _const_spec((D_MODEL, LANE)),
                  _const_spec((Q_LORA, QK_COLS)),
                  _const_spec((Q_LORA, QK_COLS)),
                  _const_spec((QK_COLS, KV_LORA)),
                  _const_spec((KV_LORA, V_COLS)),
                  pl.BlockSpec((4, TM, LANE), tab_blk)],
        out_specs=[pl.BlockSpec((TM, QK_COLS), tile),
                   pl.BlockSpec((QK_COLS, TM), lambda i: (0, i)),
                   pl.BlockSpec((TM, V_COLS), tile),
                   pl.BlockSpec((TM, D_MODEL), tile),
                   pl.BlockSpec((TM, 2 * D_MODEL), tile),
                   pl.BlockSpec((TM, KV_LORA), ptile),
                   pl.BlockSpec((TM, QK_ROPE_DIM), ptile)],
        out_shape=[jax.ShapeDtypeStruct((N_TOK, QK_COLS), BF16),
                   jax.ShapeDtypeStruct((QK_COLS, N_TOK), BF16),
                   jax.ShapeDtypeStruct((N_TOK, V_COLS), BF16),
                   jax.ShapeDtypeStruct((N_TOK, D_MODEL), BF16),
                   jax.ShapeDtypeStruct((N_TOK, 2 * D_MODEL), BF16),
                   jax.ShapeDtypeStruct((N_PROMPT, KV_LORA), F32),
                   jax.ShapeDtypeStruct((N_PROMPT, QK_ROPE_DIM), F32)],
        compiler_params=_cparams(("arbitrary",), 52),
        name="inproj",
    )(xp, xs, mods3, g1, qg, kvg, win, wkrs, wuqp, wuqs, wukp, wuv, tab)


def _ctxkv_kernel(ckv_ref, kr_ref, wukp_ref, wuv_ref, k_ref, v_ref):
    c = ckv_ref[0].astype(BF16)
    k_ref[0] = _keys_t(wukp_ref[...], c, kr_ref[0].astype(BF16))
    v_ref[0] = jnp.dot(c, wuv_ref[...], preferred_element_type=F32).astype(BF16)


def _ctxkv(cache_ckv, krope_pad, wukp, wuv):
    return pl.pallas_call(
        _ctxkv_kernel,
        grid=(DEC_BATCH,),
        in_specs=[pl.BlockSpec((1, PAST_LEN, KV_LORA), lambda b: (b, 0, 0)),
                  pl.BlockSpec((1, PAST_LEN, LANE), lambda b: (b, 0, 0)),
                  _const_spec((QK_COLS, KV_LORA)),
                  _const_spec((KV_LORA, V_COLS))],
        out_specs=[pl.BlockSpec((1, QK_COLS, PAST_LEN), lambda b: (b, 0, 0)),
                   pl.BlockSpec((1, PAST_LEN, V_COLS), lambda b: (b, 0, 0))],
        out_shape=[jax.ShapeDtypeStruct((DEC_BATCH, QK_COLS, PAST_LEN), BF16),
                   jax.ShapeDtypeStruct((DEC_BATCH, PAST_LEN, V_COLS), BF16)],
        compiler_params=_cparams(("arbitrary",), 16),
        name="ctxkv",
    )(cache_ckv, krope_pad, wukp, wuv)


def _attn_kernel(*refs, has_ctx):
    if has_ctx:
        q_ref, k_ref, v_ref, kc_ref, vc_ref, o_ref = refs
    else:
        q_ref, k_ref, v_ref, o_ref = refs
    tq = q_ref.shape[0]
    lane = lax.broadcasted_iota(jnp.int32, (tq, LANE), 1)
    for j in range(N_HEADS // 2):
        pair = slice(j * LANE, (j + 1) * LANE)
        ps, pcs, dens = [], [], []
        for h in (2 * j, 2 * j + 1):
            hs = slice(h * HEAD_PAD, (h + 1) * HEAD_PAD)
            qh = q_ref[:, hs]
            s = jnp.dot(qh, k_ref[hs, :], preferred_element_type=F32)
            mx = jnp.max(s, axis=-1, keepdims=True)
            if has_ctx:
                sc = jnp.dot(qh, kc_ref[0, hs, :], preferred_element_type=F32)
                mx = jnp.maximum(mx, jnp.max(sc, axis=-1, keepdims=True))
            p = jnp.exp2(s - mx)
            den = jnp.sum(p, axis=-1, keepdims=True)
            ps.append(p.astype(BF16))
            if has_ctx:
                pc = jnp.exp2(sc - mx)
                den = den + jnp.sum(pc, axis=-1, keepdims=True)
                pcs.append(pc.astype(BF16))
            dens.append(den)
        o = jnp.dot(jnp.concatenate(ps, axis=0), v_ref[:, pair], preferred_element_type=F32)
        if has_ctx:
            o = o + jnp.dot(jnp.concatenate(pcs, axis=0), vc_ref[0, :, pair],
                            preferred_element_type=F32)
        o_ref[:, pair] = jnp.where(lane < V_DIM, o[:tq] / dens[0], o[tq:] / dens[1]).astype(BF16)


def _attn_prompt(q, k, v):
    blk = lambda b: (b, 0)
    return pl.pallas_call(
        functools.partial(_attn_kernel, has_ctx=False),
        grid=(BATCH,),
        in_specs=[pl.BlockSpec((SEQ, QK_COLS), blk),
                  pl.BlockSpec((QK_COLS, SEQ), lambda b: (0, b)),
                  pl.BlockSpec((SEQ, V_COLS), blk)],
        out_specs=pl.BlockSpec((SEQ, N_HEADS * V_DIM), blk),
        out_shape=jax.ShapeDtypeStruct((N_PROMPT, N_HEADS * V_DIM), BF16),
        compiler_params=_cparams(("arbitrary",), 24),
        name="attn_prompt",
    )(q, k, v)


ATT_TQ = 256


def _attn_sample(q, k, v, kc, vc):
    nq = DEC_SEQ // ATT_TQ
    q0 = N_PROMPT // ATT_TQ
    s0 = N_PROMPT // DEC_SEQ
    once = pl.Buffered(1)
    return pl.pallas_call(
        functools.partial(_attn_kernel, has_ctx=True),
        grid=(DEC_BATCH, nq),
        in_specs=[pl.BlockSpec((ATT_TQ, QK_COLS), lambda b, t: (q0 + b * nq + t, 0)),
                  pl.BlockSpec((QK_COLS, DEC_SEQ), lambda b, t: (0, s0 + b), pipeline_mode=once),
                  pl.BlockSpec((DEC_SEQ, V_COLS), lambda b, t: (s0 + b, 0), pipeline_mode=once),
                  pl.BlockSpec((1, QK_COLS, PAST_LEN), lambda b, t: (b, 0, 0), pipeline_mode=once),
                  pl.BlockSpec((1, PAST_LEN, V_COLS), lambda b, t: (b, 0, 0), pipeline_mode=once)],
        out_specs=pl.BlockSpec((ATT_TQ, N_HEADS * V_DIM), lambda b, t: (b * nq + t, 0)),
        out_shape=jax.ShapeDtypeStruct((N_SAMPLE, N_HEADS * V_DIM), BF16),
        compiler_params=_cparams(("arbitrary", "arbitrary"), 54),
        name="attn_sample",
    )(q, k, v, kc, vc)


def _topk_rows(v, k, n):
    rows = lax.broadcasted_iota(jnp.int32, v.shape, 0).astype(F32)
    sel = jnp.zeros(v.shape, F32)
    for _ in range(k):
        mx = jnp.max(v, axis=0, keepdims=True)
        first = jnp.min(jnp.where(v == mx, rows, float(n)), axis=0, keepdims=True)
        hit = rows == first
        sel = jnp.where(hit, 1.0, sel)
        v = jnp.where(hit, -jnp.inf, v)
    return sel > 0.0


def _route(logits_t, bias):
    scores = jax.nn.sigmoid(logits_t)
    sel = scores + bias
    gscore = []
    for g in range(N_GROUPS):
        blk = sel[g * GROUP_SIZE:(g + 1) * GROUP_SIZE]
        top2 = _topk_rows(blk, 2, GROUP_SIZE)
        gscore.append(jnp.sum(jnp.where(top2, blk, 0.0), axis=0, keepdims=True))
    masked = []
    for g in range(N_GROUPS):
        beaten = jnp.zeros(gscore[g].shape, F32)
        for o in range(N_GROUPS):
            if o < g:
                beaten = beaten + jnp.where(gscore[o] >= gscore[g], 1.0, 0.0)
            elif o > g:
                beaten = beaten + jnp.where(gscore[o] > gscore[g], 1.0, 0.0)
        keep = beaten < float(TOPK_GROUPS)
        masked.append(jnp.where(keep, sel[g * GROUP_SIZE:(g + 1) * GROUP_SIZE], -jnp.inf))
    chosen = _topk_rows(jnp.concatenate(masked, axis=0), TOP_K, N_EXPERTS)
    w = jnp.where(chosen, scores, 0.0)
    return w / jnp.sum(w, axis=0, keepdims=True) * ROUTED_SCALE


def _mix_kernel(ap_ref, as_ref, z_ref, zp_ref, zn_ref, gt_ref, xp_ref, xs_ref, mods_ref,
                wo_ref, wpw_ref, wout_ref, cw_ref, cb_ref, lng_ref, lnb_ref, g2_ref,
                rwh_ref, rwl_ref, rb_ref,
                x1_ref, h2_ref, gates_ref, win_ref, cz_ref):
    i = pl.program_id(0)
    is_prompt = i < NP_TILES
    pos = (i - NP_TILES) % TILES_PER_DEC
    first = jnp.logical_or(is_prompt, pos == 0)
    last = jnp.logical_or(is_prompt, pos == TILES_PER_DEC - 1)

    nchunk = D_MODEL // LANE
    zprev = jnp.where(first, 0.0, zp_ref[...].astype(F32))
    znext = jnp.where(last, 0.0, zn_ref[...].astype(F32))
    zc = z_ref[...].astype(F32)
    for c in range(nchunk):
        cs = slice(c * LANE, (c + 1) * LANE)
        win_ref[c, 0:HALO, :] = zprev[:, cs]
        win_ref[c, HALO:HALO + TM, :] = zc[:, cs]
        win_ref[c, HALO + TM:, :] = znext[:, cs]

    def conv_chunk(c, carry):
        w = cw_ref[c]
        acc = jnp.zeros((TM, LANE), F32)
        for kk in range(CONV_K):
            off = HALO - CONV_PAD + kk
            acc = acc + win_ref[c, off:off + TM, :] * w[kk:kk + 1, :]
        cz_ref[c] = acc
        return carry

    lax.fori_loop(0, nchunk, conv_chunk, 0)
    conv = jnp.concatenate([cz_ref[c] for c in range(nchunk)], axis=1) + cb_ref[...]
    mu = jnp.mean(conv, axis=-1, keepdims=True)
    cen = conv - mu
    var = jnp.mean(cen * cen, axis=-1, keepdims=True)
    ln = cen * lax.rsqrt(var + EPS) * lng_ref[...] + lnb_ref[...]
    act = (ln * jax.nn.sigmoid(ln)).astype(BF16)
    conv_out = jnp.dot(act, wpw_ref[...], preferred_element_type=F32)

    attn = jnp.where(is_prompt, ap_ref[...], as_ref[...])
    attn_out = jnp.dot(attn, wo_ref[...], preferred_element_type=F32)
    gt = gt_ref[...].astype(F32)
    merged = gt[:, :D_MODEL] * attn_out + gt[:, D_MODEL:] * conv_out
    mix = jnp.dot(merged.astype(BF16), wout_ref[...], preferred_element_type=F32)

    m = mods_ref[0]
    x = jnp.where(is_prompt, xp_ref[...], xs_ref[...])
    x1 = x + m[2:3] * mix
    x1_ref[...] = x1
    h2 = _rms(x1, g2_ref[...]) * (1.0 + m[4:5]) + m[3:4]
    hh = h2.astype(BF16)
    h2_ref[...] = hh

    hl = (h2 - hh.astype(F32)).astype(BF16)
    rwh = rwh_ref[...]
    logits_t = (lax.dot_general(rwh, hh, _NT, preferred_element_type=F32)
                + lax.dot_general(rwh, hl, _NT, preferred_element_type=F32)
                + lax.dot_general(rwl_ref[...], hh, _NT, preferred_element_type=F32))
    gates_ref[...] = _route(logits_t, rb_ref[...])


def _mix(attn_p, attn_s, z, gt, xp, xs, mods3, wo, wpw, wout, cw, cb, lng, lnb, g2, rwh, rwl, rb):
    tile = lambda i: (i, 0)
    ptile = lambda i: (jnp.minimum(i, NP_TILES - 1), 0)
    stile = lambda i: (jnp.maximum(i - NP_TILES, 0), 0)
    hpt = TM // HALO
    nhb = N_TOK // HALO
    return pl.pallas_call(
        _mix_kernel,
        grid=(N_TILES,),
        in_specs=[pl.BlockSpec((TM, D_MODEL), ptile),
                  pl.BlockSpec((TM, D_MODEL), stile),
                  pl.BlockSpec((TM, D_MODEL), tile),
                  pl.BlockSpec((HALO, D_MODEL), lambda i: (jnp.maximum(i * hpt - 1, 0), 0)),
                  pl.BlockSpec((HALO, D_MODEL), lambda i: (jnp.minimum((i + 1) * hpt, nhb - 1), 0)),
                  pl.BlockSpec((TM, 2 * D_MODEL), tile),
                  pl.BlockSpec((TM, D_MODEL), ptile),
                  pl.BlockSpec((TM, D_MODEL), stile),
                  pl.BlockSpec((1, N_MOD, D_MODEL), lambda i: (_seg_of_tile(i), 0, 0)),
                  _const_spec((D_MODEL, D_MODEL)),
                  _const_spec((D_MODEL, D_MODEL)),
                  _const_spec((D_MODEL, D_MODEL)),
                  _const_spec((D_MODEL // LANE, CONV_K, LANE)),
                  _const_spec((1, D_MODEL)),
                  _const_spec((1, D_MODEL)),
                  _const_spec((1, D_MODEL)),
                  _const_spec((1, D_MODEL)),
                  _const_spec((N_EXPERTS, D_MODEL)),
                  _const_spec((N_EXPERTS, D_MODEL)),
                  _const_spec((N_EXPERTS, 1))],
        out_specs=[pl.BlockSpec((TM, D_MODEL), tile),
                   pl.BlockSpec((TM, D_MODEL), tile),
                   pl.BlockSpec((N_EXPERTS, TM), lambda i: (0, i))],
        out_shape=[jax.ShapeDtypeStruct((N_TOK, D_MODEL), F32),
                   jax.ShapeDtypeStruct((N_TOK, D_MODEL), BF16),
                   jax.ShapeDtypeStruct((N_EXPERTS, N_TOK), F32)],
        scratch_shapes=[pltpu.VMEM((D_MODEL // LANE, TM + 2 * HALO, LANE), F32),
                        pltpu.VMEM((D_MODEL // LANE, TM, LANE), F32)],
        compiler_params=_cparams(("arbitrary",), 40),
        name="mix",
    )(attn_p, attn_s, z, z, z, gt, xp, xs, mods3, wo, wpw, wout, cw, cb, lng, lnb, g2, rwh, rwl, rb)


CHUNK = 16
BLK_ROWS = TM * TOP_K + N_EXPERTS * CHUNK
BLK_CHUNKS = BLK_ROWS // CHUNK
SUB = 256
N_SUB = BLK_ROWS // SUB
SUB_ALWAYS = TM * TOP_K // SUB
N_BLK = N_TOK // TM
TILE_CHUNKS = 32
TILE_ROWS = TILE_CHUNKS * CHUNK
MAX_CHUNKS = N_TOK * TOP_K // CHUNK + N_BLK * N_EXPERTS
MAX_TILES = MAX_CHUNKS // TILE_CHUNKS + N_EXPERTS
EXP_TILES = N_TOK // TILE_ROWS + 3
EXP_SLOTS = EXP_TILES * TILE_CHUNKS
N_BUF = 3
ZERO_CHUNK = N_BLK * BLK_CHUNKS
TRASH_CHUNK = ZERO_CHUNK + 1


def _plan_kernel(g_ref, pos_ref, cnt_ref):
    g = g_ref[...]
    picked = jnp.where(g > 0.0, 1.0, 0.0)
    r = lax.broadcasted_iota(jnp.int32, (TM, TM), 0)
    c = lax.broadcasted_iota(jnp.int32, (TM, TM), 1)
    upto = jnp.where(r <= c, 1.0, 0.0).astype(BF16)
    pos = picked * jnp.dot(picked.astype(BF16), upto, preferred_element_type=F32)
    pos_ref[0:N_EXPERTS, :] = pos.astype(BF16)
    pos_ref[N_EXPERTS:, :] = jnp.zeros((LANE - N_EXPERTS, TM), BF16)
    n = jnp.sum(picked, axis=1, keepdims=True)
    nchunk = jnp.floor((n + float(CHUNK - 1)) * (1.0 / CHUNK))
    cnt_ref[0] = jnp.broadcast_to(nchunk, (N_EXPERTS, LANE))


def _plan(gates_t):
    return pl.pallas_call(
        _plan_kernel,
        grid=(N_BLK,),
        in_specs=[pl.BlockSpec((N_EXPERTS, TM), lambda b: (0, b))],
        out_specs=[pl.BlockSpec((LANE, TM), lambda b: (0, b)),
                   pl.BlockSpec((1, N_EXPERTS, LANE), lambda b: (b, 0, 0))],
        out_shape=[jax.ShapeDtypeStruct((LANE, N_TOK), BF16),
                   jax.ShapeDtypeStruct((N_BLK, N_EXPERTS, LANE), F32)],
        compiler_params=_cparams(("arbitrary",), 16),
        name="plan",
    )(gates_t)


def _tile_tables(cnt):
    c = cnt[:, :, 0].astype(jnp.int32)
    seg_chunk = jnp.cumsum(c, axis=1) - c
    ct = c.T
    seg_q = jnp.cumsum(ct, axis=1) - ct
    tiles_e = (ct.sum(axis=1) + TILE_CHUNKS - 1) // TILE_CHUNKS
    tile_end = jnp.cumsum(tiles_e)
    tile_start = tile_end - tiles_e
    q = jnp.arange(EXP_SLOTS, dtype=jnp.int32)
    inside = jnp.logical_and(q[None, :, None] >= seg_q[:, None, :],
                             q[None, :, None] < (seg_q + ct)[:, None, :])
    base = jnp.arange(N_BLK, dtype=jnp.int32)[None, :] * BLK_CHUNKS + seg_chunk.T - seg_q
    src = q[None, :] + jnp.sum(jnp.where(inside, base[:, None, :], 0), axis=2)
    valid = jnp.any(inside, axis=2)
    buf = (tile_start[:, None] + q[None, :] // TILE_CHUNKS) % N_BUF
    trash = TRASH_CHUNK + buf * TILE_CHUNKS + q[None, :] % TILE_CHUNKS
    in_src = jnp.where(valid, src, ZERO_CHUNK).astype(jnp.int32).reshape(-1)
    out_dst = jnp.where(valid, src, trash).astype(jnp.int32).reshape(-1)
    t = jnp.arange(MAX_TILES + N_BUF, dtype=jnp.int32)
    tile_expert = jnp.minimum(jnp.sum((t[:, None] >= tile_end[None, :]).astype(jnp.int32), axis=1),
                              N_EXPERTS - 1)
    tile_start = jnp.concatenate([tile_start, tile_end[-1:]]).astype(jnp.int32)
    sub_used = (jnp.sum(c, axis=1) * CHUNK + SUB - 1) // SUB
    pad = jnp.full((N_BLK, LANE - N_EXPERTS), float(2 * BLK_ROWS), F32)
    first = jnp.concatenate([(seg_chunk * CHUNK).astype(F32), pad], axis=1)
    last = jnp.concatenate([((seg_chunk + c) * CHUNK).astype(F32), pad], axis=1)
    first_chunk = jnp.concatenate([seg_chunk.astype(F32), jnp.zeros_like(pad)], axis=1)
    seg = dict(
        first_row=jnp.broadcast_to(first[:, None, :], (N_BLK, 8, LANE)),
        last_row=jnp.broadcast_to(last[:, None, :], (N_BLK, 8, LANE)),
        first_col=jnp.broadcast_to(first[:, :, None], (N_BLK, LANE, LANE)),
        last_col=jnp.broadcast_to(last[:, :, None], (N_BLK, LANE, LANE)),
        first_chunk_col=jnp.broadcast_to(first_chunk[:, :, None], (N_BLK, LANE, LANE)).astype(BF16))
    return (tile_start, tiles_e.astype(jnp.int32), tile_expert, in_src, out_dst,
            sub_used.astype(jnp.int32), seg)


def _dispatch_kernel(sub_ref, h_ref, pos_ref, first_ref, last_ref, fchunk_ref, s_ref):
    b = pl.program_id(0)
    n_sub = jnp.where(b < N_BLK, sub_ref[jnp.minimum(b, N_BLK - 1)], 0)

    def fill(sb):
        rows = (lax.broadcasted_iota(jnp.int32, (SUB, LANE), 0) + sb * SUB).astype(F32)
        own = (jnp.where(rows >= first_ref[0, 0:1, :], 1.0, 0.0)
               - jnp.where(rows >= last_ref[0, 0:1, :], 1.0, 0.0)).astype(BF16)
        want = rows + 1.0 - float(CHUNK) * jnp.dot(own, fchunk_ref[0], preferred_element_type=F32)
        have = jnp.dot(own, pos_ref[...], preferred_element_type=F32)
        p = jnp.where(have == jnp.concatenate([want] * (TM // LANE), axis=1), 1.0, 0.0)
        s_ref[sb * SUB:(sb + 1) * SUB, :] = jnp.dot(
            p.astype(BF16), h_ref[...], preferred_element_type=F32).astype(BF16)

    def clear(sb):
        s_ref[sb * SUB:(sb + 1) * SUB, :] = jnp.zeros((SUB, D_MODEL), BF16)

    @pl.when(b < N_BLK)
    def _():
        for sb in range(SUB_ALWAYS):
            fill(sb)

    @pl.when(b == N_BLK)
    def _():
        for sb in range(SUB_ALWAYS):
            clear(sb)

    for sb in range(SUB_ALWAYS, N_SUB):
        pl.when(sb < n_sub)(functools.partial(fill, sb))
        pl.when(sb >= n_sub)(functools.partial(clear, sb))


def _dispatch(sub_used, h2, pos, seg):
    last = N_BLK - 1
    blk3 = lambda b, *_: (jnp.minimum(b, last), 0, 0)
    grid_spec = pltpu.PrefetchScalarGridSpec(
        num_scalar_prefetch=1,
        grid=(N_BLK + 1,),
        in_specs=[pl.BlockSpec((TM, D_MODEL), lambda b, *_: (jnp.minimum(b, last), 0)),
                  pl.BlockSpec((LANE, TM), lambda b, *_: (0, jnp.minimum(b, last))),
                  pl.BlockSpec((1, 8, LANE), blk3),
                  pl.BlockSpec((1, 8, LANE), blk3),
                  pl.BlockSpec((1, LANE, LANE), blk3)],
        out_specs=pl.BlockSpec((BLK_ROWS, D_MODEL), lambda b, *_: (b, 0)))
    return pl.pallas_call(
        _dispatch_kernel,
        grid_spec=grid_spec,
        out_shape=jax.ShapeDtypeStruct(((N_BLK + 1) * BLK_ROWS, D_MODEL), BF16),
        compiler_params=_cparams(("arbitrary",), 40),
        name="dispatch",
    )(sub_used, h2, pos, seg["first_row"], seg["last_row"], seg["first_chunk_col"])


def _experts_kernel(tstart_ref, tcount_ref, texp_ref, insrc_ref, outdst_ref,
                    s_hbm, wg_ref, wu_ref, wd_ref, o_hbm,
                    xb0, xb1, xb2, ob0, ob1, ob2, wgub, wdb, insem, outsem):
    xbufs = (xb0, xb1, xb2)
    obufs = (ob0, ob1, ob2)
    e = pl.program_id(0)
    n_tiles = tstart_ref[N_EXPERTS]
    t0 = tstart_ref[e]

    def slot_base(t):
        te = texp_ref[t]
        j = jnp.minimum(t - tstart_ref[te], EXP_TILES - 1)
        return te * EXP_SLOTS + j * TILE_CHUNKS

    def in_copy(src_chunk, buf, i, sem):
        return pltpu.make_async_copy(s_hbm.at[src_chunk], xbufs[buf].at[i], sem)

    def out_copy(dst_chunk, buf, i, sem):
        return pltpu.make_async_copy(obufs[buf].at[i], o_hbm.at[dst_chunk], sem)

    def start_in(t, buf):
        base = slot_base(t)
        for i in range(TILE_CHUNKS):
            in_copy(insrc_ref[base + i], buf, i, insem.at[buf]).start()

    def start_out(t, buf, real):
        base = slot_base(jnp.maximum(t, 0))
        for i in range(TILE_CHUNKS):
            dst = jnp.where(real, outdst_ref[base + i], TRASH_CHUNK + buf * TILE_CHUNKS + i)
            out_copy(dst, buf, i, outsem.at[buf]).start()

    def wait_in(sem_idx):
        for i in range(TILE_CHUNKS):
            in_copy(0, 0, i, insem.at[sem_idx]).wait()

    def wait_out(sem_idx):
        for i in range(TILE_CHUNKS):
            out_copy(0, 0, i, outsem.at[sem_idx]).wait()

    def on_buffer(buf_dyn, fn):
        for k in range(N_BUF):
            pl.when(buf_dyn == k)(functools.partial(fn, k))

    wgub[:, :EXPERT_FF] = wg_ref[0].astype(BF16)
    wgub[:, EXPERT_FF:] = wu_ref[0].astype(BF16)
    wdb[...] = wd_ref[0].astype(BF16)

    @pl.when(e == 0)
    def _():
        for ob in obufs:
            ob[...] = jnp.zeros(ob.shape, BF16)
        start_out(0, 0, False)
        start_out(0, 1, False)
        start_in(0, 0)
        start_in(1, 1)

    def tile_body(j, carry):
        t = t0 + j

        def run(k):
            nxt = (k + 2) % N_BUF
            wait_in(k)
            wait_out(k)
            start_in(t + 2, nxt)
            start_out(t - 1, nxt, t >= 1)
            x = xbufs[k][...].reshape(TILE_ROWS, D_MODEL)
            au = jnp.dot(x, wgub[...], preferred_element_type=F32)
            a, u = au[:, :EXPERT_FF], au[:, EXPERT_FF:]
            hid = (a * jax.nn.sigmoid(a) * u).astype(BF16)
            out = jnp.dot(hid, wdb[...], preferred_element_type=F32).astype(BF16)
            obufs[k][...] = out.reshape(TILE_CHUNKS, CHUNK, D_MODEL)

        on_buffer(t % N_BUF, run)
        return carry

    lax.fori_loop(0, tcount_ref[e], tile_body, 0)

    @pl.when(e == N_EXPERTS - 1)
    def _():
        on_buffer((n_tiles + 2) % N_BUF, lambda k: start_out(n_tiles - 1, k, n_tiles >= 1))
        wait_in(n_tiles % N_BUF)
        wait_in((n_tiles + 1) % N_BUF)
        for k in range(N_BUF):
            wait_out(k)


def _experts(tile_start, tile_count, tile_expert, in_src, out_dst, sorted_rows, wg, wu, wd):
    wspec = lambda shape: pl.BlockSpec(shape, lambda e, *_: (e, 0, 0))
    grid_spec = pltpu.PrefetchScalarGridSpec(
        num_scalar_prefetch=5,
        grid=(N_EXPERTS,),
        in_specs=[pl.BlockSpec(memory_space=pl.ANY),
                  wspec((1, D_MODEL, EXPERT_FF)),
                  wspec((1, D_MODEL, EXPERT_FF)),
                  wspec((1, EXPERT_FF, D_MODEL))],
        out_specs=pl.BlockSpec(memory_space=pl.ANY),
        scratch_shapes=[pltpu.VMEM((TILE_CHUNKS, CHUNK, D_MODEL), BF16)] * (2 * N_BUF) + [
                        pltpu.VMEM((D_MODEL, 2 * EXPERT_FF), BF16),
                        pltpu.VMEM((EXPERT_FF, D_MODEL), BF16),
                        pltpu.SemaphoreType.DMA((N_BUF,)),
                        pltpu.SemaphoreType.DMA((N_BUF,))])
    return pl.pallas_call(
        _experts_kernel,
        grid_spec=grid_spec,
        out_shape=jax.ShapeDtypeStruct(sorted_rows.shape, BF16),
        input_output_aliases={5: 0},
        compiler_params=_cparams(("arbitrary",), 32),
        name="experts",
    )(tile_start, tile_count, tile_expert, in_src, out_dst, sorted_rows, wg, wu, wd)


def _combine_kernel(sub_ref, o_ref, post_ref, gate_ref, first_ref, last_ref, h_ref, x1_ref, mods_ref,
                    sg_ref, su_ref, sd_ref, fg_ref, yp_ref, ys_ref, acc_ref):
    i = pl.program_id(0)
    n_sub = sub_ref[i]

    def weights_of(sb):
        rows = (lax.broadcasted_iota(jnp.int32, (LANE, SUB), 1) + sb * SUB).astype(F32)
        first = jnp.concatenate([first_ref[0]] * (SUB // LANE), axis=1)
        last = jnp.concatenate([last_ref[0]] * (SUB // LANE), axis=1)
        own = jnp.where(rows >= first, 1.0, 0.0) - jnp.where(rows >= last, 1.0, 0.0)
        start = jnp.sum(jnp.where(own > 0.0, first, 0.0), axis=0, keepdims=True)
        want = rows[0:1, :] + 1.0 - start
        ownb = own.astype(BF16)
        have = jnp.dot(post_ref[...], ownb, preferred_element_type=F32)
        own2 = jnp.concatenate([ownb[:N_EXPERTS], ownb[:N_EXPERTS]], axis=0)
        gate = jnp.dot(gate_ref[...], own2, preferred_element_type=F32)
        return jnp.where(have == want, gate, 0.0).astype(BF16)

    h = h_ref[...]
    a = jnp.dot(h, sg_ref[...], preferred_element_type=F32)
    u = jnp.dot(h, su_ref[...], preferred_element_type=F32)
    shared = jnp.dot((a * jax.nn.sigmoid(a) * u).astype(BF16), sd_ref[...],
                     preferred_element_type=F32)
    pw_main = jnp.concatenate([weights_of(sb) for sb in range(SUB_ALWAYS)], axis=1)
    acc_ref[...] = shared + jnp.dot(pw_main, o_ref[0:SUB_ALWAYS * SUB, :],
                                    preferred_element_type=F32)
    for sb in range(SUB_ALWAYS, N_SUB):
        @pl.when(sb < n_sub)
        def _():
            acc_ref[...] += jnp.dot(weights_of(sb), o_ref[sb * SUB:(sb + 1) * SUB, :],
                                    preferred_element_type=F32)

    m = mods_ref[0]
    y = _rms(x1_ref[...] + m[5:6] * acc_ref[...], fg_ref[...])

    @pl.when(i < NP_TILES)
    def _():
        yp_ref[...] = y

    @pl.when(i >= NP_TILES)
    def _():
        ys_ref[...] = y


def _combine(sub_used, expert_rows, pos_t, gate_t, seg, h2, x1, mods3, sg, su, sd, fg):
    tile = lambda i, *_: (i, 0)
    blk3 = lambda i, *_: (i, 0, 0)
    grid_spec = pltpu.PrefetchScalarGridSpec(
        num_scalar_prefetch=1,
        grid=(N_BLK,),
        in_specs=[pl.BlockSpec((BLK_ROWS, D_MODEL), tile),
                  pl.BlockSpec((TM, LANE), tile),
                  pl.BlockSpec((TM, LANE), tile),
                  pl.BlockSpec((1, LANE, LANE), blk3),
                  pl.BlockSpec((1, LANE, LANE), blk3),
                  pl.BlockSpec((TM, D_MODEL), tile),
                  pl.BlockSpec((TM, D_MODEL), tile),
                  pl.BlockSpec((1, N_MOD, D_MODEL), lambda i, *_: (_seg_of_tile(i), 0, 0)),
                  _const_spec((D_MODEL, EXPERT_FF)),
                  _const_spec((D_MODEL, EXPERT_FF)),
                  _const_spec((EXPERT_FF, D_MODEL)),
                  _const_spec((1, D_MODEL))],
        out_specs=[pl.BlockSpec((TM, D_MODEL), lambda i, *_: (jnp.minimum(i, NP_TILES - 1), 0)),
                   pl.BlockSpec((TM, D_MODEL), lambda i, *_: (jnp.maximum(i - NP_TILES, 0), 0))],
        scratch_shapes=[pltpu.VMEM((TM, D_MODEL), F32)])
    return pl.pallas_call(
        _combine_kernel,
        grid_spec=grid_spec,
        out_shape=[jax.ShapeDtypeStruct((N_PROMPT, D_MODEL), F32),
                   jax.ShapeDtypeStruct((N_SAMPLE, D_MODEL), F32)],
        compiler_params=_cparams(("arbitrary",), 40),
        name="combine",
    )(sub_used, expert_rows, pos_t, gate_t, seg["first_col"], seg["last_col"], h2, x1, mods3,
      sg, su, sd, fg)


def _rope_tables():
    pos = np.arange(DEC_SEQ)
    row = (pos // GRID_W).astype(np.float64)
    col = (pos % GRID_W).astype(np.float64)
    n_freq = QK_ROPE_DIM // 4
    inv_freq = ROPE_BASE ** (-np.arange(n_freq, dtype=np.float64) / n_freq)
    ar, ac = row[:, None] * inv_freq, col[:, None] * inv_freq
    cos32 = np.concatenate([np.cos(ar), np.cos(ar), np.cos(ac), np.cos(ac)], axis=1)
    sin32 = np.concatenate([-np.sin(ar), np.sin(ar), -np.sin(ac), np.sin(ac)], axis=1)
    cos32 = np.concatenate([np.ones((TM, QK_ROPE_DIM)), cos32], axis=0)
    sin32 = np.concatenate([np.zeros((TM, QK_ROPE_DIM)), sin32], axis=0)
    n = cos32.shape[0]
    ones = np.ones((n, QK_NOPE_DIM))
    zeros = np.zeros((n, QK_NOPE_DIM))
    tail = np.zeros((n, HEAD_PAD - QK_NOPE_DIM - QK_ROPE_DIM))
    q_scale = ATTN_SCALE * math.log2(math.e)
    qc = np.concatenate([cos32, ones, tail], axis=1) * q_scale
    qs = np.concatenate([sin32, zeros, tail], axis=1) * q_scale
    kc = np.concatenate([cos32, zeros, tail], axis=1)
    ks = np.concatenate([sin32, zeros, tail], axis=1)
    return jnp.asarray(np.stack([qc, qs, kc, ks], axis=0).astype(np.float32))


def _swap_rope(w):
    q = QK_ROPE_DIM // 4
    return jnp.concatenate([w[..., q:2 * q], w[..., :q], w[..., 3 * q:], w[..., 2 * q:3 * q]], axis=-1)


def _pack_weights(w_in, w_uq, w_ukv):
    win = jnp.pad(w_in, ((0, 0), (0, IN_COLS_PADDED - w_in.shape[1]))).astype(BF16)
    wkrs = jnp.pad(_swap_rope(w_in[:, S_KR:S_CONV]), ((0, 0), (0, LANE - QK_ROPE_DIM))).astype(BF16)
    uq = w_uq.reshape(Q_LORA, N_HEADS, QK_NOPE_DIM + QK_ROPE_DIM)
    nope, rope = uq[:, :, :QK_NOPE_DIM], uq[:, :, QK_NOPE_DIM:]
    zt = jnp.zeros((Q_LORA, N_HEADS, HEAD_PAD - QK_NOPE_DIM - QK_ROPE_DIM), F32)
    wuqp = jnp.concatenate([rope, nope, zt], axis=2).reshape(Q_LORA, QK_COLS).astype(BF16)
    wuqs = jnp.concatenate([_swap_rope(rope), jnp.zeros_like(nope), zt], axis=2)
    wuqs = wuqs.reshape(Q_LORA, QK_COLS).astype(BF16)
    ukv = w_ukv.reshape(KV_LORA, N_HEADS, QK_NOPE_DIM + V_DIM)
    zr = jnp.zeros((KV_LORA, N_HEADS, QK_ROPE_DIM), F32)
    wukp = jnp.concatenate([zr, ukv[:, :, :QK_NOPE_DIM], zr], axis=2).reshape(KV_LORA, QK_COLS)
    wuv = ukv[:, :, QK_NOPE_DIM:].reshape(KV_LORA, V_COLS)
    return win, wkrs, wuqp, wuqs, wukp.T.astype(BF16), wuv.astype(BF16)


def kernel(x_prompt, x_sample, cache_ckv, cache_krope, c, c_ctx, mod_w, mod_b, norm1_g, w_in, q_norm_g, w_uq, kv_norm_g, w_ukv, w_o_attn, conv_w, conv_b, conv_ln_g, conv_ln_b, w_pw2, w_out, norm2_g, router_w, router_bias, exp_w_gate, exp_w_up, exp_w_down, sh_w_gate, sh_w_up, sh_w_down, final_g):
    xp = x_prompt.reshape(N_PROMPT, D_MODEL)
    xs = x_sample.reshape(N_SAMPLE, D_MODEL)
    cond8 = jnp.concatenate([c_ctx[None, :], c, jnp.zeros((8 - 1 - DEC_BATCH, D_MODEL), F32)], axis=0)
    mods3 = _mods(cond8, mod_w[0], mod_b).reshape(8, N_MOD, D_MODEL)

    win, wkrs, wuqp, wuqs, wukp, wuv = _pack_weights(w_in[0], w_uq[0], w_ukv[0])
    q, k, v, z, gt, ckv_new, kr_new = _inproj(
        xp, xs, mods3, norm1_g, q_norm_g, kv_norm_g, win, wkrs, wuqp, wuqs, wukp, wuv,
        _rope_tables())

    krope_pad = jnp.pad(cache_krope[:, 0], ((0, 0), (0, 0), (0, HEAD_PAD - QK_ROPE_DIM)))
    kc, vc = _ctxkv(cache_ckv[:, 0], krope_pad, wukp, wuv)
    attn_p = _attn_prompt(q, k, v)
    attn_s = _attn_sample(q, k, v, kc, vc)

    rwt = router_w[0].T
    rwh = rwt.astype(BF16)
    rwl = (rwt - rwh.astype(F32)).astype(BF16)
    cw = conv_w[0].reshape(CONV_K, D_MODEL // LANE, LANE).transpose(1, 0, 2)
    x1, h2, gates_t = _mix(
        attn_p, attn_s, z, gt, xp, xs, mods3,
        w_o_attn[0].astype(BF16), w_pw2[0].astype(BF16), w_out[0].astype(BF16),
        cw, conv_b, conv_ln_g, conv_ln_b, norm2_g, rwh, rwl, router_bias.reshape(N_EXPERTS, 1))

    pos, cnt = _plan(gates_t)
    tile_start, tile_count, tile_expert, in_src, out_dst, sub_used, seg = _tile_tables(cnt)
    expert_rows = _experts(tile_start, tile_count, tile_expert, in_src, out_dst,
                           _dispatch(sub_used, h2, pos, seg).reshape(-1, CHUNK, D_MODEL),
                           exp_w_gate[0], exp_w_up[0], exp_w_down[0])
    gates = gates_t.T
    gates_hi = gates.astype(BF16)
    gates_lo = (gates - gates_hi.astype(F32)).astype(BF16)
    y_p, y_s = _combine(sub_used, expert_rows.reshape(-1, D_MODEL), pos.T,
                        jnp.concatenate([gates_hi, gates_lo], axis=1), seg, h2, x1, mods3,
                        sh_w_gate[0].astype(BF16), sh_w_up[0].astype(BF16),
                        sh_w_down[0].astype(BF16), final_g[None, :])
    return (y_p.reshape(BATCH, SEQ, D_MODEL),
            y_s.reshape(DEC_BATCH, DEC_SEQ, D_MODEL),
            ckv_new.reshape(BATCH, 1, SEQ, KV_LORA),
            kr_new.reshape(BATCH, 1, SEQ, QK_ROPE_DIM))
```

```python
import functools
import math

import jax
import jax.numpy as jnp
import numpy as np
from jax import lax
from jax.experimental import pallas as pl
from jax.experimental.pallas import tpu as pltpu

F32 = jnp.float32
BF16 = jnp.bfloat16

D_MODEL = 1024
BATCH = 16
SEQ = 256
DEC_BATCH = 2
DEC_SEQ = 2048
PAST_LEN = 256
GRID_W = 64
N_HEADS = 16
QK_NOPE_DIM = 64
QK_ROPE_DIM = 32
V_DIM = 64
Q_LORA = 384
KV_LORA = 256
ROPE_BASE = 10000.0
ATTN_SCALE = 1.0 / math.sqrt(QK_NOPE_DIM + QK_ROPE_DIM)
CONV_K = 31
CONV_PAD = CONV_K // 2
N_EXPERTS = 64
TOP_K = 8
N_GROUPS = 8
GROUP_SIZE = N_EXPERTS // N_GROUPS
TOPK_GROUPS = 4
EXPERT_FF = 256
ROUTED_SCALE = 2.5
EPS = 1e-6
N_MOD = 6

LANE = 128
HEAD_PAD = LANE
QK_COLS = N_HEADS * HEAD_PAD
N_PROMPT = BATCH * SEQ
N_SAMPLE = DEC_BATCH * DEC_SEQ
N_TOK = N_PROMPT + N_SAMPLE

TM = 256
NP_TILES = N_PROMPT // TM
NS_TILES = N_SAMPLE // TM
N_TILES = NP_TILES + NS_TILES
TILES_PER_DEC = DEC_SEQ // TM
HALO = 16

S_Q = 0
S_KV = S_Q + Q_LORA
S_KR = S_KV + KV_LORA
S_CONV = S_KR + QK_ROPE_DIM
S_GATE = S_CONV + 2 * D_MODEL
PHASE = S_CONV % LANE
CONV_WIN = 2 * D_MODEL + LANE
GATE_WIN = 2 * D_MODEL + LANE
IN_COLS = S_GATE + 2 * D_MODEL
V_COLS = N_HEADS * V_DIM

VMEM_CAP = 56 * 1024 * 1024


def _cparams(sem, vmem_mb):
    return pltpu.CompilerParams(dimension_semantics=sem,
                                vmem_limit_bytes=min(vmem_mb * 1024 * 1024, VMEM_CAP))


def _const_spec(shape):
    n = len(shape)
    return pl.BlockSpec(shape, lambda *_: (0,) * n, pipeline_mode=pl.Buffered(1))


def _seg_of_tile(i):
    return jnp.where(i < NP_TILES, 0, 1 + (i - NP_TILES) // TILES_PER_DEC)


def _rms(x, g):
    return x * lax.rsqrt(jnp.mean(x * x, axis=-1, keepdims=True) + EPS) * g


_NT = (((1,), (1,)), ((), ()))


def _mods_kernel(cond_ref, w_ref, b_ref, o_ref):
    c = cond_ref[...]
    s = c * jax.nn.sigmoid(c)
    o_ref[...] = jnp.dot(s.astype(BF16), w_ref[...].astype(BF16),
                         preferred_element_type=F32) + b_ref[...]


def _mods(cond8, mod_w, mod_b):
    n = N_MOD * D_MODEL
    bn = 512
    return pl.pallas_call(
        _mods_kernel,
        grid=(n // bn,),
        in_specs=[pl.BlockSpec((8, D_MODEL), lambda j: (0, 0)),
                  pl.BlockSpec((D_MODEL, bn), lambda j: (0, j)),
                  pl.BlockSpec((1, bn), lambda j: (0, j))],
        out_specs=pl.BlockSpec((8, bn), lambda j: (0, j)),
        out_shape=jax.ShapeDtypeStruct((8, n), F32),
        compiler_params=_cparams(("arbitrary",), 16),
        name="mods",
    )(cond8, mod_w, mod_b)


def _keys_t(wukp_t, ckv, krope):
    eye = (lax.broadcasted_iota(jnp.int32, (HEAD_PAD, HEAD_PAD), 0)
           == lax.broadcasted_iota(jnp.int32, (HEAD_PAD, HEAD_PAD), 1))
    rope_t = lax.dot_general(jnp.where(eye, 1.0, 0.0).astype(BF16), krope, _NT,
                             preferred_element_type=F32)
    nope_t = lax.dot_general(wukp_t, ckv, _NT, preferred_element_type=F32)
    return (nope_t + jnp.concatenate([rope_t] * N_HEADS, axis=0)).astype(BF16)


def _rope_swap(x):
    n = x.shape[1]
    half = QK_ROPE_DIM // 4
    lane = lax.broadcasted_iota(jnp.int32, x.shape, 1)
    return jnp.where(lane % (2 * half) < half,
                     pltpu.roll(x, n - half, axis=1), pltpu.roll(x, half, axis=1))


def _inproj_kernel(xp_ref, xs_ref, mods_ref, g1_ref, qg_ref, kvg_ref, win_ref, wgt_ref, wuqp_ref,
                   wukp_ref, wuv_ref, tab_ref,
                   q_ref, k_ref, v_ref, z_ref, gt_ref, ckv_ref, kr_ref):
    i = pl.program_id(0)
    x = jnp.where(i < NP_TILES, xp_ref[...], xs_ref[...])
    m = mods_ref[0]
    h = _rms(x, g1_ref[...]) * (1.0 + m[1:2]) + m[0:1]
    hb = h.astype(BF16)

    def proj(c0, c1):
        return jnp.dot(hb, win_ref[:, c0:c1], preferred_element_type=F32)

    def lanes16(t):
        return jnp.concatenate([t] * N_HEADS, axis=1)

    qn = _rms(proj(S_Q, S_KV), qg_ref[...]).astype(BF16)
    q = jnp.dot(qn, wuqp_ref[...], preferred_element_type=F32)
    q_ref[...] = (q * lanes16(tab_ref[0]) + _rope_swap(q) * lanes16(tab_ref[1])).astype(BF16)

    ckv = _rms(proj(S_KV, S_KR), kvg_ref[...])
    ckvb = ckv.astype(BF16)
    u = proj(S_KR, S_KR + CONV_WIN)
    ua = u[:, :LANE]
    krot = pltpu.roll(ua * tab_ref[2] + _rope_swap(ua) * tab_ref[3], QK_NOPE_DIM, axis=1)
    k_ref[...] = _keys_t(wukp_ref[...], ckvb, krot.astype(BF16))
    v_ref[...] = jnp.dot(ckvb, wuv_ref[...], preferred_element_type=F32).astype(BF16)

    @pl.when(i < NP_TILES)
    def _():
        ckv_ref[...] = ckv
        kr_ref[...] = ua[:, :QK_ROPE_DIM]

    glu = u[:, :D_MODEL + LANE] * jax.nn.sigmoid(u[:, D_MODEL:])
    z_ref[...] = pltpu.roll(glu, D_MODEL + LANE - PHASE, axis=1)[:, :D_MODEL].astype(BF16)
    gates = jax.nn.sigmoid(jnp.concatenate(
        [proj(S_GATE - PHASE, S_GATE - PHASE + 2 * D_MODEL),
         jnp.dot(hb, wgt_ref[...], preferred_element_type=F32)], axis=1))
    gt_ref[...] = pltpu.roll(gates, GATE_WIN - PHASE, axis=1)[:, :2 * D_MODEL].astype(BF16)


def _inproj(xp, xs, mods3, g1, qg, kvg, win, wgt, wuqp, wukp, wuv, tab):
    tile = lambda i: (i, 0)
    ptile = lambda i: (jnp.minimum(i, NP_TILES - 1), 0)
    stile = lambda i: (jnp.maximum(i - NP_TILES, 0), 0)
    tab_blk = lambda i: (0, jnp.where(i < NP_TILES, 0, 1 + (i - NP_TILES) % TILES_PER_DEC), 0)
    return pl.pallas_call(
        _inproj_kernel,
        grid=(N_TILES,),
        in_specs=[pl.BlockSpec((TM, D_MODEL), ptile),
                  pl.BlockSpec((TM, D_MODEL), stile),
                  pl.BlockSpec((1, N_MOD, D_MODEL), lambda i: (_seg_of_tile(i), 0, 0)),
                  _const_spec((1, D_MODEL)),
                  _const_spec((1, Q_LORA)),
                  _const_spec((1, KV_LORA)),
                  _const_spec((D_MODEL, IN_COLS)),
                  _const_spec((D_MODEL, LANE)),
                  _const_spec((Q_LORA, QK_COLS)),
                  _const_spec((QK_COLS, KV_LORA)),
                  _const_spec((KV_LORA, V_COLS)),
                  pl.BlockSpec((4, TM, LANE), tab_blk)],
        out_specs=[pl.BlockSpec((TM, QK_COLS), tile),
                   pl.BlockSpec((QK_COLS, TM), lambda i: (0, i)),
                   pl.BlockSpec((TM, V_COLS), tile),
                   pl.BlockSpec((TM, D_MODEL), tile),
                   pl.BlockSpec((TM, 2 * D_MODEL), tile),
                   pl.BlockSpec((TM, KV_LORA), ptile),
                   pl.BlockSpec((TM, QK_ROPE_DIM), ptile)],
        out_shape=[jax.ShapeDtypeStruct((N_TOK, QK_COLS), BF16),
                   jax.ShapeDtypeStruct((QK_COLS, N_TOK), BF16),
                   jax.ShapeDtypeStruct((N_TOK, V_COLS), BF16),
                   jax.ShapeDtypeStruct((N_TOK, D_MODEL), BF16),
                   jax.ShapeDtypeStruct((N_TOK, 2 * D_MODEL), BF16),
                   jax.ShapeDtypeStruct((N_PROMPT, KV_LORA), F32),
                   jax.ShapeDtypeStruct((N_PROMPT, QK_ROPE_DIM), F32)],
        compiler_params=_cparams(("arbitrary",), 52),
        name="inproj",
    )(xp, xs, mods3, g1, qg, kvg, win, wgt, wuqp, wukp, wuv, tab)


def _ctxkv_kernel(ckv_ref, kr_ref, wukp_ref, wuv_ref, k_ref, v_ref):
    c = ckv_ref[0].astype(BF16)
    k_ref[0] = _keys_t(wukp_ref[...], c, kr_ref[0].astype(BF16))
    v_ref[0] = jnp.dot(c, wuv_ref[...], preferred_element_type=F32).astype(BF16)


def _ctxkv(cache_ckv, krope_pad, wukp, wuv):
    return pl.pallas_call(
        _ctxkv_kernel,
        grid=(DEC_BATCH,),
        in_specs=[pl.BlockSpec((1, PAST_LEN, KV_LORA), lambda b: (b, 0, 0)),
                  pl.BlockSpec((1, PAST_LEN, LANE), lambda b: (b, 0, 0)),
                  _const_spec((QK_COLS, KV_LORA)),
                  _const_spec((KV_LORA, V_COLS))],
        out_specs=[pl.BlockSpec((1, QK_COLS, PAST_LEN), lambda b: (b, 0, 0)),
                   pl.BlockSpec((1, PAST_LEN, V_COLS), lambda b: (b, 0, 0))],
        out_shape=[jax.ShapeDtypeStruct((DEC_BATCH, QK_COLS, PAST_LEN), BF16),
                   jax.ShapeDtypeStruct((DEC_BATCH, PAST_LEN, V_COLS), BF16)],
        compiler_params=_cparams(("arbitrary",), 16),
        name="ctxkv",
    )(cache_ckv, krope_pad, wukp, wuv)


def _attn_kernel(*refs, has_ctx):
    if has_ctx:
        q_ref, k_ref, v_ref, kc_ref, vc_ref, o_ref = refs
    else:
        q_ref, k_ref, v_ref, o_ref = refs
    tq = q_ref.shape[0]
    lane = lax.broadcasted_iota(jnp.int32, (tq, LANE), 1)
    for j in range(N_HEADS // 2):
        pair = slice(j * LANE, (j + 1) * LANE)
        ps, pcs, dens = [], [], []
        for h in (2 * j, 2 * j + 1):
            hs = slice(h * HEAD_PAD, (h + 1) * HEAD_PAD)
            qh = q_ref[:, hs]
            s = jnp.dot(qh, k_ref[hs, :], preferred_element_type=F32)
            mx = jnp.max(s, axis=-1, keepdims=True)
            if has_ctx:
                sc = jnp.dot(qh, kc_ref[0, hs, :], preferred_element_type=F32)
                mx = jnp.maximum(mx, jnp.max(sc, axis=-1, keepdims=True))
            p = jnp.exp2(s - mx)
            den = jnp.sum(p, axis=-1, keepdims=True)
            ps.append(p.astype(BF16))
            if has_ctx:
                pc = jnp.exp2(sc - mx)
                den = den + jnp.sum(pc, axis=-1, keepdims=True)
                pcs.append(pc.astype(BF16))
            dens.append(den)
        o = jnp.dot(jnp.concatenate(ps, axis=0), v_ref[:, pair], preferred_element_type=F32)
        if has_ctx:
            o = o + jnp.dot(jnp.concatenate(pcs, axis=0), vc_ref[0, :, pair],
                            preferred_element_type=F32)
        o_ref[:, pair] = jnp.where(lane < V_DIM, o[:tq] / dens[0], o[tq:] / dens[1]).astype(BF16)


def _attn_prompt(q, k, v):
    blk = lambda b: (b, 0)
    return pl.pallas_call(
        functools.partial(_attn_kernel, has_ctx=False),
        grid=(BATCH,),
        in_specs=[pl.BlockSpec((SEQ, QK_COLS), blk),
                  pl.BlockSpec((QK_COLS, SEQ), lambda b: (0, b)),
                  pl.BlockSpec((SEQ, V_COLS), blk)],
        out_specs=pl.BlockSpec((SEQ, N_HEADS * V_DIM), blk),
        out_shape=jax.ShapeDtypeStruct((N_PROMPT, N_HEADS * V_DIM), BF16),
        compiler_params=_cparams(("arbitrary",), 24),
        name="attn_prompt",
    )(q, k, v)


ATT_TQ = 256


def _attn_sample(q, k, v, kc, vc):
    nq = DEC_SEQ // ATT_TQ
    q0 = N_PROMPT // ATT_TQ
    s0 = N_PROMPT // DEC_SEQ
    once = pl.Buffered(1)
    return pl.pallas_call(
        functools.partial(_attn_kernel, has_ctx=True),
        grid=(DEC_BATCH, nq),
        in_specs=[pl.BlockSpec((ATT_TQ, QK_COLS), lambda b, t: (q0 + b * nq + t, 0)),
                  pl.BlockSpec((QK_COLS, DEC_SEQ), lambda b, t: (0, s0 + b), pipeline_mode=once),
                  pl.BlockSpec((DEC_SEQ, V_COLS), lambda b, t: (s0 + b, 0), pipeline_mode=once),
                  pl.BlockSpec((1, QK_COLS, PAST_LEN), lambda b, t: (b, 0, 0), pipeline_mode=once),
                  pl.BlockSpec((1, PAST_LEN, V_COLS), lambda b, t: (b, 0, 0), pipeline_mode=once)],
        out_specs=pl.BlockSpec((ATT_TQ, N_HEADS * V_DIM), lambda b, t: (b * nq + t, 0)),
        out_shape=jax.ShapeDtypeStruct((N_SAMPLE, N_HEADS * V_DIM), BF16),
        compiler_params=_cparams(("arbitrary", "arbitrary"), 54),
        name="attn_sample",
    )(q, k, v, kc, vc)


def _topk_rows(v, k, n):
    rows = lax.broadcasted_iota(jnp.int32, v.shape, 0).astype(F32)
    sel = jnp.zeros(v.shape, F32)
    for _ in range(k):
        mx = jnp.max(v, axis=0, keepdims=True)
        first = jnp.min(jnp.where(v == mx, rows, float(n)), axis=0, keepdims=True)
        hit = rows == first
        sel = jnp.where(hit, 1.0, sel)
        v = jnp.where(hit, -jnp.inf, v)
    return sel > 0.0


def _route(logits_t, bias):
    scores = jax.nn.sigmoid(logits_t)
    sel = scores + bias
    gscore = []
    for g in range(N_GROUPS):
        blk = sel[g * GROUP_SIZE:(g + 1) * GROUP_SIZE]
        top2 = _topk_rows(blk, 2, GROUP_SIZE)
        gscore.append(jnp.sum(jnp.where(top2, blk, 0.0), axis=0, keepdims=True))
    masked = []
    for g in range(N_GROUPS):
        beaten = jnp.zeros(gscore[g].shape, F32)
        for o in range(N_GROUPS):
            if o < g:
                beaten = beaten + jnp.where(gscore[o] >= gscore[g], 1.0, 0.0)
            elif o > g:
                beaten = beaten + jnp.where(gscore[o] > gscore[g], 1.0, 0.0)
        keep = beaten < float(TOPK_GROUPS)
        masked.append(jnp.where(keep, sel[g * GROUP_SIZE:(g + 1) * GROUP_SIZE], -jnp.inf))
    chosen = _topk_rows(jnp.concatenate(masked, axis=0), TOP_K, N_EXPERTS)
    w = jnp.where(chosen, scores, 0.0)
    return w / jnp.sum(w, axis=0, keepdims=True) * ROUTED_SCALE


def _mix_kernel(ap_ref, as_ref, z_ref, zp_ref, zn_ref, gt_ref, xp_ref, xs_ref, mods_ref,
                wo_ref, wpw_ref, wout_ref, cw_ref, cb_ref, lng_ref, lnb_ref, g2_ref,
                rwh_ref, rwl_ref, rb_ref,
                x1_ref, h2_ref, gates_ref, win_ref, cz_ref):
    i = pl.program_id(0)
    is_prompt = i < NP_TILES
    pos = (i - NP_TILES) % TILES_PER_DEC
    first = jnp.logical_or(is_prompt, pos == 0)
    last = jnp.logical_or(is_prompt, pos == TILES_PER_DEC - 1)

    nchunk = D_MODEL // LANE
    zprev = jnp.where(first, 0.0, zp_ref[...].astype(F32))
    znext = jnp.where(last, 0.0, zn_ref[...].astype(F32))
    zc = z_ref[...].astype(F32)
    for c in range(nchunk):
        cs = slice(c * LANE, (c + 1) * LANE)
        win_ref[c, 0:HALO, :] = zprev[:, cs]
        win_ref[c, HALO:HALO + TM, :] = zc[:, cs]
        win_ref[c, HALO + TM:, :] = znext[:, cs]

    def conv_chunk(c, carry):
        w = cw_ref[c]
        acc = jnp.zeros((TM, LANE), F32)
        for kk in range(CONV_K):
            off = HALO - CONV_PAD + kk
            acc = acc + win_ref[c, off:off + TM, :] * w[kk:kk + 1, :]
        cz_ref[c] = acc
        return carry

    lax.fori_loop(0, nchunk, conv_chunk, 0)
    conv = jnp.concatenate([cz_ref[c] for c in range(nchunk)], axis=1) + cb_ref[...]
    mu = jnp.mean(conv, axis=-1, keepdims=True)
    cen = conv - mu
    var = jnp.mean(cen * cen, axis=-1, keepdims=True)
    ln = cen * lax.rsqrt(var + EPS) * lng_ref[...] + lnb_ref[...]
    act = (ln * jax.nn.sigmoid(ln)).astype(BF16)
    conv_out = jnp.dot(act, wpw_ref[...], preferred_element_type=F32)

    attn = jnp.where(is_prompt, ap_ref[...], as_ref[...])
    attn_out = jnp.dot(attn, wo_ref[...], preferred_element_type=F32)
    gt = gt_ref[...].astype(F32)
    merged = gt[:, :D_MODEL] * attn_out + gt[:, D_MODEL:] * conv_out
    mix = jnp.dot(merged.astype(BF16), wout_ref[...], preferred_element_type=F32)

    m = mods_ref[0]
    x = jnp.where(is_prompt, xp_ref[...], xs_ref[...])
    x1 = x + m[2:3] * mix
    x1_ref[...] = x1
    h2 = _rms(x1, g2_ref[...]) * (1.0 + m[4:5]) + m[3:4]
    hh = h2.astype(BF16)
    h2_ref[...] = hh

    hl = (h2 - hh.astype(F32)).astype(BF16)
    rwh = rwh_ref[...]
    logits_t = (lax.dot_general(rwh, hh, _NT, preferred_element_type=F32)
                + lax.dot_general(rwh, hl, _NT, preferred_element_type=F32)
                + lax.dot_general(rwl_ref[...], hh, _NT, preferred_element_type=F32))
    gates_ref[...] = _route(logits_t, rb_ref[...])


def _mix(attn_p, attn_s, z, gt, xp, xs, mods3, wo, wpw, wout, cw, cb, lng, lnb, g2, rwh, rwl, rb):
    tile = lambda i: (i, 0)
    ptile = lambda i: (jnp.minimum(i, NP_TILES - 1), 0)
    stile = lambda i: (jnp.maximum(i - NP_TILES, 0), 0)
    hpt = TM // HALO
    nhb = N_TOK // HALO
    return pl.pallas_call(
        _mix_kernel,
        grid=(N_TILES,),
        in_specs=[pl.BlockSpec((TM, D_MODEL), ptile),
                  pl.BlockSpec((TM, D_MODEL), stile),
                  pl.BlockSpec((TM, D_MODEL), tile),
                  pl.BlockSpec((HALO, D_MODEL), lambda i: (jnp.maximum(i * hpt - 1, 0), 0)),
                  pl.BlockSpec((HALO, D_MODEL), lambda i: (jnp.minimum((i + 1) * hpt, nhb - 1), 0)),
                  pl.BlockSpec((TM, 2 * D_MODEL), tile),
                  pl.BlockSpec((TM, D_MODEL), ptile),
                  pl.BlockSpec((TM, D_MODEL), stile),
                  pl.BlockSpec((1, N_MOD, D_MODEL), lambda i: (_seg_of_tile(i), 0, 0)),
                  _const_spec((D_MODEL, D_MODEL)),
                  _const_spec((D_MODEL, D_MODEL)),
                  _const_spec((D_MODEL, D_MODEL)),
                  _const_spec((D_MODEL // LANE, CONV_K, LANE)),
                  _const_spec((1, D_MODEL)),
                  _const_spec((1, D_MODEL)),
                  _const_spec((1, D_MODEL)),
                  _const_spec((1, D_MODEL)),
                  _const_spec((N_EXPERTS, D_MODEL)),
                  _const_spec((N_EXPERTS, D_MODEL)),
                  _const_spec((N_EXPERTS, 1))],
        out_specs=[pl.BlockSpec((TM, D_MODEL), tile),
                   pl.BlockSpec((TM, D_MODEL), tile),
                   pl.BlockSpec((N_EXPERTS, TM), lambda i: (0, i))],
        out_shape=[jax.ShapeDtypeStruct((N_TOK, D_MODEL), F32),
                   jax.ShapeDtypeStruct((N_TOK, D_MODEL), BF16),
                   jax.ShapeDtypeStruct((N_EXPERTS, N_TOK), F32)],
        scratch_shapes=[pltpu.VMEM((D_MODEL // LANE, TM + 2 * HALO, LANE), F32),
                        pltpu.VMEM((D_MODEL // LANE, TM, LANE), F32)],
        compiler_params=_cparams(("arbitrary",), 40),
        name="mix",
    )(attn_p, attn_s, z, z, z, gt, xp, xs, mods3, wo, wpw, wout, cw, cb, lng, lnb, g2, rwh, rwl, rb)


CHUNK = 16
BLK_ROWS = TM * TOP_K + N_EXPERTS * CHUNK
BLK_CHUNKS = BLK_ROWS // CHUNK
SUB = 256
N_SUB = BLK_ROWS // SUB
SUB_ALWAYS = TM * TOP_K // SUB
N_BLK = N_TOK // TM
TILE_CHUNKS = 32
TILE_ROWS = TILE_CHUNKS * CHUNK
MAX_CHUNKS = N_TOK * TOP_K // CHUNK + N_BLK * N_EXPERTS
MAX_TILES = MAX_CHUNKS // TILE_CHUNKS + N_EXPERTS
EXP_TILES = N_TOK // TILE_ROWS + 3
EXP_SLOTS = EXP_TILES * TILE_CHUNKS
N_BUF = 3
ZERO_CHUNK = N_BLK * BLK_CHUNKS
TRASH_CHUNK = ZERO_CHUNK + 1


def _plan_kernel(g_ref, pos_ref, cnt_ref):
    g = g_ref[...]
    picked = jnp.where(g > 0.0, 1.0, 0.0)
    r = lax.broadcasted_iota(jnp.int32, (TM, TM), 0)
    c = lax.broadcasted_iota(jnp.int32, (TM, TM), 1)
    upto = jnp.where(r <= c, 1.0, 0.0).astype(BF16)
    pos = picked * jnp.dot(picked.astype(BF16), upto, preferred_element_type=F32)
    pos_ref[0:N_EXPERTS, :] = pos.astype(BF16)
    pos_ref[N_EXPERTS:, :] = jnp.zeros((LANE - N_EXPERTS, TM), BF16)
    n = jnp.sum(picked, axis=1, keepdims=True)
    nchunk = jnp.floor((n + float(CHUNK - 1)) * (1.0 / CHUNK))
    cnt_ref[0] = jnp.broadcast_to(nchunk, (N_EXPERTS, LANE))


def _plan(gates_t):
    return pl.pallas_call(
        _plan_kernel,
        grid=(N_BLK,),
        in_specs=[pl.BlockSpec((N_EXPERTS, TM), lambda b: (0, b))],
        out_specs=[pl.BlockSpec((LANE, TM), lambda b: (0, b)),
                   pl.BlockSpec((1, N_EXPERTS, LANE), lambda b: (b, 0, 0))],
        out_shape=[jax.ShapeDtypeStruct((LANE, N_TOK), BF16),
                   jax.ShapeDtypeStruct((N_BLK, N_EXPERTS, LANE), F32)],
        compiler_params=_cparams(("arbitrary",), 16),
        name="plan",
    )(gates_t)


def _tile_tables(cnt):
    c = cnt[:, :, 0].astype(jnp.int32)
    seg_chunk = jnp.cumsum(c, axis=1) - c
    ct = c.T
    seg_q = jnp.cumsum(ct, axis=1) - ct
    tiles_e = (ct.sum(axis=1) + TILE_CHUNKS - 1) // TILE_CHUNKS
    tile_end = jnp.cumsum(tiles_e)
    tile_start = tile_end - tiles_e
    q = jnp.arange(EXP_SLOTS, dtype=jnp.int32)
    inside = jnp.logical_and(q[None, :, None] >= seg_q[:, None, :],
                             q[None, :, None] < (seg_q + ct)[:, None, :])
    base = jnp.arange(N_BLK, dtype=jnp.int32)[None, :] * BLK_CHUNKS + seg_chunk.T - seg_q
    src = q[None, :] + jnp.sum(jnp.where(inside, base[:, None, :], 0), axis=2)
    valid = jnp.any(inside, axis=2)
    buf = (tile_start[:, None] + q[None, :] // TILE_CHUNKS) % N_BUF
    trash = TRASH_CHUNK + buf * TILE_CHUNKS + q[None, :] % TILE_CHUNKS
    in_src = jnp.where(valid, src, ZERO_CHUNK).astype(jnp.int32).reshape(-1)
    out_dst = jnp.where(valid, src, trash).astype(jnp.int32).reshape(-1)
    t = jnp.arange(MAX_TILES + N_BUF, dtype=jnp.int32)
    tile_expert = jnp.minimum(jnp.sum((t[:, None] >= tile_end[None, :]).astype(jnp.int32), axis=1),
                              N_EXPERTS - 1)
    tile_start = jnp.concatenate([tile_start, tile_end[-1:]]).astype(jnp.int32)
    sub_used = (jnp.sum(c, axis=1) * CHUNK + SUB - 1) // SUB
    pad = jnp.full((N_BLK, LANE - N_EXPERTS), float(2 * BLK_ROWS), F32)
    first = jnp.concatenate([(seg_chunk * CHUNK).astype(F32), pad], axis=1)
    last = jnp.concatenate([((seg_chunk + c) * CHUNK).astype(F32), pad], axis=1)
    first_chunk = jnp.concatenate([seg_chunk.astype(F32), jnp.zeros_like(pad)], axis=1)
    seg = dict(
        first_row=jnp.broadcast_to(first[:, None, :], (N_BLK, 8, LANE)),
        last_row=jnp.broadcast_to(last[:, None, :], (N_BLK, 8, LANE)),
        first_col=jnp.broadcast_to(first[:, :, None], (N_BLK, LANE, LANE)),
        last_col=jnp.broadcast_to(last[:, :, None], (N_BLK, LANE, LANE)),
        first_chunk_col=jnp.broadcast_to(first_chunk[:, :, None], (N_BLK, LANE, LANE)).astype(BF16))
    return (tile_start, tiles_e.astype(jnp.int32), tile_expert, in_src, out_dst,
            sub_used.astype(jnp.int32), seg)


def _dispatch_kernel(sub_ref, h_ref, pos_ref, first_ref, last_ref, fchunk_ref, s_ref):
    b = pl.program_id(0)
    n_sub = jnp.where(b < N_BLK, sub_ref[jnp.minimum(b, N_BLK - 1)], 0)

    def fill(sb):
        rows = (lax.broadcasted_iota(jnp.int32, (SUB, LANE), 0) + sb * SUB).astype(F32)
        own = (jnp.where(rows >= first_ref[0, 0:1, :], 1.0, 0.0)
               - jnp.where(rows >= last_ref[0, 0:1, :], 1.0, 0.0)).astype(BF16)
        both = jnp.dot(own, jnp.concatenate([pos_ref[...], fchunk_ref[0]], axis=1),
                       preferred_element_type=F32)
        have = both[:, :TM]
        want = rows + 1.0 - float(CHUNK) * both[:, TM:]
        p = jnp.where(have == jnp.concatenate([want] * (TM // LANE), axis=1), 1.0, 0.0)
        s_ref[sb * SUB:(sb + 1) * SUB, :] = jnp.dot(
            p.astype(BF16), h_ref[...], preferred_element_type=F32).astype(BF16)

    def clear(sb):
        s_ref[sb * SUB:(sb + 1) * SUB, :] = jnp.zeros((SUB, D_MODEL), BF16)

    @pl.when(b < N_BLK)
    def _():
        for sb in range(SUB_ALWAYS):
            fill(sb)

    @pl.when(b == N_BLK)
    def _():
        for sb in range(SUB_ALWAYS):
            clear(sb)

    for sb in range(SUB_ALWAYS, N_SUB):
        pl.when(sb < n_sub)(functools.partial(fill, sb))
        pl.when(sb >= n_sub)(functools.partial(clear, sb))


def _dispatch(sub_used, h2, pos, seg):
    last = N_BLK - 1
    blk3 = lambda b, *_: (jnp.minimum(b, last), 0, 0)
    grid_spec = pltpu.PrefetchScalarGridSpec(
        num_scalar_prefetch=1,
        grid=(N_BLK + 1,),
        in_specs=[pl.BlockSpec((TM, D_MODEL), lambda b, *_: (jnp.minimum(b, last), 0)),
                  pl.BlockSpec((LANE, TM), lambda b, *_: (0, jnp.minimum(b, last))),
                  pl.BlockSpec((1, 8, LANE), blk3),
                  pl.BlockSpec((1, 8, LANE), blk3),
                  pl.BlockSpec((1, LANE, LANE), blk3)],
        out_specs=pl.BlockSpec((BLK_ROWS, D_MODEL), lambda b, *_: (b, 0)))
    return pl.pallas_call(
        _dispatch_kernel,
        grid_spec=grid_spec,
        out_shape=jax.ShapeDtypeStruct(((N_BLK + 1) * BLK_ROWS, D_MODEL), BF16),
        compiler_params=_cparams(("arbitrary",), 40),
        name="dispatch",
    )(sub_used, h2, pos, seg["first_row"], seg["last_row"], seg["first_chunk_col"])


def _experts_kernel(tstart_ref, tcount_ref, texp_ref, insrc_ref, outdst_ref,
                    s_hbm, wg_ref, wu_ref, wd_ref, o_hbm,
                    xb0, xb1, xb2, ob0, ob1, ob2, wgub, wdb, insem, outsem):
    xbufs = (xb0, xb1, xb2)
    obufs = (ob0, ob1, ob2)
    e = pl.program_id(0)
    n_tiles = tstart_ref[N_EXPERTS]
    t0 = tstart_ref[e]

    def slot_base(t):
        te = texp_ref[t]
        j = jnp.minimum(t - tstart_ref[te], EXP_TILES - 1)
        return te * EXP_SLOTS + j * TILE_CHUNKS

    def in_copy(src_chunk, buf, i, sem):
        return pltpu.make_async_copy(s_hbm.at[src_chunk], xbufs[buf].at[i], sem)

    def out_copy(dst_chunk, buf, i, sem):
        return pltpu.make_async_copy(obufs[buf].at[i], o_hbm.at[dst_chunk], sem)

    def start_in(t, buf):
        base = slot_base(t)
        for i in range(TILE_CHUNKS):
            in_copy(insrc_ref[base + i], buf, i, insem.at[buf]).start()

    def start_out(t, buf, real):
        base = slot_base(jnp.maximum(t, 0))
        for i in range(TILE_CHUNKS):
            dst = jnp.where(real, outdst_ref[base + i], TRASH_CHUNK + buf * TILE_CHUNKS + i)
            out_copy(dst, buf, i, outsem.at[buf]).start()

    def wait_in(sem_idx):
        for i in range(TILE_CHUNKS):
            in_copy(0, 0, i, insem.at[sem_idx]).wait()

    def wait_out(sem_idx):
        for i in range(TILE_CHUNKS):
            out_copy(0, 0, i, outsem.at[sem_idx]).wait()

    def on_buffer(buf_dyn, fn):
        for k in range(N_BUF):
            pl.when(buf_dyn == k)(functools.partial(fn, k))

    wgub[:, :EXPERT_FF] = wg_ref[0].astype(BF16)
    wgub[:, EXPERT_FF:] = wu_ref[0].astype(BF16)
    wdb[...] = wd_ref[0].astype(BF16)

    @pl.when(e == 0)
    def _():
        for ob in obufs:
            ob[...] = jnp.zeros(ob.shape, BF16)
        start_out(0, 0, False)
        start_out(0, 1, False)
        start_in(0, 0)
        start_in(1, 1)

    def tile_body(j, carry):
        t = t0 + j

        def run(k):
            nxt = (k + 2) % N_BUF
            wait_in(k)
            wait_out(k)
            start_in(t + 2, nxt)
            start_out(t - 1, nxt, t >= 1)
            x = xbufs[k][...].reshape(TILE_ROWS, D_MODEL)
            au = jnp.dot(x, wgub[...], preferred_element_type=F32)
            a, u = au[:, :EXPERT_FF], au[:, EXPERT_FF:]
            hid = (a * jax.nn.sigmoid(a) * u).astype(BF16)
            out = jnp.dot(hid, wdb[...], preferred_element_type=F32).astype(BF16)
            obufs[k][...] = out.reshape(TILE_CHUNKS, CHUNK, D_MODEL)

        on_buffer(t % N_BUF, run)
        return carry

    lax.fori_loop(0, tcount_ref[e], tile_body, 0)

    @pl.when(e == N_EXPERTS - 1)
    def _():
        on_buffer((n_tiles + 2) % N_BUF, lambda k: start_out(n_tiles - 1, k, n_tiles >= 1))
        wait_in(n_tiles % N_BUF)
        wait_in((n_tiles + 1) % N_BUF)
        for k in range(N_BUF):
            wait_out(k)


def _experts(tile_start, tile_count, tile_expert, in_src, out_dst, sorted_rows, wg, wu, wd):
    wspec = lambda shape: pl.BlockSpec(shape, lambda e, *_: (e, 0, 0))
    grid_spec = pltpu.PrefetchScalarGridSpec(
        num_scalar_prefetch=5,
        grid=(N_EXPERTS,),
        in_specs=[pl.BlockSpec(memory_space=pl.ANY),
                  wspec((1, D_MODEL, EXPERT_FF)),
                  wspec((1, D_MODEL, EXPERT_FF)),
                  wspec((1, EXPERT_FF, D_MODEL))],
        out_specs=pl.BlockSpec(memory_space=pl.ANY),
        scratch_shapes=[pltpu.VMEM((TILE_CHUNKS, CHUNK, D_MODEL), BF16)] * (2 * N_BUF) + [
                        pltpu.VMEM((D_MODEL, 2 * EXPERT_FF), BF16),
                        pltpu.VMEM((EXPERT_FF, D_MODEL), BF16),
                        pltpu.SemaphoreType.DMA((N_BUF,)),
                        pltpu.SemaphoreType.DMA((N_BUF,))])
    return pl.pallas_call(
        _experts_kernel,
        grid_spec=grid_spec,
        out_shape=jax.ShapeDtypeStruct(sorted_rows.shape, BF16),
        input_output_aliases={5: 0},
        compiler_params=_cparams(("arbitrary",), 32),
        name="experts",
    )(tile_start, tile_count, tile_expert, in_src, out_dst, sorted_rows, wg, wu, wd)


def _combine_kernel(sub_ref, o_ref, post_ref, gate_ref, first_ref, last_ref, h_ref, x1_ref, mods_ref,
                    sg_ref, su_ref, sd_ref, fg_ref, yp_ref, ys_ref, acc_ref):
    i = pl.program_id(0)
    n_sub = sub_ref[i]

    def weights_of(sb):
        rows = (lax.broadcasted_iota(jnp.int32, (LANE, SUB), 1) + sb * SUB).astype(F32)
        first = jnp.concatenate([first_ref[0]] * (SUB // LANE), axis=1)
        last = jnp.concatenate([last_ref[0]] * (SUB // LANE), axis=1)
        own = jnp.where(rows >= first, 1.0, 0.0) - jnp.where(rows >= last, 1.0, 0.0)
        start = jnp.sum(jnp.where(own > 0.0, first, 0.0), axis=0, keepdims=True)
        want = rows[0:1, :] + 1.0 - start
        ownb = own.astype(BF16)
        have = jnp.dot(post_ref[...], ownb, preferred_element_type=F32)
        own2 = jnp.concatenate([ownb[:N_EXPERTS], ownb[:N_EXPERTS]], axis=0)
        gate = jnp.dot(gate_ref[...], own2, preferred_element_type=F32)
        return jnp.where(have == want, gate, 0.0).astype(BF16)

    h = h_ref[...]
    a = jnp.dot(h, sg_ref[...], preferred_element_type=F32)
    u = jnp.dot(h, su_ref[...], preferred_element_type=F32)
    shared = jnp.dot((a * jax.nn.sigmoid(a) * u).astype(BF16), sd_ref[...],
                     preferred_element_type=F32)
    pw_main = jnp.concatenate([weights_of(sb) for sb in range(SUB_ALWAYS)], axis=1)
    acc_ref[...] = shared + jnp.dot(pw_main, o_ref[0:SUB_ALWAYS * SUB, :],
                                    preferred_element_type=F32)
    for sb in range(SUB_ALWAYS, N_SUB):
        @pl.when(sb < n_sub)
        def _():
            acc_ref[...] += jnp.dot(weights_of(sb), o_ref[sb * SUB:(sb + 1) * SUB, :],
                                    preferred_element_type=F32)

    m = mods_ref[0]
    y = _rms(x1_ref[...] + m[5:6] * acc_ref[...], fg_ref[...])

    @pl.when(i < NP_TILES)
    def _():
        yp_ref[...] = y

    @pl.when(i >= NP_TILES)
    def _():
        ys_ref[...] = y


def _combine(sub_used, expert_rows, pos_t, gate_t, seg, h2, x1, mods3, sg, su, sd, fg):
    tile = lambda i, *_: (i, 0)
    blk3 = lambda i, *_: (i, 0, 0)
    grid_spec = pltpu.PrefetchScalarGridSpec(
        num_scalar_prefetch=1,
        grid=(N_BLK,),
        in_specs=[pl.BlockSpec((BLK_ROWS, D_MODEL), tile),
                  pl.BlockSpec((TM, LANE), tile),
                  pl.BlockSpec((TM, LANE), tile),
                  pl.BlockSpec((1, LANE, LANE), blk3),
                  pl.BlockSpec((1, LANE, LANE), blk3),
                  pl.BlockSpec((TM, D_MODEL), tile),
                  pl.BlockSpec((TM, D_MODEL), tile),
                  pl.BlockSpec((1, N_MOD, D_MODEL), lambda i, *_: (_seg_of_tile(i), 0, 0)),
                  _const_spec((D_MODEL, EXPERT_FF)),
                  _const_spec((D_MODEL, EXPERT_FF)),
                  _const_spec((EXPERT_FF, D_MODEL)),
                  _const_spec((1, D_MODEL))],
        out_specs=[pl.BlockSpec((TM, D_MODEL), lambda i, *_: (jnp.minimum(i, NP_TILES - 1), 0)),
                   pl.BlockSpec((TM, D_MODEL), lambda i, *_: (jnp.maximum(i - NP_TILES, 0), 0))],
        scratch_shapes=[pltpu.VMEM((TM, D_MODEL), F32)])
    return pl.pallas_call(
        _combine_kernel,
        grid_spec=grid_spec,
        out_shape=[jax.ShapeDtypeStruct((N_PROMPT, D_MODEL), F32),
                   jax.ShapeDtypeStruct((N_SAMPLE, D_MODEL), F32)],
        compiler_params=_cparams(("arbitrary",), 40),
        name="combine",
    )(sub_used, expert_rows, pos_t, gate_t, seg["first_col"], seg["last_col"], h2, x1, mods3,
      sg, su, sd, fg)


def _rope_tables():
    pos = np.arange(DEC_SEQ)
    row = (pos // GRID_W).astype(np.float64)
    col = (pos % GRID_W).astype(np.float64)
    n_freq = QK_ROPE_DIM // 4
    inv_freq = ROPE_BASE ** (-np.arange(n_freq, dtype=np.float64) / n_freq)
    ar, ac = row[:, None] * inv_freq, col[:, None] * inv_freq
    cos32 = np.concatenate([np.cos(ar), np.cos(ar), np.cos(ac), np.cos(ac)], axis=1)
    sin32 = np.concatenate([-np.sin(ar), np.sin(ar), -np.sin(ac), np.sin(ac)], axis=1)
    cos32 = np.concatenate([np.ones((TM, QK_ROPE_DIM)), cos32], axis=0)
    sin32 = np.concatenate([np.zeros((TM, QK_ROPE_DIM)), sin32], axis=0)
    n = cos32.shape[0]
    ones = np.ones((n, QK_NOPE_DIM))
    zeros = np.zeros((n, QK_NOPE_DIM))
    tail = np.zeros((n, HEAD_PAD - QK_NOPE_DIM - QK_ROPE_DIM))
    q_scale = ATTN_SCALE * math.log2(math.e)
    rest = np.zeros((n, HEAD_PAD - QK_ROPE_DIM))
    qc = np.concatenate([ones, cos32, tail], axis=1) * q_scale
    qs = np.concatenate([zeros, sin32, tail], axis=1) * q_scale
    kc = np.concatenate([cos32, rest], axis=1)
    ks = np.concatenate([sin32, rest], axis=1)
    return jnp.asarray(np.stack([qc, qs, kc, ks], axis=0).astype(np.float32))


def _pack_weights(w_in, w_uq, w_ukv):
    win = w_in.astype(BF16)
    gate_tail = w_in[:, S_GATE - PHASE + 2 * D_MODEL:]
    wgt = jnp.pad(gate_tail, ((0, 0), (0, LANE - PHASE))).astype(BF16)
    uq = w_uq.reshape(Q_LORA, N_HEADS, QK_NOPE_DIM + QK_ROPE_DIM)
    pad_q = HEAD_PAD - QK_NOPE_DIM - QK_ROPE_DIM
    wuqp = jnp.pad(uq, ((0, 0), (0, 0), (0, pad_q))).reshape(Q_LORA, QK_COLS).astype(BF16)
    ukv = w_ukv.reshape(KV_LORA, N_HEADS, QK_NOPE_DIM + V_DIM)
    wukp = jnp.pad(ukv[:, :, :QK_NOPE_DIM], ((0, 0), (0, 0), (0, HEAD_PAD - QK_NOPE_DIM)))
    wukp_t = wukp.reshape(KV_LORA, QK_COLS).T.astype(BF16)
    wuv = ukv[:, :, QK_NOPE_DIM:].reshape(KV_LORA, V_COLS).astype(BF16)
    return win, wgt, wuqp, wukp_t, wuv


def kernel(x_prompt, x_sample, cache_ckv, cache_krope, c, c_ctx, mod_w, mod_b, norm1_g, w_in, q_norm_g, w_uq, kv_norm_g, w_ukv, w_o_attn, conv_w, conv_b, conv_ln_g, conv_ln_b, w_pw2, w_out, norm2_g, router_w, router_bias, exp_w_gate, exp_w_up, exp_w_down, sh_w_gate, sh_w_up, sh_w_down, final_g):
    xp = x_prompt.reshape(N_PROMPT, D_MODEL)
    xs = x_sample.reshape(N_SAMPLE, D_MODEL)
    cond8 = jnp.concatenate([c_ctx[None, :], c, jnp.zeros((8 - 1 - DEC_BATCH, D_MODEL), F32)], axis=0)
    mods3 = _mods(cond8, mod_w[0], mod_b).reshape(8, N_MOD, D_MODEL)

    win, wgt, wuqp, wukp, wuv = _pack_weights(w_in[0], w_uq[0], w_ukv[0])
    q, k, v, z, gt, ckv_new, kr_new = _inproj(
        xp, xs, mods3, norm1_g, q_norm_g, kv_norm_g, win, wgt, wuqp, wukp, wuv, _rope_tables())

    krope_pad = jnp.pad(cache_krope[:, 0], ((0, 0), (0, 0),
                                            (QK_NOPE_DIM, HEAD_PAD - QK_NOPE_DIM - QK_ROPE_DIM)))
    kc, vc = _ctxkv(cache_ckv[:, 0], krope_pad, wukp, wuv)
    attn_p = _attn_prompt(q, k, v)
    attn_s = _attn_sample(q, k, v, kc, vc)

    rwt = router_w[0].T
    rwh = rwt.astype(BF16)
    rwl = (rwt - rwh.astype(F32)).astype(BF16)
    cw = conv_w[0].reshape(CONV_K, D_MODEL // LANE, LANE).transpose(1, 0, 2)
    x1, h2, gates_t = _mix(
        attn_p, attn_s, z, gt, xp, xs, mods3,
        w_o_attn[0].astype(BF16), w_pw2[0].astype(BF16), w_out[0].astype(BF16),
        cw, conv_b, conv_ln_g, conv_ln_b, norm2_g, rwh, rwl, router_bias.reshape(N_EXPERTS, 1))

    pos, cnt = _plan(gates_t)
    tile_start, tile_count, tile_expert, in_src, out_dst, sub_used, seg = _tile_tables(cnt)
    expert_rows = _experts(tile_start, tile_count, tile_expert, in_src, out_dst,
                           _dispatch(sub_used, h2, pos, seg).reshape(-1, CHUNK, D_MODEL),
                           exp_w_gate[0], exp_w_up[0], exp_w_down[0])
    gates = gates_t.T
    gates_hi = gates.astype(BF16)
    gates_lo = (gates - gates_hi.astype(F32)).astype(BF16)
    y_p, y_s = _combine(sub_used, expert_rows.reshape(-1, D_MODEL), pos.T,
                        jnp.concatenate([gates_hi, gates_lo], axis=1), seg, h2, x1, mods3,
                        sh_w_gate[0].astype(BF16), sh_w_up[0].astype(BF16),
                        sh_w_down[0].astype(BF16), final_g[None, :])
    return (y_p.reshape(BATCH, SEQ, D_MODEL),
            y_s.reshape(DEC_BATCH, DEC_SEQ, D_MODEL),
            ckv_new.reshape(BATCH, 1, SEQ, KV_LORA),
            kr_new.reshape(BATCH, 1, SEQ, QK_ROPE_DIM))
```

```python
import functools
import math

import jax
import jax.numpy as jnp
import numpy as np
from jax import lax
from jax.experimental import pallas as pl
from jax.experimental.pallas import tpu as pltpu

F32 = jnp.float32
BF16 = jnp.bfloat16

D_MODEL = 1024
BATCH = 16
SEQ = 256
DEC_BATCH = 2
DEC_SEQ = 2048
PAST_LEN = 256
GRID_W = 64
N_HEADS = 16
QK_NOPE_DIM = 64
QK_ROPE_DIM = 32
V_DIM = 64
Q_LORA = 384
KV_LORA = 256
ROPE_BASE = 10000.0
ATTN_SCALE = 1.0 / math.sqrt(QK_NOPE_DIM + QK_ROPE_DIM)
CONV_K = 31
CONV_PAD = CONV_K // 2
N_EXPERTS = 64
TOP_K = 8
N_GROUPS = 8
GROUP_SIZE = N_EXPERTS // N_GROUPS
TOPK_GROUPS = 4
EXPERT_FF = 256
ROUTED_SCALE = 2.5
EPS = 1e-6
N_MOD = 6

LANE = 128
HEAD_PAD = LANE
QK_COLS = N_HEADS * HEAD_PAD
N_PROMPT = BATCH * SEQ
N_SAMPLE = DEC_BATCH * DEC_SEQ
N_TOK = N_PROMPT + N_SAMPLE

TM = 256
NP_TILES = N_PROMPT // TM
NS_TILES = N_SAMPLE // TM
N_TILES = NP_TILES + NS_TILES
TILES_PER_DEC = DEC_SEQ // TM
HALO = 16
S_Q = 0
S_KV = S_Q + Q_LORA
S_KR = S_KV + KV_LORA
S_CONV = S_KR + QK_ROPE_DIM
S_GATE = S_CONV + 2 * D_MODEL
PHASE = S_CONV % LANE
CONV_WIN = 2 * D_MODEL + LANE
GATE_WIN = 2 * D_MODEL + LANE
IN_COLS = S_GATE + 2 * D_MODEL
V_COLS = N_HEADS * V_DIM

VMEM_CAP = 56 * 1024 * 1024


def _cparams(sem, vmem_mb):
    return pltpu.CompilerParams(dimension_semantics=sem,
                                vmem_limit_bytes=min(vmem_mb * 1024 * 1024, VMEM_CAP))


def _const_spec(shape):
    n = len(shape)
    return pl.BlockSpec(shape, lambda *_: (0,) * n, pipeline_mode=pl.Buffered(1))


def _seg_of_tile(i):
    return jnp.where(i < NP_TILES, 0, 1 + (i - NP_TILES) // TILES_PER_DEC)


def _rms(x, g):
    return x * lax.rsqrt(jnp.mean(x * x, axis=-1, keepdims=True) + EPS) * g


_NT = (((1,), (1,)), ((), ()))


def _mods_kernel(cond_ref, w_ref, b_ref, o_ref):
    c = cond_ref[...]
    s = c * jax.nn.sigmoid(c)
    o_ref[...] = jnp.dot(s.astype(BF16), w_ref[...].astype(BF16),
                         preferred_element_type=F32) + b_ref[...]


def _mods(cond8, mod_w, mod_b):
    n = N_MOD * D_MODEL
    bn = 512
    return pl.pallas_call(
        _mods_kernel,
        grid=(n // bn,),
        in_specs=[pl.BlockSpec((8, D_MODEL), lambda j: (0, 0)),
                  pl.BlockSpec((D_MODEL, bn), lambda j: (0, j)),
                  pl.BlockSpec((1, bn), lambda j: (0, j))],
        out_specs=pl.BlockSpec((8, bn), lambda j: (0, j)),
        out_shape=jax.ShapeDtypeStruct((8, n), F32),
        compiler_params=_cparams(("arbitrary",), 16),
        name="mods",
    )(cond8, mod_w, mod_b)


def _keys_t(wukp_t, ckv, krope):
    eye = (lax.broadcasted_iota(jnp.int32, (HEAD_PAD, HEAD_PAD), 0)
           == lax.broadcasted_iota(jnp.int32, (HEAD_PAD, HEAD_PAD), 1))
    rope_t = lax.dot_general(jnp.where(eye, 1.0, 0.0).astype(BF16), krope, _NT,
                             preferred_element_type=F32)
    nope_t = lax.dot_general(wukp_t, ckv, _NT, preferred_element_type=F32)
    return (nope_t + jnp.concatenate([rope_t] * N_HEADS, axis=0)).astype(BF16)


def _rope_swap(x):
    n = x.shape[1]
    half = QK_ROPE_DIM // 4
    lane = lax.broadcasted_iota(jnp.int32, x.shape, 1)
    return jnp.where(lane % (2 * half) < half,
                     pltpu.roll(x, n - half, axis=1), pltpu.roll(x, half, axis=1))


def _inproj_kernel(xp_ref, xs_ref, mods_ref, g1_ref, qg_ref, kvg_ref, win_ref, wgt_ref, wuqp_ref,
                   wukp_ref, wuv_ref, tab_ref,
                   q_ref, k_ref, v_ref, z_ref, gt_ref, ckv_ref, kr_ref):
    i = pl.program_id(0)
    x = jnp.where(i < NP_TILES, xp_ref[...], xs_ref[...])
    m = mods_ref[0]
    h = _rms(x, g1_ref[...]) * (1.0 + m[1:2]) + m[0:1]
    hb = h.astype(BF16)

    def proj(c0, c1):
        return jnp.dot(hb, win_ref[:, c0:c1], preferred_element_type=F32)

    def lanes16(t):
        return jnp.concatenate([t] * N_HEADS, axis=1)

    qn = _rms(proj(S_Q, S_KV), qg_ref[...]).astype(BF16)
    q = jnp.dot(qn, wuqp_ref[...], preferred_element_type=F32)
    q_ref[...] = (q * lanes16(tab_ref[0]) + _rope_swap(q) * lanes16(tab_ref[1])).astype(BF16)

    ckv = _rms(proj(S_KV, S_KR), kvg_ref[...])
    ckvb = ckv.astype(BF16)
    u = proj(S_KR, S_KR + CONV_WIN)
    ua = u[:, :LANE]
    krot = pltpu.roll(ua * tab_ref[2] + _rope_swap(ua) * tab_ref[3], QK_NOPE_DIM, axis=1)
    k_ref[...] = _keys_t(wukp_ref[...], ckvb, krot.astype(BF16))
    v_ref[...] = jnp.dot(ckvb, wuv_ref[...], preferred_element_type=F32).astype(BF16)

    @pl.when(i < NP_TILES)
    def _():
        ckv_ref[...] = ckv
        kr_ref[...] = ua[:, :QK_ROPE_DIM]

    glu = u[:, :D_MODEL + LANE] * jax.nn.sigmoid(u[:, D_MODEL:])
    z_ref[...] = pltpu.roll(glu, D_MODEL + LANE - PHASE, axis=1)[:, :D_MODEL].astype(BF16)
    gates = jax.nn.sigmoid(jnp.concatenate(
        [proj(S_GATE - PHASE, S_GATE - PHASE + 2 * D_MODEL),
         jnp.dot(hb, wgt_ref[...], preferred_element_type=F32)], axis=1))
    gt_ref[...] = pltpu.roll(gates, GATE_WIN - PHASE, axis=1)[:, :2 * D_MODEL].astype(BF16)


def _inproj(xp, xs, mods3, g1, qg, kvg, win, wgt, wuqp, wukp, wuv, tab):
    tile = lambda i: (i, 0)
    ptile = lambda i: (jnp.minimum(i, NP_TILES - 1), 0)
    stile = lambda i: (jnp.maximum(i - NP_TILES, 0), 0)
    tab_blk = lambda i: (0, jnp.where(i < NP_TILES, 0, 1 + (i - NP_TILES) % TILES_PER_DEC), 0)
    return pl.pallas_call(
        _inproj_kernel,
        grid=(N_TILES,),
        in_specs=[pl.BlockSpec((TM, D_MODEL), ptile),
                  pl.BlockSpec((TM, D_MODEL), stile),
                  pl.BlockSpec((1, N_MOD, D_MODEL), lambda i: (_seg_of_tile(i), 0, 0)),
                  _const_spec((1, D_MODEL)),
                  _const_spec((1, Q_LORA)),
                  _const_spec((1, KV_LORA)),
                  _const_spec((D_MODEL, IN_COLS)),
                  _const_spec((D_MODEL, LANE)),
                  _const_spec((Q_LORA, QK_COLS)),
                  _const_spec((QK_COLS, KV_LORA)),
                  _const_spec((KV_LORA, V_COLS)),
                  pl.BlockSpec((4, TM, LANE), tab_blk)],
        out_specs=[pl.BlockSpec((TM, QK_COLS), tile),
                   pl.BlockSpec((QK_COLS, TM), lambda i: (0, i)),
                   pl.BlockSpec((TM, V_COLS), tile),
                   pl.BlockSpec((TM, D_MODEL), tile),
                   pl.BlockSpec((TM, 2 * D_MODEL), tile),
                   pl.BlockSpec((TM, KV_LORA), ptile),
                   pl.BlockSpec((TM, QK_ROPE_DIM), ptile)],
        out_shape=[jax.ShapeDtypeStruct((N_TOK, QK_COLS), BF16),
                   jax.ShapeDtypeStruct((QK_COLS, N_TOK), BF16),
                   jax.ShapeDtypeStruct((N_TOK, V_COLS), BF16),
                   jax.ShapeDtypeStruct((N_TOK, D_MODEL), BF16),
                   jax.ShapeDtypeStruct((N_TOK, 2 * D_MODEL), BF16),
                   jax.ShapeDtypeStruct((N_PROMPT, KV_LORA), F32),
                   jax.ShapeDtypeStruct((N_PROMPT, QK_ROPE_DIM), F32)],
        compiler_params=_cparams(("arbitrary",), 52),
        name="inproj",
    )(xp, xs, mods3, g1, qg, kvg, win, wgt, wuqp, wukp, wuv, tab)


def _ctxkv_kernel(ckv_ref, kr_ref, wukp_ref, wuv_ref, k_ref, v_ref):
    c = ckv_ref[0].astype(BF16)
    k_ref[0] = _keys_t(wukp_ref[...], c, kr_ref[0].astype(BF16))
    v_ref[0] = jnp.dot(c, wuv_ref[...], preferred_element_type=F32).astype(BF16)


def _ctxkv(cache_ckv, krope_pad, wukp, wuv):
    return pl.pallas_call(
        _ctxkv_kernel,
        grid=(DEC_BATCH,),
        in_specs=[pl.BlockSpec((1, PAST_LEN, KV_LORA), lambda b: (b, 0, 0)),
                  pl.BlockSpec((1, PAST_LEN, LANE), lambda b: (b, 0, 0)),
                  _const_spec((QK_COLS, KV_LORA)),
                  _const_spec((KV_LORA, V_COLS))],
        out_specs=[pl.BlockSpec((1, QK_COLS, PAST_LEN), lambda b: (b, 0, 0)),
                   pl.BlockSpec((1, PAST_LEN, V_COLS), lambda b: (b, 0, 0))],
        out_shape=[jax.ShapeDtypeStruct((DEC_BATCH, QK_COLS, PAST_LEN), BF16),
                   jax.ShapeDtypeStruct((DEC_BATCH, PAST_LEN, V_COLS), BF16)],
        compiler_params=_cparams(("arbitrary",), 16),
        name="ctxkv",
    )(cache_ckv, krope_pad, wukp, wuv)


def _attn_kernel(*refs, has_ctx):
    if has_ctx:
        q_ref, k_ref, v_ref, kc_ref, vc_ref, o_ref = refs
    else:
        q_ref, k_ref, v_ref, o_ref = refs
    tq = q_ref.shape[0]
    lane = lax.broadcasted_iota(jnp.int32, (tq, LANE), 1)
    for j in range(N_HEADS // 2):
        pair = slice(j * LANE, (j + 1) * LANE)
        ps, pcs, dens = [], [], []
        for h in (2 * j, 2 * j + 1):
            hs = slice(h * HEAD_PAD, (h + 1) * HEAD_PAD)
            qh = q_ref[:, hs]
            s = jnp.dot(qh, k_ref[hs, :], preferred_element_type=F32)
            mx = jnp.max(s, axis=-1, keepdims=True)
            if has_ctx:
                sc = jnp.dot(qh, kc_ref[0, hs, :], preferred_element_type=F32)
                mx = jnp.maximum(mx, jnp.max(sc, axis=-1, keepdims=True))
            p = jnp.exp2(s - mx)
            den = jnp.sum(p, axis=-1, keepdims=True)
            ps.append(p.astype(BF16))
            if has_ctx:
                pc = jnp.exp2(sc - mx)
                den = den + jnp.sum(pc, axis=-1, keepdims=True)
                pcs.append(pc.astype(BF16))
            dens.append(den)
        o = jnp.dot(jnp.concatenate(ps, axis=0), v_ref[:, pair], preferred_element_type=F32)
        if has_ctx:
            o = o + jnp.dot(jnp.concatenate(pcs, axis=0), vc_ref[0, :, pair],
                            preferred_element_type=F32)
        o_ref[:, pair] = jnp.where(lane < V_DIM, o[:tq] / dens[0], o[tq:] / dens[1]).astype(BF16)


def _attn_prompt(q, k, v):
    blk = lambda b: (b, 0)
    return pl.pallas_call(
        functools.partial(_attn_kernel, has_ctx=False),
        grid=(BATCH,),
        in_specs=[pl.BlockSpec((SEQ, QK_COLS), blk),
                  pl.BlockSpec((QK_COLS, SEQ), lambda b: (0, b)),
                  pl.BlockSpec((SEQ, V_COLS), blk)],
        out_specs=pl.BlockSpec((SEQ, N_HEADS * V_DIM), blk),
        out_shape=jax.ShapeDtypeStruct((N_PROMPT, N_HEADS * V_DIM), BF16),
        compiler_params=_cparams(("arbitrary",), 24),
        name="attn_prompt",
    )(q, k, v)


ATT_TQ = 256


def _attn_sample(q, k, v, kc, vc):
    nq = DEC_SEQ // ATT_TQ
    q0 = N_PROMPT // ATT_TQ
    s0 = N_PROMPT // DEC_SEQ
    once = pl.Buffered(1)
    return pl.pallas_call(
        functools.partial(_attn_kernel, has_ctx=True),
        grid=(DEC_BATCH, nq),
        in_specs=[pl.BlockSpec((ATT_TQ, QK_COLS), lambda b, t: (q0 + b * nq + t, 0)),
                  pl.BlockSpec((QK_COLS, DEC_SEQ), lambda b, t: (0, s0 + b), pipeline_mode=once),
                  pl.BlockSpec((DEC_SEQ, V_COLS), lambda b, t: (s0 + b, 0), pipeline_mode=once),
                  pl.BlockSpec((1, QK_COLS, PAST_LEN), lambda b, t: (b, 0, 0), pipeline_mode=once),
                  pl.BlockSpec((1, PAST_LEN, V_COLS), lambda b, t: (b, 0, 0), pipeline_mode=once)],
        out_specs=pl.BlockSpec((ATT_TQ, N_HEADS * V_DIM), lambda b, t: (b * nq + t, 0)),
        out_shape=jax.ShapeDtypeStruct((N_SAMPLE, N_HEADS * V_DIM), BF16),
        compiler_params=_cparams(("arbitrary", "arbitrary"), 54),
        name="attn_sample",
    )(q, k, v, kc, vc)


def _topk_rows(v, k, n):
    rows = lax.broadcasted_iota(jnp.int32, v.shape, 0).astype(F32)
    sel = jnp.zeros(v.shape, F32)
    for _ in range(k):
        mx = jnp.max(v, axis=0, keepdims=True)
        first = jnp.min(jnp.where(v == mx, rows, float(n)), axis=0, keepdims=True)
        hit = rows == first
        sel = jnp.where(hit, 1.0, sel)
        v = jnp.where(hit, -jnp.inf, v)
    return sel > 0.0


def _route(logits_t, bias):
    scores = jax.nn.sigmoid(logits_t)
    sel = scores + bias
    gscore = []
    for g in range(N_GROUPS):
        blk = sel[g * GROUP_SIZE:(g + 1) * GROUP_SIZE]
        top2 = _topk_rows(blk, 2, GROUP_SIZE)
        gscore.append(jnp.sum(jnp.where(top2, blk, 0.0), axis=0, keepdims=True))
    masked = []
    for g in range(N_GROUPS):
        beaten = jnp.zeros(gscore[g].shape, F32)
        for o in range(N_GROUPS):
            if o < g:
                beaten = beaten + jnp.where(gscore[o] >= gscore[g], 1.0, 0.0)
            elif o > g:
                beaten = beaten + jnp.where(gscore[o] > gscore[g], 1.0, 0.0)
        keep = beaten < float(TOPK_GROUPS)
        masked.append(jnp.where(keep, sel[g * GROUP_SIZE:(g + 1) * GROUP_SIZE], -jnp.inf))
    chosen = _topk_rows(jnp.concatenate(masked, axis=0), TOP_K, N_EXPERTS)
    w = jnp.where(chosen, scores, 0.0)
    return w / jnp.sum(w, axis=0, keepdims=True) * ROUTED_SCALE


def _mix_kernel(ap_ref, as_ref, z_ref, zp_ref, zn_ref, gt_ref, xp_ref, xs_ref, mods_ref,
                wo_ref, wpw_ref, wout_ref, cw_ref, cb_ref, lng_ref, lnb_ref, g2_ref,
                rwh_ref, rwl_ref, rb_ref,
                x1_ref, h2_ref, gates_ref, win_ref, cz_ref):
    i = pl.program_id(0)
    is_prompt = i < NP_TILES
    pos = (i - NP_TILES) % TILES_PER_DEC
    first = jnp.logical_or(is_prompt, pos == 0)
    last = jnp.logical_or(is_prompt, pos == TILES_PER_DEC - 1)

    nchunk = D_MODEL // LANE
    zprev = jnp.where(first, 0.0, zp_ref[...].astype(F32))
    znext = jnp.where(last, 0.0, zn_ref[...].astype(F32))
    zc = z_ref[...].astype(F32)
    for c in range(nchunk):
        cs = slice(c * LANE, (c + 1) * LANE)
        win_ref[c, 0:HALO, :] = zprev[:, cs]
        win_ref[c, HALO:HALO + TM, :] = zc[:, cs]
        win_ref[c, HALO + TM:, :] = znext[:, cs]

    def conv_chunk(c, carry):
        w = cw_ref[c]
        acc = jnp.zeros((TM, LANE), F32)
        for kk in range(CONV_K):
            off = HALO - CONV_PAD + kk
            acc = acc + win_ref[c, off:off + TM, :] * w[kk:kk + 1, :]
        cz_ref[c] = acc
        return carry

    lax.fori_loop(0, nchunk, conv_chunk, 0)

    m = mods_ref[0]
    rwh = rwh_ref[...]
    rows = [slice(hf * (TM // 2), (hf + 1) * (TM // 2)) for hf in range(2)]

    def conv_act(rs):
        conv = jnp.concatenate([cz_ref[c, rs, :] for c in range(nchunk)], axis=1) + cb_ref[...]
        mu = jnp.mean(conv, axis=-1, keepdims=True)
        cen = conv - mu
        var = jnp.mean(cen * cen, axis=-1, keepdims=True)
        ln = cen * lax.rsqrt(var + EPS) * lng_ref[...] + lnb_ref[...]
        return (ln * jax.nn.sigmoid(ln)).astype(BF16)

    def branches(rs, act):
        attn = jnp.where(is_prompt, ap_ref[rs, :], as_ref[rs, :])
        return (jnp.dot(attn, wo_ref[...], preferred_element_type=F32),
                jnp.dot(act, wpw_ref[...], preferred_element_type=F32))

    def merge(rs, outs):
        gt = gt_ref[rs, :].astype(F32)
        return (gt[:, :D_MODEL] * outs[0] + gt[:, D_MODEL:] * outs[1]).astype(BF16)

    def project(merged):
        return jnp.dot(merged, wout_ref[...], preferred_element_type=F32)

    def residual(rs, mix):
        x = jnp.where(is_prompt, xp_ref[rs, :], xs_ref[rs, :])
        x1 = x + m[2:3] * mix
        x1_ref[rs, :] = x1
        h2 = _rms(x1, g2_ref[...]) * (1.0 + m[4:5]) + m[3:4]
        hh = h2.astype(BF16)
        h2_ref[rs, :] = hh
        return hh, (h2 - hh.astype(F32)).astype(BF16)

    def route(rs, hs):
        hh, hl = hs
        logits_t = (lax.dot_general(rwh, hh, _NT, preferred_element_type=F32)
                    + lax.dot_general(rwh, hl, _NT, preferred_element_type=F32)
                    + lax.dot_general(rwl_ref[...], hh, _NT, preferred_element_type=F32))
        gates_ref[:, rs] = _route(logits_t, rb_ref[...])

    r0, r1 = rows
    act0 = conv_act(r0)
    out0 = branches(r0, act0)
    act1 = conv_act(r1)
    mrg0 = merge(r0, out0)
    out1 = branches(r1, act1)
    mix0 = project(mrg0)
    mrg1 = merge(r1, out1)
    hs0 = residual(r0, mix0)
    mix1 = project(mrg1)
    route(r0, hs0)
    hs1 = residual(r1, mix1)
    route(r1, hs1)


def _mix(attn_p, attn_s, z, gt, xp, xs, mods3, wo, wpw, wout, cw, cb, lng, lnb, g2, rwh, rwl, rb):
    tile = lambda i: (i, 0)
    ptile = lambda i: (jnp.minimum(i, NP_TILES - 1), 0)
    stile = lambda i: (jnp.maximum(i - NP_TILES, 0), 0)
    hpt = TM // HALO
    nhb = N_TOK // HALO
    return pl.pallas_call(
        _mix_kernel,
        grid=(N_TILES,),
        in_specs=[pl.BlockSpec((TM, D_MODEL), ptile),
                  pl.BlockSpec((TM, D_MODEL), stile),
                  pl.BlockSpec((TM, D_MODEL), tile),
                  pl.BlockSpec((HALO, D_MODEL), lambda i: (jnp.maximum(i * hpt - 1, 0), 0)),
                  pl.BlockSpec((HALO, D_MODEL), lambda i: (jnp.minimum((i + 1) * hpt, nhb - 1), 0)),
                  pl.BlockSpec((TM, 2 * D_MODEL), tile),
                  pl.BlockSpec((TM, D_MODEL), ptile),
                  pl.BlockSpec((TM, D_MODEL), stile),
                  pl.BlockSpec((1, N_MOD, D_MODEL), lambda i: (_seg_of_tile(i), 0, 0)),
                  _const_spec((D_MODEL, D_MODEL)),
                  _const_spec((D_MODEL, D_MODEL)),
                  _const_spec((D_MODEL, D_MODEL)),
                  _const_spec((D_MODEL // LANE, CONV_K, LANE)),
                  _const_spec((1, D_MODEL)),
                  _const_spec((1, D_MODEL)),
                  _const_spec((1, D_MODEL)),
                  _const_spec((1, D_MODEL)),
                  _const_spec((N_EXPERTS, D_MODEL)),
                  _const_spec((N_EXPERTS, D_MODEL)),
                  _const_spec((N_EXPERTS, 1))],
        out_specs=[pl.BlockSpec((TM, D_MODEL), tile),
                   pl.BlockSpec((TM, D_MODEL), tile),
                   pl.BlockSpec((N_EXPERTS, TM), lambda i: (0, i))],
        out_shape=[jax.ShapeDtypeStruct((N_TOK, D_MODEL), F32),
                   jax.ShapeDtypeStruct((N_TOK, D_MODEL), BF16),
                   jax.ShapeDtypeStruct((N_EXPERTS, N_TOK), F32)],
        scratch_shapes=[pltpu.VMEM((D_MODEL // LANE, TM + 2 * HALO, LANE), F32),
                        pltpu.VMEM((D_MODEL // LANE, TM, LANE), F32)],
        compiler_params=_cparams(("arbitrary",), 40),
        name="mix",
    )(attn_p, attn_s, z, z, z, gt, xp, xs, mods3, wo, wpw, wout, cw, cb, lng, lnb, g2, rwh, rwl, rb)


CHUNK = 16
BLK_ROWS = TM * TOP_K + N_EXPERTS * CHUNK
BLK_CHUNKS = BLK_ROWS // CHUNK
SUB = 256
N_SUB = BLK_ROWS // SUB
SUB_ALWAYS = TM * TOP_K // SUB
N_BLK = N_TOK // TM
TILE_CHUNKS = 32
TILE_ROWS = TILE_CHUNKS * CHUNK
MAX_CHUNKS = N_TOK * TOP_K // CHUNK + N_BLK * N_EXPERTS
MAX_TILES = MAX_CHUNKS // TILE_CHUNKS + N_EXPERTS
EXP_TILES = N_TOK // TILE_ROWS + 3
EXP_SLOTS = EXP_TILES * TILE_CHUNKS
N_BUF = 3
ZERO_CHUNK = N_BLK * BLK_CHUNKS
TRASH_CHUNK = ZERO_CHUNK + 1


def _plan_kernel(g_ref, pos_ref, cnt_ref):
    g = g_ref[...]
    picked = jnp.where(g > 0.0, 1.0, 0.0)
    r = lax.broadcasted_iota(jnp.int32, (TM, TM), 0)
    c = lax.broadcasted_iota(jnp.int32, (TM, TM), 1)
    upto = jnp.where(r <= c, 1.0, 0.0).astype(BF16)
    pos = picked * jnp.dot(picked.astype(BF16), upto, preferred_element_type=F32)
    pos_ref[0:N_EXPERTS, :] = pos.astype(BF16)
    pos_ref[N_EXPERTS:, :] = jnp.zeros((LANE - N_EXPERTS, TM), BF16)
    n = jnp.sum(picked, axis=1, keepdims=True)
    nchunk = jnp.floor((n + float(CHUNK - 1)) * (1.0 / CHUNK))
    cnt_ref[0] = jnp.broadcast_to(nchunk, (N_EXPERTS, LANE))


def _plan(gates_t):
    return pl.pallas_call(
        _plan_kernel,
        grid=(N_BLK,),
        in_specs=[pl.BlockSpec((N_EXPERTS, TM), lambda b: (0, b))],
        out_specs=[pl.BlockSpec((LANE, TM), lambda b: (0, b)),
                   pl.BlockSpec((1, N_EXPERTS, LANE), lambda b: (b, 0, 0))],
        out_shape=[jax.ShapeDtypeStruct((LANE, N_TOK), BF16),
                   jax.ShapeDtypeStruct((N_BLK, N_EXPERTS, LANE), F32)],
        compiler_params=_cparams(("arbitrary",), 16),
        name="plan",
    )(gates_t)


def _tile_tables(cnt):
    c = cnt[:, :, 0].astype(jnp.int32)
    seg_chunk = jnp.cumsum(c, axis=1) - c
    ct = c.T
    seg_q = jnp.cumsum(ct, axis=1) - ct
    tiles_e = (ct.sum(axis=1) + TILE_CHUNKS - 1) // TILE_CHUNKS
    tile_end = jnp.cumsum(tiles_e)
    tile_start = tile_end - tiles_e
    q = jnp.arange(EXP_SLOTS, dtype=jnp.int32)
    inside = jnp.logical_and(q[None, :, None] >= seg_q[:, None, :],
                             q[None, :, None] < (seg_q + ct)[:, None, :])
    base = jnp.arange(N_BLK, dtype=jnp.int32)[None, :] * BLK_CHUNKS + seg_chunk.T - seg_q
    src = q[None, :] + jnp.sum(jnp.where(inside, base[:, None, :], 0), axis=2)
    valid = jnp.any(inside, axis=2)
    buf = (tile_start[:, None] + q[None, :] // TILE_CHUNKS) % N_BUF
    trash = TRASH_CHUNK + buf * TILE_CHUNKS + q[None, :] % TILE_CHUNKS
    in_src = jnp.where(valid, src, ZERO_CHUNK).astype(jnp.int32).reshape(-1)
    out_dst = jnp.where(valid, src, trash).astype(jnp.int32).reshape(-1)
    t = jnp.arange(MAX_TILES + N_BUF, dtype=jnp.int32)
    tile_expert = jnp.minimum(jnp.sum((t[:, None] >= tile_end[None, :]).astype(jnp.int32), axis=1),
                              N_EXPERTS - 1)
    tile_start = jnp.concatenate([tile_start, tile_end[-1:]]).astype(jnp.int32)
    sub_used = (jnp.sum(c, axis=1) * CHUNK + SUB - 1) // SUB
    pad = jnp.full((N_BLK, LANE - N_EXPERTS), float(2 * BLK_ROWS), F32)
    first = jnp.concatenate([(seg_chunk * CHUNK).astype(F32), pad], axis=1)
    last = jnp.concatenate([((seg_chunk + c) * CHUNK).astype(F32), pad], axis=1)
    first_chunk = jnp.concatenate([seg_chunk.astype(F32), jnp.zeros_like(pad)], axis=1)
    seg = dict(
        first_row=jnp.broadcast_to(first[:, None, :], (N_BLK, 8, LANE)),
        last_row=jnp.broadcast_to(last[:, None, :], (N_BLK, 8, LANE)),
        first_col=jnp.broadcast_to(first[:, :, None], (N_BLK, LANE, LANE)),
        last_col=jnp.broadcast_to(last[:, :, None], (N_BLK, LANE, LANE)),
        first_chunk_col=jnp.broadcast_to(first_chunk[:, :, None], (N_BLK, LANE, LANE)).astype(BF16))
    return (tile_start, tiles_e.astype(jnp.int32), tile_expert, in_src, out_dst,
            sub_used.astype(jnp.int32), seg)


def _dispatch_kernel(sub_ref, h_ref, pos_ref, first_ref, last_ref, fchunk_ref, s_ref):
    b = pl.program_id(0)
    n_sub = jnp.where(b < N_BLK, sub_ref[jnp.minimum(b, N_BLK - 1)], 0)

    def locate(sb):
        rows = (lax.broadcasted_iota(jnp.int32, (SUB, LANE), 0) + sb * SUB).astype(F32)
        own = (jnp.where(rows >= first_ref[0, 0:1, :], 1.0, 0.0)
               - jnp.where(rows >= last_ref[0, 0:1, :], 1.0, 0.0)).astype(BF16)
        both = jnp.dot(own, jnp.concatenate([pos_ref[...], fchunk_ref[0]], axis=1),
                       preferred_element_type=F32)
        return rows, both

    def gather(sb, located):
        rows, both = located
        have = both[:, :TM]
        want = rows + 1.0 - float(CHUNK) * both[:, TM:]
        p = jnp.where(have == jnp.concatenate([want] * (TM // LANE), axis=1), 1.0, 0.0)
        s_ref[sb * SUB:(sb + 1) * SUB, :] = jnp.dot(
            p.astype(BF16), h_ref[...], preferred_element_type=F32).astype(BF16)

    def fill(sb):
        gather(sb, locate(sb))

    def clear(sb):
        s_ref[sb * SUB:(sb + 1) * SUB, :] = jnp.zeros((SUB, D_MODEL), BF16)

    @pl.when(b < N_BLK)
    def _():
        ahead = locate(0)
        for sb in range(SUB_ALWAYS):
            cur, ahead = ahead, (locate(sb + 1) if sb + 1 < SUB_ALWAYS else None)
            gather(sb, cur)

    @pl.when(b == N_BLK)
    def _():
        for sb in range(SUB_ALWAYS):
            clear(sb)

    for sb in range(SUB_ALWAYS, N_SUB):
        pl.when(sb < n_sub)(functools.partial(fill, sb))
        pl.when(sb >= n_sub)(functools.partial(clear, sb))


def _dispatch(sub_used, h2, pos, seg):
    last = N_BLK - 1
    blk3 = lambda b, *_: (jnp.minimum(b, last), 0, 0)
    grid_spec = pltpu.PrefetchScalarGridSpec(
        num_scalar_prefetch=1,
        grid=(N_BLK + 1,),
        in_specs=[pl.BlockSpec((TM, D_MODEL), lambda b, *_: (jnp.minimum(b, last), 0)),
                  pl.BlockSpec((LANE, TM), lambda b, *_: (0, jnp.minimum(b, last))),
                  pl.BlockSpec((1, 8, LANE), blk3),
                  pl.BlockSpec((1, 8, LANE), blk3),
                  pl.BlockSpec((1, LANE, LANE), blk3)],
        out_specs=pl.BlockSpec((BLK_ROWS, D_MODEL), lambda b, *_: (b, 0)))
    return pl.pallas_call(
        _dispatch_kernel,
        grid_spec=grid_spec,
        out_shape=jax.ShapeDtypeStruct(((N_BLK + 1) * BLK_ROWS, D_MODEL), BF16),
        compiler_params=_cparams(("arbitrary",), 40),
        name="dispatch",
    )(sub_used, h2, pos, seg["first_row"], seg["last_row"], seg["first_chunk_col"])


def _experts_kernel(tstart_ref, tcount_ref, texp_ref, insrc_ref, outdst_ref,
                    s_hbm, wg_ref, wu_ref, wd_ref, o_hbm,
                    xb0, xb1, xb2, ob0, ob1, ob2, wgub, wdb, insem, outsem):
    xbufs = (xb0, xb1, xb2)
    obufs = (ob0, ob1, ob2)
    e = pl.program_id(0)
    n_tiles = tstart_ref[N_EXPERTS]
    t0 = tstart_ref[e]

    def slot_base(t):
        te = texp_ref[t]
        j = jnp.minimum(t - tstart_ref[te], EXP_TILES - 1)
        return te * EXP_SLOTS + j * TILE_CHUNKS

    def in_copy(src_chunk, buf, i, sem):
        return pltpu.make_async_copy(s_hbm.at[src_chunk], xbufs[buf].at[i], sem)

    def out_copy(dst_chunk, buf, i, sem):
        return pltpu.make_async_copy(obufs[buf].at[i], o_hbm.at[dst_chunk], sem)

    def start_in(t, buf):
        base = slot_base(t)
        for i in range(TILE_CHUNKS):
            in_copy(insrc_ref[base + i], buf, i, insem.at[buf]).start()

    def start_out(t, buf, real):
        base = slot_base(jnp.maximum(t, 0))
        for i in range(TILE_CHUNKS):
            dst = jnp.where(real, outdst_ref[base + i], TRASH_CHUNK + buf * TILE_CHUNKS + i)
            out_copy(dst, buf, i, outsem.at[buf]).start()

    def wait_in(sem_idx):
        for i in range(TILE_CHUNKS):
            in_copy(0, 0, i, insem.at[sem_idx]).wait()

    def wait_out(sem_idx):
        for i in range(TILE_CHUNKS):
            out_copy(0, 0, i, outsem.at[sem_idx]).wait()

    def on_buffer(buf_dyn, fn):
        for k in range(N_BUF):
            pl.when(buf_dyn == k)(functools.partial(fn, k))

    wgub[:, :EXPERT_FF] = wg_ref[0].astype(BF16)
    wgub[:, EXPERT_FF:] = wu_ref[0].astype(BF16)
    wdb[...] = wd_ref[0].astype(BF16)

    @pl.when(e == 0)
    def _():
        for ob in obufs:
            ob[...] = jnp.zeros(ob.shape, BF16)
        start_out(0, 0, False)
        start_out(0, 1, False)
        start_in(0, 0)
        start_in(1, 1)

    def tile_body(j, carry):
        t = t0 + j

        def run(k):
            nxt = (k + 2) % N_BUF
            wait_in(k)
            wait_out(k)
            start_in(t + 2, nxt)
            start_out(t - 1, nxt, t >= 1)
            x = xbufs[k][...].reshape(TILE_ROWS, D_MODEL)
            au = jnp.dot(x, wgub[...], preferred_element_type=F32)
            a, u = au[:, :EXPERT_FF], au[:, EXPERT_FF:]
            hid = (a * jax.nn.sigmoid(a) * u).astype(BF16)
            out = jnp.dot(hid, wdb[...], preferred_element_type=F32).astype(BF16)
            obufs[k][...] = out.reshape(TILE_CHUNKS, CHUNK, D_MODEL)

        on_buffer(t % N_BUF, run)
        return carry

    lax.fori_loop(0, tcount_ref[e], tile_body, 0)

    @pl.when(e == N_EXPERTS - 1)
    def _():
        on_buffer((n_tiles + 2) % N_BUF, lambda k: start_out(n_tiles - 1, k, n_tiles >= 1))
        wait_in(n_tiles % N_BUF)
        wait_in((n_tiles + 1) % N_BUF)
        for k in range(N_BUF):
            wait_out(k)


def _experts(tile_start, tile_count, tile_expert, in_src, out_dst, sorted_rows, wg, wu, wd):
    wspec = lambda shape: pl.BlockSpec(shape, lambda e, *_: (e, 0, 0))
    grid_spec = pltpu.PrefetchScalarGridSpec(
        num_scalar_prefetch=5,
        grid=(N_EXPERTS,),
        in_specs=[pl.BlockSpec(memory_space=pl.ANY),
                  wspec((1, D_MODEL, EXPERT_FF)),
                  wspec((1, D_MODEL, EXPERT_FF)),
                  wspec((1, EXPERT_FF, D_MODEL))],
        out_specs=pl.BlockSpec(memory_space=pl.ANY),
        scratch_shapes=[pltpu.VMEM((TILE_CHUNKS, CHUNK, D_MODEL), BF16)] * (2 * N_BUF) + [
                        pltpu.VMEM((D_MODEL, 2 * EXPERT_FF), BF16),
                        pltpu.VMEM((EXPERT_FF, D_MODEL), BF16),
                        pltpu.SemaphoreType.DMA((N_BUF,)),
                        pltpu.SemaphoreType.DMA((N_BUF,))])
    return pl.pallas_call(
        _experts_kernel,
        grid_spec=grid_spec,
        out_shape=jax.ShapeDtypeStruct(sorted_rows.shape, BF16),
        input_output_aliases={5: 0},
        compiler_params=_cparams(("arbitrary",), 32),
        name="experts",
    )(tile_start, tile_count, tile_expert, in_src, out_dst, sorted_rows, wg, wu, wd)


def _combine_kernel(sub_ref, o_ref, post_ref, gate_ref, first_ref, last_ref, h_ref, x1_ref, mods_ref,
                    sg_ref, su_ref, sd_ref, fg_ref, yp_ref, ys_ref, acc_ref):
    i = pl.program_id(0)
    n_sub = sub_ref[i]

    def locate(sb):
        rows = (lax.broadcasted_iota(jnp.int32, (LANE, SUB), 1) + sb * SUB).astype(F32)
        first = jnp.concatenate([first_ref[0]] * (SUB // LANE), axis=1)
        last = jnp.concatenate([last_ref[0]] * (SUB // LANE), axis=1)
        own = jnp.where(rows >= first, 1.0, 0.0) - jnp.where(rows >= last, 1.0, 0.0)
        start = jnp.sum(jnp.where(own > 0.0, first, 0.0), axis=0, keepdims=True)
        want = rows[0:1, :] + 1.0 - start
        ownb = own.astype(BF16)
        have = jnp.dot(post_ref[...], ownb, preferred_element_type=F32)
        own2 = jnp.concatenate([ownb[:N_EXPERTS], ownb[:N_EXPERTS]], axis=0)
        gate = jnp.dot(gate_ref[...], own2, preferred_element_type=F32)
        return have, want, gate

    def select(located):
        have, want, gate = located
        return jnp.where(have == want, gate, 0.0).astype(BF16)

    def weights_of(sb):
        return select(locate(sb))

    h = h_ref[...]
    a = jnp.dot(h, sg_ref[...], preferred_element_type=F32)
    u = jnp.dot(h, su_ref[...], preferred_element_type=F32)
    shared = jnp.dot((a * jax.nn.sigmoid(a) * u).astype(BF16), sd_ref[...],
                     preferred_element_type=F32)
    pws, ahead = [], locate(0)
    for sb in range(SUB_ALWAYS):
        cur, ahead = ahead, (locate(sb + 1) if sb + 1 < SUB_ALWAYS else None)
        pws.append(select(cur))
    pw_main = jnp.concatenate(pws, axis=1)
    acc_ref[...] = shared + jnp.dot(pw_main, o_ref[0:SUB_ALWAYS * SUB, :],
                                    preferred_element_type=F32)
    for sb in range(SUB_ALWAYS, N_SUB):
        @pl.when(sb < n_sub)
        def _():
            acc_ref[...] += jnp.dot(weights_of(sb), o_ref[sb * SUB:(sb + 1) * SUB, :],
                                    preferred_element_type=F32)

    m = mods_ref[0]
    y = _rms(x1_ref[...] + m[5:6] * acc_ref[...], fg_ref[...])

    @pl.when(i < NP_TILES)
    def _():
        yp_ref[...] = y

    @pl.when(i >= NP_TILES)
    def _():
        ys_ref[...] = y


def _combine(sub_used, expert_rows, pos_t, gate_t, seg, h2, x1, mods3, sg, su, sd, fg):
    tile = lambda i, *_: (i, 0)
    blk3 = lambda i, *_: (i, 0, 0)
    grid_spec = pltpu.PrefetchScalarGridSpec(
        num_scalar_prefetch=1,
        grid=(N_BLK,),
        in_specs=[pl.BlockSpec((BLK_ROWS, D_MODEL), tile),
                  pl.BlockSpec((TM, LANE), tile),
                  pl.BlockSpec((TM, LANE), tile),
                  pl.BlockSpec((1, LANE, LANE), blk3),
                  pl.BlockSpec((1, LANE, LANE), blk3),
                  pl.BlockSpec((TM, D_MODEL), tile),
                  pl.BlockSpec((TM, D_MODEL), tile),
                  pl.BlockSpec((1, N_MOD, D_MODEL), lambda i, *_: (_seg_of_tile(i), 0, 0)),
                  _const_spec((D_MODEL, EXPERT_FF)),
                  _const_spec((D_MODEL, EXPERT_FF)),
                  _const_spec((EXPERT_FF, D_MODEL)),
                  _const_spec((1, D_MODEL))],
        out_specs=[pl.BlockSpec((TM, D_MODEL), lambda i, *_: (jnp.minimum(i, NP_TILES - 1), 0)),
                   pl.BlockSpec((TM, D_MODEL), lambda i, *_: (jnp.maximum(i - NP_TILES, 0), 0))],
        scratch_shapes=[pltpu.VMEM((TM, D_MODEL), F32)])
    return pl.pallas_call(
        _combine_kernel,
        grid_spec=grid_spec,
        out_shape=[jax.ShapeDtypeStruct((N_PROMPT, D_MODEL), F32),
                   jax.ShapeDtypeStruct((N_SAMPLE, D_MODEL), F32)],
        compiler_params=_cparams(("arbitrary",), 40),
        name="combine",
    )(sub_used, expert_rows, pos_t, gate_t, seg["first_col"], seg["last_col"], h2, x1, mods3,
      sg, su, sd, fg)


def _rope_tables():
    pos = np.arange(DEC_SEQ)
    row = (pos // GRID_W).astype(np.float64)
    col = (pos % GRID_W).astype(np.float64)
    n_freq = QK_ROPE_DIM // 4
    inv_freq = ROPE_BASE ** (-np.arange(n_freq, dtype=np.float64) / n_freq)
    ar, ac = row[:, None] * inv_freq, col[:, None] * inv_freq
    cos32 = np.concatenate([np.cos(ar), np.cos(ar), np.cos(ac), np.cos(ac)], axis=1)
    sin32 = np.concatenate([-np.sin(ar), np.sin(ar), -np.sin(ac), np.sin(ac)], axis=1)
    cos32 = np.concatenate([np.ones((TM, QK_ROPE_DIM)), cos32], axis=0)
    sin32 = np.concatenate([np.zeros((TM, QK_ROPE_DIM)), sin32], axis=0)
    n = cos32.shape[0]
    ones = np.ones((n, QK_NOPE_DIM))
    zeros = np.zeros((n, QK_NOPE_DIM))
    tail = np.zeros((n, HEAD_PAD - QK_NOPE_DIM - QK_ROPE_DIM))
    q_scale = ATTN_SCALE * math.log2(math.e)
    rest = np.zeros((n, HEAD_PAD - QK_ROPE_DIM))
    qc = np.concatenate([ones, cos32, tail], axis=1) * q_scale
    qs = np.concatenate([zeros, sin32, tail], axis=1) * q_scale
    kc = np.concatenate([cos32, rest], axis=1)
    ks = np.concatenate([sin32, rest], axis=1)
    return jnp.asarray(np.stack([qc, qs, kc, ks], axis=0).astype(np.float32))


def _pack_weights(w_in, w_uq, w_ukv):
    win = w_in.astype(BF16)
    gate_tail = w_in[:, S_GATE - PHASE + 2 * D_MODEL:]
    wgt = jnp.pad(gate_tail, ((0, 0), (0, LANE - PHASE))).astype(BF16)
    uq = w_uq.reshape(Q_LORA, N_HEADS, QK_NOPE_DIM + QK_ROPE_DIM)
    pad_q = HEAD_PAD - QK_NOPE_DIM - QK_ROPE_DIM
    wuqp = jnp.pad(uq, ((0, 0), (0, 0), (0, pad_q))).reshape(Q_LORA, QK_COLS).astype(BF16)
    ukv = w_ukv.reshape(KV_LORA, N_HEADS, QK_NOPE_DIM + V_DIM)
    wukp = jnp.pad(ukv[:, :, :QK_NOPE_DIM], ((0, 0), (0, 0), (0, HEAD_PAD - QK_NOPE_DIM)))
    wukp_t = wukp.reshape(KV_LORA, QK_COLS).T.astype(BF16)
    wuv = ukv[:, :, QK_NOPE_DIM:].reshape(KV_LORA, V_COLS).astype(BF16)
    return win, wgt, wuqp, wukp_t, wuv


def kernel(x_prompt, x_sample, cache_ckv, cache_krope, c, c_ctx, mod_w, mod_b, norm1_g, w_in, q_norm_g, w_uq, kv_norm_g, w_ukv, w_o_attn, conv_w, conv_b, conv_ln_g, conv_ln_b, w_pw2, w_out, norm2_g, router_w, router_bias, exp_w_gate, exp_w_up, exp_w_down, sh_w_gate, sh_w_up, sh_w_down, final_g):
    xp = x_prompt.reshape(N_PROMPT, D_MODEL)
    xs = x_sample.reshape(N_SAMPLE, D_MODEL)
    cond8 = jnp.concatenate([c_ctx[None, :], c, jnp.zeros((8 - 1 - DEC_BATCH, D_MODEL), F32)], axis=0)
    mods3 = _mods(cond8, mod_w[0], mod_b).reshape(8, N_MOD, D_MODEL)

    win, wgt, wuqp, wukp, wuv = _pack_weights(w_in[0], w_uq[0], w_ukv[0])
    q, k, v, z, gt, ckv_new, kr_new = _inproj(
        xp, xs, mods3, norm1_g, q_norm_g, kv_norm_g, win, wgt, wuqp, wukp, wuv, _rope_tables())

    krope_pad = jnp.pad(cache_krope[:, 0], ((0, 0), (0, 0),
                                            (QK_NOPE_DIM, HEAD_PAD - QK_NOPE_DIM - QK_ROPE_DIM)))
    kc, vc = _ctxkv(cache_ckv[:, 0], krope_pad, wukp, wuv)
    attn_p = _attn_prompt(q, k, v)
    attn_s = _attn_sample(q, k, v, kc, vc)

    rwt = router_w[0].T
    rwh = rwt.astype(BF16)
    rwl = (rwt - rwh.astype(F32)).astype(BF16)
    cw = conv_w[0].reshape(CONV_K, D_MODEL // LANE, LANE).transpose(1, 0, 2)
    x1, h2, gates_t = _mix(
        attn_p, attn_s, z, gt, xp, xs, mods3,
        w_o_attn[0].astype(BF16), w_pw2[0].astype(BF16), w_out[0].astype(BF16),
        cw, conv_b, conv_ln_g, conv_ln_b, norm2_g, rwh, rwl, router_bias.reshape(N_EXPERTS, 1))

    pos, cnt = _plan(gates_t)
    tile_start, tile_count, tile_expert, in_src, out_dst, sub_used, seg = _tile_tables(cnt)
    expert_rows = _experts(tile_start, tile_count, tile_expert, in_src, out_dst,
                           _dispatch(sub_used, h2, pos, seg).reshape(-1, CHUNK, D_MODEL),
                           exp_w_gate[0], exp_w_up[0], exp_w_down[0])
    gates = gates_t.T
    gates_hi = gates.astype(BF16)
    gates_lo = (gates - gates_hi.astype(F32)).astype(BF16)
    y_p, y_s = _combine(sub_used, expert_rows.reshape(-1, D_MODEL), pos.T,
                        jnp.concatenate([gates_hi, gates_lo], axis=1), seg, h2, x1, mods3,
                        sh_w_gate[0].astype(BF16), sh_w_up[0].astype(BF16),
                        sh_w_down[0].astype(BF16), final_g[None, :])
    return (y_p.reshape(BATCH, SEQ, D_MODEL),
            y_s.reshape(DEC_BATCH, DEC_SEQ, D_MODEL),
            ckv_new.reshape(BATCH, 1, SEQ, KV_LORA),
            kr_new.reshape(BATCH, 1, SEQ, QK_ROPE_DIM))
```

```python
import functools
import math

import jax
import jax.numpy as jnp
import numpy as np
from jax import lax
from jax.experimental import pallas as pl
from jax.experimental.pallas import tpu as pltpu

F32 = jnp.float32
BF16 = jnp.bfloat16

D_MODEL = 1024
BATCH = 16
SEQ = 256
DEC_BATCH = 2
DEC_SEQ = 2048
PAST_LEN = 256
GRID_W = 64
N_HEADS = 16
QK_NOPE_DIM = 64
QK_ROPE_DIM = 32
V_DIM = 64
Q_LORA = 384
KV_LORA = 256
ROPE_BASE = 10000.0
ATTN_SCALE = 1.0 / math.sqrt(QK_NOPE_DIM + QK_ROPE_DIM)
CONV_K = 31
CONV_PAD = CONV_K // 2
N_EXPERTS = 64
TOP_K = 8
N_GROUPS = 8
GROUP_SIZE = N_EXPERTS // N_GROUPS
TOPK_GROUPS = 4
EXPERT_FF = 256
ROUTED_SCALE = 2.5
EPS = 1e-6
N_MOD = 6

LANE = 128
HEAD_PAD = LANE
QK_COLS = N_HEADS * HEAD_PAD
N_PROMPT = BATCH * SEQ
N_SAMPLE = DEC_BATCH * DEC_SEQ
N_TOK = N_PROMPT + N_SAMPLE

TM = 256
NP_TILES = N_PROMPT // TM
NS_TILES = N_SAMPLE // TM
N_TILES = NP_TILES + NS_TILES
TILES_PER_DEC = DEC_SEQ // TM
HALO = 16
S_Q = 0
S_KV = S_Q + Q_LORA
S_KR = S_KV + KV_LORA
S_CONV = S_KR + QK_ROPE_DIM
S_GATE = S_CONV + 2 * D_MODEL
PHASE = S_CONV % LANE
CONV_WIN = 2 * D_MODEL + LANE
GATE_WIN = 2 * D_MODEL + LANE
IN_COLS = S_GATE + 2 * D_MODEL
V_COLS = N_HEADS * V_DIM

VMEM_CAP = 56 * 1024 * 1024


def _cparams(sem, vmem_mb):
    return pltpu.CompilerParams(dimension_semantics=sem,
                                vmem_limit_bytes=min(vmem_mb * 1024 * 1024, VMEM_CAP))


def _const_spec(shape):
    n = len(shape)
    return pl.BlockSpec(shape, lambda *_: (0,) * n, pipeline_mode=pl.Buffered(1))


def _seg_of_tile(i):
    return jnp.where(i < NP_TILES, 0, 1 + (i - NP_TILES) // TILES_PER_DEC)


def _rms(x, g):
    return x * lax.rsqrt(jnp.mean(x * x, axis=-1, keepdims=True) + EPS) * g


_NT = (((1,), (1,)), ((), ()))


def _mods_kernel(cond_ref, w_ref, b_ref, o_ref):
    c = cond_ref[...]
    s = c * jax.nn.sigmoid(c)
    o_ref[...] = jnp.dot(s.astype(BF16), w_ref[...].astype(BF16),
                         preferred_element_type=F32) + b_ref[...]


def _mods(cond8, mod_w, mod_b):
    n = N_MOD * D_MODEL
    bn = 512
    return pl.pallas_call(
        _mods_kernel,
        grid=(n // bn,),
        in_specs=[pl.BlockSpec((8, D_MODEL), lambda j: (0, 0)),
                  pl.BlockSpec((D_MODEL, bn), lambda j: (0, j)),
                  pl.BlockSpec((1, bn), lambda j: (0, j))],
        out_specs=pl.BlockSpec((8, bn), lambda j: (0, j)),
        out_shape=jax.ShapeDtypeStruct((8, n), F32),
        compiler_params=_cparams(("arbitrary",), 16),
        name="mods",
    )(cond8, mod_w, mod_b)


def _keys_t(wukp_t, ckv, krope):
    eye = (lax.broadcasted_iota(jnp.int32, (HEAD_PAD, HEAD_PAD), 0)
           == lax.broadcasted_iota(jnp.int32, (HEAD_PAD, HEAD_PAD), 1))
    rope_t = lax.dot_general(jnp.where(eye, 1.0, 0.0).astype(BF16), krope, _NT,
                             preferred_element_type=F32)
    nope_t = lax.dot_general(wukp_t, ckv, _NT, preferred_element_type=F32)
    return (nope_t + jnp.concatenate([rope_t] * N_HEADS, axis=0)).astype(BF16)


def _rope_swap(x):
    n = x.shape[1]
    half = QK_ROPE_DIM // 4
    lane = lax.broadcasted_iota(jnp.int32, x.shape, 1)
    return jnp.where(lane % (2 * half) < half,
                     pltpu.roll(x, n - half, axis=1), pltpu.roll(x, half, axis=1))


def _inproj_kernel(xp_ref, xs_ref, mods_ref, g1_ref, qg_ref, kvg_ref, win_ref, wgt_ref, wuqp_ref,
                   wukp_ref, wuv_ref, tab_ref,
                   q_ref, k_ref, v_ref, z_ref, gt_ref, ckv_ref, kr_ref):
    i = pl.program_id(0)
    x = jnp.where(i < NP_TILES, xp_ref[...], xs_ref[...])
    m = mods_ref[0]
    h = _rms(x, g1_ref[...]) * (1.0 + m[1:2]) + m[0:1]
    hb = h.astype(BF16)

    def proj(c0, c1):
        return jnp.dot(hb, win_ref[:, c0:c1], preferred_element_type=F32)

    def lanes16(t):
        return jnp.concatenate([t] * N_HEADS, axis=1)

    qn = _rms(proj(S_Q, S_KV), qg_ref[...]).astype(BF16)
    q = jnp.dot(qn, wuqp_ref[...], preferred_element_type=F32)
    q_ref[...] = (q * lanes16(tab_ref[0]) + _rope_swap(q) * lanes16(tab_ref[1])).astype(BF16)

    ckv = _rms(proj(S_KV, S_KR), kvg_ref[...])
    ckvb = ckv.astype(BF16)
    u = proj(S_KR, S_KR + CONV_WIN)
    ua = u[:, :LANE]
    krot = pltpu.roll(ua * tab_ref[2] + _rope_swap(ua) * tab_ref[3], QK_NOPE_DIM, axis=1)
    k_ref[...] = _keys_t(wukp_ref[...], ckvb, krot.astype(BF16))
    v_ref[...] = jnp.dot(ckvb, wuv_ref[...], preferred_element_type=F32).astype(BF16)

    @pl.when(i < NP_TILES)
    def _():
        ckv_ref[...] = ckv
        kr_ref[...] = ua[:, :QK_ROPE_DIM]

    glu = u[:, :D_MODEL + LANE] * jax.nn.sigmoid(u[:, D_MODEL:])
    z_ref[...] = pltpu.roll(glu, D_MODEL + LANE - PHASE, axis=1)[:, :D_MODEL].astype(BF16)
    gates = jax.nn.sigmoid(jnp.concatenate(
        [proj(S_GATE - PHASE, S_GATE - PHASE + 2 * D_MODEL),
         jnp.dot(hb, wgt_ref[...], preferred_element_type=F32)], axis=1))
    gt_ref[...] = pltpu.roll(gates, GATE_WIN - PHASE, axis=1)[:, :2 * D_MODEL].astype(BF16)


def _inproj(xp, xs, mods3, g1, qg, kvg, win, wgt, wuqp, wukp, wuv, tab):
    tile = lambda i: (i, 0)
    ptile = lambda i: (jnp.minimum(i, NP_TILES - 1), 0)
    stile = lambda i: (jnp.maximum(i - NP_TILES, 0), 0)
    tab_blk = lambda i: (0, jnp.where(i < NP_TILES, 0, 1 + (i - NP_TILES) % TILES_PER_DEC), 0)
    return pl.pallas_call(
        _inproj_kernel,
        grid=(N_TILES,),
        in_specs=[pl.BlockSpec((TM, D_MODEL), ptile),
                  pl.BlockSpec((TM, D_MODEL), stile),
                  pl.BlockSpec((1, N_MOD, D_MODEL), lambda i: (_seg_of_tile(i), 0, 0)),
                  _const_spec((1, D_MODEL)),
                  _const_spec((1, Q_LORA)),
                  _const_spec((1, KV_LORA)),
                  _const_spec((D_MODEL, IN_COLS)),
                  _const_spec((D_MODEL, LANE)),
                  _const_spec((Q_LORA, QK_COLS)),
                  _const_spec((QK_COLS, KV_LORA)),
                  _const_spec((KV_LORA, V_COLS)),
                  pl.BlockSpec((4, TM, LANE), tab_blk)],
        out_specs=[pl.BlockSpec((TM, QK_COLS), tile),
                   pl.BlockSpec((QK_COLS, TM), lambda i: (0, i)),
                   pl.BlockSpec((TM, V_COLS), tile),
                   pl.BlockSpec((TM, D_MODEL), tile),
                   pl.BlockSpec((TM, 2 * D_MODEL), tile),
                   pl.BlockSpec((TM, KV_LORA), ptile),
                   pl.BlockSpec((TM, QK_ROPE_DIM), ptile)],
        out_shape=[jax.ShapeDtypeStruct((N_TOK, QK_COLS), BF16),
                   jax.ShapeDtypeStruct((QK_COLS, N_TOK), BF16),
                   jax.ShapeDtypeStruct((N_TOK, V_COLS), BF16),
                   jax.ShapeDtypeStruct((N_TOK, D_MODEL), BF16),
                   jax.ShapeDtypeStruct((N_TOK, 2 * D_MODEL), BF16),
                   jax.ShapeDtypeStruct((N_PROMPT, KV_LORA), F32),
                   jax.ShapeDtypeStruct((N_PROMPT, QK_ROPE_DIM), F32)],
        compiler_params=_cparams(("arbitrary",), 52),
        name="inproj",
    )(xp, xs, mods3, g1, qg, kvg, win, wgt, wuqp, wukp, wuv, tab)


def _ctxkv_kernel(ckv_ref, kr_ref, wukp_ref, wuv_ref, k_ref, v_ref):
    c = ckv_ref[0].astype(BF16)
    k_ref[0] = _keys_t(wukp_ref[...], c, kr_ref[0].astype(BF16))
    v_ref[0] = jnp.dot(c, wuv_ref[...], preferred_element_type=F32).astype(BF16)


def _ctxkv(cache_ckv, krope_pad, wukp, wuv):
    return pl.pallas_call(
        _ctxkv_kernel,
        grid=(DEC_BATCH,),
        in_specs=[pl.BlockSpec((1, PAST_LEN, KV_LORA), lambda b: (b, 0, 0)),
                  pl.BlockSpec((1, PAST_LEN, LANE), lambda b: (b, 0, 0)),
                  _const_spec((QK_COLS, KV_LORA)),
                  _const_spec((KV_LORA, V_COLS))],
        out_specs=[pl.BlockSpec((1, QK_COLS, PAST_LEN), lambda b: (b, 0, 0)),
                   pl.BlockSpec((1, PAST_LEN, V_COLS), lambda b: (b, 0, 0))],
        out_shape=[jax.ShapeDtypeStruct((DEC_BATCH, QK_COLS, PAST_LEN), BF16),
                   jax.ShapeDtypeStruct((DEC_BATCH, PAST_LEN, V_COLS), BF16)],
        compiler_params=_cparams(("arbitrary",), 16),
        name="ctxkv",
    )(cache_ckv, krope_pad, wukp, wuv)


def _attn_kernel(*refs, has_ctx):
    if has_ctx:
        q_ref, k_ref, v_ref, kc_ref, vc_ref, o_ref = refs
    else:
        q_ref, k_ref, v_ref, o_ref = refs
    tq = q_ref.shape[0]
    lane = lax.broadcasted_iota(jnp.int32, (tq, LANE), 1)
    for j in range(N_HEADS // 2):
        pair = slice(j * LANE, (j + 1) * LANE)
        ps, pcs, dens = [], [], []
        for h in (2 * j, 2 * j + 1):
            hs = slice(h * HEAD_PAD, (h + 1) * HEAD_PAD)
            qh = q_ref[:, hs]
            s = jnp.dot(qh, k_ref[hs, :], preferred_element_type=F32)
            mx = jnp.max(s, axis=-1, keepdims=True)
            if has_ctx:
                sc = jnp.dot(qh, kc_ref[0, hs, :], preferred_element_type=F32)
                mx = jnp.maximum(mx, jnp.max(sc, axis=-1, keepdims=True))
            p = jnp.exp2(s - mx)
            den = jnp.sum(p, axis=-1, keepdims=True)
            ps.append(p.astype(BF16))
            if has_ctx:
                pc = jnp.exp2(sc - mx)
                den = den + jnp.sum(pc, axis=-1, keepdims=True)
                pcs.append(pc.astype(BF16))
            dens.append(den)
        o = jnp.dot(jnp.concatenate(ps, axis=0), v_ref[:, pair], preferred_element_type=F32)
        if has_ctx:
            o = o + jnp.dot(jnp.concatenate(pcs, axis=0), vc_ref[0, :, pair],
                            preferred_element_type=F32)
        o_ref[:, pair] = jnp.where(lane < V_DIM, o[:tq] / dens[0], o[tq:] / dens[1]).astype(BF16)


def _attn_prompt(q, k, v):
    blk = lambda b: (b, 0)
    return pl.pallas_call(
        functools.partial(_attn_kernel, has_ctx=False),
        grid=(BATCH,),
        in_specs=[pl.BlockSpec((SEQ, QK_COLS), blk),
                  pl.BlockSpec((QK_COLS, SEQ), lambda b: (0, b)),
                  pl.BlockSpec((SEQ, V_COLS), blk)],
        out_specs=pl.BlockSpec((SEQ, N_HEADS * V_DIM), blk),
        out_shape=jax.ShapeDtypeStruct((N_PROMPT, N_HEADS * V_DIM), BF16),
        compiler_params=_cparams(("arbitrary",), 24),
        name="attn_prompt",
    )(q, k, v)


ATT_TQ = 256


def _attn_sample(q, k, v, kc, vc):
    nq = DEC_SEQ // ATT_TQ
    q0 = N_PROMPT // ATT_TQ
    s0 = N_PROMPT // DEC_SEQ
    once = pl.Buffered(1)
    return pl.pallas_call(
        functools.partial(_attn_kernel, has_ctx=True),
        grid=(DEC_BATCH, nq),
        in_specs=[pl.BlockSpec((ATT_TQ, QK_COLS), lambda b, t: (q0 + b * nq + t, 0)),
                  pl.BlockSpec((QK_COLS, DEC_SEQ), lambda b, t: (0, s0 + b), pipeline_mode=once),
                  pl.BlockSpec((DEC_SEQ, V_COLS), lambda b, t: (s0 + b, 0), pipeline_mode=once),
                  pl.BlockSpec((1, QK_COLS, PAST_LEN), lambda b, t: (b, 0, 0), pipeline_mode=once),
                  pl.BlockSpec((1, PAST_LEN, V_COLS), lambda b, t: (b, 0, 0), pipeline_mode=once)],
        out_specs=pl.BlockSpec((ATT_TQ, N_HEADS * V_DIM), lambda b, t: (b * nq + t, 0)),
        out_shape=jax.ShapeDtypeStruct((N_SAMPLE, N_HEADS * V_DIM), BF16),
        compiler_params=_cparams(("arbitrary", "arbitrary"), 54),
        name="attn_sample",
    )(q, k, v, kc, vc)


def _topk_rows(v, k, n):
    rows = lax.broadcasted_iota(jnp.int32, v.shape, 0).astype(F32)
    sel = jnp.zeros(v.shape, F32)
    for _ in range(k):
        mx = jnp.max(v, axis=0, keepdims=True)
        first = jnp.min(jnp.where(v == mx, rows, float(n)), axis=0, keepdims=True)
        hit = rows == first
        sel = jnp.where(hit, 1.0, sel)
        v = jnp.where(hit, -jnp.inf, v)
    return sel > 0.0


def _route(logits_t, bias):
    scores = jax.nn.sigmoid(logits_t)
    sel = scores + bias
    gscore = []
    for g in range(N_GROUPS):
        blk = sel[g * GROUP_SIZE:(g + 1) * GROUP_SIZE]
        top2 = _topk_rows(blk, 2, GROUP_SIZE)
        gscore.append(jnp.sum(jnp.where(top2, blk, 0.0), axis=0, keepdims=True))
    masked = []
    for g in range(N_GROUPS):
        beaten = jnp.zeros(gscore[g].shape, F32)
        for o in range(N_GROUPS):
            if o < g:
                beaten = beaten + jnp.where(gscore[o] >= gscore[g], 1.0, 0.0)
            elif o > g:
                beaten = beaten + jnp.where(gscore[o] > gscore[g], 1.0, 0.0)
        keep = beaten < float(TOPK_GROUPS)
        masked.append(jnp.where(keep, sel[g * GROUP_SIZE:(g + 1) * GROUP_SIZE], -jnp.inf))
    chosen = _topk_rows(jnp.concatenate(masked, axis=0), TOP_K, N_EXPERTS)
    w = jnp.where(chosen, scores, 0.0)
    return w / jnp.sum(w, axis=0, keepdims=True) * ROUTED_SCALE


def _mix_kernel(ap_ref, as_ref, z_ref, zp_ref, zn_ref, gt_ref, xp_ref, xs_ref, mods_ref,
                wo32_ref, wpw32_ref, wout32_ref, cw_ref, cb_ref, lng_ref, lnb_ref, g2_ref,
                rwh_ref, rwl_ref, rb_ref,
                x1_ref, h2_ref, gates_ref, win_ref, cz_ref, wo_ref, wpw_ref, wout_ref):
    i = pl.program_id(0)

    @pl.when(i == 0)
    def _():
        wo_ref[...] = wo32_ref[...].astype(BF16)
        wpw_ref[...] = wpw32_ref[...].astype(BF16)
        wout_ref[...] = wout32_ref[...].astype(BF16)

    is_prompt = i < NP_TILES
    pos = (i - NP_TILES) % TILES_PER_DEC
    first = jnp.logical_or(is_prompt, pos == 0)
    last = jnp.logical_or(is_prompt, pos == TILES_PER_DEC - 1)

    nchunk = D_MODEL // LANE
    zprev = jnp.where(first, 0.0, zp_ref[...].astype(F32))
    znext = jnp.where(last, 0.0, zn_ref[...].astype(F32))
    zc = z_ref[...].astype(F32)
    for c in range(nchunk):
        cs = slice(c * LANE, (c + 1) * LANE)
        win_ref[c, 0:HALO, :] = zprev[:, cs]
        win_ref[c, HALO:HALO + TM, :] = zc[:, cs]
        win_ref[c, HALO + TM:, :] = znext[:, cs]

    def conv_chunk(c, carry):
        w = cw_ref[c]
        acc = jnp.zeros((TM, LANE), F32)
        for kk in range(CONV_K):
            off = HALO - CONV_PAD + kk
            acc = acc + win_ref[c, off:off + TM, :] * w[kk:kk + 1, :]
        cz_ref[c] = acc
        return carry

    lax.fori_loop(0, nchunk, conv_chunk, 0)

    m = mods_ref[0]
    rwh = rwh_ref[...]
    rows = [slice(hf * (TM // 2), (hf + 1) * (TM // 2)) for hf in range(2)]

    def conv_act(rs):
        conv = jnp.concatenate([cz_ref[c, rs, :] for c in range(nchunk)], axis=1) + cb_ref[...]
        mu = jnp.mean(conv, axis=-1, keepdims=True)
        cen = conv - mu
        var = jnp.mean(cen * cen, axis=-1, keepdims=True)
        ln = cen * lax.rsqrt(var + EPS) * lng_ref[...] + lnb_ref[...]
        return (ln * jax.nn.sigmoid(ln)).astype(BF16)

    def branches(rs, act):
        attn = jnp.where(is_prompt, ap_ref[rs, :], as_ref[rs, :])
        return (jnp.dot(attn, wo_ref[...], preferred_element_type=F32),
                jnp.dot(act, wpw_ref[...], preferred_element_type=F32))

    def merge(rs, outs):
        gt = gt_ref[rs, :].astype(F32)
        return (gt[:, :D_MODEL] * outs[0] + gt[:, D_MODEL:] * outs[1]).astype(BF16)

    def project(merged):
        return jnp.dot(merged, wout_ref[...], preferred_element_type=F32)

    def residual(rs, mix):
        x = jnp.where(is_prompt, xp_ref[rs, :], xs_ref[rs, :])
        x1 = x + m[2:3] * mix
        x1_ref[rs, :] = x1
        h2 = _rms(x1, g2_ref[...]) * (1.0 + m[4:5]) + m[3:4]
        hh = h2.astype(BF16)
        h2_ref[rs, :] = hh
        return hh, (h2 - hh.astype(F32)).astype(BF16)

    def route(rs, hs):
        hh, hl = hs
        logits_t = (lax.dot_general(rwh, hh, _NT, preferred_element_type=F32)
                    + lax.dot_general(rwh, hl, _NT, preferred_element_type=F32)
                    + lax.dot_general(rwl_ref[...], hh, _NT, preferred_element_type=F32))
        gates_ref[:, rs] = _route(logits_t, rb_ref[...])

    r0, r1 = rows
    act0 = conv_act(r0)
    out0 = branches(r0, act0)
    act1 = conv_act(r1)
    mrg0 = merge(r0, out0)
    out1 = branches(r1, act1)
    mix0 = project(mrg0)
    mrg1 = merge(r1, out1)
    hs0 = residual(r0, mix0)
    mix1 = project(mrg1)
    route(r0, hs0)
    hs1 = residual(r1, mix1)
    route(r1, hs1)


def _mix(attn_p, attn_s, z, gt, xp, xs, mods3, wo, wpw, wout, cw, cb, lng, lnb, g2, rwh, rwl, rb):
    tile = lambda i: (i, 0)
    ptile = lambda i: (jnp.minimum(i, NP_TILES - 1), 0)
    stile = lambda i: (jnp.maximum(i - NP_TILES, 0), 0)
    hpt = TM // HALO
    nhb = N_TOK // HALO
    return pl.pallas_call(
        _mix_kernel,
        grid=(N_TILES,),
        in_specs=[pl.BlockSpec((TM, D_MODEL), ptile),
                  pl.BlockSpec((TM, D_MODEL), stile),
                  pl.BlockSpec((TM, D_MODEL), tile),
                  pl.BlockSpec((HALO, D_MODEL), lambda i: (jnp.maximum(i * hpt - 1, 0), 0)),
                  pl.BlockSpec((HALO, D_MODEL), lambda i: (jnp.minimum((i + 1) * hpt, nhb - 1), 0)),
                  pl.BlockSpec((TM, 2 * D_MODEL), tile),
                  pl.BlockSpec((TM, D_MODEL), ptile),
                  pl.BlockSpec((TM, D_MODEL), stile),
                  pl.BlockSpec((1, N_MOD, D_MODEL), lambda i: (_seg_of_tile(i), 0, 0)),
                  _const_spec((D_MODEL, D_MODEL)),
                  _const_spec((D_MODEL, D_MODEL)),
                  _const_spec((D_MODEL, D_MODEL)),
                  _const_spec((D_MODEL // LANE, CONV_K, LANE)),
                  _const_spec((1, D_MODEL)),
                  _const_spec((1, D_MODEL)),
                  _const_spec((1, D_MODEL)),
                  _const_spec((1, D_MODEL)),
                  _const_spec((N_EXPERTS, D_MODEL)),
                  _const_spec((N_EXPERTS, D_MODEL)),
                  _const_spec((N_EXPERTS, 1))],
        out_specs=[pl.BlockSpec((TM, D_MODEL), tile),
                   pl.BlockSpec((TM, D_MODEL), tile),
                   pl.BlockSpec((N_EXPERTS, TM), lambda i: (0, i))],
        out_shape=[jax.ShapeDtypeStruct((N_TOK, D_MODEL), F32),
                   jax.ShapeDtypeStruct((N_TOK, D_MODEL), BF16),
                   jax.ShapeDtypeStruct((N_EXPERTS, N_TOK), F32)],
        scratch_shapes=[pltpu.VMEM((D_MODEL // LANE, TM + 2 * HALO, LANE), F32),
                        pltpu.VMEM((D_MODEL // LANE, TM, LANE), F32)]
        + [pltpu.VMEM((D_MODEL, D_MODEL), BF16)] * 3,
        compiler_params=_cparams(("arbitrary",), 52),
        name="mix",
    )(attn_p, attn_s, z, z, z, gt, xp, xs, mods3, wo, wpw, wout, cw, cb, lng, lnb, g2, rwh, rwl, rb)


CHUNK = 16
BLK_ROWS = TM * TOP_K + N_EXPERTS * CHUNK
BLK_CHUNKS = BLK_ROWS // CHUNK
SUB = 256
N_SUB = BLK_ROWS // SUB
SUB_ALWAYS = TM * TOP_K // SUB + 2
N_BLK = N_TOK // TM
TILE_CHUNKS = 32
TILE_ROWS = TILE_CHUNKS * CHUNK
MAX_CHUNKS = N_TOK * TOP_K // CHUNK + N_BLK * N_EXPERTS
MAX_TILES = MAX_CHUNKS // TILE_CHUNKS + N_EXPERTS
EXP_TILES = N_TOK // TILE_ROWS + 3
EXP_SLOTS = EXP_TILES * TILE_CHUNKS
N_BUF = 3
ZERO_CHUNK = N_BLK * BLK_CHUNKS
TRASH_CHUNK = ZERO_CHUNK + 1


def _plan_kernel(g_ref, pos_ref, cnt_ref):
    g = g_ref[...]
    picked = jnp.where(g > 0.0, 1.0, 0.0)
    r = lax.broadcasted_iota(jnp.int32, (TM, TM), 0)
    c = lax.broadcasted_iota(jnp.int32, (TM, TM), 1)
    upto = jnp.where(r <= c, 1.0, 0.0).astype(BF16)
    pos = picked * jnp.dot(picked.astype(BF16), upto, preferred_element_type=F32)
    pos_ref[0:N_EXPERTS, :] = pos.astype(BF16)
    pos_ref[N_EXPERTS:, :] = jnp.zeros((LANE - N_EXPERTS, TM), BF16)
    n = jnp.sum(picked, axis=1, keepdims=True)
    nchunk = jnp.floor((n + float(CHUNK - 1)) * (1.0 / CHUNK))
    cnt_ref[0] = jnp.broadcast_to(nchunk, (N_EXPERTS, LANE))


def _plan(gates_t):
    return pl.pallas_call(
        _plan_kernel,
        grid=(N_BLK,),
        in_specs=[pl.BlockSpec((N_EXPERTS, TM), lambda b: (0, b))],
        out_specs=[pl.BlockSpec((LANE, TM), lambda b: (0, b)),
                   pl.BlockSpec((1, N_EXPERTS, LANE), lambda b: (b, 0, 0))],
        out_shape=[jax.ShapeDtypeStruct((LANE, N_TOK), BF16),
                   jax.ShapeDtypeStruct((N_BLK, N_EXPERTS, LANE), F32)],
        compiler_params=_cparams(("arbitrary",), 16),
        name="plan",
    )(gates_t)


def _tile_tables(cnt):
    c = cnt[:, :, 0].astype(jnp.int32)
    seg_chunk = jnp.cumsum(c, axis=1) - c
    ct = c.T
    seg_q = jnp.cumsum(ct, axis=1) - ct
    tiles_e = (ct.sum(axis=1) + TILE_CHUNKS - 1) // TILE_CHUNKS
    tile_end = jnp.cumsum(tiles_e)
    tile_start = tile_end - tiles_e
    q = jnp.arange(EXP_SLOTS, dtype=jnp.int32)
    inside = jnp.logical_and(q[None, :, None] >= seg_q[:, None, :],
                             q[None, :, None] < (seg_q + ct)[:, None, :])
    base = jnp.arange(N_BLK, dtype=jnp.int32)[None, :] * BLK_CHUNKS + seg_chunk.T - seg_q
    src = q[None, :] + jnp.sum(jnp.where(inside, base[:, None, :], 0), axis=2)
    valid = jnp.any(inside, axis=2)
    buf = (tile_start[:, None] + q[None, :] // TILE_CHUNKS) % N_BUF
    trash = TRASH_CHUNK + buf * TILE_CHUNKS + q[None, :] % TILE_CHUNKS
    in_src = jnp.where(valid, src, ZERO_CHUNK).astype(jnp.int32).reshape(-1)
    out_dst = jnp.where(valid, src, trash).astype(jnp.int32).reshape(-1)
    t = jnp.arange(MAX_TILES + N_BUF, dtype=jnp.int32)
    tile_expert = jnp.minimum(jnp.sum((t[:, None] >= tile_end[None, :]).astype(jnp.int32), axis=1),
                              N_EXPERTS - 1)
    tile_start = jnp.concatenate([tile_start, tile_end[-1:]]).astype(jnp.int32)
    sub_used = (jnp.sum(c, axis=1) * CHUNK + SUB - 1) // SUB
    pad = jnp.full((N_BLK, LANE - N_EXPERTS), float(2 * BLK_ROWS), F32)
    first = jnp.concatenate([(seg_chunk * CHUNK).astype(F32), pad], axis=1)
    last = jnp.concatenate([((seg_chunk + c) * CHUNK).astype(F32), pad], axis=1)
    first_chunk = jnp.concatenate([seg_chunk.astype(F32), jnp.zeros_like(pad)], axis=1)
    seg = dict(
        first_row=jnp.broadcast_to(first[:, None, :], (N_BLK, 8, LANE)),
        last_row=jnp.broadcast_to(last[:, None, :], (N_BLK, 8, LANE)),
        first_col=jnp.broadcast_to(first[:, :, None], (N_BLK, LANE, LANE)),
        last_col=jnp.broadcast_to(last[:, :, None], (N_BLK, LANE, LANE)),
        first_chunk_col=jnp.broadcast_to(first_chunk[:, :, None], (N_BLK, LANE, LANE)).astype(BF16))
    return (tile_start, tiles_e.astype(jnp.int32), tile_expert, in_src, out_dst,
            sub_used.astype(jnp.int32), seg)


def _dispatch_kernel(sub_ref, h_ref, pos_ref, first_ref, last_ref, fchunk_ref, s_ref):
    b = pl.program_id(0)
    n_sub = jnp.where(b < N_BLK, sub_ref[jnp.minimum(b, N_BLK - 1)], 0)

    def locate(sb):
        rows = (lax.broadcasted_iota(jnp.int32, (SUB, LANE), 0) + sb * SUB).astype(F32)
        own = (jnp.where(rows >= first_ref[0, 0:1, :], 1.0, 0.0)
               - jnp.where(rows >= last_ref[0, 0:1, :], 1.0, 0.0)).astype(BF16)
        both = jnp.dot(own, jnp.concatenate([pos_ref[...], fchunk_ref[0]], axis=1),
                       preferred_element_type=F32)
        return rows, both

    def gather(sb, located):
        rows, both = located
        have = both[:, :TM]
        want = rows + 1.0 - float(CHUNK) * both[:, TM:]
        p = jnp.where(have == jnp.concatenate([want] * (TM // LANE), axis=1), 1.0, 0.0)
        s_ref[sb * SUB:(sb + 1) * SUB, :] = jnp.dot(
            p.astype(BF16), h_ref[...], preferred_element_type=F32).astype(BF16)

    def fill(sb):
        gather(sb, locate(sb))

    def clear(sb):
        s_ref[sb * SUB:(sb + 1) * SUB, :] = jnp.zeros((SUB, D_MODEL), BF16)

    @pl.when(b < N_BLK)
    def _():
        ahead = locate(0)
        for sb in range(SUB_ALWAYS):
            cur, ahead = ahead, (locate(sb + 1) if sb + 1 < SUB_ALWAYS else None)
            gather(sb, cur)

    @pl.when(b == N_BLK)
    def _():
        for sb in range(SUB_ALWAYS):
            clear(sb)

    for sb in range(SUB_ALWAYS, N_SUB):
        pl.when(sb < n_sub)(functools.partial(fill, sb))
        pl.when(sb >= n_sub)(functools.partial(clear, sb))


def _dispatch(sub_used, h2, pos, seg):
    last = N_BLK - 1
    blk3 = lambda b, *_: (jnp.minimum(b, last), 0, 0)
    grid_spec = pltpu.PrefetchScalarGridSpec(
        num_scalar_prefetch=1,
        grid=(N_BLK + 1,),
        in_specs=[pl.BlockSpec((TM, D_MODEL), lambda b, *_: (jnp.minimum(b, last), 0)),
                  pl.BlockSpec((LANE, TM), lambda b, *_: (0, jnp.minimum(b, last))),
                  pl.BlockSpec((1, 8, LANE), blk3),
                  pl.BlockSpec((1, 8, LANE), blk3),
                  pl.BlockSpec((1, LANE, LANE), blk3)],
        out_specs=pl.BlockSpec((BLK_ROWS, D_MODEL), lambda b, *_: (b, 0)))
    return pl.pallas_call(
        _dispatch_kernel,
        grid_spec=grid_spec,
        out_shape=jax.ShapeDtypeStruct(((N_BLK + 1) * BLK_ROWS, D_MODEL), BF16),
        compiler_params=_cparams(("arbitrary",), 40),
        name="dispatch",
    )(sub_used, h2, pos, seg["first_row"], seg["last_row"], seg["first_chunk_col"])


def _experts_kernel(tstart_ref, tcount_ref, texp_ref, insrc_ref, outdst_ref,
                    s_hbm, wg_ref, wu_ref, wd_ref, o_hbm,
                    xb0, xb1, xb2, ob0, ob1, ob2, wgub, wdb, insem, outsem):
    xbufs = (xb0, xb1, xb2)
    obufs = (ob0, ob1, ob2)
    e = pl.program_id(0)
    n_tiles = tstart_ref[N_EXPERTS]
    t0 = tstart_ref[e]

    def slot_base(t):
        te = texp_ref[t]
        j = jnp.minimum(t - tstart_ref[te], EXP_TILES - 1)
        return te * EXP_SLOTS + j * TILE_CHUNKS

    def in_copy(src_chunk, buf, i, sem):
        return pltpu.make_async_copy(s_hbm.at[src_chunk], xbufs[buf].at[i], sem)

    def out_copy(dst_chunk, buf, i, sem):
        return pltpu.make_async_copy(obufs[buf].at[i], o_hbm.at[dst_chunk], sem)

    def start_in(t, buf):
        base = slot_base(t)
        for i in range(TILE_CHUNKS):
            in_copy(insrc_ref[base + i], buf, i, insem.at[buf]).start()

    def start_out(t, buf, real):
        base = slot_base(jnp.maximum(t, 0))
        for i in range(TILE_CHUNKS):
            dst = jnp.where(real, outdst_ref[base + i], TRASH_CHUNK + buf * TILE_CHUNKS + i)
            out_copy(dst, buf, i, outsem.at[buf]).start()

    def wait_in(sem_idx):
        for i in range(TILE_CHUNKS):
            in_copy(0, 0, i, insem.at[sem_idx]).wait()

    def wait_out(sem_idx):
        for i in range(TILE_CHUNKS):
            out_copy(0, 0, i, outsem.at[sem_idx]).wait()

    def on_buffer(buf_dyn, fn):
        for k in range(N_BUF):
            pl.when(buf_dyn == k)(functools.partial(fn, k))

    wgub[:, :EXPERT_FF] = wg_ref[0].astype(BF16)
    wgub[:, EXPERT_FF:] = wu_ref[0].astype(BF16)
    wdb[...] = wd_ref[0].astype(BF16)

    @pl.when(e == 0)
    def _():
        for ob in obufs:
            ob[...] = jnp.zeros(ob.shape, BF16)
        start_out(0, 0, False)
        start_out(0, 1, False)
        start_in(0, 0)
        start_in(1, 1)

    def tile_body(j, carry):
        t = t0 + j

        def run(k):
            nxt = (k + 2) % N_BUF
            wait_in(k)
            wait_out(k)
            start_in(t + 2, nxt)
            start_out(t - 1, nxt, t >= 1)
            x = xbufs[k][...].reshape(TILE_ROWS, D_MODEL)
            au = jnp.dot(x, wgub[...], preferred_element_type=F32)
            a, u = au[:, :EXPERT_FF], au[:, EXPERT_FF:]
            hid = (a * jax.nn.sigmoid(a) * u).astype(BF16)
            out = jnp.dot(hid, wdb[...], preferred_element_type=F32).astype(BF16)
            obufs[k][...] = out.reshape(TILE_CHUNKS, CHUNK, D_MODEL)

        on_buffer(t % N_BUF, run)
        return carry

    lax.fori_loop(0, tcount_ref[e], tile_body, 0)

    @pl.when(e == N_EXPERTS - 1)
    def _():
        on_buffer((n_tiles + 2) % N_BUF, lambda k: start_out(n_tiles - 1, k, n_tiles >= 1))
        wait_in(n_tiles % N_BUF)
        wait_in((n_tiles + 1) % N_BUF)
        for k in range(N_BUF):
            wait_out(k)


def _experts(tile_start, tile_count, tile_expert, in_src, out_dst, sorted_rows, wg, wu, wd):
    wspec = lambda shape: pl.BlockSpec(shape, lambda e, *_: (e, 0, 0))
    grid_spec = pltpu.PrefetchScalarGridSpec(
        num_scalar_prefetch=5,
        grid=(N_EXPERTS,),
        in_specs=[pl.BlockSpec(memory_space=pl.ANY),
                  wspec((1, D_MODEL, EXPERT_FF)),
                  wspec((1, D_MODEL, EXPERT_FF)),
                  wspec((1, EXPERT_FF, D_MODEL))],
        out_specs=pl.BlockSpec(memory_space=pl.ANY),
        scratch_shapes=[pltpu.VMEM((TILE_CHUNKS, CHUNK, D_MODEL), BF16)] * (2 * N_BUF) + [
                        pltpu.VMEM((D_MODEL, 2 * EXPERT_FF), BF16),
                        pltpu.VMEM((EXPERT_FF, D_MODEL), BF16),
                        pltpu.SemaphoreType.DMA((N_BUF,)),
                        pltpu.SemaphoreType.DMA((N_BUF,))])
    return pl.pallas_call(
        _experts_kernel,
        grid_spec=grid_spec,
        out_shape=jax.ShapeDtypeStruct(sorted_rows.shape, BF16),
        input_output_aliases={5: 0},
        compiler_params=_cparams(("arbitrary",), 32),
        name="experts",
    )(tile_start, tile_count, tile_expert, in_src, out_dst, sorted_rows, wg, wu, wd)


def _combine_kernel(sub_ref, o_ref, post_ref, gate_ref, first_ref, last_ref, h_ref, x1_ref, mods_ref,
                    sg32_ref, su32_ref, sd32_ref, fg_ref, yp_ref, ys_ref, acc_ref, sg_ref, su_ref, sd_ref):
    i = pl.program_id(0)
    n_sub = sub_ref[i]

    @pl.when(i == 0)
    def _():
        sg_ref[...] = sg32_ref[...].astype(BF16)
        su_ref[...] = su32_ref[...].astype(BF16)
        sd_ref[...] = sd32_ref[...].astype(BF16)

    def locate(sb):
        rows = (lax.broadcasted_iota(jnp.int32, (LANE, SUB), 1) + sb * SUB).astype(F32)
        first = jnp.concatenate([first_ref[0]] * (SUB // LANE), axis=1)
        last = jnp.concatenate([last_ref[0]] * (SUB // LANE), axis=1)
        own = jnp.where(rows >= first, 1.0, 0.0) - jnp.where(rows >= last, 1.0, 0.0)
        start = jnp.sum(jnp.where(own > 0.0, first, 0.0), axis=0, keepdims=True)
        want = rows[0:1, :] + 1.0 - start
        ownb = own.astype(BF16)
        have = jnp.dot(post_ref[...], ownb, preferred_element_type=F32)
        own2 = jnp.concatenate([ownb[:N_EXPERTS], ownb[:N_EXPERTS]], axis=0)
        gate = jnp.dot(gate_ref[...], own2, preferred_element_type=F32)
        return have, want, gate

    def select(located):
        have, want, gate = located
        return jnp.where(have == want, gate, 0.0).astype(BF16)

    def weights_of(sb):
        return select(locate(sb))

    h = h_ref[...]
    a = jnp.dot(h, sg_ref[...], preferred_element_type=F32)
    u = jnp.dot(h, su_ref[...], preferred_element_type=F32)
    shared = jnp.dot((a * jax.nn.sigmoid(a) * u).astype(BF16), sd_ref[...],
                     preferred_element_type=F32)
    pws, ahead = [], locate(0)
    for sb in range(SUB_ALWAYS):
        cur, ahead = ahead, (locate(sb + 1) if sb + 1 < SUB_ALWAYS else None)
        pws.append(select(cur))
    pw_main = jnp.concatenate(pws, axis=1)
    acc_ref[...] = shared + jnp.dot(pw_main, o_ref[0:SUB_ALWAYS * SUB, :],
                                    preferred_element_type=F32)
    for sb in range(SUB_ALWAYS, N_SUB):
        @pl.when(sb < n_sub)
        def _():
            acc_ref[...] += jnp.dot(weights_of(sb), o_ref[sb * SUB:(sb + 1) * SUB, :],
                                    preferred_element_type=F32)

    m = mods_ref[0]
    y = _rms(x1_ref[...] + m[5:6] * acc_ref[...], fg_ref[...])

    @pl.when(i < NP_TILES)
    def _():
        yp_ref[...] = y

    @pl.when(i >= NP_TILES)
    def _():
        ys_ref[...] = y


def _combine(sub_used, expert_rows, pos_t, gate_t, seg, h2, x1, mods3, sg, su, sd, fg):
    tile = lambda i, *_: (i, 0)
    blk3 = lambda i, *_: (i, 0, 0)
    grid_spec = pltpu.PrefetchScalarGridSpec(
        num_scalar_prefetch=1,
        grid=(N_BLK,),
        in_specs=[pl.BlockSpec((BLK_ROWS, D_MODEL), tile),
                  pl.BlockSpec((TM, LANE), tile),
                  pl.BlockSpec((TM, LANE), tile),
                  pl.BlockSpec((1, LANE, LANE), blk3),
                  pl.BlockSpec((1, LANE, LANE), blk3),
                  pl.BlockSpec((TM, D_MODEL), tile),
                  pl.BlockSpec((TM, D_MODEL), tile),
                  pl.BlockSpec((1, N_MOD, D_MODEL), lambda i, *_: (_seg_of_tile(i), 0, 0)),
                  _const_spec((D_MODEL, EXPERT_FF)),
                  _const_spec((D_MODEL, EXPERT_FF)),
                  _const_spec((EXPERT_FF, D_MODEL)),
                  _const_spec((1, D_MODEL))],
        out_specs=[pl.BlockSpec((TM, D_MODEL), lambda i, *_: (jnp.minimum(i, NP_TILES - 1), 0)),
                   pl.BlockSpec((TM, D_MODEL), lambda i, *_: (jnp.maximum(i - NP_TILES, 0), 0))],
        scratch_shapes=[pltpu.VMEM((TM, D_MODEL), F32),
                        pltpu.VMEM((D_MODEL, EXPERT_FF), BF16),
                        pltpu.VMEM((D_MODEL, EXPERT_FF), BF16),
                        pltpu.VMEM((EXPERT_FF, D_MODEL), BF16)])
    return pl.pallas_call(
        _combine_kernel,
        grid_spec=grid_spec,
        out_shape=[jax.ShapeDtypeStruct((N_PROMPT, D_MODEL), F32),
                   jax.ShapeDtypeStruct((N_SAMPLE, D_MODEL), F32)],
        compiler_params=_cparams(("arbitrary",), 40),
        name="combine",
    )(sub_used, expert_rows, pos_t, gate_t, seg["first_col"], seg["last_col"], h2, x1, mods3,
      sg, su, sd, fg)


def _rope_tables():
    pos = np.arange(DEC_SEQ)
    row = (pos // GRID_W).astype(np.float64)
    col = (pos % GRID_W).astype(np.float64)
    n_freq = QK_ROPE_DIM // 4
    inv_freq = ROPE_BASE ** (-np.arange(n_freq, dtype=np.float64) / n_freq)
    ar, ac = row[:, None] * inv_freq, col[:, None] * inv_freq
    cos32 = np.concatenate([np.cos(ar), np.cos(ar), np.cos(ac), np.cos(ac)], axis=1)
    sin32 = np.concatenate([-np.sin(ar), np.sin(ar), -np.sin(ac), np.sin(ac)], axis=1)
    cos32 = np.concatenate([np.ones((TM, QK_ROPE_DIM)), cos32], axis=0)
    sin32 = np.concatenate([np.zeros((TM, QK_ROPE_DIM)), sin32], axis=0)
    n = cos32.shape[0]
    ones = np.ones((n, QK_NOPE_DIM))
    zeros = np.zeros((n, QK_NOPE_DIM))
    tail = np.zeros((n, HEAD_PAD - QK_NOPE_DIM - QK_ROPE_DIM))
    q_scale = ATTN_SCALE * math.log2(math.e)
    rest = np.zeros((n, HEAD_PAD - QK_ROPE_DIM))
    qc = np.concatenate([ones, cos32, tail], axis=1) * q_scale
    qs = np.concatenate([zeros, sin32, tail], axis=1) * q_scale
    kc = np.concatenate([cos32, rest], axis=1)
    ks = np.concatenate([sin32, rest], axis=1)
    return jnp.asarray(np.stack([qc, qs, kc, ks], axis=0).astype(np.float32))


def _pack_weights(w_in, w_uq, w_ukv):
    win = w_in.astype(BF16)
    gate_tail = w_in[:, S_GATE - PHASE + 2 * D_MODEL:]
    wgt = jnp.pad(gate_tail, ((0, 0), (0, LANE - PHASE))).astype(BF16)
    uq = w_uq.reshape(Q_LORA, N_HEADS, QK_NOPE_DIM + QK_ROPE_DIM)
    pad_q = HEAD_PAD - QK_NOPE_DIM - QK_ROPE_DIM
    wuqp = jnp.pad(uq, ((0, 0), (0, 0), (0, pad_q))).reshape(Q_LORA, QK_COLS).astype(BF16)
    ukv = w_ukv.reshape(KV_LORA, N_HEADS, QK_NOPE_DIM + V_DIM)
    wukp = jnp.pad(ukv[:, :, :QK_NOPE_DIM], ((0, 0), (0, 0), (0, HEAD_PAD - QK_NOPE_DIM)))
    wukp_t = wukp.reshape(KV_LORA, QK_COLS).T.astype(BF16)
    wuv = ukv[:, :, QK_NOPE_DIM:].reshape(KV_LORA, V_COLS).astype(BF16)
    return win, wgt, wuqp, wukp_t, wuv


def kernel(x_prompt, x_sample, cache_ckv, cache_krope, c, c_ctx, mod_w, mod_b, norm1_g, w_in, q_norm_g, w_uq, kv_norm_g, w_ukv, w_o_attn, conv_w, conv_b, conv_ln_g, conv_ln_b, w_pw2, w_out, norm2_g, router_w, router_bias, exp_w_gate, exp_w_up, exp_w_down, sh_w_gate, sh_w_up, sh_w_down, final_g):
    xp = x_prompt.reshape(N_PROMPT, D_MODEL)
    xs = x_sample.reshape(N_SAMPLE, D_MODEL)
    cond8 = jnp.concatenate([c_ctx[None, :], c, jnp.zeros((8 - 1 - DEC_BATCH, D_MODEL), F32)], axis=0)
    mods3 = _mods(cond8, mod_w[0], mod_b).reshape(8, N_MOD, D_MODEL)

    win, wgt, wuqp, wukp, wuv = _pack_weights(w_in[0], w_uq[0], w_ukv[0])
    q, k, v, z, gt, ckv_new, kr_new = _inproj(
        xp, xs, mods3, norm1_g, q_norm_g, kv_norm_g, win, wgt, wuqp, wukp, wuv, _rope_tables())

    krope_pad = jnp.pad(cache_krope[:, 0], ((0, 0), (0, 0),
                                            (QK_NOPE_DIM, HEAD_PAD - QK_NOPE_DIM - QK_ROPE_DIM)))
    kc, vc = _ctxkv(cache_ckv[:, 0], krope_pad, wukp, wuv)
    attn_p = _attn_prompt(q, k, v)
    attn_s = _attn_sample(q, k, v, kc, vc)

    rwt = router_w[0].T
    rwh = rwt.astype(BF16)
    rwl = (rwt - rwh.astype(F32)).astype(BF16)
    cw = conv_w[0].reshape(CONV_K, D_MODEL // LANE, LANE).transpose(1, 0, 2)
    x1, h2, gates_t = _mix(
        attn_p, attn_s, z, gt, xp, xs, mods3,
        w_o_attn[0], w_pw2[0], w_out[0],
        cw, conv_b, conv_ln_g, conv_ln_b, norm2_g, rwh, rwl, router_bias.reshape(N_EXPERTS, 1))

    pos, cnt = _plan(gates_t)
    tile_start, tile_count, tile_expert, in_src, out_dst, sub_used, seg = _tile_tables(cnt)
    expert_rows = _experts(tile_start, tile_count, tile_expert, in_src, out_dst,
                           _dispatch(sub_used, h2, pos, seg).reshape(-1, CHUNK, D_MODEL),
                           exp_w_gate[0], exp_w_up[0], exp_w_down[0])
    gates = gates_t.T
    gates_hi = gates.astype(BF16)
    gates_lo = (gates - gates_hi.astype(F32)).astype(BF16)
    y_p, y_s = _combine(sub_used, expert_rows.reshape(-1, D_MODEL), pos.T,
                        jnp.concatenate([gates_hi, gates_lo], axis=1), seg, h2, x1, mods3,
                        sh_w_gate[0], sh_w_up[0], sh_w_down[0], final_g[None, :])
    return (y_p.reshape(BATCH, SEQ, D_MODEL),
            y_s.reshape(DEC_BATCH, DEC_SEQ, D_MODEL),
            ckv_new.reshape(BATCH, 1, SEQ, KV_LORA),
            kr_new.reshape(BATCH, 1, SEQ, QK_ROPE_DIM))
```

```python
import functools
import math

import jax
import jax.numpy as jnp
import numpy as np
from jax import lax
from jax.experimental import pallas as pl
from jax.experimental.pallas import tpu as pltpu

F32 = jnp.float32
BF16 = jnp.bfloat16

D_MODEL = 1024
BATCH = 16
SEQ = 256
DEC_BATCH = 2
DEC_SEQ = 2048
PAST_LEN = 256
GRID_W = 64
N_HEADS = 16
QK_NOPE_DIM = 64
QK_ROPE_DIM = 32
V_DIM = 64
Q_LORA = 384
KV_LORA = 256
ROPE_BASE = 10000.0
ATTN_SCALE = 1.0 / math.sqrt(QK_NOPE_DIM + QK_ROPE_DIM)
CONV_K = 31
CONV_PAD = CONV_K // 2
N_EXPERTS = 64
TOP_K = 8
N_GROUPS = 8
GROUP_SIZE = N_EXPERTS // N_GROUPS
TOPK_GROUPS = 4
EXPERT_FF = 256
ROUTED_SCALE = 2.5
EPS = 1e-6
N_MOD = 6

LANE = 128
HEAD_PAD = LANE
QK_COLS = N_HEADS * HEAD_PAD
N_PROMPT = BATCH * SEQ
N_SAMPLE = DEC_BATCH * DEC_SEQ
N_TOK = N_PROMPT + N_SAMPLE

TM = 256
NP_TILES = N_PROMPT // TM
NS_TILES = N_SAMPLE // TM
N_TILES = NP_TILES + NS_TILES
TILES_PER_DEC = DEC_SEQ // TM
HALO = 16
S_Q = 0
S_KV = S_Q + Q_LORA
S_KR = S_KV + KV_LORA
S_CONV = S_KR + QK_ROPE_DIM
S_GATE = S_CONV + 2 * D_MODEL
PHASE = S_CONV % LANE
CONV_WIN = 2 * D_MODEL + LANE
GATE_WIN = 2 * D_MODEL + LANE
IN_COLS = S_GATE + 2 * D_MODEL
V_COLS = N_HEADS * V_DIM

VMEM_CAP = 56 * 1024 * 1024


def _cparams(sem, vmem_mb):
    return pltpu.CompilerParams(dimension_semantics=sem,
                                vmem_limit_bytes=min(vmem_mb * 1024 * 1024, VMEM_CAP))


def _const_spec(shape):
    n = len(shape)
    return pl.BlockSpec(shape, lambda *_: (0,) * n, pipeline_mode=pl.Buffered(1))


def _seg_of_tile(i):
    return jnp.where(i < NP_TILES, 0, 1 + (i - NP_TILES) // TILES_PER_DEC)


def _rms(x, g):
    return x * lax.rsqrt(jnp.mean(x * x, axis=-1, keepdims=True) + EPS) * g


_NT = (((1,), (1,)), ((), ()))


def _mods_kernel(cond_ref, w_ref, b_ref, o_ref):
    c = cond_ref[...]
    s = c * jax.nn.sigmoid(c)
    o_ref[...] = jnp.dot(s.astype(BF16), w_ref[...].astype(BF16),
                         preferred_element_type=F32) + b_ref[...]


def _mods(cond8, mod_w, mod_b):
    n = N_MOD * D_MODEL
    bn = 512
    return pl.pallas_call(
        _mods_kernel,
        grid=(n // bn,),
        in_specs=[pl.BlockSpec((8, D_MODEL), lambda j: (0, 0)),
                  pl.BlockSpec((D_MODEL, bn), lambda j: (0, j)),
                  pl.BlockSpec((1, bn), lambda j: (0, j))],
        out_specs=pl.BlockSpec((8, bn), lambda j: (0, j)),
        out_shape=jax.ShapeDtypeStruct((8, n), F32),
        compiler_params=_cparams(("arbitrary",), 16),
        name="mods",
    )(cond8, mod_w, mod_b)


def _keys_t(wukp_t, ckv, krope):
    eye = (lax.broadcasted_iota(jnp.int32, (HEAD_PAD, HEAD_PAD), 0)
           == lax.broadcasted_iota(jnp.int32, (HEAD_PAD, HEAD_PAD), 1))
    rope_t = lax.dot_general(jnp.where(eye, 1.0, 0.0).astype(BF16), krope, _NT,
                             preferred_element_type=F32)
    nope_t = lax.dot_general(wukp_t, ckv, _NT, preferred_element_type=F32)
    return (nope_t + jnp.concatenate([rope_t] * N_HEADS, axis=0)).astype(BF16)


def _rope_swap(x):
    n = x.shape[1]
    half = QK_ROPE_DIM // 4
    lane = lax.broadcasted_iota(jnp.int32, x.shape, 1)
    return jnp.where(lane % (2 * half) < half,
                     pltpu.roll(x, n - half, axis=1), pltpu.roll(x, half, axis=1))


def _inproj_kernel(xp_ref, xs_ref, mods_ref, g1_ref, qg_ref, kvg_ref, win_ref, wgt_ref, wuqp_ref,
                   wukp_ref, wuv_ref, tab_ref,
                   q_ref, k_ref, v_ref, z_ref, gt_ref, ckv_ref, kr_ref):
    i = pl.program_id(0)
    x = jnp.where(i < NP_TILES, xp_ref[...], xs_ref[...])
    m = mods_ref[0]
    h = _rms(x, g1_ref[...]) * (1.0 + m[1:2]) + m[0:1]
    hb = h.astype(BF16)

    def proj(c0, c1):
        return jnp.dot(hb, win_ref[:, c0:c1], preferred_element_type=F32)

    def lanes16(t):
        return jnp.concatenate([t] * N_HEADS, axis=1)


    qn = _rms(proj(S_Q, S_KV), qg_ref[...]).astype(BF16)
    q = jnp.dot(qn, wuqp_ref[...], preferred_element_type=F32)
    q_ref[...] = (q * lanes16(tab_ref[0]) + _rope_swap(q) * lanes16(tab_ref[1])).astype(BF16)

    ckv = _rms(proj(S_KV, S_KR), kvg_ref[...])
    ckvb = ckv.astype(BF16)
    u = proj(S_KR, S_KR + CONV_WIN)
    ua = u[:, :LANE]
    krot = pltpu.roll(ua * tab_ref[2] + _rope_swap(ua) * tab_ref[3], QK_NOPE_DIM, axis=1)
    k_ref[...] = _keys_t(wukp_ref[...], ckvb, krot.astype(BF16))
    v_ref[...] = jnp.dot(ckvb, wuv_ref[...], preferred_element_type=F32).astype(BF16)

    @pl.when(i < NP_TILES)
    def _():
        ckv_ref[...] = ckv
        kr_ref[...] = ua[:, :QK_ROPE_DIM]

    glu = u[:, :D_MODEL + LANE] * jax.nn.sigmoid(u[:, D_MODEL:])
    z_ref[...] = pltpu.roll(glu, D_MODEL + LANE - PHASE, axis=1)[:, :D_MODEL].astype(BF16)
    gates = jax.nn.sigmoid(jnp.concatenate(
        [proj(S_GATE - PHASE, S_GATE - PHASE + 2 * D_MODEL),
         jnp.dot(hb, wgt_ref[...], preferred_element_type=F32)], axis=1))
    gt_ref[...] = pltpu.roll(gates, GATE_WIN - PHASE, axis=1)[:, :2 * D_MODEL].astype(BF16)


def _inproj(xp, xs, mods3, g1, qg, kvg, win, wgt, wuqp, wukp, wuv, tab):
    tile = lambda i: (i, 0)
    ptile = lambda i: (jnp.minimum(i, NP_TILES - 1), 0)
    stile = lambda i: (jnp.maximum(i - NP_TILES, 0), 0)
    tab_blk = lambda i: (0, jnp.where(i < NP_TILES, 0, 1 + (i - NP_TILES) % TILES_PER_DEC), 0)
    return pl.pallas_call(
        _inproj_kernel,
        grid=(N_TILES,),
        in_specs=[pl.BlockSpec((TM, D_MODEL), ptile),
                  pl.BlockSpec((TM, D_MODEL), stile),
                  pl.BlockSpec((1, N_MOD, D_MODEL), lambda i: (_seg_of_tile(i), 0, 0)),
                  _const_spec((1, D_MODEL)),
                  _const_spec((1, Q_LORA)),
                  _const_spec((1, KV_LORA)),
                  _const_spec((D_MODEL, IN_COLS)),
                  _const_spec((D_MODEL, LANE)),
                  _const_spec((Q_LORA, QK_COLS)),
                  _const_spec((QK_COLS, KV_LORA)),
                  _const_spec((KV_LORA, V_COLS)),
                  pl.BlockSpec((4, TM, LANE), tab_blk)],
        out_specs=[pl.BlockSpec((TM, QK_COLS), tile),
                   pl.BlockSpec((QK_COLS, TM), lambda i: (0, i)),
                   pl.BlockSpec((TM, V_COLS), tile),
                   pl.BlockSpec((TM, D_MODEL), tile),
                   pl.BlockSpec((TM, 2 * D_MODEL), tile),
                   pl.BlockSpec((TM, KV_LORA), ptile),
                   pl.BlockSpec((TM, QK_ROPE_DIM), ptile)],
        out_shape=[jax.ShapeDtypeStruct((N_TOK, QK_COLS), BF16),
                   jax.ShapeDtypeStruct((QK_COLS, N_TOK), BF16),
                   jax.ShapeDtypeStruct((N_TOK, V_COLS), BF16),
                   jax.ShapeDtypeStruct((N_TOK, D_MODEL), BF16),
                   jax.ShapeDtypeStruct((N_TOK, 2 * D_MODEL), BF16),
                   jax.ShapeDtypeStruct((N_PROMPT, KV_LORA), F32),
                   jax.ShapeDtypeStruct((N_PROMPT, QK_ROPE_DIM), F32)],
        compiler_params=_cparams(("arbitrary",), 52),
        name="inproj",
    )(xp, xs, mods3, g1, qg, kvg, win, wgt, wuqp, wukp, wuv, tab)


def _ctxkv_kernel(ckv_ref, kr_ref, wukp_ref, wuv_ref, k_ref, v_ref):
    c = ckv_ref[0].astype(BF16)
    k_ref[0] = _keys_t(wukp_ref[...], c, kr_ref[0].astype(BF16))
    v_ref[0] = jnp.dot(c, wuv_ref[...], preferred_element_type=F32).astype(BF16)


def _ctxkv(cache_ckv, krope_pad, wukp, wuv):
    return pl.pallas_call(
        _ctxkv_kernel,
        grid=(DEC_BATCH,),
        in_specs=[pl.BlockSpec((1, PAST_LEN, KV_LORA), lambda b: (b, 0, 0)),
                  pl.BlockSpec((1, PAST_LEN, LANE), lambda b: (b, 0, 0)),
                  _const_spec((QK_COLS, KV_LORA)),
                  _const_spec((KV_LORA, V_COLS))],
        out_specs=[pl.BlockSpec((1, QK_COLS, PAST_LEN), lambda b: (b, 0, 0)),
                   pl.BlockSpec((1, PAST_LEN, V_COLS), lambda b: (b, 0, 0))],
        out_shape=[jax.ShapeDtypeStruct((DEC_BATCH, QK_COLS, PAST_LEN), BF16),
                   jax.ShapeDtypeStruct((DEC_BATCH, PAST_LEN, V_COLS), BF16)],
        compiler_params=_cparams(("arbitrary",), 16),
        name="ctxkv",
    )(cache_ckv, krope_pad, wukp, wuv)


def _attn_kernel(*refs, has_ctx):
    if has_ctx:
        q_ref, k_ref, v_ref, kc_ref, vc_ref, o_ref = refs
    else:
        q_ref, k_ref, v_ref, o_ref = refs
    tq = q_ref.shape[0]
    lane = lax.broadcasted_iota(jnp.int32, (tq, LANE), 1)
    for j in range(N_HEADS // 2):
        pair = slice(j * LANE, (j + 1) * LANE)
        ps, pcs, dens = [], [], []
        for h in (2 * j, 2 * j + 1):
            hs = slice(h * HEAD_PAD, (h + 1) * HEAD_PAD)
            qh = q_ref[:, hs]
            s = jnp.dot(qh, k_ref[hs, :], preferred_element_type=F32)
            mx = jnp.max(s, axis=-1, keepdims=True)
            if has_ctx:
                sc = jnp.dot(qh, kc_ref[0, hs, :], preferred_element_type=F32)
                mx = jnp.maximum(mx, jnp.max(sc, axis=-1, keepdims=True))
            p = jnp.exp2(s - mx)
            den = jnp.sum(p, axis=-1, keepdims=True)
            ps.append(p.astype(BF16))
            if has_ctx:
                pc = jnp.exp2(sc - mx)
                den = den + jnp.sum(pc, axis=-1, keepdims=True)
                pcs.append(pc.astype(BF16))
            dens.append(den)
        o = jnp.dot(jnp.concatenate(ps, axis=0), v_ref[:, pair], preferred_element_type=F32)
        if has_ctx:
            o = o + jnp.dot(jnp.concatenate(pcs, axis=0), vc_ref[0, :, pair],
                            preferred_element_type=F32)
        o_ref[:, pair] = jnp.where(lane < V_DIM, o[:tq] / dens[0], o[tq:] / dens[1]).astype(BF16)


def _attn_prompt(q, k, v):
    blk = lambda b: (b, 0)
    return pl.pallas_call(
        functools.partial(_attn_kernel, has_ctx=False),
        grid=(BATCH,),
        in_specs=[pl.BlockSpec((SEQ, QK_COLS), blk),
                  pl.BlockSpec((QK_COLS, SEQ), lambda b: (0, b)),
                  pl.BlockSpec((SEQ, V_COLS), blk)],
        out_specs=pl.BlockSpec((SEQ, N_HEADS * V_DIM), blk),
        out_shape=jax.ShapeDtypeStruct((N_PROMPT, N_HEADS * V_DIM), BF16),
        compiler_params=_cparams(("arbitrary",), 24),
        name="attn_prompt",
    )(q, k, v)


ATT_TQ = 256


def _attn_sample(q, k, v, kc, vc):
    nq = DEC_SEQ // ATT_TQ
    q0 = N_PROMPT // ATT_TQ
    s0 = N_PROMPT // DEC_SEQ
    once = pl.Buffered(1)
    return pl.pallas_call(
        functools.partial(_attn_kernel, has_ctx=True),
        grid=(DEC_BATCH, nq),
        in_specs=[pl.BlockSpec((ATT_TQ, QK_COLS), lambda b, t: (q0 + b * nq + t, 0)),
                  pl.BlockSpec((QK_COLS, DEC_SEQ), lambda b, t: (0, s0 + b), pipeline_mode=once),
                  pl.BlockSpec((DEC_SEQ, V_COLS), lambda b, t: (s0 + b, 0), pipeline_mode=once),
                  pl.BlockSpec((1, QK_COLS, PAST_LEN), lambda b, t: (b, 0, 0), pipeline_mode=once),
                  pl.BlockSpec((1, PAST_LEN, V_COLS), lambda b, t: (b, 0, 0), pipeline_mode=once)],
        out_specs=pl.BlockSpec((ATT_TQ, N_HEADS * V_DIM), lambda b, t: (b * nq + t, 0)),
        out_shape=jax.ShapeDtypeStruct((N_SAMPLE, N_HEADS * V_DIM), BF16),
        compiler_params=_cparams(("arbitrary", "arbitrary"), 54),
        name="attn_sample",
    )(q, k, v, kc, vc)


def _topk_rows(v, k, n):
    rows = lax.broadcasted_iota(jnp.int32, v.shape, 0).astype(F32)
    sel = jnp.zeros(v.shape, F32)
    for _ in range(k):
        mx = jnp.max(v, axis=0, keepdims=True)
        first = jnp.min(jnp.where(v == mx, rows, float(n)), axis=0, keepdims=True)
        hit = rows == first
        sel = jnp.where(hit, 1.0, sel)
        v = jnp.where(hit, -jnp.inf, v)
    return sel > 0.0


def _route(logits_t, bias):
    scores = jax.nn.sigmoid(logits_t)
    sel = scores + bias
    gscore = []
    for g in range(N_GROUPS):
        blk = sel[g * GROUP_SIZE:(g + 1) * GROUP_SIZE]
        top2 = _topk_rows(blk, 2, GROUP_SIZE)
        gscore.append(jnp.sum(jnp.where(top2, blk, 0.0), axis=0, keepdims=True))
    masked = []
    for g in range(N_GROUPS):
        beaten = jnp.zeros(gscore[g].shape, F32)
        for o in range(N_GROUPS):
            if o < g:
                beaten = beaten + jnp.where(gscore[o] >= gscore[g], 1.0, 0.0)
            elif o > g:
                beaten = beaten + jnp.where(gscore[o] > gscore[g], 1.0, 0.0)
        keep = beaten < float(TOPK_GROUPS)
        masked.append(jnp.where(keep, sel[g * GROUP_SIZE:(g + 1) * GROUP_SIZE], -jnp.inf))
    chosen = _topk_rows(jnp.concatenate(masked, axis=0), TOP_K, N_EXPERTS)
    w = jnp.where(chosen, scores, 0.0)
    return w / jnp.sum(w, axis=0, keepdims=True) * ROUTED_SCALE


def _mix_kernel(ap_ref, as_ref, z_ref, zp_ref, zn_ref, gt_ref, xp_ref, xs_ref, mods_ref,
                wo32_ref, wpw32_ref, wout32_ref, cw_ref, cb_ref, lng_ref, lnb_ref, g2_ref,
                rwh_ref, rwl_ref, rb_ref,
                x1_ref, h2_ref, gates_ref, win_ref, cz_ref, wo_ref, wpw_ref, wout_ref):
    i = pl.program_id(0)

    @pl.when(i == 0)
    def _():
        wo_ref[...] = wo32_ref[...].astype(BF16)
        wpw_ref[...] = wpw32_ref[...].astype(BF16)
        wout_ref[...] = wout32_ref[...].astype(BF16)

    is_prompt = i < NP_TILES
    pos = (i - NP_TILES) % TILES_PER_DEC
    first = jnp.logical_or(is_prompt, pos == 0)
    last = jnp.logical_or(is_prompt, pos == TILES_PER_DEC - 1)

    nchunk = D_MODEL // LANE
    zprev = jnp.where(first, 0.0, zp_ref[...].astype(F32))
    znext = jnp.where(last, 0.0, zn_ref[...].astype(F32))
    zc = z_ref[...].astype(F32)
    for c in range(nchunk):
        cs = slice(c * LANE, (c + 1) * LANE)
        win_ref[c, 0:HALO, :] = zprev[:, cs]
        win_ref[c, HALO:HALO + TM, :] = zc[:, cs]
        win_ref[c, HALO + TM:, :] = znext[:, cs]

    def conv_chunk(c, carry):
        w = cw_ref[c]
        acc = jnp.zeros((TM, LANE), F32)
        for kk in range(CONV_K):
            off = HALO - CONV_PAD + kk
            acc = acc + win_ref[c, off:off + TM, :] * w[kk:kk + 1, :]
        cz_ref[c] = acc
        return carry

    lax.fori_loop(0, nchunk, conv_chunk, 0)

    m = mods_ref[0]
    rwh = rwh_ref[...]
    rows = [slice(hf * (TM // 2), (hf + 1) * (TM // 2)) for hf in range(2)]

    def conv_act(rs):
        conv = jnp.concatenate([cz_ref[c, rs, :] for c in range(nchunk)], axis=1) + cb_ref[...]
        mu = jnp.mean(conv, axis=-1, keepdims=True)
        cen = conv - mu
        var = jnp.mean(cen * cen, axis=-1, keepdims=True)
        ln = cen * lax.rsqrt(var + EPS) * lng_ref[...] + lnb_ref[...]
        return (ln * jax.nn.sigmoid(ln)).astype(BF16)

    def branches(rs, act):
        attn = jnp.where(is_prompt, ap_ref[rs, :], as_ref[rs, :])
        return (jnp.dot(attn, wo_ref[...], preferred_element_type=F32),
                jnp.dot(act, wpw_ref[...], preferred_element_type=F32))

    def merge(rs, outs):
        gt = gt_ref[rs, :].astype(F32)
        return (gt[:, :D_MODEL] * outs[0] + gt[:, D_MODEL:] * outs[1]).astype(BF16)

    def project(merged):
        return jnp.dot(merged, wout_ref[...], preferred_element_type=F32)

    def residual(rs, mix):
        x = jnp.where(is_prompt, xp_ref[rs, :], xs_ref[rs, :])
        x1 = x + m[2:3] * mix
        x1_ref[rs, :] = x1
        h2 = _rms(x1, g2_ref[...]) * (1.0 + m[4:5]) + m[3:4]
        hh = h2.astype(BF16)
        h2_ref[rs, :] = hh
        return hh, (h2 - hh.astype(F32)).astype(BF16)

    def route(rs, hs):
        hh, hl = hs
        logits_t = (lax.dot_general(rwh, hh, _NT, preferred_element_type=F32)
                    + lax.dot_general(rwh, hl, _NT, preferred_element_type=F32)
                    + lax.dot_general(rwl_ref[...], hh, _NT, preferred_element_type=F32))
        gates_ref[:, rs] = _route(logits_t, rb_ref[...])

    r0, r1 = rows
    act0 = conv_act(r0)
    out0 = branches(r0, act0)
    act1 = conv_act(r1)
    mrg0 = merge(r0, out0)
    out1 = branches(r1, act1)
    mix0 = project(mrg0)
    mrg1 = merge(r1, out1)
    hs0 = residual(r0, mix0)
    mix1 = project(mrg1)
    route(r0, hs0)
    hs1 = residual(r1, mix1)
    route(r1, hs1)


def _mix(attn_p, attn_s, z, gt, xp, xs, mods3, wo, wpw, wout, cw, cb, lng, lnb, g2, rwh, rwl, rb):
    tile = lambda i: (i, 0)
    ptile = lambda i: (jnp.minimum(i, NP_TILES - 1), 0)
    stile = lambda i: (jnp.maximum(i - NP_TILES, 0), 0)
    hpt = TM // HALO
    nhb = N_TOK // HALO
    return pl.pallas_call(
        _mix_kernel,
        grid=(N_TILES,),
        in_specs=[pl.BlockSpec((TM, D_MODEL), ptile),
                  pl.BlockSpec((TM, D_MODEL), stile),
                  pl.BlockSpec((TM, D_MODEL), tile),
                  pl.BlockSpec((HALO, D_MODEL), lambda i: (jnp.maximum(i * hpt - 1, 0), 0)),
                  pl.BlockSpec((HALO, D_MODEL), lambda i: (jnp.minimum((i + 1) * hpt, nhb - 1), 0)),
                  pl.BlockSpec((TM, 2 * D_MODEL), tile),
                  pl.BlockSpec((TM, D_MODEL), ptile),
                  pl.BlockSpec((TM, D_MODEL), stile),
                  pl.BlockSpec((1, N_MOD, D_MODEL), lambda i: (_seg_of_tile(i), 0, 0)),
                  _const_spec((D_MODEL, D_MODEL)),
                  _const_spec((D_MODEL, D_MODEL)),
                  _const_spec((D_MODEL, D_MODEL)),
                  _const_spec((D_MODEL // LANE, CONV_K, LANE)),
                  _const_spec((1, D_MODEL)),
                  _const_spec((1, D_MODEL)),
                  _const_spec((1, D_MODEL)),
                  _const_spec((1, D_MODEL)),
                  _const_spec((N_EXPERTS, D_MODEL)),
                  _const_spec((N_EXPERTS, D_MODEL)),
                  _const_spec((N_EXPERTS, 1))],
        out_specs=[pl.BlockSpec((TM, D_MODEL), tile),
                   pl.BlockSpec((TM, D_MODEL), tile),
                   pl.BlockSpec((N_EXPERTS, TM), lambda i: (0, i))],
        out_shape=[jax.ShapeDtypeStruct((N_TOK, D_MODEL), F32),
                   jax.ShapeDtypeStruct((N_TOK, D_MODEL), BF16),
                   jax.ShapeDtypeStruct((N_EXPERTS, N_TOK), F32)],
        scratch_shapes=[pltpu.VMEM((D_MODEL // LANE, TM + 2 * HALO, LANE), F32),
                        pltpu.VMEM((D_MODEL // LANE, TM, LANE), F32)]
        + [pltpu.VMEM((D_MODEL, D_MODEL), BF16)] * 3,
        compiler_params=_cparams(("arbitrary",), 52),
        name="mix",
    )(attn_p, attn_s, z, z, z, gt, xp, xs, mods3, wo, wpw, wout, cw, cb, lng, lnb, g2, rwh, rwl, rb)


CHUNK = 16
BLK_ROWS = TM * TOP_K + N_EXPERTS * CHUNK
BLK_CHUNKS = BLK_ROWS // CHUNK
SUB = 256
N_SUB = BLK_ROWS // SUB
SUB_ALWAYS = TM * TOP_K // SUB + 2
N_BLK = N_TOK // TM
TILE_CHUNKS = 32
TILE_ROWS = TILE_CHUNKS * CHUNK
MAX_CHUNKS = N_TOK * TOP_K // CHUNK + N_BLK * N_EXPERTS
MAX_TILES = MAX_CHUNKS // TILE_CHUNKS + N_EXPERTS
EXP_TILES = N_TOK // TILE_ROWS + 3
EXP_SLOTS = EXP_TILES * TILE_CHUNKS
N_BUF = 3
ZERO_CHUNK = N_BLK * BLK_CHUNKS
TRASH_CHUNK = ZERO_CHUNK + 1


def _plan_kernel(g_ref, pos_ref, cnt_ref):
    g = g_ref[...]
    picked = jnp.where(g > 0.0, 1.0, 0.0)
    r = lax.broadcasted_iota(jnp.int32, (TM, TM), 0)
    c = lax.broadcasted_iota(jnp.int32, (TM, TM), 1)
    upto = jnp.where(r <= c, 1.0, 0.0).astype(BF16)
    pos = picked * jnp.dot(picked.astype(BF16), upto, preferred_element_type=F32)
    pos_ref[0:N_EXPERTS, :] = pos.astype(BF16)
    pos_ref[N_EXPERTS:, :] = jnp.zeros((LANE - N_EXPERTS, TM), BF16)
    n = jnp.sum(picked, axis=1, keepdims=True)
    nchunk = jnp.floor((n + float(CHUNK - 1)) * (1.0 / CHUNK))
    cnt_ref[0] = jnp.broadcast_to(nchunk, (N_EXPERTS, LANE))


def _plan(gates_t):
    return pl.pallas_call(
        _plan_kernel,
        grid=(N_BLK,),
        in_specs=[pl.BlockSpec((N_EXPERTS, TM), lambda b: (0, b))],
        out_specs=[pl.BlockSpec((LANE, TM), lambda b: (0, b)),
                   pl.BlockSpec((1, N_EXPERTS, LANE), lambda b: (b, 0, 0))],
        out_shape=[jax.ShapeDtypeStruct((LANE, N_TOK), BF16),
                   jax.ShapeDtypeStruct((N_BLK, N_EXPERTS, LANE), F32)],
        compiler_params=_cparams(("arbitrary",), 16),
        name="plan",
    )(gates_t)


def _tile_tables(cnt):
    c = cnt[:, :, 0].astype(jnp.int32)
    seg_chunk = jnp.cumsum(c, axis=1) - c
    ct = c.T
    seg_q = jnp.cumsum(ct, axis=1) - ct
    tiles_e = (ct.sum(axis=1) + TILE_CHUNKS - 1) // TILE_CHUNKS
    tile_end = jnp.cumsum(tiles_e)
    tile_start = tile_end - tiles_e
    q = jnp.arange(EXP_SLOTS, dtype=jnp.int32)
    inside = jnp.logical_and(q[None, :, None] >= seg_q[:, None, :],
                             q[None, :, None] < (seg_q + ct)[:, None, :])
    base = jnp.arange(N_BLK, dtype=jnp.int32)[None, :] * BLK_CHUNKS + seg_chunk.T - seg_q
    src = q[None, :] + jnp.sum(jnp.where(inside, base[:, None, :], 0), axis=2)
    valid = jnp.any(inside, axis=2)
    buf = (tile_start[:, None] + q[None, :] // TILE_CHUNKS) % N_BUF
    trash = TRASH_CHUNK + buf * TILE_CHUNKS + q[None, :] % TILE_CHUNKS
    in_src = jnp.where(valid, src, ZERO_CHUNK).astype(jnp.int32).reshape(-1)
    out_dst = jnp.where(valid, src, trash).astype(jnp.int32).reshape(-1)
    t = jnp.arange(MAX_TILES + N_BUF, dtype=jnp.int32)
    tile_expert = jnp.minimum(jnp.sum((t[:, None] >= tile_end[None, :]).astype(jnp.int32), axis=1),
                              N_EXPERTS - 1)
    tile_start = jnp.concatenate([tile_start, tile_end[-1:]]).astype(jnp.int32)
    sub_used = (jnp.sum(c, axis=1) * CHUNK + SUB - 1) // SUB
    pad = jnp.full((N_BLK, LANE - N_EXPERTS), float(2 * BLK_ROWS), F32)
    first = jnp.concatenate([(seg_chunk * CHUNK).astype(F32), pad], axis=1)
    last = jnp.concatenate([((seg_chunk + c) * CHUNK).astype(F32), pad], axis=1)
    first_chunk = jnp.concatenate([seg_chunk.astype(F32), jnp.zeros_like(pad)], axis=1)
    seg = dict(
        first_row=jnp.broadcast_to(first[:, None, :], (N_BLK, 8, LANE)),
        last_row=jnp.broadcast_to(last[:, None, :], (N_BLK, 8, LANE)),
        first_col=jnp.broadcast_to(first[:, :, None], (N_BLK, LANE, LANE)),
        last_col=jnp.broadcast_to(last[:, :, None], (N_BLK, LANE, LANE)),
        first_chunk_col=jnp.broadcast_to(first_chunk[:, :, None], (N_BLK, LANE, LANE)).astype(BF16))
    return (tile_start, tiles_e.astype(jnp.int32), tile_expert, in_src, out_dst,
            sub_used.astype(jnp.int32), seg)


def _dispatch_kernel(sub_ref, h_ref, pos_ref, first_ref, last_ref, fchunk_ref, s_ref):
    b = pl.program_id(0)
    n_sub = jnp.where(b < N_BLK, sub_ref[jnp.minimum(b, N_BLK - 1)], 0)

    def locate(sb):
        rows = (lax.broadcasted_iota(jnp.int32, (SUB, LANE), 0) + sb * SUB).astype(F32)
        own = (jnp.where(rows >= first_ref[0, 0:1, :], 1.0, 0.0)
               - jnp.where(rows >= last_ref[0, 0:1, :], 1.0, 0.0)).astype(BF16)
        both = jnp.dot(own, jnp.concatenate([pos_ref[...], fchunk_ref[0]], axis=1),
                       preferred_element_type=F32)
        return rows, both

    def gather(sb, located):
        rows, both = located
        have = both[:, :TM]
        want = rows + 1.0 - float(CHUNK) * both[:, TM:]
        p = jnp.where(have == jnp.concatenate([want] * (TM // LANE), axis=1), 1.0, 0.0)
        s_ref[sb * SUB:(sb + 1) * SUB, :] = jnp.dot(
            p.astype(BF16), h_ref[...], preferred_element_type=F32).astype(BF16)

    def fill(sb):
        gather(sb, locate(sb))

    def clear(sb):
        s_ref[sb * SUB:(sb + 1) * SUB, :] = jnp.zeros((SUB, D_MODEL), BF16)

    @pl.when(b < N_BLK)
    def _():
        ahead = locate(0)
        for sb in range(SUB_ALWAYS):
            cur, ahead = ahead, (locate(sb + 1) if sb + 1 < SUB_ALWAYS else None)
            gather(sb, cur)

    @pl.when(b == N_BLK)
    def _():
        for sb in range(SUB_ALWAYS):
            clear(sb)

    for sb in range(SUB_ALWAYS, N_SUB):
        pl.when(sb < n_sub)(functools.partial(fill, sb))
        pl.when(sb >= n_sub)(functools.partial(clear, sb))


def _dispatch(sub_used, h2, pos, seg):
    last = N_BLK - 1
    blk3 = lambda b, *_: (jnp.minimum(b, last), 0, 0)
    grid_spec = pltpu.PrefetchScalarGridSpec(
        num_scalar_prefetch=1,
        grid=(N_BLK + 1,),
        in_specs=[pl.BlockSpec((TM, D_MODEL), lambda b, *_: (jnp.minimum(b, last), 0)),
                  pl.BlockSpec((LANE, TM), lambda b, *_: (0, jnp.minimum(b, last))),
                  pl.BlockSpec((1, 8, LANE), blk3),
                  pl.BlockSpec((1, 8, LANE), blk3),
                  pl.BlockSpec((1, LANE, LANE), blk3)],
        out_specs=pl.BlockSpec((BLK_ROWS, D_MODEL), lambda b, *_: (b, 0)))
    return pl.pallas_call(
        _dispatch_kernel,
        grid_spec=grid_spec,
        out_shape=jax.ShapeDtypeStruct(((N_BLK + 1) * BLK_ROWS, D_MODEL), BF16),
        compiler_params=_cparams(("arbitrary",), 40),
        name="dispatch",
    )(sub_used, h2, pos, seg["first_row"], seg["last_row"], seg["first_chunk_col"])


def _experts_kernel(tstart_ref, tcount_ref, texp_ref, insrc_ref, outdst_ref,
                    s_hbm, wg_ref, wu_ref, wd_ref, o_hbm,
                    xb0, xb1, xb2, ob0, ob1, ob2, insem, outsem):
    xbufs = (xb0, xb1, xb2)
    obufs = (ob0, ob1, ob2)
    e = pl.program_id(0)
    n_tiles = tstart_ref[N_EXPERTS]
    t0 = tstart_ref[e]

    def slot_base(t):
        te = texp_ref[t]
        j = jnp.minimum(t - tstart_ref[te], EXP_TILES - 1)
        return te * EXP_SLOTS + j * TILE_CHUNKS

    def in_copy(src_chunk, buf, i, sem):
        return pltpu.make_async_copy(s_hbm.at[src_chunk], xbufs[buf].at[i], sem)

    def out_copy(dst_chunk, buf, i, sem):
        return pltpu.make_async_copy(obufs[buf].at[i], o_hbm.at[dst_chunk], sem)

    def start_in(t, buf):
        base = slot_base(t)
        for i in range(TILE_CHUNKS):
            in_copy(insrc_ref[base + i], buf, i, insem.at[buf]).start()

    def start_out(t, buf, real):
        base = slot_base(jnp.maximum(t, 0))
        for i in range(TILE_CHUNKS):
            dst = jnp.where(real, outdst_ref[base + i], TRASH_CHUNK + buf * TILE_CHUNKS + i)
            out_copy(dst, buf, i, outsem.at[buf]).start()

    def wait_in(sem_idx):
        for i in range(TILE_CHUNKS):
            in_copy(0, 0, i, insem.at[sem_idx]).wait()

    def wait_out(sem_idx):
        for i in range(TILE_CHUNKS):
            out_copy(0, 0, i, outsem.at[sem_idx]).wait()

    def on_buffer(buf_dyn, fn):
        for k in range(N_BUF):
            pl.when(buf_dyn == k)(functools.partial(fn, k))


    @pl.when(e == 0)
    def _():
        for ob in obufs:
            ob[...] = jnp.zeros(ob.shape, BF16)
        start_out(0, 0, False)
        start_out(0, 1, False)
        start_in(0, 0)
        start_in(1, 1)

    def tile_body(j, carry):
        t = t0 + j

        def run(k):
            nxt = (k + 2) % N_BUF
            wait_in(k)
            wait_out(k)
            start_in(t + 2, nxt)
            start_out(t - 1, nxt, t >= 1)
            x = xbufs[k][...].reshape(TILE_ROWS, D_MODEL)
            wgu = jnp.concatenate([wg_ref[0], wu_ref[0]], axis=1).astype(BF16)
            au = jnp.dot(x, wgu, preferred_element_type=F32)
            a, u = au[:, :EXPERT_FF], au[:, EXPERT_FF:]
            hid = (a * jax.nn.sigmoid(a) * u).astype(BF16)
            out = jnp.dot(hid, wd_ref[0].astype(BF16), preferred_element_type=F32).astype(BF16)
            obufs[k][...] = out.reshape(TILE_CHUNKS, CHUNK, D_MODEL)

        on_buffer(t % N_BUF, run)
        return carry

    lax.fori_loop(0, tcount_ref[e], tile_body, 0)

    @pl.when(e == N_EXPERTS - 1)
    def _():
        on_buffer((n_tiles + 2) % N_BUF, lambda k: start_out(n_tiles - 1, k, n_tiles >= 1))
        wait_in(n_tiles % N_BUF)
        wait_in((n_tiles + 1) % N_BUF)
        for k in range(N_BUF):
            wait_out(k)


def _experts(tile_start, tile_count, tile_expert, in_src, out_dst, sorted_rows, wg, wu, wd):
    wspec = lambda shape: pl.BlockSpec(shape, lambda e, *_: (e, 0, 0))
    grid_spec = pltpu.PrefetchScalarGridSpec(
        num_scalar_prefetch=5,
        grid=(N_EXPERTS,),
        in_specs=[pl.BlockSpec(memory_space=pl.ANY),
                  wspec((1, D_MODEL, EXPERT_FF)),
                  wspec((1, D_MODEL, EXPERT_FF)),
                  wspec((1, EXPERT_FF, D_MODEL))],
        out_specs=pl.BlockSpec(memory_space=pl.ANY),
        scratch_shapes=[pltpu.VMEM((TILE_CHUNKS, CHUNK, D_MODEL), BF16)] * (2 * N_BUF) + [
                        pltpu.SemaphoreType.DMA((N_BUF,)),
                        pltpu.SemaphoreType.DMA((N_BUF,))])
    return pl.pallas_call(
        _experts_kernel,
        grid_spec=grid_spec,
        out_shape=jax.ShapeDtypeStruct(sorted_rows.shape, BF16),
        input_output_aliases={5: 0},
        compiler_params=_cparams(("arbitrary",), 32),
        name="experts",
    )(tile_start, tile_count, tile_expert, in_src, out_dst, sorted_rows, wg, wu, wd)


def _combine_kernel(sub_ref, o_ref, post_ref, gate_ref, first_ref, last_ref, h_ref, x1_ref, mods_ref,
                    sg32_ref, su32_ref, sd32_ref, fg_ref, yp_ref, ys_ref, acc_ref, sg_ref, su_ref, sd_ref):
    i = pl.program_id(0)
    n_sub = sub_ref[i]

    @pl.when(i == 0)
    def _():
        sg_ref[...] = sg32_ref[...].astype(BF16)
        su_ref[...] = su32_ref[...].astype(BF16)
        sd_ref[...] = sd32_ref[...].astype(BF16)

    def locate(sb):
        rows = (lax.broadcasted_iota(jnp.int32, (LANE, SUB), 1) + sb * SUB).astype(F32)
        first = jnp.concatenate([first_ref[0]] * (SUB // LANE), axis=1)
        last = jnp.concatenate([last_ref[0]] * (SUB // LANE), axis=1)
        own = jnp.where(rows >= first, 1.0, 0.0) - jnp.where(rows >= last, 1.0, 0.0)
        start = jnp.sum(jnp.where(own > 0.0, first, 0.0), axis=0, keepdims=True)
        want = rows[0:1, :] + 1.0 - start
        ownb = own.astype(BF16)
        have = jnp.dot(post_ref[...], ownb, preferred_element_type=F32)
        own2 = jnp.concatenate([ownb[:N_EXPERTS], ownb[:N_EXPERTS]], axis=0)
        gate = jnp.dot(gate_ref[...], own2, preferred_element_type=F32)
        return have, want, gate

    def select(located):
        have, want, gate = located
        return jnp.where(have == want, gate, 0.0).astype(BF16)

    def weights_of(sb):
        return select(locate(sb))

    h = h_ref[...]
    a = jnp.dot(h, sg_ref[...], preferred_element_type=F32)
    u = jnp.dot(h, su_ref[...], preferred_element_type=F32)
    shared = jnp.dot((a * jax.nn.sigmoid(a) * u).astype(BF16), sd_ref[...],
                     preferred_element_type=F32)
    pws, ahead = [], locate(0)
    for sb in range(SUB_ALWAYS):
        cur, ahead = ahead, (locate(sb + 1) if sb + 1 < SUB_ALWAYS else None)
        pws.append(select(cur))
    pw_main = jnp.concatenate(pws, axis=1)
    acc_ref[...] = shared + jnp.dot(pw_main, o_ref[0:SUB_ALWAYS * SUB, :],
                                    preferred_element_type=F32)
    for sb in range(SUB_ALWAYS, N_SUB):
        @pl.when(sb < n_sub)
        def _():
            acc_ref[...] += jnp.dot(weights_of(sb), o_ref[sb * SUB:(sb + 1) * SUB, :],
                                    preferred_element_type=F32)

    m = mods_ref[0]
    y = _rms(x1_ref[...] + m[5:6] * acc_ref[...], fg_ref[...])

    @pl.when(i < NP_TILES)
    def _():
        yp_ref[...] = y

    @pl.when(i >= NP_TILES)
    def _():
        ys_ref[...] = y


def _combine(sub_used, expert_rows, pos_t, gate_t, seg, h2, x1, mods3, sg, su, sd, fg):
    tile = lambda i, *_: (i, 0)
    blk3 = lambda i, *_: (i, 0, 0)
    grid_spec = pltpu.PrefetchScalarGridSpec(
        num_scalar_prefetch=1,
        grid=(N_BLK,),
        in_specs=[pl.BlockSpec((BLK_ROWS, D_MODEL), tile),
                  pl.BlockSpec((TM, LANE), tile),
                  pl.BlockSpec((TM, LANE), tile),
                  pl.BlockSpec((1, LANE, LANE), blk3),
                  pl.BlockSpec((1, LANE, LANE), blk3),
                  pl.BlockSpec((TM, D_MODEL), tile),
                  pl.BlockSpec((TM, D_MODEL), tile),
                  pl.BlockSpec((1, N_MOD, D_MODEL), lambda i, *_: (_seg_of_tile(i), 0, 0)),
                  _const_spec((D_MODEL, EXPERT_FF)),
                  _const_spec((D_MODEL, EXPERT_FF)),
                  _const_spec((EXPERT_FF, D_MODEL)),
                  _const_spec((1, D_MODEL))],
        out_specs=[pl.BlockSpec((TM, D_MODEL), lambda i, *_: (jnp.minimum(i, NP_TILES - 1), 0)),
                   pl.BlockSpec((TM, D_MODEL), lambda i, *_: (jnp.maximum(i - NP_TILES, 0), 0))],
        scratch_shapes=[pltpu.VMEM((TM, D_MODEL), F32),
                        pltpu.VMEM((D_MODEL, EXPERT_FF), BF16),
                        pltpu.VMEM((D_MODEL, EXPERT_FF), BF16),
                        pltpu.VMEM((EXPERT_FF, D_MODEL), BF16)])
    return pl.pallas_call(
        _combine_kernel,
        grid_spec=grid_spec,
        out_shape=[jax.ShapeDtypeStruct((N_PROMPT, D_MODEL), F32),
                   jax.ShapeDtypeStruct((N_SAMPLE, D_MODEL), F32)],
        compiler_params=_cparams(("arbitrary",), 40),
        name="combine",
    )(sub_used, expert_rows, pos_t, gate_t, seg["first_col"], seg["last_col"], h2, x1, mods3,
      sg, su, sd, fg)


def _rope_tables():
    pos = np.arange(DEC_SEQ)
    row = (pos // GRID_W).astype(np.float64)
    col = (pos % GRID_W).astype(np.float64)
    n_freq = QK_ROPE_DIM // 4
    inv_freq = ROPE_BASE ** (-np.arange(n_freq, dtype=np.float64) / n_freq)
    ar, ac = row[:, None] * inv_freq, col[:, None] * inv_freq
    cos32 = np.concatenate([np.cos(ar), np.cos(ar), np.cos(ac), np.cos(ac)], axis=1)
    sin32 = np.concatenate([-np.sin(ar), np.sin(ar), -np.sin(ac), np.sin(ac)], axis=1)
    cos32 = np.concatenate([np.ones((TM, QK_ROPE_DIM)), cos32], axis=0)
    sin32 = np.concatenate([np.zeros((TM, QK_ROPE_DIM)), sin32], axis=0)
    n = cos32.shape[0]
    ones = np.ones((n, QK_NOPE_DIM))
    zeros = np.zeros((n, QK_NOPE_DIM))
    tail = np.zeros((n, HEAD_PAD - QK_NOPE_DIM - QK_ROPE_DIM))
    q_scale = ATTN_SCALE * math.log2(math.e)
    rest = np.zeros((n, HEAD_PAD - QK_ROPE_DIM))
    qc = np.concatenate([ones, cos32, tail], axis=1) * q_scale
    qs = np.concatenate([zeros, sin32, tail], axis=1) * q_scale
    kc = np.concatenate([cos32, rest], axis=1)
    ks = np.concatenate([sin32, rest], axis=1)
    return jnp.asarray(np.stack([qc, qs, kc, ks], axis=0).astype(np.float32))


def _pack_weights(w_in, w_uq, w_ukv):
    win = w_in.astype(BF16)
    gate_tail = w_in[:, S_GATE - PHASE + 2 * D_MODEL:]
    wgt = jnp.pad(gate_tail, ((0, 0), (0, LANE - PHASE))).astype(BF16)
    uq = w_uq.reshape(Q_LORA, N_HEADS, QK_NOPE_DIM + QK_ROPE_DIM)
    pad_q = HEAD_PAD - QK_NOPE_DIM - QK_ROPE_DIM
    wuqp = jnp.pad(uq, ((0, 0), (0, 0), (0, pad_q))).reshape(Q_LORA, QK_COLS).astype(BF16)
    ukv = w_ukv.reshape(KV_LORA, N_HEADS, QK_NOPE_DIM + V_DIM)
    wukp = jnp.pad(ukv[:, :, :QK_NOPE_DIM], ((0, 0), (0, 0), (0, HEAD_PAD - QK_NOPE_DIM)))
    wukp_t = wukp.reshape(KV_LORA, QK_COLS).T.astype(BF16)
    wuv = ukv[:, :, QK_NOPE_DIM:].reshape(KV_LORA, V_COLS).astype(BF16)
    return win, wgt, wuqp, wukp_t, wuv


def kernel(x_prompt, x_sample, cache_ckv, cache_krope, c, c_ctx, mod_w, mod_b, norm1_g, w_in, q_norm_g, w_uq, kv_norm_g, w_ukv, w_o_attn, conv_w, conv_b, conv_ln_g, conv_ln_b, w_pw2, w_out, norm2_g, router_w, router_bias, exp_w_gate, exp_w_up, exp_w_down, sh_w_gate, sh_w_up, sh_w_down, final_g):
    xp = x_prompt.reshape(N_PROMPT, D_MODEL)
    xs = x_sample.reshape(N_SAMPLE, D_MODEL)
    cond8 = jnp.concatenate([c_ctx[None, :], c, jnp.zeros((8 - 1 - DEC_BATCH, D_MODEL), F32)], axis=0)
    mods3 = _mods(cond8, mod_w[0], mod_b).reshape(8, N_MOD, D_MODEL)

    win, wgt, wuqp, wukp, wuv = _pack_weights(w_in[0], w_uq[0], w_ukv[0])
    q, k, v, z, gt, ckv_new, kr_new = _inproj(
        xp, xs, mods3, norm1_g, q_norm_g, kv_norm_g, win, wgt, wuqp, wukp, wuv, _rope_tables())

    krope_pad = jnp.pad(cache_krope[:, 0], ((0, 0), (0, 0),
                                            (QK_NOPE_DIM, HEAD_PAD - QK_NOPE_DIM - QK_ROPE_DIM)))
    kc, vc = _ctxkv(cache_ckv[:, 0], krope_pad, wukp, wuv)
    attn_p = _attn_prompt(q, k, v)
    attn_s = _attn_sample(q, k, v, kc, vc)

    rwt = router_w[0].T
    rwh = rwt.astype(BF16)
    rwl = (rwt - rwh.astype(F32)).astype(BF16)
    cw = conv_w[0].reshape(CONV_K, D_MODEL // LANE, LANE).transpose(1, 0, 2)
    x1, h2, gates_t = _mix(
        attn_p, attn_s, z, gt, xp, xs, mods3,
        w_o_attn[0], w_pw2[0], w_out[0],
        cw, conv_b, conv_ln_g, conv_ln_b, norm2_g, rwh, rwl, router_bias.reshape(N_EXPERTS, 1))

    pos, cnt = _plan(gates_t)
    tile_start, tile_count, tile_expert, in_src, out_dst, sub_used, seg = _tile_tables(cnt)
    expert_rows = _experts(tile_start, tile_count, tile_expert, in_src, out_dst,
                           _dispatch(sub_used, h2, pos, seg).reshape(-1, CHUNK, D_MODEL),
                           exp_w_gate[0], exp_w_up[0], exp_w_down[0])
    gates = gates_t.T
    gates_hi = gates.astype(BF16)
    gates_lo = (gates - gates_hi.astype(F32)).astype(BF16)
    y_p, y_s = _combine(sub_used, expert_rows.reshape(-1, D_MODEL), pos.T,
                        jnp.concatenate([gates_hi, gates_lo], axis=1), seg, h2, x1, mods3,
                        sh_w_gate[0], sh_w_up[0], sh_w_down[0], final_g[None, :])
    return (y_p.reshape(BATCH, SEQ, D_MODEL),
            y_s.reshape(DEC_BATCH, DEC_SEQ, D_MODEL),
            ckv_new.reshape(BATCH, 1, SEQ, KV_LORA),
            kr_new.reshape(BATCH, 1, SEQ, QK_ROPE_DIM))
```

```python
import functools
import math

import jax
import jax.numpy as jnp
import numpy as np
from jax import lax
from jax.experimental import pallas as pl
from jax.experimental.pallas import tpu as pltpu

F32 = jnp.float32
BF16 = jnp.bfloat16

D_MODEL = 1024
BATCH = 16
SEQ = 256
DEC_BATCH = 2
DEC_SEQ = 2048
PAST_LEN = 256
GRID_W = 64
N_HEADS = 16
QK_NOPE_DIM = 64
QK_ROPE_DIM = 32
V_DIM = 64
Q_LORA = 384
KV_LORA = 256
ROPE_BASE = 10000.0
ATTN_SCALE = 1.0 / math.sqrt(QK_NOPE_DIM + QK_ROPE_DIM)
CONV_K = 31
CONV_PAD = CONV_K // 2
N_EXPERTS = 64
TOP_K = 8
N_GROUPS = 8
GROUP_SIZE = N_EXPERTS // N_GROUPS
TOPK_GROUPS = 4
EXPERT_FF = 256
ROUTED_SCALE = 2.5
EPS = 1e-6
N_MOD = 6

LANE = 128
HEAD_PAD = LANE
QK_COLS = N_HEADS * HEAD_PAD
N_PROMPT = BATCH * SEQ
N_SAMPLE = DEC_BATCH * DEC_SEQ
N_TOK = N_PROMPT + N_SAMPLE

TM = 256
NP_TILES = N_PROMPT // TM
NS_TILES = N_SAMPLE // TM
N_TILES = NP_TILES + NS_TILES
TILES_PER_DEC = DEC_SEQ // TM
HALO = 16
S_Q = 0
S_KV = S_Q + Q_LORA
S_KR = S_KV + KV_LORA
S_CONV = S_KR + QK_ROPE_DIM
S_GATE = S_CONV + 2 * D_MODEL
PHASE = S_CONV % LANE
CONV_WIN = 2 * D_MODEL + LANE
GATE_WIN = 2 * D_MODEL + LANE
IN_COLS = S_GATE + 2 * D_MODEL
V_COLS = N_HEADS * V_DIM

VMEM_CAP = 56 * 1024 * 1024


def _cparams(sem, vmem_mb):
    return pltpu.CompilerParams(dimension_semantics=sem,
                                vmem_limit_bytes=min(vmem_mb * 1024 * 1024, VMEM_CAP))


def _const_spec(shape):
    n = len(shape)
    return pl.BlockSpec(shape, lambda *_: (0,) * n, pipeline_mode=pl.Buffered(1))


def _seg_of_tile(i):
    return jnp.where(i < NP_TILES, 0, 1 + (i - NP_TILES) // TILES_PER_DEC)


def _rms(x, g):
    return x * lax.rsqrt(jnp.mean(x * x, axis=-1, keepdims=True) + EPS) * g


_NT = (((1,), (1,)), ((), ()))


def _mods_kernel(cond_ref, w_ref, b_ref, o_ref):
    c = cond_ref[...]
    s = c * jax.nn.sigmoid(c)
    o_ref[...] = jnp.dot(s.astype(BF16), w_ref[...].astype(BF16),
                         preferred_element_type=F32) + b_ref[...]


def _mods(cond8, mod_w, mod_b):
    n = N_MOD * D_MODEL
    bn = 512
    return pl.pallas_call(
        _mods_kernel,
        grid=(n // bn,),
        in_specs=[pl.BlockSpec((8, D_MODEL), lambda j: (0, 0)),
                  pl.BlockSpec((D_MODEL, bn), lambda j: (0, j)),
                  pl.BlockSpec((1, bn), lambda j: (0, j))],
        out_specs=pl.BlockSpec((8, bn), lambda j: (0, j)),
        out_shape=jax.ShapeDtypeStruct((8, n), F32),
        compiler_params=_cparams(("arbitrary",), 16),
        name="mods",
    )(cond8, mod_w, mod_b)


def _keys_t(wukp_t, ckv, krope):
    eye = (lax.broadcasted_iota(jnp.int32, (HEAD_PAD, HEAD_PAD), 0)
           == lax.broadcasted_iota(jnp.int32, (HEAD_PAD, HEAD_PAD), 1))
    rope_t = lax.dot_general(jnp.where(eye, 1.0, 0.0).astype(BF16), krope, _NT,
                             preferred_element_type=F32)
    nope_t = lax.dot_general(wukp_t, ckv, _NT, preferred_element_type=F32)
    return (nope_t + jnp.concatenate([rope_t] * N_HEADS, axis=0)).astype(BF16)


def _rope_swap(x):
    n = x.shape[1]
    half = QK_ROPE_DIM // 4
    lane = lax.broadcasted_iota(jnp.int32, x.shape, 1)
    return jnp.where(lane % (2 * half) < half,
                     pltpu.roll(x, n - half, axis=1), pltpu.roll(x, half, axis=1))


def _inproj_kernel(xp_ref, xs_ref, mods_ref, g1_ref, qg_ref, kvg_ref, win_ref, wgt_ref, wuqp_ref,
                   wukp_ref, wuv_ref, tab_ref,
                   q_ref, k_ref, v_ref, z_ref, gt_ref, ckv_ref, kr_ref):
    i = pl.program_id(0)
    x = jnp.where(i < NP_TILES, xp_ref[...], xs_ref[...])
    m = mods_ref[0]
    h = _rms(x, g1_ref[...]) * (1.0 + m[1:2]) + m[0:1]
    hb = h.astype(BF16)

    def proj(c0, c1):
        return jnp.dot(hb, win_ref[:, c0:c1], preferred_element_type=F32)

    def lanes16(t):
        return jnp.concatenate([t] * N_HEADS, axis=1)

    qn = _rms(proj(S_Q, S_KV), qg_ref[...]).astype(BF16)
    q = jnp.dot(qn, wuqp_ref[...], preferred_element_type=F32)
    q_ref[...] = (q * lanes16(tab_ref[0]) + _rope_swap(q) * lanes16(tab_ref[1])).astype(BF16)

    ckv = _rms(proj(S_KV, S_KR), kvg_ref[...])
    ckvb = ckv.astype(BF16)
    u = proj(S_KR, S_KR + CONV_WIN)
    ua = u[:, :LANE]
    krot = pltpu.roll(ua * tab_ref[2] + _rope_swap(ua) * tab_ref[3], QK_NOPE_DIM, axis=1)
    k_ref[...] = _keys_t(wukp_ref[...], ckvb, krot.astype(BF16))
    v_ref[...] = jnp.dot(ckvb, wuv_ref[...], preferred_element_type=F32).astype(BF16)

    @pl.when(i < NP_TILES)
    def _():
        ckv_ref[...] = ckv
        kr_ref[...] = ua[:, :QK_ROPE_DIM]

    glu = u[:, :D_MODEL + LANE] * jax.nn.sigmoid(u[:, D_MODEL:])
    z_ref[...] = pltpu.roll(glu, D_MODEL + LANE - PHASE, axis=1)[:, :D_MODEL].astype(BF16)
    gates = jax.nn.sigmoid(jnp.concatenate(
        [proj(S_GATE - PHASE, S_GATE - PHASE + 2 * D_MODEL),
         jnp.dot(hb, wgt_ref[...], preferred_element_type=F32)], axis=1))
    gt_ref[...] = pltpu.roll(gates, GATE_WIN - PHASE, axis=1)[:, :2 * D_MODEL].astype(BF16)


def _inproj(xp, xs, mods3, g1, qg, kvg, win, wgt, wuqp, wukp, wuv, tab):
    tile = lambda i: (i, 0)
    ptile = lambda i: (jnp.minimum(i, NP_TILES - 1), 0)
    stile = lambda i: (jnp.maximum(i - NP_TILES, 0), 0)
    tab_blk = lambda i: (0, jnp.where(i < NP_TILES, 0, 1 + (i - NP_TILES) % TILES_PER_DEC), 0)
    return pl.pallas_call(
        _inproj_kernel,
        grid=(N_TILES,),
        in_specs=[pl.BlockSpec((TM, D_MODEL), ptile),
                  pl.BlockSpec((TM, D_MODEL), stile),
                  pl.BlockSpec((1, N_MOD, D_MODEL), lambda i: (_seg_of_tile(i), 0, 0)),
                  _const_spec((1, D_MODEL)),
                  _const_spec((1, Q_LORA)),
                  _const_spec((1, KV_LORA)),
                  _const_spec((D_MODEL, IN_COLS)),
                  _const_spec((D_MODEL, LANE)),
                  _const_spec((Q_LORA, QK_COLS)),
                  _const_spec((QK_COLS, KV_LORA)),
                  _const_spec((KV_LORA, V_COLS)),
                  pl.BlockSpec((4, TM, LANE), tab_blk)],
        out_specs=[pl.BlockSpec((TM, QK_COLS), tile),
                   pl.BlockSpec((QK_COLS, TM), lambda i: (0, i)),
                   pl.BlockSpec((TM, V_COLS), tile),
                   pl.BlockSpec((TM, D_MODEL), tile),
                   pl.BlockSpec((TM, 2 * D_MODEL), tile),
                   pl.BlockSpec((TM, KV_LORA), ptile),
                   pl.BlockSpec((TM, QK_ROPE_DIM), ptile)],
        out_shape=[jax.ShapeDtypeStruct((N_TOK, QK_COLS), BF16),
                   jax.ShapeDtypeStruct((QK_COLS, N_TOK), BF16),
                   jax.ShapeDtypeStruct((N_TOK, V_COLS), BF16),
                   jax.ShapeDtypeStruct((N_TOK, D_MODEL), BF16),
                   jax.ShapeDtypeStruct((N_TOK, 2 * D_MODEL), BF16),
                   jax.ShapeDtypeStruct((N_PROMPT, KV_LORA), F32),
                   jax.ShapeDtypeStruct((N_PROMPT, QK_ROPE_DIM), F32)],
        compiler_params=_cparams(("arbitrary",), 52),
        name="inproj",
    )(xp, xs, mods3, g1, qg, kvg, win, wgt, wuqp, wukp, wuv, tab)


def _ctxkv_kernel(ckv_ref, kr_ref, wukp_ref, wuv_ref, k_ref, v_ref):
    c = ckv_ref[0].astype(BF16)
    k_ref[0] = _keys_t(wukp_ref[...], c, kr_ref[0].astype(BF16))
    v_ref[0] = jnp.dot(c, wuv_ref[...], preferred_element_type=F32).astype(BF16)


def _ctxkv(cache_ckv, krope_pad, wukp, wuv):
    return pl.pallas_call(
        _ctxkv_kernel,
        grid=(DEC_BATCH,),
        in_specs=[pl.BlockSpec((1, PAST_LEN, KV_LORA), lambda b: (b, 0, 0)),
                  pl.BlockSpec((1, PAST_LEN, LANE), lambda b: (b, 0, 0)),
                  _const_spec((QK_COLS, KV_LORA)),
                  _const_spec((KV_LORA, V_COLS))],
        out_specs=[pl.BlockSpec((1, QK_COLS, PAST_LEN), lambda b: (b, 0, 0)),
                   pl.BlockSpec((1, PAST_LEN, V_COLS), lambda b: (b, 0, 0))],
        out_shape=[jax.ShapeDtypeStruct((DEC_BATCH, QK_COLS, PAST_LEN), BF16),
                   jax.ShapeDtypeStruct((DEC_BATCH, PAST_LEN, V_COLS), BF16)],
        compiler_params=_cparams(("arbitrary",), 16),
        name="ctxkv",
    )(cache_ckv, krope_pad, wukp, wuv)


def _attn_kernel(*refs, has_ctx):
    if has_ctx:
        q_ref, k_ref, v_ref, kc_ref, vc_ref, o_ref = refs
    else:
        q_ref, k_ref, v_ref, o_ref = refs
    tq = q_ref.shape[0]
    lane = lax.broadcasted_iota(jnp.int32, (tq, LANE), 1)
    for j in range(N_HEADS // 2):
        pair = slice(j * LANE, (j + 1) * LANE)
        ps, pcs, dens = [], [], []
        for h in (2 * j, 2 * j + 1):
            hs = slice(h * HEAD_PAD, (h + 1) * HEAD_PAD)
            qh = q_ref[:, hs]
            s = jnp.dot(qh, k_ref[hs, :], preferred_element_type=F32)
            mx = jnp.max(s, axis=-1, keepdims=True)
            if has_ctx:
                sc = jnp.dot(qh, kc_ref[0, hs, :], preferred_element_type=F32)
                mx = jnp.maximum(mx, jnp.max(sc, axis=-1, keepdims=True))
            p = jnp.exp2(s - mx)
            den = jnp.sum(p, axis=-1, keepdims=True)
            ps.append(p.astype(BF16))
            if has_ctx:
                pc = jnp.exp2(sc - mx)
                den = den + jnp.sum(pc, axis=-1, keepdims=True)
                pcs.append(pc.astype(BF16))
            dens.append(den)
        o = jnp.dot(jnp.concatenate(ps, axis=0), v_ref[:, pair], preferred_element_type=F32)
        if has_ctx:
            o = o + jnp.dot(jnp.concatenate(pcs, axis=0), vc_ref[0, :, pair],
                            preferred_element_type=F32)
        o_ref[:, pair] = jnp.where(lane < V_DIM, o[:tq] / dens[0], o[tq:] / dens[1]).astype(BF16)


def _attn_prompt(q, k, v):
    blk = lambda b: (b, 0)
    return pl.pallas_call(
        functools.partial(_attn_kernel, has_ctx=False),
        grid=(BATCH,),
        in_specs=[pl.BlockSpec((SEQ, QK_COLS), blk),
                  pl.BlockSpec((QK_COLS, SEQ), lambda b: (0, b)),
                  pl.BlockSpec((SEQ, V_COLS), blk)],
        out_specs=pl.BlockSpec((SEQ, N_HEADS * V_DIM), blk),
        out_shape=jax.ShapeDtypeStruct((N_PROMPT, N_HEADS * V_DIM), BF16),
        compiler_params=_cparams(("arbitrary",), 24),
        name="attn_prompt",
    )(q, k, v)


ATT_TQ = 256


def _attn_sample(q, k, v, kc, vc):
    nq = DEC_SEQ // ATT_TQ
    q0 = N_PROMPT // ATT_TQ
    s0 = N_PROMPT // DEC_SEQ
    once = pl.Buffered(1)
    return pl.pallas_call(
        functools.partial(_attn_kernel, has_ctx=True),
        grid=(DEC_BATCH, nq),
        in_specs=[pl.BlockSpec((ATT_TQ, QK_COLS), lambda b, t: (q0 + b * nq + t, 0)),
                  pl.BlockSpec((QK_COLS, DEC_SEQ), lambda b, t: (0, s0 + b), pipeline_mode=once),
                  pl.BlockSpec((DEC_SEQ, V_COLS), lambda b, t: (s0 + b, 0), pipeline_mode=once),
                  pl.BlockSpec((1, QK_COLS, PAST_LEN), lambda b, t: (b, 0, 0), pipeline_mode=once),
                  pl.BlockSpec((1, PAST_LEN, V_COLS), lambda b, t: (b, 0, 0), pipeline_mode=once)],
        out_specs=pl.BlockSpec((ATT_TQ, N_HEADS * V_DIM), lambda b, t: (b * nq + t, 0)),
        out_shape=jax.ShapeDtypeStruct((N_SAMPLE, N_HEADS * V_DIM), BF16),
        compiler_params=_cparams(("arbitrary", "arbitrary"), 54),
        name="attn_sample",
    )(q, k, v, kc, vc)


def _topk_rows(v, k, n):
    rows = lax.broadcasted_iota(jnp.int32, v.shape, 0).astype(F32)
    sel = jnp.zeros(v.shape, F32)
    for _ in range(k):
        mx = jnp.max(v, axis=0, keepdims=True)
        first = jnp.min(jnp.where(v == mx, rows, float(n)), axis=0, keepdims=True)
        hit = rows == first
        sel = jnp.where(hit, 1.0, sel)
        v = jnp.where(hit, -jnp.inf, v)
    return sel > 0.0


def _route(logits_t, bias):
    scores = jax.nn.sigmoid(logits_t)
    sel = scores + bias
    gscore = []
    for g in range(N_GROUPS):
        blk = sel[g * GROUP_SIZE:(g + 1) * GROUP_SIZE]
        top2 = _topk_rows(blk, 2, GROUP_SIZE)
        gscore.append(jnp.sum(jnp.where(top2, blk, 0.0), axis=0, keepdims=True))
    masked = []
    for g in range(N_GROUPS):
        beaten = jnp.zeros(gscore[g].shape, F32)
        for o in range(N_GROUPS):
            if o < g:
                beaten = beaten + jnp.where(gscore[o] >= gscore[g], 1.0, 0.0)
            elif o > g:
                beaten = beaten + jnp.where(gscore[o] > gscore[g], 1.0, 0.0)
        keep = beaten < float(TOPK_GROUPS)
        masked.append(jnp.where(keep, sel[g * GROUP_SIZE:(g + 1) * GROUP_SIZE], -jnp.inf))
    chosen = _topk_rows(jnp.concatenate(masked, axis=0), TOP_K, N_EXPERTS)
    w = jnp.where(chosen, scores, 0.0)
    return w / jnp.sum(w, axis=0, keepdims=True) * ROUTED_SCALE


def _mix_kernel(ap_ref, as_ref, z_ref, zp_ref, zn_ref, gt_ref, xp_ref, xs_ref, mods_ref,
                wo32_ref, wpw32_ref, wout32_ref, cw_ref, cb_ref, lng_ref, lnb_ref, g2_ref,
                rwh_ref, rwl_ref, rb_ref,
                x1_ref, h2_ref, gates_ref, win_ref, cz_ref, wo_ref, wpw_ref, wout_ref):
    i = pl.program_id(0)

    @pl.when(i == 0)
    def _():
        wo_ref[...] = wo32_ref[...].astype(BF16)
        wpw_ref[...] = wpw32_ref[...].astype(BF16)
        wout_ref[...] = wout32_ref[...].astype(BF16)

    is_prompt = i < NP_TILES
    pos = (i - NP_TILES) % TILES_PER_DEC
    first = jnp.logical_or(is_prompt, pos == 0)
    last = jnp.logical_or(is_prompt, pos == TILES_PER_DEC - 1)

    nchunk = D_MODEL // LANE
    zprev = jnp.where(first, 0.0, zp_ref[...].astype(F32))
    znext = jnp.where(last, 0.0, zn_ref[...].astype(F32))
    zc = z_ref[...].astype(F32)
    for c in range(nchunk):
        cs = slice(c * LANE, (c + 1) * LANE)
        win_ref[c, 0:HALO, :] = zprev[:, cs]
        win_ref[c, HALO:HALO + TM, :] = zc[:, cs]
        win_ref[c, HALO + TM:, :] = znext[:, cs]

    def conv_chunk(c, carry):
        w = cw_ref[c]
        acc = jnp.zeros((TM, LANE), F32)
        for kk in range(CONV_K):
            off = HALO - CONV_PAD + kk
            acc = acc + win_ref[c, off:off + TM, :] * w[kk:kk + 1, :]
        cz_ref[c] = acc
        return carry

    lax.fori_loop(0, nchunk, conv_chunk, 0)

    m = mods_ref[0]
    rwh = rwh_ref[...]
    rows = [slice(hf * (TM // 2), (hf + 1) * (TM // 2)) for hf in range(2)]

    def conv_act(rs):
        conv = jnp.concatenate([cz_ref[c, rs, :] for c in range(nchunk)], axis=1) + cb_ref[...]
        mu = jnp.mean(conv, axis=-1, keepdims=True)
        cen = conv - mu
        var = jnp.mean(cen * cen, axis=-1, keepdims=True)
        ln = cen * lax.rsqrt(var + EPS) * lng_ref[...] + lnb_ref[...]
        return (ln * jax.nn.sigmoid(ln)).astype(BF16)

    def branches(rs, act):
        attn = jnp.where(is_prompt, ap_ref[rs, :], as_ref[rs, :])
        return (jnp.dot(attn, wo_ref[...], preferred_element_type=F32),
                jnp.dot(act, wpw_ref[...], preferred_element_type=F32))

    def merge(rs, outs):
        gt = gt_ref[rs, :].astype(F32)
        return (gt[:, :D_MODEL] * outs[0] + gt[:, D_MODEL:] * outs[1]).astype(BF16)

    def project(merged):
        return jnp.dot(merged, wout_ref[...], preferred_element_type=F32)

    def residual(rs, mix):
        x = jnp.where(is_prompt, xp_ref[rs, :], xs_ref[rs, :])
        x1 = x + m[2:3] * mix
        x1_ref[rs, :] = x1
        h2 = _rms(x1, g2_ref[...]) * (1.0 + m[4:5]) + m[3:4]
        hh = h2.astype(BF16)
        h2_ref[rs, :] = hh
        return hh, (h2 - hh.astype(F32)).astype(BF16)

    def route(rs, hs):
        hh, hl = hs
        logits_t = (lax.dot_general(rwh, hh, _NT, preferred_element_type=F32)
                    + lax.dot_general(rwh, hl, _NT, preferred_element_type=F32)
                    + lax.dot_general(rwl_ref[...], hh, _NT, preferred_element_type=F32))
        gates_ref[:, rs] = _route(logits_t, rb_ref[...])

    r0, r1 = rows
    act0 = conv_act(r0)
    out0 = branches(r0, act0)
    act1 = conv_act(r1)
    mrg0 = merge(r0, out0)
    out1 = branches(r1, act1)
    mix0 = project(mrg0)
    mrg1 = merge(r1, out1)
    hs0 = residual(r0, mix0)
    mix1 = project(mrg1)
    route(r0, hs0)
    hs1 = residual(r1, mix1)
    route(r1, hs1)


def _mix(attn_p, attn_s, z, gt, xp, xs, mods3, wo, wpw, wout, cw, cb, lng, lnb, g2, rwh, rwl, rb):
    tile = lambda i: (i, 0)
    ptile = lambda i: (jnp.minimum(i, NP_TILES - 1), 0)
    stile = lambda i: (jnp.maximum(i - NP_TILES, 0), 0)
    hpt = TM // HALO
    nhb = N_TOK // HALO
    return pl.pallas_call(
        _mix_kernel,
        grid=(N_TILES,),
        in_specs=[pl.BlockSpec((TM, D_MODEL), ptile),
                  pl.BlockSpec((TM, D_MODEL), stile),
                  pl.BlockSpec((TM, D_MODEL), tile),
                  pl.BlockSpec((HALO, D_MODEL), lambda i: (jnp.maximum(i * hpt - 1, 0), 0)),
                  pl.BlockSpec((HALO, D_MODEL), lambda i: (jnp.minimum((i + 1) * hpt, nhb - 1), 0)),
                  pl.BlockSpec((TM, 2 * D_MODEL), tile),
                  pl.BlockSpec((TM, D_MODEL), ptile),
                  pl.BlockSpec((TM, D_MODEL), stile),
                  pl.BlockSpec((1, N_MOD, D_MODEL), lambda i: (_seg_of_tile(i), 0, 0)),
                  _const_spec((D_MODEL, D_MODEL)),
                  _const_spec((D_MODEL, D_MODEL)),
                  _const_spec((D_MODEL, D_MODEL)),
                  _const_spec((D_MODEL // LANE, CONV_K, LANE)),
                  _const_spec((1, D_MODEL)),
                  _const_spec((1, D_MODEL)),
                  _const_spec((1, D_MODEL)),
                  _const_spec((1, D_MODEL)),
                  _const_spec((N_EXPERTS, D_MODEL)),
                  _const_spec((N_EXPERTS, D_MODEL)),
                  _const_spec((N_EXPERTS, 1))],
        out_specs=[pl.BlockSpec((TM, D_MODEL), tile),
                   pl.BlockSpec((TM, D_MODEL), tile),
                   pl.BlockSpec((N_EXPERTS, TM), lambda i: (0, i))],
        out_shape=[jax.ShapeDtypeStruct((N_TOK, D_MODEL), F32),
                   jax.ShapeDtypeStruct((N_TOK, D_MODEL), BF16),
                   jax.ShapeDtypeStruct((N_EXPERTS, N_TOK), F32)],
        scratch_shapes=[pltpu.VMEM((D_MODEL // LANE, TM + 2 * HALO, LANE), F32),
                        pltpu.VMEM((D_MODEL // LANE, TM, LANE), F32)]
        + [pltpu.VMEM((D_MODEL, D_MODEL), BF16)] * 3,
        compiler_params=_cparams(("arbitrary",), 52),
        name="mix",
    )(attn_p, attn_s, z, z, z, gt, xp, xs, mods3, wo, wpw, wout, cw, cb, lng, lnb, g2, rwh, rwl, rb)


CHUNK = 16
BLK_ROWS = TM * TOP_K + N_EXPERTS * CHUNK
BLK_CHUNKS = BLK_ROWS // CHUNK
SUB = 256
N_SUB = BLK_ROWS // SUB
SUB_ALWAYS = TM * TOP_K // SUB + 2
N_BLK = N_TOK // TM
TILE_CHUNKS = 32
TILE_ROWS = TILE_CHUNKS * CHUNK
MAX_CHUNKS = N_TOK * TOP_K // CHUNK + N_BLK * N_EXPERTS
MAX_TILES = MAX_CHUNKS // TILE_CHUNKS + N_EXPERTS
EXP_TILES = N_TOK // TILE_ROWS + 3
EXP_SLOTS = EXP_TILES * TILE_CHUNKS
N_BUF = 3
ZERO_CHUNK = N_BLK * BLK_CHUNKS
TRASH_CHUNK = ZERO_CHUNK + 1


def _plan_kernel(g_ref, pos_ref, cnt_ref):
    g = g_ref[...]
    picked = jnp.where(g > 0.0, 1.0, 0.0)
    r = lax.broadcasted_iota(jnp.int32, (TM, TM), 0)
    c = lax.broadcasted_iota(jnp.int32, (TM, TM), 1)
    upto = jnp.where(r <= c, 1.0, 0.0).astype(BF16)
    pos = picked * jnp.dot(picked.astype(BF16), upto, preferred_element_type=F32)
    pos_ref[0:N_EXPERTS, :] = pos.astype(BF16)
    pos_ref[N_EXPERTS:, :] = jnp.zeros((LANE - N_EXPERTS, TM), BF16)
    n = jnp.sum(picked, axis=1, keepdims=True)
    nchunk = jnp.floor((n + float(CHUNK - 1)) * (1.0 / CHUNK))
    cnt_ref[0] = jnp.broadcast_to(nchunk, (N_EXPERTS, LANE))


def _plan(gates_t):
    return pl.pallas_call(
        _plan_kernel,
        grid=(N_BLK,),
        in_specs=[pl.BlockSpec((N_EXPERTS, TM), lambda b: (0, b))],
        out_specs=[pl.BlockSpec((LANE, TM), lambda b: (0, b)),
                   pl.BlockSpec((1, N_EXPERTS, LANE), lambda b: (b, 0, 0))],
        out_shape=[jax.ShapeDtypeStruct((LANE, N_TOK), BF16),
                   jax.ShapeDtypeStruct((N_BLK, N_EXPERTS, LANE), F32)],
        compiler_params=_cparams(("arbitrary",), 16),
        name="plan",
    )(gates_t)


def _tile_tables(cnt):
    c = cnt[:, :, 0].astype(jnp.int32)
    seg_chunk = jnp.cumsum(c, axis=1) - c
    ct = c.T
    seg_q = jnp.cumsum(ct, axis=1) - ct
    tiles_e = (ct.sum(axis=1) + TILE_CHUNKS - 1) // TILE_CHUNKS
    tile_end = jnp.cumsum(tiles_e)
    tile_start = tile_end - tiles_e
    q = jnp.arange(EXP_SLOTS, dtype=jnp.int32)
    inside = jnp.logical_and(q[None, :, None] >= seg_q[:, None, :],
                             q[None, :, None] < (seg_q + ct)[:, None, :])
    base = jnp.arange(N_BLK, dtype=jnp.int32)[None, :] * BLK_CHUNKS + seg_chunk.T - seg_q
    src = q[None, :] + jnp.sum(jnp.where(inside, base[:, None, :], 0), axis=2)
    valid = jnp.any(inside, axis=2)
    buf = (tile_start[:, None] + q[None, :] // TILE_CHUNKS) % N_BUF
    trash = TRASH_CHUNK + buf * TILE_CHUNKS + q[None, :] % TILE_CHUNKS
    in_src = jnp.where(valid, src, ZERO_CHUNK).astype(jnp.int32).reshape(-1)
    out_dst = jnp.where(valid, src, trash).astype(jnp.int32).reshape(-1)
    t = jnp.arange(MAX_TILES + N_BUF, dtype=jnp.int32)
    tile_expert = jnp.minimum(jnp.sum((t[:, None] >= tile_end[None, :]).astype(jnp.int32), axis=1),
                              N_EXPERTS - 1)
    tile_start = jnp.concatenate([tile_start, tile_end[-1:]]).astype(jnp.int32)
    sub_used = (jnp.sum(c, axis=1) * CHUNK + SUB - 1) // SUB
    pad = jnp.full((N_BLK, LANE - N_EXPERTS), float(2 * BLK_ROWS), F32)
    first = jnp.concatenate([(seg_chunk * CHUNK).astype(F32), pad], axis=1)
    last = jnp.concatenate([((seg_chunk + c) * CHUNK).astype(F32), pad], axis=1)
    first_chunk = jnp.concatenate([seg_chunk.astype(F32), jnp.zeros_like(pad)], axis=1)
    seg = dict(
        first_row=jnp.broadcast_to(first[:, None, :], (N_BLK, 8, LANE)),
        last_row=jnp.broadcast_to(last[:, None, :], (N_BLK, 8, LANE)),
        first_col=jnp.broadcast_to(first[:, :, None], (N_BLK, LANE, LANE)),
        last_col=jnp.broadcast_to(last[:, :, None], (N_BLK, LANE, LANE)),
        first_chunk_col=jnp.broadcast_to(first_chunk[:, :, None], (N_BLK, LANE, LANE)).astype(BF16))
    return (tile_start, tiles_e.astype(jnp.int32), tile_expert, in_src, out_dst,
            sub_used.astype(jnp.int32), seg)


def _dispatch_kernel(sub_ref, h_ref, pos_ref, first_ref, last_ref, fchunk_ref, s_ref):
    b = pl.program_id(0)
    n_sub = jnp.where(b < N_BLK, sub_ref[jnp.minimum(b, N_BLK - 1)], 0)

    def locate(sb):
        rows = (lax.broadcasted_iota(jnp.int32, (SUB, LANE), 0) + sb * SUB).astype(F32)
        own = (jnp.where(rows >= first_ref[0, 0:1, :], 1.0, 0.0)
               - jnp.where(rows >= last_ref[0, 0:1, :], 1.0, 0.0)).astype(BF16)
        both = jnp.dot(own, jnp.concatenate([pos_ref[...], fchunk_ref[0]], axis=1),
                       preferred_element_type=F32)
        return rows, both

    def gather(sb, located):
        rows, both = located
        have = both[:, :TM]
        want = rows + 1.0 - float(CHUNK) * both[:, TM:]
        p = jnp.where(have == jnp.concatenate([want] * (TM // LANE), axis=1), 1.0, 0.0)
        s_ref[sb * SUB:(sb + 1) * SUB, :] = jnp.dot(
            p.astype(BF16), h_ref[...], preferred_element_type=F32).astype(BF16)

    def fill(sb):
        gather(sb, locate(sb))

    def clear(sb):
        s_ref[sb * SUB:(sb + 1) * SUB, :] = jnp.zeros((SUB, D_MODEL), BF16)

    @pl.when(b < N_BLK)
    def _():
        ahead = locate(0)
        for sb in range(SUB_ALWAYS):
            cur, ahead = ahead, (locate(sb + 1) if sb + 1 < SUB_ALWAYS else None)
            gather(sb, cur)

    @pl.when(b == N_BLK)
    def _():
        for sb in range(SUB_ALWAYS):
            clear(sb)

    for sb in range(SUB_ALWAYS, N_SUB):
        pl.when(sb < n_sub)(functools.partial(fill, sb))
        pl.when(sb >= n_sub)(functools.partial(clear, sb))


def _dispatch(sub_used, h2, pos, seg):
    last = N_BLK - 1
    blk3 = lambda b, *_: (jnp.minimum(b, last), 0, 0)
    grid_spec = pltpu.PrefetchScalarGridSpec(
        num_scalar_prefetch=1,
        grid=(N_BLK + 1,),
        in_specs=[pl.BlockSpec((TM, D_MODEL), lambda b, *_: (jnp.minimum(b, last), 0)),
                  pl.BlockSpec((LANE, TM), lambda b, *_: (0, jnp.minimum(b, last))),
                  pl.BlockSpec((1, 8, LANE), blk3),
                  pl.BlockSpec((1, 8, LANE), blk3),
                  pl.BlockSpec((1, LANE, LANE), blk3)],
        out_specs=pl.BlockSpec((BLK_ROWS, D_MODEL), lambda b, *_: (b, 0)))
    return pl.pallas_call(
        _dispatch_kernel,
        grid_spec=grid_spec,
        out_shape=jax.ShapeDtypeStruct(((N_BLK + 1) * BLK_ROWS, D_MODEL), BF16),
        compiler_params=_cparams(("arbitrary",), 40),
        name="dispatch",
    )(sub_used, h2, pos, seg["first_row"], seg["last_row"], seg["first_chunk_col"])


def _experts_kernel(tstart_ref, tcount_ref, texp_ref, insrc_ref, outdst_ref,
                    s_hbm, wg_ref, wu_ref, wd_ref, o_hbm,
                    xb0, xb1, xb2, ob0, ob1, ob2, wgub, wdb, insem, outsem):
    xbufs = (xb0, xb1, xb2)
    obufs = (ob0, ob1, ob2)
    e = pl.program_id(0)
    n_tiles = tstart_ref[N_EXPERTS]
    t0 = tstart_ref[e]

    def slot_base(t):
        te = texp_ref[t]
        j = jnp.minimum(t - tstart_ref[te], EXP_TILES - 1)
        return te * EXP_SLOTS + j * TILE_CHUNKS

    def in_copy(src_chunk, buf, i, sem):
        return pltpu.make_async_copy(s_hbm.at[src_chunk], xbufs[buf].at[i], sem)

    def out_copy(dst_chunk, buf, i, sem):
        return pltpu.make_async_copy(obufs[buf].at[i], o_hbm.at[dst_chunk], sem)

    def start_in(t, buf):
        base = slot_base(t)
        for i in range(TILE_CHUNKS):
            in_copy(insrc_ref[base + i], buf, i, insem.at[buf]).start()

    def start_out(t, buf, real):
        base = slot_base(jnp.maximum(t, 0))
        for i in range(TILE_CHUNKS):
            dst = jnp.where(real, outdst_ref[base + i], TRASH_CHUNK + buf * TILE_CHUNKS + i)
            out_copy(dst, buf, i, outsem.at[buf]).start()

    def wait_in(sem_idx):
        for i in range(TILE_CHUNKS):
            in_copy(0, 0, i, insem.at[sem_idx]).wait()

    def wait_out(sem_idx):
        for i in range(TILE_CHUNKS):
            out_copy(0, 0, i, outsem.at[sem_idx]).wait()

    def on_buffer(buf_dyn, fn):
        for k in range(N_BUF):
            pl.when(buf_dyn == k)(functools.partial(fn, k))

    wgub[:, :EXPERT_FF] = wg_ref[0].astype(BF16)
    wgub[:, EXPERT_FF:] = wu_ref[0].astype(BF16)
    wdb[...] = wd_ref[0].astype(BF16)

    @pl.when(e == 0)
    def _():
        for ob in obufs:
            ob[...] = jnp.zeros(ob.shape, BF16)
        start_out(0, 0, False)
        start_out(0, 1, False)
        start_in(0, 0)
        start_in(1, 1)

    def tile_body(j, carry):
        t = t0 + j

        def run(k):
            nxt = (k + 2) % N_BUF
            wait_in(k)
            wait_out(k)
            start_in(t + 2, nxt)
            start_out(t - 1, nxt, t >= 1)
            x = xbufs[k][...].reshape(TILE_ROWS, D_MODEL)
            au = jnp.dot(x, wgub[...], preferred_element_type=F32)
            a, u = au[:, :EXPERT_FF], au[:, EXPERT_FF:]
            hid = (a * jax.nn.sigmoid(a) * u).astype(BF16)
            out = jnp.dot(hid, wdb[...], preferred_element_type=F32).astype(BF16)
            obufs[k][...] = out.reshape(TILE_CHUNKS, CHUNK, D_MODEL)

        on_buffer(t % N_BUF, run)
        return carry

    lax.fori_loop(0, tcount_ref[e], tile_body, 0)

    @pl.when(e == N_EXPERTS - 1)
    def _():
        on_buffer((n_tiles + 2) % N_BUF, lambda k: start_out(n_tiles - 1, k, n_tiles >= 1))
        wait_in(n_tiles % N_BUF)
        wait_in((n_tiles + 1) % N_BUF)
        for k in range(N_BUF):
            wait_out(k)


def _experts(tile_start, tile_count, tile_expert, in_src, out_dst, sorted_rows, wg, wu, wd):
    wspec = lambda shape: pl.BlockSpec(shape, lambda e, *_: (e, 0, 0))
    grid_spec = pltpu.PrefetchScalarGridSpec(
        num_scalar_prefetch=5,
        grid=(N_EXPERTS,),
        in_specs=[pl.BlockSpec(memory_space=pl.ANY),
                  wspec((1, D_MODEL, EXPERT_FF)),
                  wspec((1, D_MODEL, EXPERT_FF)),
                  wspec((1, EXPERT_FF, D_MODEL))],
        out_specs=pl.BlockSpec(memory_space=pl.ANY),
        scratch_shapes=[pltpu.VMEM((TILE_CHUNKS, CHUNK, D_MODEL), BF16)] * (2 * N_BUF) + [
                        pltpu.VMEM((D_MODEL, 2 * EXPERT_FF), BF16),
                        pltpu.VMEM((EXPERT_FF, D_MODEL), BF16),
                        pltpu.SemaphoreType.DMA((N_BUF,)),
                        pltpu.SemaphoreType.DMA((N_BUF,))])
    return pl.pallas_call(
        _experts_kernel,
        grid_spec=grid_spec,
        out_shape=jax.ShapeDtypeStruct(sorted_rows.shape, BF16),
        input_output_aliases={5: 0},
        compiler_params=_cparams(("arbitrary",), 32),
        name="experts",
    )(tile_start, tile_count, tile_expert, in_src, out_dst, sorted_rows, wg, wu, wd)


def _combine_kernel(sub_ref, o_ref, post_ref, gate_ref, first_ref, last_ref, h_ref, x1_ref, mods_ref,
                    sg32_ref, su32_ref, sd32_ref, fg_ref, yp_ref, ys_ref, acc_ref, sg_ref, su_ref, sd_ref):
    i = pl.program_id(0)
    n_sub = sub_ref[i]

    @pl.when(i == 0)
    def _():
        sg_ref[...] = sg32_ref[...].astype(BF16)
        su_ref[...] = su32_ref[...].astype(BF16)
        sd_ref[...] = sd32_ref[...].astype(BF16)

    def locate(sb):
        rows = (lax.broadcasted_iota(jnp.int32, (LANE, SUB), 1) + sb * SUB).astype(F32)
        first = jnp.concatenate([first_ref[0]] * (SUB // LANE), axis=1)
        last = jnp.concatenate([last_ref[0]] * (SUB // LANE), axis=1)
        own = jnp.where(rows >= first, 1.0, 0.0) - jnp.where(rows >= last, 1.0, 0.0)
        start = jnp.sum(jnp.where(own > 0.0, first, 0.0), axis=0, keepdims=True)
        want = rows[0:1, :] + 1.0 - start
        ownb = own.astype(BF16)
        have = jnp.dot(post_ref[...], ownb, preferred_element_type=F32)
        own2 = jnp.concatenate([ownb[:N_EXPERTS], ownb[:N_EXPERTS]], axis=0)
        gate = jnp.dot(gate_ref[...], own2, preferred_element_type=F32)
        return have, want, gate

    def select(located):
        have, want, gate = located
        return jnp.where(have == want, gate, 0.0).astype(BF16)

    def weights_of(sb):
        return select(locate(sb))

    h = h_ref[...]
    a = jnp.dot(h, sg_ref[...], preferred_element_type=F32)
    u = jnp.dot(h, su_ref[...], preferred_element_type=F32)
    shared = jnp.dot((a * jax.nn.sigmoid(a) * u).astype(BF16), sd_ref[...],
                     preferred_element_type=F32)
    pws, ahead = [], locate(0)
    for sb in range(SUB_ALWAYS):
        cur, ahead = ahead, (locate(sb + 1) if sb + 1 < SUB_ALWAYS else None)
        pws.append(select(cur))
    pw_main = jnp.concatenate(pws, axis=1)
    acc_ref[...] = shared + jnp.dot(pw_main, o_ref[0:SUB_ALWAYS * SUB, :],
                                    preferred_element_type=F32)
    for sb in range(SUB_ALWAYS, N_SUB):
        @pl.when(sb < n_sub)
        def _():
            acc_ref[...] += jnp.dot(weights_of(sb), o_ref[sb * SUB:(sb + 1) * SUB, :],
                                    preferred_element_type=F32)

    m = mods_ref[0]
    y = _rms(x1_ref[...] + m[5:6] * acc_ref[...], fg_ref[...])

    @pl.when(i < NP_TILES)
    def _():
        yp_ref[...] = y

    @pl.when(i >= NP_TILES)
    def _():
        ys_ref[...] = y


def _combine(sub_used, expert_rows, pos_t, gate_t, seg, h2, x1, mods3, sg, su, sd, fg):
    tile = lambda i, *_: (i, 0)
    blk3 = lambda i, *_: (i, 0, 0)
    grid_spec = pltpu.PrefetchScalarGridSpec(
        num_scalar_prefetch=1,
        grid=(N_BLK,),
        in_specs=[pl.BlockSpec((BLK_ROWS, D_MODEL), tile),
                  pl.BlockSpec((TM, LANE), tile),
                  pl.BlockSpec((TM, LANE), tile),
                  pl.BlockSpec((1, LANE, LANE), blk3),
                  pl.BlockSpec((1, LANE, LANE), blk3),
                  pl.BlockSpec((TM, D_MODEL), tile),
                  pl.BlockSpec((TM, D_MODEL), tile),
                  pl.BlockSpec((1, N_MOD, D_MODEL), lambda i, *_: (_seg_of_tile(i), 0, 0)),
                  _const_spec((D_MODEL, EXPERT_FF)),
                  _const_spec((D_MODEL, EXPERT_FF)),
                  _const_spec((EXPERT_FF, D_MODEL)),
                  _const_spec((1, D_MODEL))],
        out_specs=[pl.BlockSpec((TM, D_MODEL), lambda i, *_: (jnp.minimum(i, NP_TILES - 1), 0)),
                   pl.BlockSpec((TM, D_MODEL), lambda i, *_: (jnp.maximum(i - NP_TILES, 0), 0))],
        scratch_shapes=[pltpu.VMEM((TM, D_MODEL), F32),
                        pltpu.VMEM((D_MODEL, EXPERT_FF), BF16),
                        pltpu.VMEM((D_MODEL, EXPERT_FF), BF16),
                        pltpu.VMEM((EXPERT_FF, D_MODEL), BF16)])
    return pl.pallas_call(
        _combine_kernel,
        grid_spec=grid_spec,
        out_shape=[jax.ShapeDtypeStruct((N_PROMPT, D_MODEL), F32),
                   jax.ShapeDtypeStruct((N_SAMPLE, D_MODEL), F32)],
        compiler_params=_cparams(("arbitrary",), 40),
        name="combine",
    )(sub_used, expert_rows, pos_t, gate_t, seg["first_col"], seg["last_col"], h2, x1, mods3,
      sg, su, sd, fg)


def _rope_tables():
    pos = np.arange(DEC_SEQ)
    row = (pos // GRID_W).astype(np.float64)
    col = (pos % GRID_W).astype(np.float64)
    n_freq = QK_ROPE_DIM // 4
    inv_freq = ROPE_BASE ** (-np.arange(n_freq, dtype=np.float64) / n_freq)
    ar, ac = row[:, None] * inv_freq, col[:, None] * inv_freq
    cos32 = np.concatenate([np.cos(ar), np.cos(ar), np.cos(ac), np.cos(ac)], axis=1)
    sin32 = np.concatenate([-np.sin(ar), np.sin(ar), -np.sin(ac), np.sin(ac)], axis=1)
    cos32 = np.concatenate([np.ones((TM, QK_ROPE_DIM)), cos32], axis=0)
    sin32 = np.concatenate([np.zeros((TM, QK_ROPE_DIM)), sin32], axis=0)
    n = cos32.shape[0]
    ones = np.ones((n, QK_NOPE_DIM))
    zeros = np.zeros((n, QK_NOPE_DIM))
    tail = np.zeros((n, HEAD_PAD - QK_NOPE_DIM - QK_ROPE_DIM))
    q_scale = ATTN_SCALE * math.log2(math.e)
    rest = np.zeros((n, HEAD_PAD - QK_ROPE_DIM))
    qc = np.concatenate([ones, cos32, tail], axis=1) * q_scale
    qs = np.concatenate([zeros, sin32, tail], axis=1) * q_scale
    kc = np.concatenate([cos32, rest], axis=1)
    ks = np.concatenate([sin32, rest], axis=1)
    return jnp.asarray(np.stack([qc, qs, kc, ks], axis=0).astype(np.float32))


def _pack_weights(w_in, w_uq, w_ukv):
    win = w_in.astype(BF16)
    gate_tail = w_in[:, S_GATE - PHASE + 2 * D_MODEL:]
    wgt = jnp.pad(gate_tail, ((0, 0), (0, LANE - PHASE))).astype(BF16)
    uq = w_uq.reshape(Q_LORA, N_HEADS, QK_NOPE_DIM + QK_ROPE_DIM)
    pad_q = HEAD_PAD - QK_NOPE_DIM - QK_ROPE_DIM
    wuqp = jnp.pad(uq, ((0, 0), (0, 0), (0, pad_q))).reshape(Q_LORA, QK_COLS).astype(BF16)
    ukv = w_ukv.reshape(KV_LORA, N_HEADS, QK_NOPE_DIM + V_DIM)
    wukp = jnp.pad(ukv[:, :, :QK_NOPE_DIM], ((0, 0), (0, 0), (0, HEAD_PAD - QK_NOPE_DIM)))
    wukp_t = wukp.reshape(KV_LORA, QK_COLS).T.astype(BF16)
    wuv = ukv[:, :, QK_NOPE_DIM:].reshape(KV_LORA, V_COLS).astype(BF16)
    return win, wgt, wuqp, wukp_t, wuv


def kernel(x_prompt, x_sample, cache_ckv, cache_krope, c, c_ctx, mod_w, mod_b, norm1_g, w_in, q_norm_g, w_uq, kv_norm_g, w_ukv, w_o_attn, conv_w, conv_b, conv_ln_g, conv_ln_b, w_pw2, w_out, norm2_g, router_w, router_bias, exp_w_gate, exp_w_up, exp_w_down, sh_w_gate, sh_w_up, sh_w_down, final_g):
    xp = x_prompt.reshape(N_PROMPT, D_MODEL)
    xs = x_sample.reshape(N_SAMPLE, D_MODEL)
    cond8 = jnp.concatenate([c_ctx[None, :], c, jnp.zeros((8 - 1 - DEC_BATCH, D_MODEL), F32)], axis=0)
    mods3 = _mods(cond8, mod_w[0], mod_b).reshape(8, N_MOD, D_MODEL)

    win, wgt, wuqp, wukp, wuv = _pack_weights(w_in[0], w_uq[0], w_ukv[0])
    q, k, v, z, gt, ckv_new, kr_new = _inproj(
        xp, xs, mods3, norm1_g, q_norm_g, kv_norm_g, win, wgt, wuqp, wukp, wuv, _rope_tables())

    krope_pad = jnp.pad(cache_krope[:, 0], ((0, 0), (0, 0),
                                            (QK_NOPE_DIM, HEAD_PAD - QK_NOPE_DIM - QK_ROPE_DIM)))
    kc, vc = _ctxkv(cache_ckv[:, 0], krope_pad, wukp, wuv)
    attn_p = _attn_prompt(q, k, v)
    attn_s = _attn_sample(q, k, v, kc, vc)

    rwt = router_w[0].T
    rwh = rwt.astype(BF16)
    rwl = (rwt - rwh.astype(F32)).astype(BF16)
    cw = conv_w[0].reshape(CONV_K, D_MODEL // LANE, LANE).transpose(1, 0, 2)
    x1, h2, gates_t = _mix(
        attn_p, attn_s, z, gt, xp, xs, mods3,
        w_o_attn[0], w_pw2[0], w_out[0],
        cw, conv_b, conv_ln_g, conv_ln_b, norm2_g, rwh, rwl, router_bias.reshape(N_EXPERTS, 1))

    pos, cnt = _plan(gates_t)
    tile_start, tile_count, tile_expert, in_src, out_dst, sub_used, seg = _tile_tables(cnt)
    expert_rows = _experts(tile_start, tile_count, tile_expert, in_src, out_dst,
                           _dispatch(sub_used, h2, pos, seg).reshape(-1, CHUNK, D_MODEL),
                           exp_w_gate[0], exp_w_up[0], exp_w_down[0])
    gates = gates_t.T
    gates_hi = gates.astype(BF16)
    gates_lo = (gates - gates_hi.astype(F32)).astype(BF16)
    y_p, y_s = _combine(sub_used, expert_rows.reshape(-1, D_MODEL), pos.T,
                        jnp.concatenate([gates_hi, gates_lo], axis=1), seg, h2, x1, mods3,
                        sh_w_gate[0], sh_w_up[0], sh_w_down[0], final_g[None, :])
    return (y_p.reshape(BATCH, SEQ, D_MODEL),
            y_s.reshape(DEC_BATCH, DEC_SEQ, D_MODEL),
            ckv_new.reshape(BATCH, 1, SEQ, KV_LORA),
            kr_new.reshape(BATCH, 1, SEQ, QK_ROPE_DIM))
```

```python
import functools
import math

import jax
import jax.numpy as jnp
import numpy as np
from jax import lax
from jax.experimental import pallas as pl
from jax.experimental.pallas import tpu as pltpu

F32 = jnp.float32
BF16 = jnp.bfloat16

D_MODEL = 1024
BATCH = 16
SEQ = 256
DEC_BATCH = 2
DEC_SEQ = 2048
PAST_LEN = 256
GRID_W = 64
N_HEADS = 16
QK_NOPE_DIM = 64
QK_ROPE_DIM = 32
V_DIM = 64
Q_LORA = 384
KV_LORA = 256
ROPE_BASE = 10000.0
ATTN_SCALE = 1.0 / math.sqrt(QK_NOPE_DIM + QK_ROPE_DIM)
CONV_K = 31
CONV_PAD = CONV_K // 2
N_EXPERTS = 64
TOP_K = 8
N_GROUPS = 8
GROUP_SIZE = N_EXPERTS // N_GROUPS
TOPK_GROUPS = 4
EXPERT_FF = 256
ROUTED_SCALE = 2.5
EPS = 1e-6
N_MOD = 6

LANE = 128
HEAD_PAD = LANE
QK_COLS = N_HEADS * HEAD_PAD
N_PROMPT = BATCH * SEQ
N_SAMPLE = DEC_BATCH * DEC_SEQ
N_TOK = N_PROMPT + N_SAMPLE

TM = 256
NP_TILES = N_PROMPT // TM
NS_TILES = N_SAMPLE // TM
N_TILES = NP_TILES + NS_TILES
TILES_PER_DEC = DEC_SEQ // TM
HALO = 16
S_Q = 0
S_KV = S_Q + Q_LORA
S_KR = S_KV + KV_LORA
S_CONV = S_KR + QK_ROPE_DIM
S_GATE = S_CONV + 2 * D_MODEL
PHASE = S_CONV % LANE
CONV_WIN = 2 * D_MODEL + LANE
GATE_WIN = 2 * D_MODEL + LANE
IN_COLS = S_GATE + 2 * D_MODEL
V_COLS = N_HEADS * V_DIM

VMEM_CAP = 56 * 1024 * 1024


def _cparams(sem):
    return pltpu.CompilerParams(dimension_semantics=sem, vmem_limit_bytes=VMEM_CAP)


def _const_spec(shape):
    n = len(shape)
    return pl.BlockSpec(shape, lambda *_: (0,) * n, pipeline_mode=pl.Buffered(1))


def _seg_of_tile(i):
    return jnp.where(i < NP_TILES, 0, 1 + (i - NP_TILES) // TILES_PER_DEC)


def _rms(x, g):
    return x * lax.rsqrt(jnp.mean(x * x, axis=-1, keepdims=True) + EPS) * g


_NT = (((1,), (1,)), ((), ()))


def _mods_kernel(cond_ref, w_ref, b_ref, o_ref):
    c = cond_ref[...]
    s = c * jax.nn.sigmoid(c)
    o_ref[...] = jnp.dot(s.astype(BF16), w_ref[...].astype(BF16),
                         preferred_element_type=F32) + b_ref[...]


def _mods(cond8, mod_w, mod_b):
    n = N_MOD * D_MODEL
    bn = 512
    return pl.pallas_call(
        _mods_kernel,
        grid=(n // bn,),
        in_specs=[pl.BlockSpec((8, D_MODEL), lambda j: (0, 0)),
                  pl.BlockSpec((D_MODEL, bn), lambda j: (0, j)),
                  pl.BlockSpec((1, bn), lambda j: (0, j))],
        out_specs=pl.BlockSpec((8, bn), lambda j: (0, j)),
        out_shape=jax.ShapeDtypeStruct((8, n), F32),
        compiler_params=_cparams(("arbitrary",)),
        name="mods",
    )(cond8, mod_w, mod_b)


def _keys_t(wukp_t, ckv, krope):
    eye = (lax.broadcasted_iota(jnp.int32, (HEAD_PAD, HEAD_PAD), 0)
           == lax.broadcasted_iota(jnp.int32, (HEAD_PAD, HEAD_PAD), 1))
    rope_t = lax.dot_general(jnp.where(eye, 1.0, 0.0).astype(BF16), krope, _NT,
                             preferred_element_type=F32)
    nope_t = lax.dot_general(wukp_t, ckv, _NT, preferred_element_type=F32)
    return (nope_t + jnp.concatenate([rope_t] * N_HEADS, axis=0)).astype(BF16)


def _rope_swap(x):
    n = x.shape[1]
    half = QK_ROPE_DIM // 4
    lane = lax.broadcasted_iota(jnp.int32, x.shape, 1)
    return jnp.where(lane % (2 * half) < half,
                     pltpu.roll(x, n - half, axis=1), pltpu.roll(x, half, axis=1))


def _inproj_kernel(xp_ref, xs_ref, mods_ref, g1_ref, qg_ref, kvg_ref, win_ref, wgt_ref, wuqp_ref,
                   wukp_ref, wuv_ref, tab_ref,
                   q_ref, k_ref, v_ref, z_ref, gt_ref, ckv_ref, kr_ref):
    i = pl.program_id(0)
    x = jnp.where(i < NP_TILES, xp_ref[...], xs_ref[...])
    m = mods_ref[0]
    h = _rms(x, g1_ref[...]) * (1.0 + m[1:2]) + m[0:1]
    hb = h.astype(BF16)

    def proj(c0, c1):
        return jnp.dot(hb, win_ref[:, c0:c1], preferred_element_type=F32)

    def lanes16(t):
        return jnp.concatenate([t] * N_HEADS, axis=1)

    qn = _rms(proj(S_Q, S_KV), qg_ref[...]).astype(BF16)
    q = jnp.dot(qn, wuqp_ref[...], preferred_element_type=F32)
    q_ref[...] = (q * lanes16(tab_ref[0]) + _rope_swap(q) * lanes16(tab_ref[1])).astype(BF16)

    ckv = _rms(proj(S_KV, S_KR), kvg_ref[...])
    ckvb = ckv.astype(BF16)
    u = proj(S_KR, S_KR + CONV_WIN)
    ua = u[:, :LANE]
    krot = pltpu.roll(ua * tab_ref[2] + _rope_swap(ua) * tab_ref[3], QK_NOPE_DIM, axis=1)
    k_ref[...] = _keys_t(wukp_ref[...], ckvb, krot.astype(BF16))
    v_ref[...] = jnp.dot(ckvb, wuv_ref[...], preferred_element_type=F32).astype(BF16)

    @pl.when(i < NP_TILES)
    def _():
        ckv_ref[...] = ckv
        kr_ref[...] = ua[:, :QK_ROPE_DIM]

    glu = u[:, :D_MODEL + LANE] * jax.nn.sigmoid(u[:, D_MODEL:])
    z_ref[...] = pltpu.roll(glu, D_MODEL + LANE - PHASE, axis=1)[:, :D_MODEL].astype(BF16)
    gates = jax.nn.sigmoid(jnp.concatenate(
        [proj(S_GATE - PHASE, S_GATE - PHASE + 2 * D_MODEL),
         jnp.dot(hb, wgt_ref[...], preferred_element_type=F32)], axis=1))
    gt_ref[...] = pltpu.roll(gates, GATE_WIN - PHASE, axis=1)[:, :2 * D_MODEL].astype(BF16)


def _inproj(xp, xs, mods3, g1, qg, kvg, win, wgt, wuqp, wukp, wuv, tab):
    tile = lambda i: (i, 0)
    ptile = lambda i: (jnp.minimum(i, NP_TILES - 1), 0)
    stile = lambda i: (jnp.maximum(i - NP_TILES, 0), 0)
    tab_blk = lambda i: (0, jnp.where(i < NP_TILES, 0, 1 + (i - NP_TILES) % TILES_PER_DEC), 0)
    return pl.pallas_call(
        _inproj_kernel,
        grid=(N_TILES,),
        in_specs=[pl.BlockSpec((TM, D_MODEL), ptile),
                  pl.BlockSpec((TM, D_MODEL), stile),
                  pl.BlockSpec((1, N_MOD, D_MODEL), lambda i: (_seg_of_tile(i), 0, 0)),
                  _const_spec((1, D_MODEL)),
                  _const_spec((1, Q_LORA)),
                  _const_spec((1, KV_LORA)),
                  _const_spec((D_MODEL, IN_COLS)),
                  _const_spec((D_MODEL, LANE)),
                  _const_spec((Q_LORA, QK_COLS)),
                  _const_spec((QK_COLS, KV_LORA)),
                  _const_spec((KV_LORA, V_COLS)),
                  pl.BlockSpec((4, TM, LANE), tab_blk)],
        out_specs=[pl.BlockSpec((TM, QK_COLS), tile),
                   pl.BlockSpec((QK_COLS, TM), lambda i: (0, i)),
                   pl.BlockSpec((TM, V_COLS), tile),
                   pl.BlockSpec((TM, D_MODEL), tile),
                   pl.BlockSpec((TM, 2 * D_MODEL), tile),
                   pl.BlockSpec((TM, KV_LORA), ptile),
                   pl.BlockSpec((TM, QK_ROPE_DIM), ptile)],
        out_shape=[jax.ShapeDtypeStruct((N_TOK, QK_COLS), BF16),
                   jax.ShapeDtypeStruct((QK_COLS, N_TOK), BF16),
                   jax.ShapeDtypeStruct((N_TOK, V_COLS), BF16),
                   jax.ShapeDtypeStruct((N_TOK, D_MODEL), BF16),
                   jax.ShapeDtypeStruct((N_TOK, 2 * D_MODEL), BF16),
                   jax.ShapeDtypeStruct((N_PROMPT, KV_LORA), F32),
                   jax.ShapeDtypeStruct((N_PROMPT, QK_ROPE_DIM), F32)],
        compiler_params=_cparams(("arbitrary",)),
        name="inproj",
    )(xp, xs, mods3, g1, qg, kvg, win, wgt, wuqp, wukp, wuv, tab)


def _ctxkv_kernel(ckv_ref, kr_ref, wukp_ref, wuv_ref, k_ref, v_ref):
    c = ckv_ref[0].astype(BF16)
    k_ref[0] = _keys_t(wukp_ref[...], c, kr_ref[0].astype(BF16))
    v_ref[0] = jnp.dot(c, wuv_ref[...], preferred_element_type=F32).astype(BF16)


def _ctxkv(cache_ckv, krope_pad, wukp, wuv):
    return pl.pallas_call(
        _ctxkv_kernel,
        grid=(DEC_BATCH,),
        in_specs=[pl.BlockSpec((1, PAST_LEN, KV_LORA), lambda b: (b, 0, 0)),
                  pl.BlockSpec((1, PAST_LEN, LANE), lambda b: (b, 0, 0)),
                  _const_spec((QK_COLS, KV_LORA)),
                  _const_spec((KV_LORA, V_COLS))],
        out_specs=[pl.BlockSpec((1, QK_COLS, PAST_LEN), lambda b: (b, 0, 0)),
                   pl.BlockSpec((1, PAST_LEN, V_COLS), lambda b: (b, 0, 0))],
        out_shape=[jax.ShapeDtypeStruct((DEC_BATCH, QK_COLS, PAST_LEN), BF16),
                   jax.ShapeDtypeStruct((DEC_BATCH, PAST_LEN, V_COLS), BF16)],
        compiler_params=_cparams(("arbitrary",)),
        name="ctxkv",
    )(cache_ckv, krope_pad, wukp, wuv)


def _attn_kernel(*refs, has_ctx):
    if has_ctx:
        q_ref, k_ref, v_ref, kc_ref, vc_ref, o_ref = refs
    else:
        q_ref, k_ref, v_ref, o_ref = refs
    tq = q_ref.shape[0]
    lane = lax.broadcasted_iota(jnp.int32, (tq, LANE), 1)
    for j in range(N_HEADS // 2):
        pair = slice(j * LANE, (j + 1) * LANE)
        ps, pcs, dens = [], [], []
        for h in (2 * j, 2 * j + 1):
            hs = slice(h * HEAD_PAD, (h + 1) * HEAD_PAD)
            qh = q_ref[:, hs]
            s = jnp.dot(qh, k_ref[hs, :], preferred_element_type=F32)
            mx = jnp.max(s, axis=-1, keepdims=True)
            if has_ctx:
                sc = jnp.dot(qh, kc_ref[0, hs, :], preferred_element_type=F32)
                mx = jnp.maximum(mx, jnp.max(sc, axis=-1, keepdims=True))
            p = jnp.exp2(s - mx)
            den = jnp.sum(p, axis=-1, keepdims=True)
            ps.append(p.astype(BF16))
            if has_ctx:
                pc = jnp.exp2(sc - mx)
                den = den + jnp.sum(pc, axis=-1, keepdims=True)
                pcs.append(pc.astype(BF16))
            dens.append(den)
        o = jnp.dot(jnp.concatenate(ps, axis=0), v_ref[:, pair], preferred_element_type=F32)
        if has_ctx:
            o = o + jnp.dot(jnp.concatenate(pcs, axis=0), vc_ref[0, :, pair],
                            preferred_element_type=F32)
        o_ref[:, pair] = jnp.where(lane < V_DIM, o[:tq] / dens[0], o[tq:] / dens[1]).astype(BF16)


def _attn_prompt(q, k, v):
    blk = lambda b: (b, 0)
    return pl.pallas_call(
        functools.partial(_attn_kernel, has_ctx=False),
        grid=(BATCH,),
        in_specs=[pl.BlockSpec((SEQ, QK_COLS), blk),
                  pl.BlockSpec((QK_COLS, SEQ), lambda b: (0, b)),
                  pl.BlockSpec((SEQ, V_COLS), blk)],
        out_specs=pl.BlockSpec((SEQ, N_HEADS * V_DIM), blk),
        out_shape=jax.ShapeDtypeStruct((N_PROMPT, N_HEADS * V_DIM), BF16),
        compiler_params=_cparams(("arbitrary",)),
        name="attn_prompt",
    )(q, k, v)


ATT_TQ = 256


def _attn_sample(q, k, v, kc, vc):
    nq = DEC_SEQ // ATT_TQ
    q0 = N_PROMPT // ATT_TQ
    s0 = N_PROMPT // DEC_SEQ
    once = pl.Buffered(1)
    return pl.pallas_call(
        functools.partial(_attn_kernel, has_ctx=True),
        grid=(DEC_BATCH, nq),
        in_specs=[pl.BlockSpec((ATT_TQ, QK_COLS), lambda b, t: (q0 + b * nq + t, 0)),
                  pl.BlockSpec((QK_COLS, DEC_SEQ), lambda b, t: (0, s0 + b), pipeline_mode=once),
                  pl.BlockSpec((DEC_SEQ, V_COLS), lambda b, t: (s0 + b, 0), pipeline_mode=once),
                  pl.BlockSpec((1, QK_COLS, PAST_LEN), lambda b, t: (b, 0, 0), pipeline_mode=once),
                  pl.BlockSpec((1, PAST_LEN, V_COLS), lambda b, t: (b, 0, 0), pipeline_mode=once)],
        out_specs=pl.BlockSpec((ATT_TQ, N_HEADS * V_DIM), lambda b, t: (b * nq + t, 0)),
        out_shape=jax.ShapeDtypeStruct((N_SAMPLE, N_HEADS * V_DIM), BF16),
        compiler_params=_cparams(("arbitrary", "arbitrary")),
        name="attn_sample",
    )(q, k, v, kc, vc)


def _topk_rows(v, k, n):
    rows = lax.broadcasted_iota(jnp.int32, v.shape, 0).astype(F32)
    sel = jnp.zeros(v.shape, F32)
    for _ in range(k):
        mx = jnp.max(v, axis=0, keepdims=True)
        first = jnp.min(jnp.where(v == mx, rows, float(n)), axis=0, keepdims=True)
        hit = rows == first
        sel = jnp.where(hit, 1.0, sel)
        v = jnp.where(hit, -jnp.inf, v)
    return sel > 0.0


def _route(logits_t, bias):
    scores = jax.nn.sigmoid(logits_t)
    sel = scores + bias
    gscore = []
    for g in range(N_GROUPS):
        blk = sel[g * GROUP_SIZE:(g + 1) * GROUP_SIZE]
        top2 = _topk_rows(blk, 2, GROUP_SIZE)
        gscore.append(jnp.sum(jnp.where(top2, blk, 0.0), axis=0, keepdims=True))
    masked = []
    for g in range(N_GROUPS):
        beaten = jnp.zeros(gscore[g].shape, F32)
        for o in range(N_GROUPS):
            if o < g:
                beaten = beaten + jnp.where(gscore[o] >= gscore[g], 1.0, 0.0)
            elif o > g:
                beaten = beaten + jnp.where(gscore[o] > gscore[g], 1.0, 0.0)
        keep = beaten < float(TOPK_GROUPS)
        masked.append(jnp.where(keep, sel[g * GROUP_SIZE:(g + 1) * GROUP_SIZE], -jnp.inf))
    chosen = _topk_rows(jnp.concatenate(masked, axis=0), TOP_K, N_EXPERTS)
    w = jnp.where(chosen, scores, 0.0)
    return w / jnp.sum(w, axis=0, keepdims=True) * ROUTED_SCALE


def _mix_kernel(ap_ref, as_ref, z_ref, zp_ref, zn_ref, gt_ref, xp_ref, xs_ref, mods_ref,
                wo32_ref, wpw32_ref, wout32_ref, cw_ref, cb_ref, lng_ref, lnb_ref, g2_ref,
                rwh_ref, rwl_ref, rb_ref,
                x1_ref, h2_ref, gates_ref, win_ref, cz_ref, wo_ref, wpw_ref, wout_ref):
    i = pl.program_id(0)

    @pl.when(i == 0)
    def _():
        wo_ref[...] = wo32_ref[...].astype(BF16)
        wpw_ref[...] = wpw32_ref[...].astype(BF16)
        wout_ref[...] = wout32_ref[...].astype(BF16)

    is_prompt = i < NP_TILES
    pos = (i - NP_TILES) % TILES_PER_DEC
    first = jnp.logical_or(is_prompt, pos == 0)
    last = jnp.logical_or(is_prompt, pos == TILES_PER_DEC - 1)

    nchunk = D_MODEL // LANE
    zprev = jnp.where(first, 0.0, zp_ref[...].astype(F32))
    znext = jnp.where(last, 0.0, zn_ref[...].astype(F32))
    zc = z_ref[...].astype(F32)
    for c in range(nchunk):
        cs = slice(c * LANE, (c + 1) * LANE)
        win_ref[c, 0:HALO, :] = zprev[:, cs]
        win_ref[c, HALO:HALO + TM, :] = zc[:, cs]
        win_ref[c, HALO + TM:, :] = znext[:, cs]

    def conv_chunk(c, carry):
        w = cw_ref[c]
        acc = jnp.zeros((TM, LANE), F32)
        for kk in range(CONV_K):
            off = HALO - CONV_PAD + kk
            acc = acc + win_ref[c, off:off + TM, :] * w[kk:kk + 1, :]
        cz_ref[c] = acc
        return carry

    lax.fori_loop(0, nchunk, conv_chunk, 0)

    m = mods_ref[0]
    rwh = rwh_ref[...]
    rows = [slice(hf * (TM // 2), (hf + 1) * (TM // 2)) for hf in range(2)]

    def conv_act(rs):
        conv = jnp.concatenate([cz_ref[c, rs, :] for c in range(nchunk)], axis=1) + cb_ref[...]
        mu = jnp.mean(conv, axis=-1, keepdims=True)
        cen = conv - mu
        var = jnp.mean(cen * cen, axis=-1, keepdims=True)
        ln = cen * lax.rsqrt(var + EPS) * lng_ref[...] + lnb_ref[...]
        return (ln * jax.nn.sigmoid(ln)).astype(BF16)

    def branches(rs, act):
        attn = jnp.where(is_prompt, ap_ref[rs, :], as_ref[rs, :])
        return (jnp.dot(attn, wo_ref[...], preferred_element_type=F32),
                jnp.dot(act, wpw_ref[...], preferred_element_type=F32))

    def merge(rs, outs):
        gt = gt_ref[rs, :].astype(F32)
        return (gt[:, :D_MODEL] * outs[0] + gt[:, D_MODEL:] * outs[1]).astype(BF16)

    def project(merged):
        return jnp.dot(merged, wout_ref[...], preferred_element_type=F32)

    def residual(rs, mix):
        x = jnp.where(is_prompt, xp_ref[rs, :], xs_ref[rs, :])
        x1 = x + m[2:3] * mix
        x1_ref[rs, :] = x1
        h2 = _rms(x1, g2_ref[...]) * (1.0 + m[4:5]) + m[3:4]
        hh = h2.astype(BF16)
        h2_ref[rs, :] = hh
        return hh, (h2 - hh.astype(F32)).astype(BF16)

    def route(rs, hs):
        hh, hl = hs
        logits_t = (lax.dot_general(rwh, hh, _NT, preferred_element_type=F32)
                    + lax.dot_general(rwh, hl, _NT, preferred_element_type=F32)
                    + lax.dot_general(rwl_ref[...], hh, _NT, preferred_element_type=F32))
        gates_ref[:, rs] = _route(logits_t, rb_ref[...])

    r0, r1 = rows
    act0 = conv_act(r0)
    out0 = branches(r0, act0)
    act1 = conv_act(r1)
    mrg0 = merge(r0, out0)
    out1 = branches(r1, act1)
    mix0 = project(mrg0)
    mrg1 = merge(r1, out1)
    hs0 = residual(r0, mix0)
    mix1 = project(mrg1)
    route(r0, hs0)
    hs1 = residual(r1, mix1)
    route(r1, hs1)


def _mix(attn_p, attn_s, z, gt, xp, xs, mods3, wo, wpw, wout, cw, cb, lng, lnb, g2, rwh, rwl, rb):
    tile = lambda i: (i, 0)
    ptile = lambda i: (jnp.minimum(i, NP_TILES - 1), 0)
    stile = lambda i: (jnp.maximum(i - NP_TILES, 0), 0)
    hpt = TM // HALO
    nhb = N_TOK // HALO
    return pl.pallas_call(
        _mix_kernel,
        grid=(N_TILES,),
        in_specs=[pl.BlockSpec((TM, D_MODEL), ptile),
                  pl.BlockSpec((TM, D_MODEL), stile),
                  pl.BlockSpec((TM, D_MODEL), tile),
                  pl.BlockSpec((HALO, D_MODEL), lambda i: (jnp.maximum(i * hpt - 1, 0), 0)),
                  pl.BlockSpec((HALO, D_MODEL), lambda i: (jnp.minimum((i + 1) * hpt, nhb - 1), 0)),
                  pl.BlockSpec((TM, 2 * D_MODEL), tile),
                  pl.BlockSpec((TM, D_MODEL), ptile),
                  pl.BlockSpec((TM, D_MODEL), stile),
                  pl.BlockSpec((1, N_MOD, D_MODEL), lambda i: (_seg_of_tile(i), 0, 0)),
                  _const_spec((D_MODEL, D_MODEL)),
                  _const_spec((D_MODEL, D_MODEL)),
                  _const_spec((D_MODEL, D_MODEL)),
                  _const_spec((D_MODEL // LANE, CONV_K, LANE)),
                  _const_spec((1, D_MODEL)),
                  _const_spec((1, D_MODEL)),
                  _const_spec((1, D_MODEL)),
                  _const_spec((1, D_MODEL)),
                  _const_spec((N_EXPERTS, D_MODEL)),
                  _const_spec((N_EXPERTS, D_MODEL)),
                  _const_spec((N_EXPERTS, 1))],
        out_specs=[pl.BlockSpec((TM, D_MODEL), tile),
                   pl.BlockSpec((TM, D_MODEL), tile),
                   pl.BlockSpec((N_EXPERTS, TM), lambda i: (0, i))],
        out_shape=[jax.ShapeDtypeStruct((N_TOK, D_MODEL), F32),
                   jax.ShapeDtypeStruct((N_TOK, D_MODEL), BF16),
                   jax.ShapeDtypeStruct((N_EXPERTS, N_TOK), F32)],
        scratch_shapes=[pltpu.VMEM((D_MODEL // LANE, TM + 2 * HALO, LANE), F32),
                        pltpu.VMEM((D_MODEL // LANE, TM, LANE), F32)]
        + [pltpu.VMEM((D_MODEL, D_MODEL), BF16)] * 3,
        compiler_params=_cparams(("arbitrary",)),
        name="mix",
    )(attn_p, attn_s, z, z, z, gt, xp, xs, mods3, wo, wpw, wout, cw, cb, lng, lnb, g2, rwh, rwl, rb)


CHUNK = 16
BLK_ROWS = TM * TOP_K + N_EXPERTS * CHUNK
BLK_CHUNKS = BLK_ROWS // CHUNK
SUB = 256
N_SUB = BLK_ROWS // SUB
SUB_ALWAYS = TM * TOP_K // SUB + 2
N_BLK = N_TOK // TM
TILE_CHUNKS = 32
TILE_ROWS = TILE_CHUNKS * CHUNK
MAX_CHUNKS = N_TOK * TOP_K // CHUNK + N_BLK * N_EXPERTS
MAX_TILES = MAX_CHUNKS // TILE_CHUNKS + N_EXPERTS
EXP_TILES = N_TOK // TILE_ROWS + 3
EXP_SLOTS = EXP_TILES * TILE_CHUNKS
N_BUF = 3
ZERO_CHUNK = N_BLK * BLK_CHUNKS
TRASH_CHUNK = ZERO_CHUNK + 1


def _plan_kernel(g_ref, pos_ref, cnt_ref):
    g = g_ref[...]
    picked = jnp.where(g > 0.0, 1.0, 0.0)
    r = lax.broadcasted_iota(jnp.int32, (TM, TM), 0)
    c = lax.broadcasted_iota(jnp.int32, (TM, TM), 1)
    upto = jnp.where(r <= c, 1.0, 0.0).astype(BF16)
    pos = picked * jnp.dot(picked.astype(BF16), upto, preferred_element_type=F32)
    pos_ref[0:N_EXPERTS, :] = pos.astype(BF16)
    pos_ref[N_EXPERTS:, :] = jnp.zeros((LANE - N_EXPERTS, TM), BF16)
    n = jnp.sum(picked, axis=1, keepdims=True)
    nchunk = jnp.floor((n + float(CHUNK - 1)) * (1.0 / CHUNK))
    cnt_ref[0] = jnp.broadcast_to(nchunk, (N_EXPERTS, LANE))


def _plan(gates_t):
    return pl.pallas_call(
        _plan_kernel,
        grid=(N_BLK,),
        in_specs=[pl.BlockSpec((N_EXPERTS, TM), lambda b: (0, b))],
        out_specs=[pl.BlockSpec((LANE, TM), lambda b: (0, b)),
                   pl.BlockSpec((1, N_EXPERTS, LANE), lambda b: (b, 0, 0))],
        out_shape=[jax.ShapeDtypeStruct((LANE, N_TOK), BF16),
                   jax.ShapeDtypeStruct((N_BLK, N_EXPERTS, LANE), F32)],
        compiler_params=_cparams(("arbitrary",)),
        name="plan",
    )(gates_t)


def _tile_tables(cnt):
    c = cnt[:, :, 0].astype(jnp.int32)
    seg_chunk = jnp.cumsum(c, axis=1) - c
    ct = c.T
    seg_q = jnp.cumsum(ct, axis=1) - ct
    tiles_e = (ct.sum(axis=1) + TILE_CHUNKS - 1) // TILE_CHUNKS
    tile_end = jnp.cumsum(tiles_e)
    tile_start = tile_end - tiles_e
    q = jnp.arange(EXP_SLOTS, dtype=jnp.int32)
    inside = jnp.logical_and(q[None, :, None] >= seg_q[:, None, :],
                             q[None, :, None] < (seg_q + ct)[:, None, :])
    base = jnp.arange(N_BLK, dtype=jnp.int32)[None, :] * BLK_CHUNKS + seg_chunk.T - seg_q
    src = q[None, :] + jnp.sum(jnp.where(inside, base[:, None, :], 0), axis=2)
    valid = jnp.any(inside, axis=2)
    buf = (tile_start[:, None] + q[None, :] // TILE_CHUNKS) % N_BUF
    trash = TRASH_CHUNK + buf * TILE_CHUNKS + q[None, :] % TILE_CHUNKS
    in_src = jnp.where(valid, src, ZERO_CHUNK).astype(jnp.int32).reshape(-1)
    out_dst = jnp.where(valid, src, trash).astype(jnp.int32).reshape(-1)
    t = jnp.arange(MAX_TILES + N_BUF, dtype=jnp.int32)
    tile_expert = jnp.minimum(jnp.sum((t[:, None] >= tile_end[None, :]).astype(jnp.int32), axis=1),
                              N_EXPERTS - 1)
    tile_start = jnp.concatenate([tile_start, tile_end[-1:]]).astype(jnp.int32)
    sub_used = (jnp.sum(c, axis=1) * CHUNK + SUB - 1) // SUB
    pad = jnp.full((N_BLK, LANE - N_EXPERTS), float(2 * BLK_ROWS), F32)
    first = jnp.concatenate([(seg_chunk * CHUNK).astype(F32), pad], axis=1)
    last = jnp.concatenate([((seg_chunk + c) * CHUNK).astype(F32), pad], axis=1)
    first_chunk = jnp.concatenate([seg_chunk.astype(F32), jnp.zeros_like(pad)], axis=1)
    seg = dict(
        first_row=jnp.broadcast_to(first[:, None, :], (N_BLK, 8, LANE)),
        last_row=jnp.broadcast_to(last[:, None, :], (N_BLK, 8, LANE)),
        first_col=jnp.broadcast_to(first[:, :, None], (N_BLK, LANE, LANE)),
        last_col=jnp.broadcast_to(last[:, :, None], (N_BLK, LANE, LANE)),
        first_chunk_col=jnp.broadcast_to(first_chunk[:, :, None], (N_BLK, LANE, LANE)).astype(BF16))
    return (tile_start, tiles_e.astype(jnp.int32), tile_expert, in_src, out_dst,
            sub_used.astype(jnp.int32), seg)


def _dispatch_kernel(sub_ref, h_ref, pos_ref, first_ref, last_ref, fchunk_ref, s_ref):
    b = pl.program_id(0)
    n_sub = jnp.where(b < N_BLK, sub_ref[jnp.minimum(b, N_BLK - 1)], 0)

    def locate(sb):
        rows = (lax.broadcasted_iota(jnp.int32, (SUB, LANE), 0) + sb * SUB).astype(F32)
        own = (jnp.where(rows >= first_ref[0, 0:1, :], 1.0, 0.0)
               - jnp.where(rows >= last_ref[0, 0:1, :], 1.0, 0.0)).astype(BF16)
        both = jnp.dot(own, jnp.concatenate([pos_ref[...], fchunk_ref[0]], axis=1),
                       preferred_element_type=F32)
        return rows, both

    def gather(sb, located):
        rows, both = located
        have = both[:, :TM]
        want = rows + 1.0 - float(CHUNK) * both[:, TM:]
        p = jnp.where(have == jnp.concatenate([want] * (TM // LANE), axis=1), 1.0, 0.0)
        s_ref[sb * SUB:(sb + 1) * SUB, :] = jnp.dot(
            p.astype(BF16), h_ref[...], preferred_element_type=F32).astype(BF16)

    def fill(sb):
        gather(sb, locate(sb))

    def clear(sb):
        s_ref[sb * SUB:(sb + 1) * SUB, :] = jnp.zeros((SUB, D_MODEL), BF16)

    @pl.when(b < N_BLK)
    def _():
        ahead = locate(0)
        for sb in range(SUB_ALWAYS):
            cur, ahead = ahead, (locate(sb + 1) if sb + 1 < SUB_ALWAYS else None)
            gather(sb, cur)

    @pl.when(b == N_BLK)
    def _():
        for sb in range(SUB_ALWAYS):
            clear(sb)

    for sb in range(SUB_ALWAYS, N_SUB):
        pl.when(sb < n_sub)(functools.partial(fill, sb))
        pl.when(sb >= n_sub)(functools.partial(clear, sb))


def _dispatch(sub_used, h2, pos, seg):
    last = N_BLK - 1
    blk3 = lambda b, *_: (jnp.minimum(b, last), 0, 0)
    grid_spec = pltpu.PrefetchScalarGridSpec(
        num_scalar_prefetch=1,
        grid=(N_BLK + 1,),
        in_specs=[pl.BlockSpec((TM, D_MODEL), lambda b, *_: (jnp.minimum(b, last), 0)),
                  pl.BlockSpec((LANE, TM), lambda b, *_: (0, jnp.minimum(b, last))),
                  pl.BlockSpec((1, 8, LANE), blk3),
                  pl.BlockSpec((1, 8, LANE), blk3),
                  pl.BlockSpec((1, LANE, LANE), blk3)],
        out_specs=pl.BlockSpec((BLK_ROWS, D_MODEL), lambda b, *_: (b, 0)))
    return pl.pallas_call(
        _dispatch_kernel,
        grid_spec=grid_spec,
        out_shape=jax.ShapeDtypeStruct(((N_BLK + 1) * BLK_ROWS, D_MODEL), BF16),
        compiler_params=_cparams(("arbitrary",)),
        name="dispatch",
    )(sub_used, h2, pos, seg["first_row"], seg["last_row"], seg["first_chunk_col"])


def _experts_kernel(tstart_ref, tcount_ref, texp_ref, insrc_ref, outdst_ref,
                    s_hbm, wg_ref, wu_ref, wd_ref, o_hbm,
                    xb0, xb1, xb2, ob0, ob1, ob2, wgub, wdb, insem, outsem):
    xbufs = (xb0, xb1, xb2)
    obufs = (ob0, ob1, ob2)
    e = pl.program_id(0)
    n_tiles = tstart_ref[N_EXPERTS]
    t0 = tstart_ref[e]

    def slot_base(t):
        te = texp_ref[t]
        j = jnp.minimum(t - tstart_ref[te], EXP_TILES - 1)
        return te * EXP_SLOTS + j * TILE_CHUNKS

    def in_copy(src_chunk, buf, i, sem):
        return pltpu.make_async_copy(s_hbm.at[src_chunk], xbufs[buf].at[i], sem)

    def out_copy(dst_chunk, buf, i, sem):
        return pltpu.make_async_copy(obufs[buf].at[i], o_hbm.at[dst_chunk], sem)

    def start_in(t, buf):
        base = slot_base(t)
        for i in range(TILE_CHUNKS):
            in_copy(insrc_ref[base + i], buf, i, insem.at[buf]).start()

    def start_out(t, buf, real):
        base = slot_base(jnp.maximum(t, 0))
        for i in range(TILE_CHUNKS):
            dst = jnp.where(real, outdst_ref[base + i], TRASH_CHUNK + buf * TILE_CHUNKS + i)
            out_copy(dst, buf, i, outsem.at[buf]).start()

    def wait_in(sem_idx):
        for i in range(TILE_CHUNKS):
            in_copy(0, 0, i, insem.at[sem_idx]).wait()

    def wait_out(sem_idx):
        for i in range(TILE_CHUNKS):
            out_copy(0, 0, i, outsem.at[sem_idx]).wait()

    def on_buffer(buf_dyn, fn):
        for k in range(N_BUF):
            pl.when(buf_dyn == k)(functools.partial(fn, k))

    wgub[:, :EXPERT_FF] = wg_ref[0].astype(BF16)
    wgub[:, EXPERT_FF:] = wu_ref[0].astype(BF16)
    wdb[...] = wd_ref[0].astype(BF16)

    @pl.when(e == 0)
    def _():
        for ob in obufs:
            ob[...] = jnp.zeros(ob.shape, BF16)
        start_out(0, 0, False)
        start_out(0, 1, False)
        start_in(0, 0)
        start_in(1, 1)

    def tile_body(j, carry):
        t = t0 + j

        def run(k):
            nxt = (k + 2) % N_BUF
            wait_in(k)
            wait_out(k)
            start_in(t + 2, nxt)
            start_out(t - 1, nxt, t >= 1)
            x = xbufs[k][...].reshape(TILE_ROWS, D_MODEL)
            au = jnp.dot(x, wgub[...], preferred_element_type=F32)
            a, u = au[:, :EXPERT_FF], au[:, EXPERT_FF:]
            hid = (a * jax.nn.sigmoid(a) * u).astype(BF16)
            out = jnp.dot(hid, wdb[...], preferred_element_type=F32).astype(BF16)
            obufs[k][...] = out.reshape(TILE_CHUNKS, CHUNK, D_MODEL)

        on_buffer(t % N_BUF, run)
        return carry

    lax.fori_loop(0, tcount_ref[e], tile_body, 0)

    @pl.when(e == N_EXPERTS - 1)
    def _():
        on_buffer((n_tiles + 2) % N_BUF, lambda k: start_out(n_tiles - 1, k, n_tiles >= 1))
        wait_in(n_tiles % N_BUF)
        wait_in((n_tiles + 1) % N_BUF)
        for k in range(N_BUF):
            wait_out(k)


def _experts(tile_start, tile_count, tile_expert, in_src, out_dst, sorted_rows, wg, wu, wd):
    wspec = lambda shape: pl.BlockSpec(shape, lambda e, *_: (e, 0, 0))
    grid_spec = pltpu.PrefetchScalarGridSpec(
        num_scalar_prefetch=5,
        grid=(N_EXPERTS,),
        in_specs=[pl.BlockSpec(memory_space=pl.ANY),
                  wspec((1, D_MODEL, EXPERT_FF)),
                  wspec((1, D_MODEL, EXPERT_FF)),
                  wspec((1, EXPERT_FF, D_MODEL))],
        out_specs=pl.BlockSpec(memory_space=pl.ANY),
        scratch_shapes=[pltpu.VMEM((TILE_CHUNKS, CHUNK, D_MODEL), BF16)] * (2 * N_BUF) + [
                        pltpu.VMEM((D_MODEL, 2 * EXPERT_FF), BF16),
                        pltpu.VMEM((EXPERT_FF, D_MODEL), BF16),
                        pltpu.SemaphoreType.DMA((N_BUF,)),
                        pltpu.SemaphoreType.DMA((N_BUF,))])
    return pl.pallas_call(
        _experts_kernel,
        grid_spec=grid_spec,
        out_shape=jax.ShapeDtypeStruct(sorted_rows.shape, BF16),
        input_output_aliases={5: 0},
        compiler_params=_cparams(("arbitrary",)),
        name="experts",
    )(tile_start, tile_count, tile_expert, in_src, out_dst, sorted_rows, wg, wu, wd)


def _combine_kernel(sub_ref, o_ref, post_ref, gate_ref, first_ref, last_ref, h_ref, x1_ref, mods_ref,
                    sg32_ref, su32_ref, sd32_ref, fg_ref, yp_ref, ys_ref, acc_ref, sg_ref, su_ref, sd_ref):
    i = pl.program_id(0)
    n_sub = sub_ref[i]

    @pl.when(i == 0)
    def _():
        sg_ref[...] = sg32_ref[...].astype(BF16)
        su_ref[...] = su32_ref[...].astype(BF16)
        sd_ref[...] = sd32_ref[...].astype(BF16)

    def locate(sb):
        rows = (lax.broadcasted_iota(jnp.int32, (LANE, SUB), 1) + sb * SUB).astype(F32)
        first = jnp.concatenate([first_ref[0]] * (SUB // LANE), axis=1)
        last = jnp.concatenate([last_ref[0]] * (SUB // LANE), axis=1)
        own = jnp.where(rows >= first, 1.0, 0.0) - jnp.where(rows >= last, 1.0, 0.0)
        start = jnp.sum(jnp.where(own > 0.0, first, 0.0), axis=0, keepdims=True)
        want = rows[0:1, :] + 1.0 - start
        ownb = own.astype(BF16)
        have = jnp.dot(post_ref[...], ownb, preferred_element_type=F32)
        own2 = jnp.concatenate([ownb[:N_EXPERTS], ownb[:N_EXPERTS]], axis=0)
        gate = jnp.dot(gate_ref[...], own2, preferred_element_type=F32)
        return have, want, gate

    def select(located):
        have, want, gate = located
        return jnp.where(have == want, gate, 0.0).astype(BF16)

    def weights_of(sb):
        return select(locate(sb))

    h = h_ref[...]
    a = jnp.dot(h, sg_ref[...], preferred_element_type=F32)
    u = jnp.dot(h, su_ref[...], preferred_element_type=F32)
    shared = jnp.dot((a * jax.nn.sigmoid(a) * u).astype(BF16), sd_ref[...],
                     preferred_element_type=F32)
    pws, ahead = [], locate(0)
    for sb in range(SUB_ALWAYS):
        cur, ahead = ahead, (locate(sb + 1) if sb + 1 < SUB_ALWAYS else None)
        pws.append(select(cur))
    pw_main = jnp.concatenate(pws, axis=1)
    acc_ref[...] = shared + jnp.dot(pw_main, o_ref[0:SUB_ALWAYS * SUB, :],
                                    preferred_element_type=F32)
    for sb in range(SUB_ALWAYS, N_SUB):
        @pl.when(sb < n_sub)
        def _():
            acc_ref[...] += jnp.dot(weights_of(sb), o_ref[sb * SUB:(sb + 1) * SUB, :],
                                    preferred_element_type=F32)

    m = mods_ref[0]
    y = _rms(x1_ref[...] + m[5:6] * acc_ref[...], fg_ref[...])

    @pl.when(i < NP_TILES)
    def _():
        yp_ref[...] = y

    @pl.when(i >= NP_TILES)
    def _():
        ys_ref[...] = y


def _combine(sub_used, expert_rows, pos_t, gate_t, seg, h2, x1, mods3, sg, su, sd, fg):
    tile = lambda i, *_: (i, 0)
    blk3 = lambda i, *_: (i, 0, 0)
    grid_spec = pltpu.PrefetchScalarGridSpec(
        num_scalar_prefetch=1,
        grid=(N_BLK,),
        in_specs=[pl.BlockSpec((BLK_ROWS, D_MODEL), tile),
                  pl.BlockSpec((TM, LANE), tile),
                  pl.BlockSpec((TM, LANE), tile),
                  pl.BlockSpec((1, LANE, LANE), blk3),
                  pl.BlockSpec((1, LANE, LANE), blk3),
                  pl.BlockSpec((TM, D_MODEL), tile),
                  pl.BlockSpec((TM, D_MODEL), tile),
                  pl.BlockSpec((1, N_MOD, D_MODEL), lambda i, *_: (_seg_of_tile(i), 0, 0)),
                  _const_spec((D_MODEL, EXPERT_FF)),
                  _const_spec((D_MODEL, EXPERT_FF)),
                  _const_spec((EXPERT_FF, D_MODEL)),
                  _const_spec((1, D_MODEL))],
        out_specs=[pl.BlockSpec((TM, D_MODEL), lambda i, *_: (jnp.minimum(i, NP_TILES - 1), 0)),
                   pl.BlockSpec((TM, D_MODEL), lambda i, *_: (jnp.maximum(i - NP_TILES, 0), 0))],
        scratch_shapes=[pltpu.VMEM((TM, D_MODEL), F32),
                        pltpu.VMEM((D_MODEL, EXPERT_FF), BF16),
                        pltpu.VMEM((D_MODEL, EXPERT_FF), BF16),
                        pltpu.VMEM((EXPERT_FF, D_MODEL), BF16)])
    return pl.pallas_call(
        _combine_kernel,
        grid_spec=grid_spec,
        out_shape=[jax.ShapeDtypeStruct((N_PROMPT, D_MODEL), F32),
                   jax.ShapeDtypeStruct((N_SAMPLE, D_MODEL), F32)],
        compiler_params=_cparams(("arbitrary",)),
        name="combine",
    )(sub_used, expert_rows, pos_t, gate_t, seg["first_col"], seg["last_col"], h2, x1, mods3,
      sg, su, sd, fg)


def _rope_tables():
    pos = np.arange(DEC_SEQ)
    row = (pos // GRID_W).astype(np.float64)
    col = (pos % GRID_W).astype(np.float64)
    n_freq = QK_ROPE_DIM // 4
    inv_freq = ROPE_BASE ** (-np.arange(n_freq, dtype=np.float64) / n_freq)
    ar, ac = row[:, None] * inv_freq, col[:, None] * inv_freq
    cos32 = np.concatenate([np.cos(ar), np.cos(ar), np.cos(ac), np.cos(ac)], axis=1)
    sin32 = np.concatenate([-np.sin(ar), np.sin(ar), -np.sin(ac), np.sin(ac)], axis=1)
    cos32 = np.concatenate([np.ones((TM, QK_ROPE_DIM)), cos32], axis=0)
    sin32 = np.concatenate([np.zeros((TM, QK_ROPE_DIM)), sin32], axis=0)
    n = cos32.shape[0]
    ones = np.ones((n, QK_NOPE_DIM))
    zeros = np.zeros((n, QK_NOPE_DIM))
    tail = np.zeros((n, HEAD_PAD - QK_NOPE_DIM - QK_ROPE_DIM))
    q_scale = ATTN_SCALE * math.log2(math.e)
    rest = np.zeros((n, HEAD_PAD - QK_ROPE_DIM))
    qc = np.concatenate([ones, cos32, tail], axis=1) * q_scale
    qs = np.concatenate([zeros, sin32, tail], axis=1) * q_scale
    kc = np.concatenate([cos32, rest], axis=1)
    ks = np.concatenate([sin32, rest], axis=1)
    return jnp.asarray(np.stack([qc, qs, kc, ks], axis=0).astype(np.float32))


def _pack_weights(w_in, w_uq, w_ukv):
    win = w_in.astype(BF16)
    gate_tail = w_in[:, S_GATE - PHASE + 2 * D_MODEL:]
    wgt = jnp.pad(gate_tail, ((0, 0), (0, LANE - PHASE))).astype(BF16)
    uq = w_uq.reshape(Q_LORA, N_HEADS, QK_NOPE_DIM + QK_ROPE_DIM)
    pad_q = HEAD_PAD - QK_NOPE_DIM - QK_ROPE_DIM
    wuqp = jnp.pad(uq, ((0, 0), (0, 0), (0, pad_q))).reshape(Q_LORA, QK_COLS).astype(BF16)
    ukv = w_ukv.reshape(KV_LORA, N_HEADS, QK_NOPE_DIM + V_DIM)
    wukp = jnp.pad(ukv[:, :, :QK_NOPE_DIM], ((0, 0), (0, 0), (0, HEAD_PAD - QK_NOPE_DIM)))
    wukp_t = wukp.reshape(KV_LORA, QK_COLS).T.astype(BF16)
    wuv = ukv[:, :, QK_NOPE_DIM:].reshape(KV_LORA, V_COLS).astype(BF16)
    return win, wgt, wuqp, wukp_t, wuv


def kernel(x_prompt, x_sample, cache_ckv, cache_krope, c, c_ctx, mod_w, mod_b, norm1_g, w_in, q_norm_g, w_uq, kv_norm_g, w_ukv, w_o_attn, conv_w, conv_b, conv_ln_g, conv_ln_b, w_pw2, w_out, norm2_g, router_w, router_bias, exp_w_gate, exp_w_up, exp_w_down, sh_w_gate, sh_w_up, sh_w_down, final_g):
    xp = x_prompt.reshape(N_PROMPT, D_MODEL)
    xs = x_sample.reshape(N_SAMPLE, D_MODEL)
    cond8 = jnp.concatenate([c_ctx[None, :], c, jnp.zeros((8 - 1 - DEC_BATCH, D_MODEL), F32)], axis=0)
    mods3 = _mods(cond8, mod_w[0], mod_b).reshape(8, N_MOD, D_MODEL)

    win, wgt, wuqp, wukp, wuv = _pack_weights(w_in[0], w_uq[0], w_ukv[0])
    q, k, v, z, gt, ckv_new, kr_new = _inproj(
        xp, xs, mods3, norm1_g, q_norm_g, kv_norm_g, win, wgt, wuqp, wukp, wuv, _rope_tables())

    krope_pad = jnp.pad(cache_krope[:, 0], ((0, 0), (0, 0),
                                            (QK_NOPE_DIM, HEAD_PAD - QK_NOPE_DIM - QK_ROPE_DIM)))
    kc, vc = _ctxkv(cache_ckv[:, 0], krope_pad, wukp, wuv)
    attn_p = _attn_prompt(q, k, v)
    attn_s = _attn_sample(q, k, v, kc, vc)

    rwt = router_w[0].T
    rwh = rwt.astype(BF16)
    rwl = (rwt - rwh.astype(F32)).astype(BF16)
    cw = conv_w[0].reshape(CONV_K, D_MODEL // LANE, LANE).transpose(1, 0, 2)
    x1, h2, gates_t = _mix(
        attn_p, attn_s, z, gt, xp, xs, mods3,
        w_o_attn[0], w_pw2[0], w_out[0],
        cw, conv_b, conv_ln_g, conv_ln_b, norm2_g, rwh, rwl, router_bias.reshape(N_EXPERTS, 1))

    pos, cnt = _plan(gates_t)
    tile_start, tile_count, tile_expert, in_src, out_dst, sub_used, seg = _tile_tables(cnt)
    expert_rows = _experts(tile_start, tile_count, tile_expert, in_src, out_dst,
                           _dispatch(sub_used, h2, pos, seg).reshape(-1, CHUNK, D_MODEL),
                           exp_w_gate[0], exp_w_up[0], exp_w_down[0])
    gates = gates_t.T
    gates_hi = gates.astype(BF16)
    gates_lo = (gates - gates_hi.astype(F32)).astype(BF16)
    y_p, y_s = _combine(sub_used, expert_rows.reshape(-1, D_MODEL), pos.T,
                        jnp.concatenate([gates_hi, gates_lo], axis=1), seg, h2, x1, mods3,
                        sh_w_gate[0], sh_w_up[0], sh_w_down[0], final_g[None, :])
    return (y_p.reshape(BATCH, SEQ, D_MODEL),
            y_s.reshape(DEC_BATCH, DEC_SEQ, D_MODEL),
            ckv_new.reshape(BATCH, 1, SEQ, KV_LORA),
            kr_new.reshape(BATCH, 1, SEQ, QK_ROPE_DIM))
```

```python
import functools
import math

import jax
import jax.numpy as jnp
import numpy as np
from jax import lax
from jax.experimental import pallas as pl
from jax.experimental.pallas import tpu as pltpu

F32 = jnp.float32
BF16 = jnp.bfloat16

D_MODEL = 1024
BATCH = 16
SEQ = 256
DEC_BATCH = 2
DEC_SEQ = 2048
PAST_LEN = 256
GRID_W = 64
N_HEADS = 16
QK_NOPE_DIM = 64
QK_ROPE_DIM = 32
V_DIM = 64
Q_LORA = 384
KV_LORA = 256
ROPE_BASE = 10000.0
ATTN_SCALE = 1.0 / math.sqrt(QK_NOPE_DIM + QK_ROPE_DIM)
CONV_K = 31
CONV_PAD = CONV_K // 2
N_EXPERTS = 64
TOP_K = 8
N_GROUPS = 8
GROUP_SIZE = N_EXPERTS // N_GROUPS
TOPK_GROUPS = 4
EXPERT_FF = 256
ROUTED_SCALE = 2.5
EPS = 1e-6
N_MOD = 6

LANE = 128
HEAD_PAD = LANE
QK_COLS = N_HEADS * HEAD_PAD
N_PROMPT = BATCH * SEQ
N_SAMPLE = DEC_BATCH * DEC_SEQ
N_TOK = N_PROMPT + N_SAMPLE

TM = 256
NP_TILES = N_PROMPT // TM
NS_TILES = N_SAMPLE // TM
N_TILES = NP_TILES + NS_TILES
TILES_PER_DEC = DEC_SEQ // TM
HALO = 16
S_Q = 0
S_KV = S_Q + Q_LORA
S_KR = S_KV + KV_LORA
S_CONV = S_KR + QK_ROPE_DIM
S_GATE = S_CONV + 2 * D_MODEL
PHASE = S_CONV % LANE
CONV_WIN = 2 * D_MODEL + LANE
GATE_WIN = 2 * D_MODEL + LANE
IN_COLS = S_GATE + 2 * D_MODEL
V_COLS = N_HEADS * V_DIM

VMEM_CAP = 56 * 1024 * 1024


def _cparams(sem, vmem_mb):
    return pltpu.CompilerParams(dimension_semantics=sem,
                                vmem_limit_bytes=min(vmem_mb * 1024 * 1024, VMEM_CAP))


def _const_spec(shape):
    n = len(shape)
    return pl.BlockSpec(shape, lambda *_: (0,) * n, pipeline_mode=pl.Buffered(1))


def _seg_of_tile(i):
    return jnp.where(i < NP_TILES, 0, 1 + (i - NP_TILES) // TILES_PER_DEC)


def _rms(x, g):
    return x * lax.rsqrt(jnp.mean(x * x, axis=-1, keepdims=True) + EPS) * g


_NT = (((1,), (1,)), ((), ()))


def _mods_kernel(cond_ref, w_ref, b_ref, o_ref):
    c = cond_ref[...]
    s = c * jax.nn.sigmoid(c)
    o_ref[...] = jnp.dot(s.astype(BF16), w_ref[...].astype(BF16),
                         preferred_element_type=F32) + b_ref[...]


def _mods(cond8, mod_w, mod_b):
    n = N_MOD * D_MODEL
    bn = 512
    return pl.pallas_call(
        _mods_kernel,
        grid=(n // bn,),
        in_specs=[pl.BlockSpec((8, D_MODEL), lambda j: (0, 0)),
                  pl.BlockSpec((D_MODEL, bn), lambda j: (0, j)),
                  pl.BlockSpec((1, bn), lambda j: (0, j))],
        out_specs=pl.BlockSpec((8, bn), lambda j: (0, j)),
        out_shape=jax.ShapeDtypeStruct((8, n), F32),
        compiler_params=_cparams(("arbitrary",), 16),
        name="mods",
    )(cond8, mod_w, mod_b)


def _keys_t(wukp_t, ckv, krope):
    eye = (lax.broadcasted_iota(jnp.int32, (HEAD_PAD, HEAD_PAD), 0)
           == lax.broadcasted_iota(jnp.int32, (HEAD_PAD, HEAD_PAD), 1))
    rope_t = lax.dot_general(jnp.where(eye, 1.0, 0.0).astype(BF16), krope, _NT,
                             preferred_element_type=F32)
    nope_t = lax.dot_general(wukp_t, ckv, _NT, preferred_element_type=F32)
    return (nope_t + jnp.concatenate([rope_t] * N_HEADS, axis=0)).astype(BF16)


def _rope_swap(x):
    n = x.shape[1]
    half = QK_ROPE_DIM // 4
    lane = lax.broadcasted_iota(jnp.int32, x.shape, 1)
    return jnp.where(lane % (2 * half) < half,
                     pltpu.roll(x, n - half, axis=1), pltpu.roll(x, half, axis=1))


def _inproj_kernel(xp_ref, xs_ref, mods_ref, g1_ref, qg_ref, kvg_ref, win_ref, wgt_ref, wuqp_ref,
                   wukp_ref, wuv_ref, tab_ref,
                   q_ref, k_ref, v_ref, z_ref, gt_ref, ckv_ref, kr_ref):
    i = pl.program_id(0)
    x = jnp.where(i < NP_TILES, xp_ref[...], xs_ref[...])
    m = mods_ref[0]
    h = _rms(x, g1_ref[...]) * (1.0 + m[1:2]) + m[0:1]
    hb = h.astype(BF16)

    def proj(c0, c1):
        return jnp.dot(hb, win_ref[:, c0:c1], preferred_element_type=F32)

    def lanes16(t):
        return jnp.concatenate([t] * N_HEADS, axis=1)

    qn = _rms(proj(S_Q, S_KV), qg_ref[...]).astype(BF16)
    q = jnp.dot(qn, wuqp_ref[...], preferred_element_type=F32)
    q_ref[...] = (q * lanes16(tab_ref[0]) + _rope_swap(q) * lanes16(tab_ref[1])).astype(BF16)

    ckv = _rms(proj(S_KV, S_KR), kvg_ref[...])
    ckvb = ckv.astype(BF16)
    u = proj(S_KR, S_KR + CONV_WIN)
    ua = u[:, :LANE]
    krot = pltpu.roll(ua * tab_ref[2] + _rope_swap(ua) * tab_ref[3], QK_NOPE_DIM, axis=1)
    k_ref[...] = _keys_t(wukp_ref[...], ckvb, krot.astype(BF16))
    v_ref[...] = jnp.dot(ckvb, wuv_ref[...], preferred_element_type=F32).astype(BF16)

    @pl.when(i < NP_TILES)
    def _():
        ckv_ref[...] = ckv
        kr_ref[...] = ua[:, :QK_ROPE_DIM]

    glu = u[:, :D_MODEL + LANE] * jax.nn.sigmoid(u[:, D_MODEL:])
    z_ref[...] = pltpu.roll(glu, D_MODEL + LANE - PHASE, axis=1)[:, :D_MODEL].astype(BF16)
    gates = jax.nn.sigmoid(jnp.concatenate(
        [proj(S_GATE - PHASE, S_GATE - PHASE + 2 * D_MODEL),
         jnp.dot(hb, wgt_ref[...], preferred_element_type=F32)], axis=1))
    gt_ref[...] = pltpu.roll(gates, GATE_WIN - PHASE, axis=1)[:, :2 * D_MODEL].astype(BF16)


def _inproj(xp, xs, mods3, g1, qg, kvg, win, wgt, wuqp, wukp, wuv, tab):
    tile = lambda i: (i, 0)
    ptile = lambda i: (jnp.minimum(i, NP_TILES - 1), 0)
    stile = lambda i: (jnp.maximum(i - NP_TILES, 0), 0)
    tab_blk = lambda i: (0, jnp.where(i < NP_TILES, 0, 1 + (i - NP_TILES) % TILES_PER_DEC), 0)
    return pl.pallas_call(
        _inproj_kernel,
        grid=(N_TILES,),
        in_specs=[pl.BlockSpec((TM, D_MODEL), ptile),
                  pl.BlockSpec((TM, D_MODEL), stile),
                  pl.BlockSpec((1, N_MOD, D_MODEL), lambda i: (_seg_of_tile(i), 0, 0)),
                  _const_spec((1, D_MODEL)),
                  _const_spec((1, Q_LORA)),
                  _const_spec((1, KV_LORA)),
                  _const_spec((D_MODEL, IN_COLS)),
                  _const_spec((D_MODEL, LANE)),
                  _const_spec((Q_LORA, QK_COLS)),
                  _const_spec((QK_COLS, KV_LORA)),
                  _const_spec((KV_LORA, V_COLS)),
                  pl.BlockSpec((4, TM, LANE), tab_blk)],
        out_specs=[pl.BlockSpec((TM, QK_COLS), tile),
                   pl.BlockSpec((QK_COLS, TM), lambda i: (0, i)),
                   pl.BlockSpec((TM, V_COLS), tile),
                   pl.BlockSpec((TM, D_MODEL), tile),
                   pl.BlockSpec((TM, 2 * D_MODEL), tile),
                   pl.BlockSpec((TM, KV_LORA), ptile),
                   pl.BlockSpec((TM, QK_ROPE_DIM), ptile)],
        out_shape=[jax.ShapeDtypeStruct((N_TOK, QK_COLS), BF16),
                   jax.ShapeDtypeStruct((QK_COLS, N_TOK), BF16),
                   jax.ShapeDtypeStruct((N_TOK, V_COLS), BF16),
                   jax.ShapeDtypeStruct((N_TOK, D_MODEL), BF16),
                   jax.ShapeDtypeStruct((N_TOK, 2 * D_MODEL), BF16),
                   jax.ShapeDtypeStruct((N_PROMPT, KV_LORA), F32),
                   jax.ShapeDtypeStruct((N_PROMPT, QK_ROPE_DIM), F32)],
        compiler_params=_cparams(("arbitrary",), 52),
        name="inproj",
    )(xp, xs, mods3, g1, qg, kvg, win, wgt, wuqp, wukp, wuv, tab)


def _ctxkv_kernel(ckv_ref, kr_ref, wukp_ref, wuv_ref, k_ref, v_ref):
    c = ckv_ref[0].astype(BF16)
    k_ref[0] = _keys_t(wukp_ref[...], c, kr_ref[0].astype(BF16))
    v_ref[0] = jnp.dot(c, wuv_ref[...], preferred_element_type=F32).astype(BF16)


def _ctxkv(cache_ckv, krope_pad, wukp, wuv):
    return pl.pallas_call(
        _ctxkv_kernel,
        grid=(DEC_BATCH,),
        in_specs=[pl.BlockSpec((1, PAST_LEN, KV_LORA), lambda b: (b, 0, 0)),
                  pl.BlockSpec((1, PAST_LEN, LANE), lambda b: (b, 0, 0)),
                  _const_spec((QK_COLS, KV_LORA)),
                  _const_spec((KV_LORA, V_COLS))],
        out_specs=[pl.BlockSpec((1, QK_COLS, PAST_LEN), lambda b: (b, 0, 0)),
                   pl.BlockSpec((1, PAST_LEN, V_COLS), lambda b: (b, 0, 0))],
        out_shape=[jax.ShapeDtypeStruct((DEC_BATCH, QK_COLS, PAST_LEN), BF16),
                   jax.ShapeDtypeStruct((DEC_BATCH, PAST_LEN, V_COLS), BF16)],
        compiler_params=_cparams(("arbitrary",), 16),
        name="ctxkv",
    )(cache_ckv, krope_pad, wukp, wuv)


def _attn_kernel(*refs, has_ctx):
    if has_ctx:
        q_ref, k_ref, v_ref, kc_ref, vc_ref, o_ref = refs
    else:
        q_ref, k_ref, v_ref, o_ref = refs
    tq = q_ref.shape[0]
    lane = lax.broadcasted_iota(jnp.int32, (tq, LANE), 1)
    for j in range(N_HEADS // 2):
        pair = slice(j * LANE, (j + 1) * LANE)
        ps, pcs, dens = [], [], []
        for h in (2 * j, 2 * j + 1):
            hs = slice(h * HEAD_PAD, (h + 1) * HEAD_PAD)
            qh = q_ref[:, hs]
            s = jnp.dot(qh, k_ref[hs, :], preferred_element_type=F32)
            mx = jnp.max(s, axis=-1, keepdims=True)
            if has_ctx:
                sc = jnp.dot(qh, kc_ref[0, hs, :], preferred_element_type=F32)
                mx = jnp.maximum(mx, jnp.max(sc, axis=-1, keepdims=True))
            p = jnp.exp2(s - mx)
            den = jnp.sum(p, axis=-1, keepdims=True)
            ps.append(p.astype(BF16))
            if has_ctx:
                pc = jnp.exp2(sc - mx)
                den = den + jnp.sum(pc, axis=-1, keepdims=True)
                pcs.append(pc.astype(BF16))
            dens.append(den)
        o = jnp.dot(jnp.concatenate(ps, axis=0), v_ref[:, pair], preferred_element_type=F32)
        if has_ctx:
            o = o + jnp.dot(jnp.concatenate(pcs, axis=0), vc_ref[0, :, pair],
                            preferred_element_type=F32)
        o_ref[:, pair] = jnp.where(lane < V_DIM, o[:tq] / dens[0], o[tq:] / dens[1]).astype(BF16)


def _attn_prompt(q, k, v):
    blk = lambda b: (b, 0)
    return pl.pallas_call(
        functools.partial(_attn_kernel, has_ctx=False),
        grid=(BATCH,),
        in_specs=[pl.BlockSpec((SEQ, QK_COLS), blk),
                  pl.BlockSpec((QK_COLS, SEQ), lambda b: (0, b)),
                  pl.BlockSpec((SEQ, V_COLS), blk)],
        out_specs=pl.BlockSpec((SEQ, N_HEADS * V_DIM), blk),
        out_shape=jax.ShapeDtypeStruct((N_PROMPT, N_HEADS * V_DIM), BF16),
        compiler_params=_cparams(("arbitrary",), 56),
        name="attn_prompt",
    )(q, k, v)


ATT_TQ = 256


def _attn_sample(q, k, v, kc, vc):
    nq = DEC_SEQ // ATT_TQ
    q0 = N_PROMPT // ATT_TQ
    s0 = N_PROMPT // DEC_SEQ
    once = pl.Buffered(1)
    return pl.pallas_call(
        functools.partial(_attn_kernel, has_ctx=True),
        grid=(DEC_BATCH, nq),
        in_specs=[pl.BlockSpec((ATT_TQ, QK_COLS), lambda b, t: (q0 + b * nq + t, 0)),
                  pl.BlockSpec((QK_COLS, DEC_SEQ), lambda b, t: (0, s0 + b), pipeline_mode=once),
                  pl.BlockSpec((DEC_SEQ, V_COLS), lambda b, t: (s0 + b, 0), pipeline_mode=once),
                  pl.BlockSpec((1, QK_COLS, PAST_LEN), lambda b, t: (b, 0, 0), pipeline_mode=once),
                  pl.BlockSpec((1, PAST_LEN, V_COLS), lambda b, t: (b, 0, 0), pipeline_mode=once)],
        out_specs=pl.BlockSpec((ATT_TQ, N_HEADS * V_DIM), lambda b, t: (b * nq + t, 0)),
        out_shape=jax.ShapeDtypeStruct((N_SAMPLE, N_HEADS * V_DIM), BF16),
        compiler_params=_cparams(("arbitrary", "arbitrary"), 54),
        name="attn_sample",
    )(q, k, v, kc, vc)


def _topk_rows(v, k, n):
    rows = lax.broadcasted_iota(jnp.int32, v.shape, 0).astype(F32)
    sel = jnp.zeros(v.shape, F32)
    for _ in range(k):
        mx = jnp.max(v, axis=0, keepdims=True)
        first = jnp.min(jnp.where(v == mx, rows, float(n)), axis=0, keepdims=True)
        hit = rows == first
        sel = jnp.where(hit, 1.0, sel)
        v = jnp.where(hit, -jnp.inf, v)
    return sel > 0.0


def _route(logits_t, bias):
    scores = jax.nn.sigmoid(logits_t)
    sel = scores + bias
    gscore = []
    for g in range(N_GROUPS):
        blk = sel[g * GROUP_SIZE:(g + 1) * GROUP_SIZE]
        top2 = _topk_rows(blk, 2, GROUP_SIZE)
        gscore.append(jnp.sum(jnp.where(top2, blk, 0.0), axis=0, keepdims=True))
    masked = []
    for g in range(N_GROUPS):
        beaten = jnp.zeros(gscore[g].shape, F32)
        for o in range(N_GROUPS):
            if o < g:
                beaten = beaten + jnp.where(gscore[o] >= gscore[g], 1.0, 0.0)
            elif o > g:
                beaten = beaten + jnp.where(gscore[o] > gscore[g], 1.0, 0.0)
        keep = beaten < float(TOPK_GROUPS)
        masked.append(jnp.where(keep, sel[g * GROUP_SIZE:(g + 1) * GROUP_SIZE], -jnp.inf))
    chosen = _topk_rows(jnp.concatenate(masked, axis=0), TOP_K, N_EXPERTS)
    w = jnp.where(chosen, scores, 0.0)
    return w / jnp.sum(w, axis=0, keepdims=True) * ROUTED_SCALE


def _mix_kernel(ap_ref, as_ref, z_ref, zp_ref, zn_ref, gt_ref, xp_ref, xs_ref, mods_ref,
                wo32_ref, wpw32_ref, wout32_ref, cw_ref, cb_ref, lng_ref, lnb_ref, g2_ref,
                rwh_ref, rwl_ref, rb_ref,
                x1_ref, h2_ref, gates_ref, win_ref, cz_ref, wo_ref, wpw_ref, wout_ref):
    i = pl.program_id(0)

    @pl.when(i == 0)
    def _():
        wo_ref[...] = wo32_ref[...].astype(BF16)
        wpw_ref[...] = wpw32_ref[...].astype(BF16)
        wout_ref[...] = wout32_ref[...].astype(BF16)

    is_prompt = i < NP_TILES
    pos = (i - NP_TILES) % TILES_PER_DEC
    first = jnp.logical_or(is_prompt, pos == 0)
    last = jnp.logical_or(is_prompt, pos == TILES_PER_DEC - 1)

    nchunk = D_MODEL // LANE
    zprev = jnp.where(first, 0.0, zp_ref[...].astype(F32))
    znext = jnp.where(last, 0.0, zn_ref[...].astype(F32))
    zc = z_ref[...].astype(F32)
    for c in range(nchunk):
        cs = slice(c * LANE, (c + 1) * LANE)
        win_ref[c, 0:HALO, :] = zprev[:, cs]
        win_ref[c, HALO:HALO + TM, :] = zc[:, cs]
        win_ref[c, HALO + TM:, :] = znext[:, cs]

    def conv_chunk(c, carry):
        w = cw_ref[c]
        acc = jnp.zeros((TM, LANE), F32)
        for kk in range(CONV_K):
            off = HALO - CONV_PAD + kk
            acc = acc + win_ref[c, off:off + TM, :] * w[kk:kk + 1, :]
        cz_ref[c] = acc
        return carry

    lax.fori_loop(0, nchunk, conv_chunk, 0)

    m = mods_ref[0]
    rwh = rwh_ref[...]
    rows = [slice(hf * (TM // 2), (hf + 1) * (TM // 2)) for hf in range(2)]

    def conv_act(rs):
        conv = jnp.concatenate([cz_ref[c, rs, :] for c in range(nchunk)], axis=1) + cb_ref[...]
        mu = jnp.mean(conv, axis=-1, keepdims=True)
        cen = conv - mu
        var = jnp.mean(cen * cen, axis=-1, keepdims=True)
        ln = cen * lax.rsqrt(var + EPS) * lng_ref[...] + lnb_ref[...]
        return (ln * jax.nn.sigmoid(ln)).astype(BF16)

    def branches(rs, act):
        attn = jnp.where(is_prompt, ap_ref[rs, :], as_ref[rs, :])
        return (jnp.dot(attn, wo_ref[...], preferred_element_type=F32),
                jnp.dot(act, wpw_ref[...], preferred_element_type=F32))

    def merge(rs, outs):
        gt = gt_ref[rs, :].astype(F32)
        return (gt[:, :D_MODEL] * outs[0] + gt[:, D_MODEL:] * outs[1]).astype(BF16)

    def project(merged):
        return jnp.dot(merged, wout_ref[...], preferred_element_type=F32)

    def residual(rs, mix):
        x = jnp.where(is_prompt, xp_ref[rs, :], xs_ref[rs, :])
        x1 = x + m[2:3] * mix
        x1_ref[rs, :] = x1
        h2 = _rms(x1, g2_ref[...]) * (1.0 + m[4:5]) + m[3:4]
        hh = h2.astype(BF16)
        h2_ref[rs, :] = hh
        return hh, (h2 - hh.astype(F32)).astype(BF16)

    def route(rs, hs):
        hh, hl = hs
        logits_t = (lax.dot_general(rwh, hh, _NT, preferred_element_type=F32)
                    + lax.dot_general(rwh, hl, _NT, preferred_element_type=F32)
                    + lax.dot_general(rwl_ref[...], hh, _NT, preferred_element_type=F32))
        gates_ref[:, rs] = _route(logits_t, rb_ref[...])

    r0, r1 = rows
    act0 = conv_act(r0)
    out0 = branches(r0, act0)
    act1 = conv_act(r1)
    mrg0 = merge(r0, out0)
    out1 = branches(r1, act1)
    mix0 = project(mrg0)
    mrg1 = merge(r1, out1)
    hs0 = residual(r0, mix0)
    mix1 = project(mrg1)
    route(r0, hs0)
    hs1 = residual(r1, mix1)
    route(r1, hs1)


def _mix(attn_p, attn_s, z, gt, xp, xs, mods3, wo, wpw, wout, cw, cb, lng, lnb, g2, rwh, rwl, rb):
    tile = lambda i: (i, 0)
    ptile = lambda i: (jnp.minimum(i, NP_TILES - 1), 0)
    stile = lambda i: (jnp.maximum(i - NP_TILES, 0), 0)
    hpt = TM // HALO
    nhb = N_TOK // HALO
    return pl.pallas_call(
        _mix_kernel,
        grid=(N_TILES,),
        in_specs=[pl.BlockSpec((TM, D_MODEL), ptile),
                  pl.BlockSpec((TM, D_MODEL), stile),
                  pl.BlockSpec((TM, D_MODEL), tile),
                  pl.BlockSpec((HALO, D_MODEL), lambda i: (jnp.maximum(i * hpt - 1, 0), 0)),
                  pl.BlockSpec((HALO, D_MODEL), lambda i: (jnp.minimum((i + 1) * hpt, nhb - 1), 0)),
                  pl.BlockSpec((TM, 2 * D_MODEL), tile),
                  pl.BlockSpec((TM, D_MODEL), ptile),
                  pl.BlockSpec((TM, D_MODEL), stile),
                  pl.BlockSpec((1, N_MOD, D_MODEL), lambda i: (_seg_of_tile(i), 0, 0)),
                  _const_spec((D_MODEL, D_MODEL)),
                  _const_spec((D_MODEL, D_MODEL)),
                  _const_spec((D_MODEL, D_MODEL)),
                  _const_spec((D_MODEL // LANE, CONV_K, LANE)),
                  _const_spec((1, D_MODEL)),
                  _const_spec((1, D_MODEL)),
                  _const_spec((1, D_MODEL)),
                  _const_spec((1, D_MODEL)),
                  _const_spec((N_EXPERTS, D_MODEL)),
                  _const_spec((N_EXPERTS, D_MODEL)),
                  _const_spec((N_EXPERTS, 1))],
        out_specs=[pl.BlockSpec((TM, D_MODEL), tile),
                   pl.BlockSpec((TM, D_MODEL), tile),
                   pl.BlockSpec((N_EXPERTS, TM), lambda i: (0, i))],
        out_shape=[jax.ShapeDtypeStruct((N_TOK, D_MODEL), F32),
                   jax.ShapeDtypeStruct((N_TOK, D_MODEL), BF16),
                   jax.ShapeDtypeStruct((N_EXPERTS, N_TOK), F32)],
        scratch_shapes=[pltpu.VMEM((D_MODEL // LANE, TM + 2 * HALO, LANE), F32),
                        pltpu.VMEM((D_MODEL // LANE, TM, LANE), F32)]
        + [pltpu.VMEM((D_MODEL, D_MODEL), BF16)] * 3,
        compiler_params=_cparams(("arbitrary",), 52),
        name="mix",
    )(attn_p, attn_s, z, z, z, gt, xp, xs, mods3, wo, wpw, wout, cw, cb, lng, lnb, g2, rwh, rwl, rb)


CHUNK = 16
BLK_ROWS = TM * TOP_K + N_EXPERTS * CHUNK
BLK_CHUNKS = BLK_ROWS // CHUNK
SUB = 256
N_SUB = BLK_ROWS // SUB
SUB_ALWAYS = TM * TOP_K // SUB + 2
N_BLK = N_TOK // TM
TILE_CHUNKS = 32
TILE_ROWS = TILE_CHUNKS * CHUNK
MAX_CHUNKS = N_TOK * TOP_K // CHUNK + N_BLK * N_EXPERTS
MAX_TILES = MAX_CHUNKS // TILE_CHUNKS + N_EXPERTS
EXP_TILES = N_TOK // TILE_ROWS + 3
EXP_SLOTS = EXP_TILES * TILE_CHUNKS
N_BUF = 3
ZERO_CHUNK = N_BLK * BLK_CHUNKS
TRASH_CHUNK = ZERO_CHUNK + 1


def _plan_kernel(g_ref, pos_ref, cnt_ref):
    g = g_ref[...]
    picked = jnp.where(g > 0.0, 1.0, 0.0)
    r = lax.broadcasted_iota(jnp.int32, (TM, TM), 0)
    c = lax.broadcasted_iota(jnp.int32, (TM, TM), 1)
    upto = jnp.where(r <= c, 1.0, 0.0).astype(BF16)
    pos = picked * jnp.dot(picked.astype(BF16), upto, preferred_element_type=F32)
    pos_ref[0:N_EXPERTS, :] = pos.astype(BF16)
    pos_ref[N_EXPERTS:, :] = jnp.zeros((LANE - N_EXPERTS, TM), BF16)
    n = jnp.sum(picked, axis=1, keepdims=True)
    nchunk = jnp.floor((n + float(CHUNK - 1)) * (1.0 / CHUNK))
    cnt_ref[0] = jnp.broadcast_to(nchunk, (N_EXPERTS, LANE))


def _plan(gates_t):
    return pl.pallas_call(
        _plan_kernel,
        grid=(N_BLK,),
        in_specs=[pl.BlockSpec((N_EXPERTS, TM), lambda b: (0, b))],
        out_specs=[pl.BlockSpec((LANE, TM), lambda b: (0, b)),
                   pl.BlockSpec((1, N_EXPERTS, LANE), lambda b: (b, 0, 0))],
        out_shape=[jax.ShapeDtypeStruct((LANE, N_TOK), BF16),
                   jax.ShapeDtypeStruct((N_BLK, N_EXPERTS, LANE), F32)],
        compiler_params=_cparams(("arbitrary",), 16),
        name="plan",
    )(gates_t)


def _tile_tables(cnt):
    c = cnt[:, :, 0].astype(jnp.int32)
    seg_chunk = jnp.cumsum(c, axis=1) - c
    ct = c.T
    seg_q = jnp.cumsum(ct, axis=1) - ct
    tiles_e = (ct.sum(axis=1) + TILE_CHUNKS - 1) // TILE_CHUNKS
    tile_end = jnp.cumsum(tiles_e)
    tile_start = tile_end - tiles_e
    q = jnp.arange(EXP_SLOTS, dtype=jnp.int32)
    inside = jnp.logical_and(q[None, :, None] >= seg_q[:, None, :],
                             q[None, :, None] < (seg_q + ct)[:, None, :])
    base = jnp.arange(N_BLK, dtype=jnp.int32)[None, :] * BLK_CHUNKS + seg_chunk.T - seg_q
    src = q[None, :] + jnp.sum(jnp.where(inside, base[:, None, :], 0), axis=2)
    valid = jnp.any(inside, axis=2)
    buf = (tile_start[:, None] + q[None, :] // TILE_CHUNKS) % N_BUF
    trash = TRASH_CHUNK + buf * TILE_CHUNKS + q[None, :] % TILE_CHUNKS
    in_src = jnp.where(valid, src, ZERO_CHUNK).astype(jnp.int32).reshape(-1)
    out_dst = jnp.where(valid, src, trash).astype(jnp.int32).reshape(-1)
    t = jnp.arange(MAX_TILES + N_BUF, dtype=jnp.int32)
    tile_expert = jnp.minimum(jnp.sum((t[:, None] >= tile_end[None, :]).astype(jnp.int32), axis=1),
                              N_EXPERTS - 1)
    tile_start = jnp.concatenate([tile_start, tile_end[-1:]]).astype(jnp.int32)
    sub_used = (jnp.sum(c, axis=1) * CHUNK + SUB - 1) // SUB
    pad = jnp.full((N_BLK, LANE - N_EXPERTS), float(2 * BLK_ROWS), F32)
    first = jnp.concatenate([(seg_chunk * CHUNK).astype(F32), pad], axis=1)
    last = jnp.concatenate([((seg_chunk + c) * CHUNK).astype(F32), pad], axis=1)
    first_chunk = jnp.concatenate([seg_chunk.astype(F32), jnp.zeros_like(pad)], axis=1)
    seg = dict(
        first_row=jnp.broadcast_to(first[:, None, :], (N_BLK, 8, LANE)),
        last_row=jnp.broadcast_to(last[:, None, :], (N_BLK, 8, LANE)),
        first_col=jnp.broadcast_to(first[:, :, None], (N_BLK, LANE, LANE)),
        last_col=jnp.broadcast_to(last[:, :, None], (N_BLK, LANE, LANE)),
        first_chunk_col=jnp.broadcast_to(first_chunk[:, :, None], (N_BLK, LANE, LANE)).astype(BF16))
    return (tile_start, tiles_e.astype(jnp.int32), tile_expert, in_src, out_dst,
            sub_used.astype(jnp.int32), seg)


def _dispatch_kernel(sub_ref, h_ref, pos_ref, first_ref, last_ref, fchunk_ref, s_ref):
    b = pl.program_id(0)
    n_sub = jnp.where(b < N_BLK, sub_ref[jnp.minimum(b, N_BLK - 1)], 0)

    def locate(sb):
        rows = (lax.broadcasted_iota(jnp.int32, (SUB, LANE), 0) + sb * SUB).astype(F32)
        own = (jnp.where(rows >= first_ref[0, 0:1, :], 1.0, 0.0)
               - jnp.where(rows >= last_ref[0, 0:1, :], 1.0, 0.0)).astype(BF16)
        both = jnp.dot(own, jnp.concatenate([pos_ref[...], fchunk_ref[0]], axis=1),
                       preferred_element_type=F32)
        return rows, both

    def gather(sb, located):
        rows, both = located
        have = both[:, :TM]
        want = rows + 1.0 - float(CHUNK) * both[:, TM:]
        p = jnp.where(have == jnp.concatenate([want] * (TM // LANE), axis=1), 1.0, 0.0)
        s_ref[sb * SUB:(sb + 1) * SUB, :] = jnp.dot(
            p.astype(BF16), h_ref[...], preferred_element_type=F32).astype(BF16)

    def fill(sb):
        gather(sb, locate(sb))

    def clear(sb):
        s_ref[sb * SUB:(sb + 1) * SUB, :] = jnp.zeros((SUB, D_MODEL), BF16)

    @pl.when(b < N_BLK)
    def _():
        ahead = locate(0)
        for sb in range(SUB_ALWAYS):
            cur, ahead = ahead, (locate(sb + 1) if sb + 1 < SUB_ALWAYS else None)
            gather(sb, cur)

    @pl.when(b == N_BLK)
    def _():
        for sb in range(SUB_ALWAYS):
            clear(sb)

    for sb in range(SUB_ALWAYS, N_SUB):
        pl.when(sb < n_sub)(functools.partial(fill, sb))
        pl.when(sb >= n_sub)(functools.partial(clear, sb))


def _dispatch(sub_used, h2, pos, seg):
    last = N_BLK - 1
    blk3 = lambda b, *_: (jnp.minimum(b, last), 0, 0)
    grid_spec = pltpu.PrefetchScalarGridSpec(
        num_scalar_prefetch=1,
        grid=(N_BLK + 1,),
        in_specs=[pl.BlockSpec((TM, D_MODEL), lambda b, *_: (jnp.minimum(b, last), 0)),
                  pl.BlockSpec((LANE, TM), lambda b, *_: (0, jnp.minimum(b, last))),
                  pl.BlockSpec((1, 8, LANE), blk3),
                  pl.BlockSpec((1, 8, LANE), blk3),
                  pl.BlockSpec((1, LANE, LANE), blk3)],
        out_specs=pl.BlockSpec((BLK_ROWS, D_MODEL), lambda b, *_: (b, 0)))
    return pl.pallas_call(
        _dispatch_kernel,
        grid_spec=grid_spec,
        out_shape=jax.ShapeDtypeStruct(((N_BLK + 1) * BLK_ROWS, D_MODEL), BF16),
        compiler_params=_cparams(("arbitrary",), 56),
        name="dispatch",
    )(sub_used, h2, pos, seg["first_row"], seg["last_row"], seg["first_chunk_col"])


def _experts_kernel(tstart_ref, tcount_ref, texp_ref, insrc_ref, outdst_ref,
                    s_hbm, wg_ref, wu_ref, wd_ref, o_hbm,
                    xb0, xb1, xb2, ob0, ob1, ob2, wgub, wdb, insem, outsem):
    xbufs = (xb0, xb1, xb2)
    obufs = (ob0, ob1, ob2)
    e = pl.program_id(0)
    n_tiles = tstart_ref[N_EXPERTS]
    t0 = tstart_ref[e]

    def slot_base(t):
        te = texp_ref[t]
        j = jnp.minimum(t - tstart_ref[te], EXP_TILES - 1)
        return te * EXP_SLOTS + j * TILE_CHUNKS

    def in_copy(src_chunk, buf, i, sem):
        return pltpu.make_async_copy(s_hbm.at[src_chunk], xbufs[buf].at[i], sem)

    def out_copy(dst_chunk, buf, i, sem):
        return pltpu.make_async_copy(obufs[buf].at[i], o_hbm.at[dst_chunk], sem)

    def start_in(t, buf):
        base = slot_base(t)
        for i in range(TILE_CHUNKS):
            in_copy(insrc_ref[base + i], buf, i, insem.at[buf]).start()

    def start_out(t, buf, real):
        base = slot_base(jnp.maximum(t, 0))
        for i in range(TILE_CHUNKS):
            dst = jnp.where(real, outdst_ref[base + i], TRASH_CHUNK + buf * TILE_CHUNKS + i)
            out_copy(dst, buf, i, outsem.at[buf]).start()

    def wait_in(sem_idx):
        for i in range(TILE_CHUNKS):
            in_copy(0, 0, i, insem.at[sem_idx]).wait()

    def wait_out(sem_idx):
        for i in range(TILE_CHUNKS):
            out_copy(0, 0, i, outsem.at[sem_idx]).wait()

    def on_buffer(buf_dyn, fn):
        for k in range(N_BUF):
            pl.when(buf_dyn == k)(functools.partial(fn, k))

    wgub[:, :EXPERT_FF] = wg_ref[0].astype(BF16)
    wgub[:, EXPERT_FF:] = wu_ref[0].astype(BF16)
    wdb[...] = wd_ref[0].astype(BF16)

    @pl.when(e == 0)
    def _():
        for ob in obufs:
            ob[...] = jnp.zeros(ob.shape, BF16)
        start_out(0, 0, False)
        start_out(0, 1, False)
        start_in(0, 0)
        start_in(1, 1)

    def tile_body(j, carry):
        t = t0 + j

        def run(k):
            nxt = (k + 2) % N_BUF
            wait_in(k)
            wait_out(k)
            start_in(t + 2, nxt)
            start_out(t - 1, nxt, t >= 1)
            x = xbufs[k][...].reshape(TILE_ROWS, D_MODEL)
            au = jnp.dot(x, wgub[...], preferred_element_type=F32)
            a, u = au[:, :EXPERT_FF], au[:, EXPERT_FF:]
            hid = (a * jax.nn.sigmoid(a) * u).astype(BF16)
            out = jnp.dot(hid, wdb[...], preferred_element_type=F32).astype(BF16)
            obufs[k][...] = out.reshape(TILE_CHUNKS, CHUNK, D_MODEL)

        on_buffer(t % N_BUF, run)
        return carry

    lax.fori_loop(0, tcount_ref[e], tile_body, 0)

    @pl.when(e == N_EXPERTS - 1)
    def _():
        on_buffer((n_tiles + 2) % N_BUF, lambda k: start_out(n_tiles - 1, k, n_tiles >= 1))
        wait_in(n_tiles % N_BUF)
        wait_in((n_tiles + 1) % N_BUF)
        for k in range(N_BUF):
            wait_out(k)


def _experts(tile_start, tile_count, tile_expert, in_src, out_dst, sorted_rows, wg, wu, wd):
    wspec = lambda shape: pl.BlockSpec(shape, lambda e, *_: (e, 0, 0))
    grid_spec = pltpu.PrefetchScalarGridSpec(
        num_scalar_prefetch=5,
        grid=(N_EXPERTS,),
        in_specs=[pl.BlockSpec(memory_space=pl.ANY),
                  wspec((1, D_MODEL, EXPERT_FF)),
                  wspec((1, D_MODEL, EXPERT_FF)),
                  wspec((1, EXPERT_FF, D_MODEL))],
        out_specs=pl.BlockSpec(memory_space=pl.ANY),
        scratch_shapes=[pltpu.VMEM((TILE_CHUNKS, CHUNK, D_MODEL), BF16)] * (2 * N_BUF) + [
                        pltpu.VMEM((D_MODEL, 2 * EXPERT_FF), BF16),
                        pltpu.VMEM((EXPERT_FF, D_MODEL), BF16),
                        pltpu.SemaphoreType.DMA((N_BUF,)),
                        pltpu.SemaphoreType.DMA((N_BUF,))])
    return pl.pallas_call(
        _experts_kernel,
        grid_spec=grid_spec,
        out_shape=jax.ShapeDtypeStruct(sorted_rows.shape, BF16),
        input_output_aliases={5: 0},
        compiler_params=_cparams(("arbitrary",), 32),
        name="experts",
    )(tile_start, tile_count, tile_expert, in_src, out_dst, sorted_rows, wg, wu, wd)


def _combine_kernel(sub_ref, o_ref, post_ref, gate_ref, first_ref, last_ref, h_ref, x1_ref, mods_ref,
                    sg32_ref, su32_ref, sd32_ref, fg_ref, yp_ref, ys_ref, acc_ref, sg_ref, su_ref, sd_ref):
    i = pl.program_id(0)
    n_sub = sub_ref[i]

    @pl.when(i == 0)
    def _():
        sg_ref[...] = sg32_ref[...].astype(BF16)
        su_ref[...] = su32_ref[...].astype(BF16)
        sd_ref[...] = sd32_ref[...].astype(BF16)

    def locate(sb):
        rows = (lax.broadcasted_iota(jnp.int32, (LANE, SUB), 1) + sb * SUB).astype(F32)
        first = jnp.concatenate([first_ref[0]] * (SUB // LANE), axis=1)
        last = jnp.concatenate([last_ref[0]] * (SUB // LANE), axis=1)
        own = jnp.where(rows >= first, 1.0, 0.0) - jnp.where(rows >= last, 1.0, 0.0)
        start = jnp.sum(jnp.where(own > 0.0, first, 0.0), axis=0, keepdims=True)
        want = rows[0:1, :] + 1.0 - start
        ownb = own.astype(BF16)
        have = jnp.dot(post_ref[...], ownb, preferred_element_type=F32)
        own2 = jnp.concatenate([ownb[:N_EXPERTS], ownb[:N_EXPERTS]], axis=0)
        gate = jnp.dot(gate_ref[...], own2, preferred_element_type=F32)
        return have, want, gate

    def select(located):
        have, want, gate = located
        return jnp.where(have == want, gate, 0.0).astype(BF16)

    def weights_of(sb):
        return select(locate(sb))

    h = h_ref[...]
    a = jnp.dot(h, sg_ref[...], preferred_element_type=F32)
    u = jnp.dot(h, su_ref[...], preferred_element_type=F32)
    shared = jnp.dot((a * jax.nn.sigmoid(a) * u).astype(BF16), sd_ref[...],
                     preferred_element_type=F32)
    pws, ahead = [], locate(0)
    for sb in range(SUB_ALWAYS):
        cur, ahead = ahead, (locate(sb + 1) if sb + 1 < SUB_ALWAYS else None)
        pws.append(select(cur))
    pw_main = jnp.concatenate(pws, axis=1)
    acc_ref[...] = shared + jnp.dot(pw_main, o_ref[0:SUB_ALWAYS * SUB, :],
                                    preferred_element_type=F32)
    for sb in range(SUB_ALWAYS, N_SUB):
        @pl.when(sb < n_sub)
        def _():
            acc_ref[...] += jnp.dot(weights_of(sb), o_ref[sb * SUB:(sb + 1) * SUB, :],
                                    preferred_element_type=F32)

    m = mods_ref[0]
    y = _rms(x1_ref[...] + m[5:6] * acc_ref[...], fg_ref[...])

    @pl.when(i < NP_TILES)
    def _():
        yp_ref[...] = y

    @pl.when(i >= NP_TILES)
    def _():
        ys_ref[...] = y


def _combine(sub_used, expert_rows, pos_t, gate_t, seg, h2, x1, mods3, sg, su, sd, fg):
    tile = lambda i, *_: (i, 0)
    blk3 = lambda i, *_: (i, 0, 0)
    grid_spec = pltpu.PrefetchScalarGridSpec(
        num_scalar_prefetch=1,
        grid=(N_BLK,),
        in_specs=[pl.BlockSpec((BLK_ROWS, D_MODEL), tile),
                  pl.BlockSpec((TM, LANE), tile),
                  pl.BlockSpec((TM, LANE), tile),
                  pl.BlockSpec((1, LANE, LANE), blk3),
                  pl.BlockSpec((1, LANE, LANE), blk3),
                  pl.BlockSpec((TM, D_MODEL), tile),
                  pl.BlockSpec((TM, D_MODEL), tile),
                  pl.BlockSpec((1, N_MOD, D_MODEL), lambda i, *_: (_seg_of_tile(i), 0, 0)),
                  _const_spec((D_MODEL, EXPERT_FF)),
                  _const_spec((D_MODEL, EXPERT_FF)),
                  _const_spec((EXPERT_FF, D_MODEL)),
                  _const_spec((1, D_MODEL))],
        out_specs=[pl.BlockSpec((TM, D_MODEL), lambda i, *_: (jnp.minimum(i, NP_TILES - 1), 0)),
                   pl.BlockSpec((TM, D_MODEL), lambda i, *_: (jnp.maximum(i - NP_TILES, 0), 0))],
        scratch_shapes=[pltpu.VMEM((TM, D_MODEL), F32),
                        pltpu.VMEM((D_MODEL, EXPERT_FF), BF16),
                        pltpu.VMEM((D_MODEL, EXPERT_FF), BF16),
                        pltpu.VMEM((EXPERT_FF, D_MODEL), BF16)])
    return pl.pallas_call(
        _combine_kernel,
        grid_spec=grid_spec,
        out_shape=[jax.ShapeDtypeStruct((N_PROMPT, D_MODEL), F32),
                   jax.ShapeDtypeStruct((N_SAMPLE, D_MODEL), F32)],
        compiler_params=_cparams(("arbitrary",), 56),
        name="combine",
    )(sub_used, expert_rows, pos_t, gate_t, seg["first_col"], seg["last_col"], h2, x1, mods3,
      sg, su, sd, fg)


def _rope_tables():
    pos = np.arange(DEC_SEQ)
    row = (pos // GRID_W).astype(np.float64)
    col = (pos % GRID_W).astype(np.float64)
    n_freq = QK_ROPE_DIM // 4
    inv_freq = ROPE_BASE ** (-np.arange(n_freq, dtype=np.float64) / n_freq)
    ar, ac = row[:, None] * inv_freq, col[:, None] * inv_freq
    cos32 = np.concatenate([np.cos(ar), np.cos(ar), np.cos(ac), np.cos(ac)], axis=1)
    sin32 = np.concatenate([-np.sin(ar), np.sin(ar), -np.sin(ac), np.sin(ac)], axis=1)
    cos32 = np.concatenate([np.ones((TM, QK_ROPE_DIM)), cos32], axis=0)
    sin32 = np.concatenate([np.zeros((TM, QK_ROPE_DIM)), sin32], axis=0)
    n = cos32.shape[0]
    ones = np.ones((n, QK_NOPE_DIM))
    zeros = np.zeros((n, QK_NOPE_DIM))
    tail = np.zeros((n, HEAD_PAD - QK_NOPE_DIM - QK_ROPE_DIM))
    q_scale = ATTN_SCALE * math.log2(math.e)
    rest = np.zeros((n, HEAD_PAD - QK_ROPE_DIM))
    qc = np.concatenate([ones, cos32, tail], axis=1) * q_scale
    qs = np.concatenate([zeros, sin32, tail], axis=1) * q_scale
    kc = np.concatenate([cos32, rest], axis=1)
    ks = np.concatenate([sin32, rest], axis=1)
    return jnp.asarray(np.stack([qc, qs, kc, ks], axis=0).astype(np.float32))


def _pack_weights(w_in, w_uq, w_ukv):
    win = w_in.astype(BF16)
    gate_tail = w_in[:, S_GATE - PHASE + 2 * D_MODEL:]
    wgt = jnp.pad(gate_tail, ((0, 0), (0, LANE - PHASE))).astype(BF16)
    uq = w_uq.reshape(Q_LORA, N_HEADS, QK_NOPE_DIM + QK_ROPE_DIM)
    pad_q = HEAD_PAD - QK_NOPE_DIM - QK_ROPE_DIM
    wuqp = jnp.pad(uq, ((0, 0), (0, 0), (0, pad_q))).reshape(Q_LORA, QK_COLS).astype(BF16)
    ukv = w_ukv.reshape(KV_LORA, N_HEADS, QK_NOPE_DIM + V_DIM)
    wukp = jnp.pad(ukv[:, :, :QK_NOPE_DIM], ((0, 0), (0, 0), (0, HEAD_PAD - QK_NOPE_DIM)))
    wukp_t = wukp.reshape(KV_LORA, QK_COLS).T.astype(BF16)
    wuv = ukv[:, :, QK_NOPE_DIM:].reshape(KV_LORA, V_COLS).astype(BF16)
    return win, wgt, wuqp, wukp_t, wuv


def kernel(x_prompt, x_sample, cache_ckv, cache_krope, c, c_ctx, mod_w, mod_b, norm1_g, w_in, q_norm_g, w_uq, kv_norm_g, w_ukv, w_o_attn, conv_w, conv_b, conv_ln_g, conv_ln_b, w_pw2, w_out, norm2_g, router_w, router_bias, exp_w_gate, exp_w_up, exp_w_down, sh_w_gate, sh_w_up, sh_w_down, final_g):
    xp = x_prompt.reshape(N_PROMPT, D_MODEL)
    xs = x_sample.reshape(N_SAMPLE, D_MODEL)
    cond8 = jnp.concatenate([c_ctx[None, :], c, jnp.zeros((8 - 1 - DEC_BATCH, D_MODEL), F32)], axis=0)
    mods3 = _mods(cond8, mod_w[0], mod_b).reshape(8, N_MOD, D_MODEL)

    win, wgt, wuqp, wukp, wuv = _pack_weights(w_in[0], w_uq[0], w_ukv[0])
    q, k, v, z, gt, ckv_new, kr_new = _inproj(
        xp, xs, mods3, norm1_g, q_norm_g, kv_norm_g, win, wgt, wuqp, wukp, wuv, _rope_tables())

    krope_pad = jnp.pad(cache_krope[:, 0], ((0, 0), (0, 0),
                                            (QK_NOPE_DIM, HEAD_PAD - QK_NOPE_DIM - QK_ROPE_DIM)))
    kc, vc = _ctxkv(cache_ckv[:, 0], krope_pad, wukp, wuv)
    attn_p = _attn_prompt(q, k, v)
    attn_s = _attn_sample(q, k, v, kc, vc)

    rwt = router_w[0].T
    rwh = rwt.astype(BF16)
    rwl = (rwt - rwh.astype(F32)).astype(BF16)
    cw = conv_w[0].reshape(CONV_K, D_MODEL // LANE, LANE).transpose(1, 0, 2)
    x1, h2, gates_t = _mix(
        attn_p, attn_s, z, gt, xp, xs, mods3,
        w_o_attn[0], w_pw2[0], w_out[0],
        cw, conv_b, conv_ln_g, conv_ln_b, norm2_g, rwh, rwl, router_bias.reshape(N_EXPERTS, 1))

    pos, cnt = _plan(gates_t)
    tile_start, tile_count, tile_expert, in_src, out_dst, sub_used, seg = _tile_tables(cnt)
    expert_rows = _experts(tile_start, tile_count, tile_expert, in_src, out_dst,
                           _dispatch(sub_used, h2, pos, seg).reshape(-1, CHUNK, D_MODEL),
                           exp_w_gate[0], exp_w_up[0], exp_w_down[0])
    gates = gates_t.T
    gates_hi = gates.astype(BF16)
    gates_lo = (gates - gates_hi.astype(F32)).astype(BF16)
    y_p, y_s = _combine(sub_used, expert_rows.reshape(-1, D_MODEL), pos.T,
                        jnp.concatenate([gates_hi, gates_lo], axis=1), seg, h2, x1, mods3,
                        sh_w_gate[0], sh_w_up[0], sh_w_down[0], final_g[None, :])
    return (y_p.reshape(BATCH, SEQ, D_MODEL),
            y_s.reshape(DEC_BATCH, DEC_SEQ, D_MODEL),
            ckv_new.reshape(BATCH, 1, SEQ, KV_LORA),
            kr_new.reshape(BATCH, 1, SEQ, QK_ROPE_DIM))
```

```python
import functools
import math

import jax
import jax.numpy as jnp
import numpy as np
from jax import lax
from jax.experimental import pallas as pl
from jax.experimental.pallas import tpu as pltpu

F32 = jnp.float32
BF16 = jnp.bfloat16

D_MODEL = 1024
BATCH = 16
SEQ = 256
DEC_BATCH = 2
DEC_SEQ = 2048
PAST_LEN = 256
GRID_W = 64
N_HEADS = 16
QK_NOPE_DIM = 64
QK_ROPE_DIM = 32
V_DIM = 64
Q_LORA = 384
KV_LORA = 256
ROPE_BASE = 10000.0
ATTN_SCALE = 1.0 / math.sqrt(QK_NOPE_DIM + QK_ROPE_DIM)
CONV_K = 31
CONV_PAD = CONV_K // 2
N_EXPERTS = 64
TOP_K = 8
N_GROUPS = 8
GROUP_SIZE = N_EXPERTS // N_GROUPS
TOPK_GROUPS = 4
EXPERT_FF = 256
ROUTED_SCALE = 2.5
EPS = 1e-6
N_MOD = 6

LANE = 128
HEAD_PAD = LANE
QK_COLS = N_HEADS * HEAD_PAD
N_PROMPT = BATCH * SEQ
N_SAMPLE = DEC_BATCH * DEC_SEQ
N_TOK = N_PROMPT + N_SAMPLE

TM = 256
NP_TILES = N_PROMPT // TM
NS_TILES = N_SAMPLE // TM
N_TILES = NP_TILES + NS_TILES
TILES_PER_DEC = DEC_SEQ // TM
HALO = 16
S_Q = 0
S_KV = S_Q + Q_LORA
S_KR = S_KV + KV_LORA
S_CONV = S_KR + QK_ROPE_DIM
S_GATE = S_CONV + 2 * D_MODEL
PHASE = S_CONV % LANE
CONV_WIN = 2 * D_MODEL + LANE
GATE_WIN = 2 * D_MODEL + LANE
IN_COLS = S_GATE + 2 * D_MODEL
V_COLS = N_HEADS * V_DIM

VMEM_CAP = 56 * 1024 * 1024


def _cparams(sem, vmem_mb):
    return pltpu.CompilerParams(dimension_semantics=sem,
                                vmem_limit_bytes=min(vmem_mb * 1024 * 1024, VMEM_CAP))


def _const_spec(shape):
    n = len(shape)
    return pl.BlockSpec(shape, lambda *_: (0,) * n, pipeline_mode=pl.Buffered(1))


def _seg_of_tile(i):
    return jnp.where(i < NP_TILES, 0, 1 + (i - NP_TILES) // TILES_PER_DEC)


def _rms(x, g):
    return x * lax.rsqrt(jnp.mean(x * x, axis=-1, keepdims=True) + EPS) * g


_NT = (((1,), (1,)), ((), ()))


def _mods_kernel(cond_ref, w_ref, b_ref, o_ref):
    c = cond_ref[...]
    s = c * jax.nn.sigmoid(c)
    o_ref[...] = jnp.dot(s.astype(BF16), w_ref[...].astype(BF16),
                         preferred_element_type=F32) + b_ref[...]


def _mods(cond8, mod_w, mod_b):
    n = N_MOD * D_MODEL
    bn = 512
    return pl.pallas_call(
        _mods_kernel,
        grid=(n // bn,),
        in_specs=[pl.BlockSpec((8, D_MODEL), lambda j: (0, 0)),
                  pl.BlockSpec((D_MODEL, bn), lambda j: (0, j)),
                  pl.BlockSpec((1, bn), lambda j: (0, j))],
        out_specs=pl.BlockSpec((8, bn), lambda j: (0, j)),
        out_shape=jax.ShapeDtypeStruct((8, n), F32),
        compiler_params=_cparams(("arbitrary",), 16),
        name="mods",
    )(cond8, mod_w, mod_b)


def _keys_t(wukp_t, ckv, krope):
    eye = (lax.broadcasted_iota(jnp.int32, (HEAD_PAD, HEAD_PAD), 0)
           == lax.broadcasted_iota(jnp.int32, (HEAD_PAD, HEAD_PAD), 1))
    rope_t = lax.dot_general(jnp.where(eye, 1.0, 0.0).astype(BF16), krope, _NT,
                             preferred_element_type=F32)
    nope_t = lax.dot_general(wukp_t, ckv, _NT, preferred_element_type=F32)
    return (nope_t + jnp.concatenate([rope_t] * N_HEADS, axis=0)).astype(BF16)


def _rope_swap(x):
    n = x.shape[1]
    half = QK_ROPE_DIM // 4
    lane = lax.broadcasted_iota(jnp.int32, x.shape, 1)
    return jnp.where(lane % (2 * half) < half,
                     pltpu.roll(x, n - half, axis=1), pltpu.roll(x, half, axis=1))


def _inproj_kernel(xp_ref, xs_ref, mods_ref, g1_ref, qg_ref, kvg_ref, win_ref, wgt_ref, wuqp_ref,
                   wukp_ref, wuv_ref, tab_ref,
                   q_ref, k_ref, v_ref, z_ref, gt_ref, ckv_ref, kr_ref):
    i = pl.program_id(0)
    x = jnp.where(i < NP_TILES, xp_ref[...], xs_ref[...])
    m = mods_ref[0]
    h = _rms(x, g1_ref[...]) * (1.0 + m[1:2]) + m[0:1]
    hb = h.astype(BF16)

    def proj(c0, c1):
        return jnp.dot(hb, win_ref[:, c0:c1], preferred_element_type=F32)

    def lanes16(t):
        return jnp.concatenate([t] * N_HEADS, axis=1)

    qn = _rms(proj(S_Q, S_KV), qg_ref[...]).astype(BF16)
    q = jnp.dot(qn, wuqp_ref[...], preferred_element_type=F32)
    q_ref[...] = (q * lanes16(tab_ref[0]) + _rope_swap(q) * lanes16(tab_ref[1])).astype(BF16)

    ckv = _rms(proj(S_KV, S_KR), kvg_ref[...])
    ckvb = ckv.astype(BF16)
    u = proj(S_KR, S_KR + CONV_WIN)
    ua = u[:, :LANE]
    krot = pltpu.roll(ua * tab_ref[2] + _rope_swap(ua) * tab_ref[3], QK_NOPE_DIM, axis=1)
    k_ref[...] = _keys_t(wukp_ref[...], ckvb, krot.astype(BF16))
    v_ref[...] = jnp.dot(ckvb, wuv_ref[...], preferred_element_type=F32).astype(BF16)

    @pl.when(i < NP_TILES)
    def _():
        ckv_ref[...] = ckv
        kr_ref[...] = ua[:, :QK_ROPE_DIM]

    glu = u[:, :D_MODEL + LANE] * jax.nn.sigmoid(u[:, D_MODEL:])
    z_ref[...] = pltpu.roll(glu, D_MODEL + LANE - PHASE, axis=1)[:, :D_MODEL].astype(BF16)
    gates = jax.nn.sigmoid(jnp.concatenate(
        [proj(S_GATE - PHASE, S_GATE - PHASE + 2 * D_MODEL),
         jnp.dot(hb, wgt_ref[...], preferred_element_type=F32)], axis=1))
    gt_ref[...] = pltpu.roll(gates, GATE_WIN - PHASE, axis=1)[:, :2 * D_MODEL].astype(BF16)


def _inproj(xp, xs, mods3, g1, qg, kvg, win, wgt, wuqp, wukp, wuv, tab):
    tile = lambda i: (i, 0)
    ptile = lambda i: (jnp.minimum(i, NP_TILES - 1), 0)
    stile = lambda i: (jnp.maximum(i - NP_TILES, 0), 0)
    tab_blk = lambda i: (0, jnp.where(i < NP_TILES, 0, 1 + (i - NP_TILES) % TILES_PER_DEC), 0)
    return pl.pallas_call(
        _inproj_kernel,
        grid=(N_TILES,),
        in_specs=[pl.BlockSpec((TM, D_MODEL), ptile),
                  pl.BlockSpec((TM, D_MODEL), stile),
                  pl.BlockSpec((1, N_MOD, D_MODEL), lambda i: (_seg_of_tile(i), 0, 0)),
                  _const_spec((1, D_MODEL)),
                  _const_spec((1, Q_LORA)),
                  _const_spec((1, KV_LORA)),
                  _const_spec((D_MODEL, IN_COLS)),
                  _const_spec((D_MODEL, LANE)),
                  _const_spec((Q_LORA, QK_COLS)),
                  _const_spec((QK_COLS, KV_LORA)),
                  _const_spec((KV_LORA, V_COLS)),
                  pl.BlockSpec((4, TM, LANE), tab_blk)],
        out_specs=[pl.BlockSpec((TM, QK_COLS), tile),
                   pl.BlockSpec((QK_COLS, TM), lambda i: (0, i)),
                   pl.BlockSpec((TM, V_COLS), tile),
                   pl.BlockSpec((TM, D_MODEL), tile),
                   pl.BlockSpec((TM, 2 * D_MODEL), tile),
                   pl.BlockSpec((TM, KV_LORA), ptile),
                   pl.BlockSpec((TM, QK_ROPE_DIM), ptile)],
        out_shape=[jax.ShapeDtypeStruct((N_TOK, QK_COLS), BF16),
                   jax.ShapeDtypeStruct((QK_COLS, N_TOK), BF16),
                   jax.ShapeDtypeStruct((N_TOK, V_COLS), BF16),
                   jax.ShapeDtypeStruct((N_TOK, D_MODEL), BF16),
                   jax.ShapeDtypeStruct((N_TOK, 2 * D_MODEL), BF16),
                   jax.ShapeDtypeStruct((N_PROMPT, KV_LORA), F32),
                   jax.ShapeDtypeStruct((N_PROMPT, QK_ROPE_DIM), F32)],
        compiler_params=_cparams(("arbitrary",), 52),
        name="inproj",
    )(xp, xs, mods3, g1, qg, kvg, win, wgt, wuqp, wukp, wuv, tab)


def _ctxkv_kernel(ckv_ref, kr_ref, wukp_ref, wuv_ref, k_ref, v_ref):
    c = ckv_ref[0].astype(BF16)
    k_ref[0] = _keys_t(wukp_ref[...], c, kr_ref[0].astype(BF16))
    v_ref[0] = jnp.dot(c, wuv_ref[...], preferred_element_type=F32).astype(BF16)


def _ctxkv(cache_ckv, krope_pad, wukp, wuv):
    return pl.pallas_call(
        _ctxkv_kernel,
        grid=(DEC_BATCH,),
        in_specs=[pl.BlockSpec((1, PAST_LEN, KV_LORA), lambda b: (b, 0, 0)),
                  pl.BlockSpec((1, PAST_LEN, LANE), lambda b: (b, 0, 0)),
                  _const_spec((QK_COLS, KV_LORA)),
                  _const_spec((KV_LORA, V_COLS))],
        out_specs=[pl.BlockSpec((1, QK_COLS, PAST_LEN), lambda b: (b, 0, 0)),
                   pl.BlockSpec((1, PAST_LEN, V_COLS), lambda b: (b, 0, 0))],
        out_shape=[jax.ShapeDtypeStruct((DEC_BATCH, QK_COLS, PAST_LEN), BF16),
                   jax.ShapeDtypeStruct((DEC_BATCH, PAST_LEN, V_COLS), BF16)],
        compiler_params=_cparams(("arbitrary",), 16),
        name="ctxkv",
    )(cache_ckv, krope_pad, wukp, wuv)


def _attn_kernel(*refs, has_ctx):
    if has_ctx:
        q_ref, k_ref, v_ref, kc_ref, vc_ref, o_ref = refs
    else:
        q_ref, k_ref, v_ref, o_ref = refs
    tq = q_ref.shape[0]
    lane = lax.broadcasted_iota(jnp.int32, (tq, LANE), 1)

    def scores(h):
        hs = slice(h * HEAD_PAD, (h + 1) * HEAD_PAD)
        qh = q_ref[:, hs]
        s = jnp.dot(qh, k_ref[hs, :], preferred_element_type=F32)
        sc = jnp.dot(qh, kc_ref[0, hs, :], preferred_element_type=F32) if has_ctx else None
        return s, sc

    def softmax(s, sc):
        mx = jnp.max(s, axis=-1, keepdims=True)
        if has_ctx:
            mx = jnp.maximum(mx, jnp.max(sc, axis=-1, keepdims=True))
        p = jnp.exp2(s - mx)
        den = jnp.sum(p, axis=-1, keepdims=True)
        pc = None
        if has_ctx:
            pc = jnp.exp2(sc - mx)
            den = den + jnp.sum(pc, axis=-1, keepdims=True)
            pc = pc.astype(BF16)
        return p.astype(BF16), pc, den

    def values(j, heads):
        pair = slice(j * LANE, (j + 1) * LANE)
        (p0, pc0, den0), (p1, pc1, den1) = heads
        o = jnp.dot(jnp.concatenate([p0, p1], axis=0), v_ref[:, pair], preferred_element_type=F32)
        if has_ctx:
            o = o + jnp.dot(jnp.concatenate([pc0, pc1], axis=0), vc_ref[0, :, pair],
                            preferred_element_type=F32)
        o_ref[:, pair] = jnp.where(lane < V_DIM, o[:tq] / den0, o[tq:] / den1).astype(BF16)

    ahead = scores(0)
    done = []
    for h in range(N_HEADS):
        cur, ahead = ahead, (scores(h + 1) if h + 1 < N_HEADS else None)
        done.append(softmax(*cur))
        if h % 2 == 1:
            values(h // 2, done)
            done = []


def _attn_prompt(q, k, v):
    blk = lambda b: (b, 0)
    return pl.pallas_call(
        functools.partial(_attn_kernel, has_ctx=False),
        grid=(BATCH,),
        in_specs=[pl.BlockSpec((SEQ, QK_COLS), blk),
                  pl.BlockSpec((QK_COLS, SEQ), lambda b: (0, b)),
                  pl.BlockSpec((SEQ, V_COLS), blk)],
        out_specs=pl.BlockSpec((SEQ, N_HEADS * V_DIM), blk),
        out_shape=jax.ShapeDtypeStruct((N_PROMPT, N_HEADS * V_DIM), BF16),
        compiler_params=_cparams(("arbitrary",), 56),
        name="attn_prompt",
    )(q, k, v)


ATT_TQ = 256


def _attn_sample(q, k, v, kc, vc):
    nq = DEC_SEQ // ATT_TQ
    q0 = N_PROMPT // ATT_TQ
    s0 = N_PROMPT // DEC_SEQ
    once = pl.Buffered(1)
    return pl.pallas_call(
        functools.partial(_attn_kernel, has_ctx=True),
        grid=(DEC_BATCH, nq),
        in_specs=[pl.BlockSpec((ATT_TQ, QK_COLS), lambda b, t: (q0 + b * nq + t, 0)),
                  pl.BlockSpec((QK_COLS, DEC_SEQ), lambda b, t: (0, s0 + b), pipeline_mode=once),
                  pl.BlockSpec((DEC_SEQ, V_COLS), lambda b, t: (s0 + b, 0), pipeline_mode=once),
                  pl.BlockSpec((1, QK_COLS, PAST_LEN), lambda b, t: (b, 0, 0), pipeline_mode=once),
                  pl.BlockSpec((1, PAST_LEN, V_COLS), lambda b, t: (b, 0, 0), pipeline_mode=once)],
        out_specs=pl.BlockSpec((ATT_TQ, N_HEADS * V_DIM), lambda b, t: (b * nq + t, 0)),
        out_shape=jax.ShapeDtypeStruct((N_SAMPLE, N_HEADS * V_DIM), BF16),
        compiler_params=_cparams(("arbitrary", "arbitrary"), 54),
        name="attn_sample",
    )(q, k, v, kc, vc)


def _topk_rows(v, k, n):
    rows = lax.broadcasted_iota(jnp.int32, v.shape, 0).astype(F32)
    sel = jnp.zeros(v.shape, F32)
    for _ in range(k):
        mx = jnp.max(v, axis=0, keepdims=True)
        first = jnp.min(jnp.where(v == mx, rows, float(n)), axis=0, keepdims=True)
        hit = rows == first
        sel = jnp.where(hit, 1.0, sel)
        v = jnp.where(hit, -jnp.inf, v)
    return sel > 0.0


def _route(logits_t, bias):
    scores = jax.nn.sigmoid(logits_t)
    sel = scores + bias
    gscore = []
    for g in range(N_GROUPS):
        blk = sel[g * GROUP_SIZE:(g + 1) * GROUP_SIZE]
        top2 = _topk_rows(blk, 2, GROUP_SIZE)
        gscore.append(jnp.sum(jnp.where(top2, blk, 0.0), axis=0, keepdims=True))
    masked = []
    for g in range(N_GROUPS):
        beaten = jnp.zeros(gscore[g].shape, F32)
        for o in range(N_GROUPS):
            if o < g:
                beaten = beaten + jnp.where(gscore[o] >= gscore[g], 1.0, 0.0)
            elif o > g:
                beaten = beaten + jnp.where(gscore[o] > gscore[g], 1.0, 0.0)
        keep = beaten < float(TOPK_GROUPS)
        masked.append(jnp.where(keep, sel[g * GROUP_SIZE:(g + 1) * GROUP_SIZE], -jnp.inf))
    chosen = _topk_rows(jnp.concatenate(masked, axis=0), TOP_K, N_EXPERTS)
    w = jnp.where(chosen, scores, 0.0)
    return w / jnp.sum(w, axis=0, keepdims=True) * ROUTED_SCALE


def _mix_kernel(ap_ref, as_ref, z_ref, zp_ref, zn_ref, gt_ref, xp_ref, xs_ref, mods_ref,
                wo32_ref, wpw32_ref, wout32_ref, cw_ref, cb_ref, lng_ref, lnb_ref, g2_ref,
                rwh_ref, rwl_ref, rb_ref,
                x1_ref, h2_ref, gates_ref, win_ref, cz_ref, wo_ref, wpw_ref, wout_ref):
    i = pl.program_id(0)

    @pl.when(i == 0)
    def _():
        wo_ref[...] = wo32_ref[...].astype(BF16)
        wpw_ref[...] = wpw32_ref[...].astype(BF16)
        wout_ref[...] = wout32_ref[...].astype(BF16)

    is_prompt = i < NP_TILES
    pos = (i - NP_TILES) % TILES_PER_DEC
    first = jnp.logical_or(is_prompt, pos == 0)
    last = jnp.logical_or(is_prompt, pos == TILES_PER_DEC - 1)

    nchunk = D_MODEL // LANE
    zprev = jnp.where(first, 0.0, zp_ref[...].astype(F32))
    znext = jnp.where(last, 0.0, zn_ref[...].astype(F32))
    zc = z_ref[...].astype(F32)
    for c in range(nchunk):
        cs = slice(c * LANE, (c + 1) * LANE)
        win_ref[c, 0:HALO, :] = zprev[:, cs]
        win_ref[c, HALO:HALO + TM, :] = zc[:, cs]
        win_ref[c, HALO + TM:, :] = znext[:, cs]

    def conv_chunk(c, carry):
        w = cw_ref[c]
        acc = jnp.zeros((TM, LANE), F32)
        for kk in range(CONV_K):
            off = HALO - CONV_PAD + kk
            acc = acc + win_ref[c, off:off + TM, :] * w[kk:kk + 1, :]
        cz_ref[c] = acc
        return carry

    lax.fori_loop(0, nchunk, conv_chunk, 0)

    m = mods_ref[0]
    rwh = rwh_ref[...]
    rows = [slice(hf * (TM // 2), (hf + 1) * (TM // 2)) for hf in range(2)]

    def conv_act(rs):
        conv = jnp.concatenate([cz_ref[c, rs, :] for c in range(nchunk)], axis=1) + cb_ref[...]
        mu = jnp.mean(conv, axis=-1, keepdims=True)
        cen = conv - mu
        var = jnp.mean(cen * cen, axis=-1, keepdims=True)
        ln = cen * lax.rsqrt(var + EPS) * lng_ref[...] + lnb_ref[...]
        return (ln * jax.nn.sigmoid(ln)).astype(BF16)

    def branches(rs, act):
        attn = jnp.where(is_prompt, ap_ref[rs, :], as_ref[rs, :])
        return (jnp.dot(attn, wo_ref[...], preferred_element_type=F32),
                jnp.dot(act, wpw_ref[...], preferred_element_type=F32))

    def merge(rs, outs):
        gt = gt_ref[rs, :].astype(F32)
        return (gt[:, :D_MODEL] * outs[0] + gt[:, D_MODEL:] * outs[1]).astype(BF16)

    def project(merged):
        return jnp.dot(merged, wout_ref[...], preferred_element_type=F32)

    def residual(rs, mix):
        x = jnp.where(is_prompt, xp_ref[rs, :], xs_ref[rs, :])
        x1 = x + m[2:3] * mix
        x1_ref[rs, :] = x1
        h2 = _rms(x1, g2_ref[...]) * (1.0 + m[4:5]) + m[3:4]
        hh = h2.astype(BF16)
        h2_ref[rs, :] = hh
        return hh, (h2 - hh.astype(F32)).astype(BF16)

    def route(rs, hs):
        hh, hl = hs
        logits_t = (lax.dot_general(rwh, hh, _NT, preferred_element_type=F32)
                    + lax.dot_general(rwh, hl, _NT, preferred_element_type=F32)
                    + lax.dot_general(rwl_ref[...], hh, _NT, preferred_element_type=F32))
        gates_ref[:, rs] = _route(logits_t, rb_ref[...])

    r0, r1 = rows
    act0 = conv_act(r0)
    out0 = branches(r0, act0)
    act1 = conv_act(r1)
    mrg0 = merge(r0, out0)
    out1 = branches(r1, act1)
    mix0 = project(mrg0)
    mrg1 = merge(r1, out1)
    hs0 = residual(r0, mix0)
    mix1 = project(mrg1)
    route(r0, hs0)
    hs1 = residual(r1, mix1)
    route(r1, hs1)


def _mix(attn_p, attn_s, z, gt, xp, xs, mods3, wo, wpw, wout, cw, cb, lng, lnb, g2, rwh, rwl, rb):
    tile = lambda i: (i, 0)
    ptile = lambda i: (jnp.minimum(i, NP_TILES - 1), 0)
    stile = lambda i: (jnp.maximum(i - NP_TILES, 0), 0)
    hpt = TM // HALO
    nhb = N_TOK // HALO
    return pl.pallas_call(
        _mix_kernel,
        grid=(N_TILES,),
        in_specs=[pl.BlockSpec((TM, D_MODEL), ptile),
                  pl.BlockSpec((TM, D_MODEL), stile),
                  pl.BlockSpec((TM, D_MODEL), tile),
                  pl.BlockSpec((HALO, D_MODEL), lambda i: (jnp.maximum(i * hpt - 1, 0), 0)),
                  pl.BlockSpec((HALO, D_MODEL), lambda i: (jnp.minimum((i + 1) * hpt, nhb - 1), 0)),
                  pl.BlockSpec((TM, 2 * D_MODEL), tile),
                  pl.BlockSpec((TM, D_MODEL), ptile),
                  pl.BlockSpec((TM, D_MODEL), stile),
                  pl.BlockSpec((1, N_MOD, D_MODEL), lambda i: (_seg_of_tile(i), 0, 0)),
                  _const_spec((D_MODEL, D_MODEL)),
                  _const_spec((D_MODEL, D_MODEL)),
                  _const_spec((D_MODEL, D_MODEL)),
                  _const_spec((D_MODEL // LANE, CONV_K, LANE)),
                  _const_spec((1, D_MODEL)),
                  _const_spec((1, D_MODEL)),
                  _const_spec((1, D_MODEL)),
                  _const_spec((1, D_MODEL)),
                  _const_spec((N_EXPERTS, D_MODEL)),
                  _const_spec((N_EXPERTS, D_MODEL)),
                  _const_spec((N_EXPERTS, 1))],
        out_specs=[pl.BlockSpec((TM, D_MODEL), tile),
                   pl.BlockSpec((TM, D_MODEL), tile),
                   pl.BlockSpec((N_EXPERTS, TM), lambda i: (0, i))],
        out_shape=[jax.ShapeDtypeStruct((N_TOK, D_MODEL), F32),
                   jax.ShapeDtypeStruct((N_TOK, D_MODEL), BF16),
                   jax.ShapeDtypeStruct((N_EXPERTS, N_TOK), F32)],
        scratch_shapes=[pltpu.VMEM((D_MODEL // LANE, TM + 2 * HALO, LANE), F32),
                        pltpu.VMEM((D_MODEL // LANE, TM, LANE), F32)]
        + [pltpu.VMEM((D_MODEL, D_MODEL), BF16)] * 3,
        compiler_params=_cparams(("arbitrary",), 52),
        name="mix",
    )(attn_p, attn_s, z, z, z, gt, xp, xs, mods3, wo, wpw, wout, cw, cb, lng, lnb, g2, rwh, rwl, rb)


CHUNK = 16
BLK_ROWS = TM * TOP_K + N_EXPERTS * CHUNK
BLK_CHUNKS = BLK_ROWS // CHUNK
SUB = 256
N_SUB = BLK_ROWS // SUB
SUB_ALWAYS = TM * TOP_K // SUB + 2
N_BLK = N_TOK // TM
TILE_CHUNKS = 32
TILE_ROWS = TILE_CHUNKS * CHUNK
MAX_CHUNKS = N_TOK * TOP_K // CHUNK + N_BLK * N_EXPERTS
MAX_TILES = MAX_CHUNKS // TILE_CHUNKS + N_EXPERTS
EXP_TILES = N_TOK // TILE_ROWS + 3
EXP_SLOTS = EXP_TILES * TILE_CHUNKS
N_BUF = 3
ZERO_CHUNK = N_BLK * BLK_CHUNKS
TRASH_CHUNK = ZERO_CHUNK + 1


def _plan_kernel(g_ref, pos_ref, cnt_ref):
    g = g_ref[...]
    picked = jnp.where(g > 0.0, 1.0, 0.0)
    r = lax.broadcasted_iota(jnp.int32, (TM, TM), 0)
    c = lax.broadcasted_iota(jnp.int32, (TM, TM), 1)
    upto = jnp.where(r <= c, 1.0, 0.0).astype(BF16)
    pos = picked * jnp.dot(picked.astype(BF16), upto, preferred_element_type=F32)
    pos_ref[0:N_EXPERTS, :] = pos.astype(BF16)
    pos_ref[N_EXPERTS:, :] = jnp.zeros((LANE - N_EXPERTS, TM), BF16)
    n = jnp.sum(picked, axis=1, keepdims=True)
    nchunk = jnp.floor((n + float(CHUNK - 1)) * (1.0 / CHUNK))
    cnt_ref[0] = jnp.broadcast_to(nchunk, (N_EXPERTS, LANE))


def _plan(gates_t):
    return pl.pallas_call(
        _plan_kernel,
        grid=(N_BLK,),
        in_specs=[pl.BlockSpec((N_EXPERTS, TM), lambda b: (0, b))],
        out_specs=[pl.BlockSpec((LANE, TM), lambda b: (0, b)),
                   pl.BlockSpec((1, N_EXPERTS, LANE), lambda b: (b, 0, 0))],
        out_shape=[jax.ShapeDtypeStruct((LANE, N_TOK), BF16),
                   jax.ShapeDtypeStruct((N_BLK, N_EXPERTS, LANE), F32)],
        compiler_params=_cparams(("arbitrary",), 16),
        name="plan",
    )(gates_t)


def _tile_tables(cnt):
    c = cnt[:, :, 0].astype(jnp.int32)
    seg_chunk = jnp.cumsum(c, axis=1) - c
    ct = c.T
    seg_q = jnp.cumsum(ct, axis=1) - ct
    tiles_e = (ct.sum(axis=1) + TILE_CHUNKS - 1) // TILE_CHUNKS
    tile_end = jnp.cumsum(tiles_e)
    tile_start = tile_end - tiles_e
    q = jnp.arange(EXP_SLOTS, dtype=jnp.int32)
    inside = jnp.logical_and(q[None, :, None] >= seg_q[:, None, :],
                             q[None, :, None] < (seg_q + ct)[:, None, :])
    base = jnp.arange(N_BLK, dtype=jnp.int32)[None, :] * BLK_CHUNKS + seg_chunk.T - seg_q
    src = q[None, :] + jnp.sum(jnp.where(inside, base[:, None, :], 0), axis=2)
    valid = jnp.any(inside, axis=2)
    buf = (tile_start[:, None] + q[None, :] // TILE_CHUNKS) % N_BUF
    trash = TRASH_CHUNK + buf * TILE_CHUNKS + q[None, :] % TILE_CHUNKS
    in_src = jnp.where(valid, src, ZERO_CHUNK).astype(jnp.int32).reshape(-1)
    out_dst = jnp.where(valid, src, trash).astype(jnp.int32).reshape(-1)
    t = jnp.arange(MAX_TILES + N_BUF, dtype=jnp.int32)
    tile_expert = jnp.minimum(jnp.sum((t[:, None] >= tile_end[None, :]).astype(jnp.int32), axis=1),
                              N_EXPERTS - 1)
    tile_start = jnp.concatenate([tile_start, tile_end[-1:]]).astype(jnp.int32)
    sub_used = (jnp.sum(c, axis=1) * CHUNK + SUB - 1) // SUB
    pad = jnp.full((N_BLK, LANE - N_EXPERTS), float(2 * BLK_ROWS), F32)
    first = jnp.concatenate([(seg_chunk * CHUNK).astype(F32), pad], axis=1)
    last = jnp.concatenate([((seg_chunk + c) * CHUNK).astype(F32), pad], axis=1)
    first_chunk = jnp.concatenate([seg_chunk.astype(F32), jnp.zeros_like(pad)], axis=1)
    seg = dict(
        first_row=jnp.broadcast_to(first[:, None, :], (N_BLK, 8, LANE)),
        last_row=jnp.broadcast_to(last[:, None, :], (N_BLK, 8, LANE)),
        first_col=jnp.broadcast_to(first[:, :, None], (N_BLK, LANE, LANE)),
        last_col=jnp.broadcast_to(last[:, :, None], (N_BLK, LANE, LANE)),
        first_chunk_col=jnp.broadcast_to(first_chunk[:, :, None], (N_BLK, LANE, LANE)).astype(BF16))
    return (tile_start, tiles_e.astype(jnp.int32), tile_expert, in_src, out_dst,
            sub_used.astype(jnp.int32), seg)


def _dispatch_kernel(sub_ref, h_ref, pos_ref, first_ref, last_ref, fchunk_ref, s_ref):
    b = pl.program_id(0)
    n_sub = jnp.where(b < N_BLK, sub_ref[jnp.minimum(b, N_BLK - 1)], 0)

    def locate(sb):
        rows = (lax.broadcasted_iota(jnp.int32, (SUB, LANE), 0) + sb * SUB).astype(F32)
        own = (jnp.where(rows >= first_ref[0, 0:1, :], 1.0, 0.0)
               - jnp.where(rows >= last_ref[0, 0:1, :], 1.0, 0.0)).astype(BF16)
        both = jnp.dot(own, jnp.concatenate([pos_ref[...], fchunk_ref[0]], axis=1),
                       preferred_element_type=F32)
        return rows, both

    def gather(sb, located):
        rows, both = located
        have = both[:, :TM]
        want = rows + 1.0 - float(CHUNK) * both[:, TM:]
        p = jnp.where(have == jnp.concatenate([want] * (TM // LANE), axis=1), 1.0, 0.0)
        s_ref[sb * SUB:(sb + 1) * SUB, :] = jnp.dot(
            p.astype(BF16), h_ref[...], preferred_element_type=F32).astype(BF16)

    def fill(sb):
        gather(sb, locate(sb))

    def clear(sb):
        s_ref[sb * SUB:(sb + 1) * SUB, :] = jnp.zeros((SUB, D_MODEL), BF16)

    @pl.when(b < N_BLK)
    def _():
        ahead = locate(0)
        for sb in range(SUB_ALWAYS):
            cur, ahead = ahead, (locate(sb + 1) if sb + 1 < SUB_ALWAYS else None)
            gather(sb, cur)

    @pl.when(b == N_BLK)
    def _():
        for sb in range(SUB_ALWAYS):
            clear(sb)

    for sb in range(SUB_ALWAYS, N_SUB):
        pl.when(sb < n_sub)(functools.partial(fill, sb))
        pl.when(sb >= n_sub)(functools.partial(clear, sb))


def _dispatch(sub_used, h2, pos, seg):
    last = N_BLK - 1
    blk3 = lambda b, *_: (jnp.minimum(b, last), 0, 0)
    grid_spec = pltpu.PrefetchScalarGridSpec(
        num_scalar_prefetch=1,
        grid=(N_BLK + 1,),
        in_specs=[pl.BlockSpec((TM, D_MODEL), lambda b, *_: (jnp.minimum(b, last), 0)),
                  pl.BlockSpec((LANE, TM), lambda b, *_: (0, jnp.minimum(b, last))),
                  pl.BlockSpec((1, 8, LANE), blk3),
                  pl.BlockSpec((1, 8, LANE), blk3),
                  pl.BlockSpec((1, LANE, LANE), blk3)],
        out_specs=pl.BlockSpec((BLK_ROWS, D_MODEL), lambda b, *_: (b, 0)))
    return pl.pallas_call(
        _dispatch_kernel,
        grid_spec=grid_spec,
        out_shape=jax.ShapeDtypeStruct(((N_BLK + 1) * BLK_ROWS, D_MODEL), BF16),
        compiler_params=_cparams(("arbitrary",), 56),
        name="dispatch",
    )(sub_used, h2, pos, seg["first_row"], seg["last_row"], seg["first_chunk_col"])


def _experts_kernel(tstart_ref, tcount_ref, texp_ref, insrc_ref, outdst_ref,
                    s_hbm, wg_ref, wu_ref, wd_ref, o_hbm,
                    xb0, xb1, xb2, ob0, ob1, ob2, wgub, wdb, insem, outsem):
    xbufs = (xb0, xb1, xb2)
    obufs = (ob0, ob1, ob2)
    e = pl.program_id(0)
    n_tiles = tstart_ref[N_EXPERTS]
    t0 = tstart_ref[e]

    def slot_base(t):
        te = texp_ref[t]
        j = jnp.minimum(t - tstart_ref[te], EXP_TILES - 1)
        return te * EXP_SLOTS + j * TILE_CHUNKS

    def in_copy(src_chunk, buf, i, sem):
        return pltpu.make_async_copy(s_hbm.at[src_chunk], xbufs[buf].at[i], sem)

    def out_copy(dst_chunk, buf, i, sem):
        return pltpu.make_async_copy(obufs[buf].at[i], o_hbm.at[dst_chunk], sem)

    def start_in(t, buf):
        base = slot_base(t)
        for i in range(TILE_CHUNKS):
            in_copy(insrc_ref[base + i], buf, i, insem.at[buf]).start()

    def start_out(t, buf, real):
        base = slot_base(jnp.maximum(t, 0))
        for i in range(TILE_CHUNKS):
            dst = jnp.where(real, outdst_ref[base + i], TRASH_CHUNK + buf * TILE_CHUNKS + i)
            out_copy(dst, buf, i, outsem.at[buf]).start()

    def wait_in(sem_idx):
        for i in range(TILE_CHUNKS):
            in_copy(0, 0, i, insem.at[sem_idx]).wait()

    def wait_out(sem_idx):
        for i in range(TILE_CHUNKS):
            out_copy(0, 0, i, outsem.at[sem_idx]).wait()

    def on_buffer(buf_dyn, fn):
        for k in range(N_BUF):
            pl.when(buf_dyn == k)(functools.partial(fn, k))

    wgub[:, :EXPERT_FF] = wg_ref[0].astype(BF16)
    wgub[:, EXPERT_FF:] = wu_ref[0].astype(BF16)
    wdb[...] = wd_ref[0].astype(BF16)

    @pl.when(e == 0)
    def _():
        for ob in obufs:
            ob[...] = jnp.zeros(ob.shape, BF16)
        start_out(0, 0, False)
        start_out(0, 1, False)
        start_in(0, 0)
        start_in(1, 1)

    def tile_body(j, carry):
        t = t0 + j

        def run(k):
            nxt = (k + 2) % N_BUF
            wait_in(k)
            wait_out(k)
            start_in(t + 2, nxt)
            start_out(t - 1, nxt, t >= 1)
            x = xbufs[k][...].reshape(TILE_ROWS, D_MODEL)
            au = jnp.dot(x, wgub[...], preferred_element_type=F32)
            a, u = au[:, :EXPERT_FF], au[:, EXPERT_FF:]
            hid = (a * jax.nn.sigmoid(a) * u).astype(BF16)
            out = jnp.dot(hid, wdb[...], preferred_element_type=F32).astype(BF16)
            obufs[k][...] = out.reshape(TILE_CHUNKS, CHUNK, D_MODEL)

        on_buffer(t % N_BUF, run)
        return carry

    lax.fori_loop(0, tcount_ref[e], tile_body, 0)

    @pl.when(e == N_EXPERTS - 1)
    def _():
        on_buffer((n_tiles + 2) % N_BUF, lambda k: start_out(n_tiles - 1, k, n_tiles >= 1))
        wait_in(n_tiles % N_BUF)
        wait_in((n_tiles + 1) % N_BUF)
        for k in range(N_BUF):
            wait_out(k)


def _experts(tile_start, tile_count, tile_expert, in_src, out_dst, sorted_rows, wg, wu, wd):
    wspec = lambda shape: pl.BlockSpec(shape, lambda e, *_: (e, 0, 0))
    grid_spec = pltpu.PrefetchScalarGridSpec(
        num_scalar_prefetch=5,
        grid=(N_EXPERTS,),
        in_specs=[pl.BlockSpec(memory_space=pl.ANY),
                  wspec((1, D_MODEL, EXPERT_FF)),
                  wspec((1, D_MODEL, EXPERT_FF)),
                  wspec((1, EXPERT_FF, D_MODEL))],
        out_specs=pl.BlockSpec(memory_space=pl.ANY),
        scratch_shapes=[pltpu.VMEM((TILE_CHUNKS, CHUNK, D_MODEL), BF16)] * (2 * N_BUF) + [
                        pltpu.VMEM((D_MODEL, 2 * EXPERT_FF), BF16),
                        pltpu.VMEM((EXPERT_FF, D_MODEL), BF16),
                        pltpu.SemaphoreType.DMA((N_BUF,)),
                        pltpu.SemaphoreType.DMA((N_BUF,))])
    return pl.pallas_call(
        _experts_kernel,
        grid_spec=grid_spec,
        out_shape=jax.ShapeDtypeStruct(sorted_rows.shape, BF16),
        input_output_aliases={5: 0},
        compiler_params=_cparams(("arbitrary",), 32),
        name="experts",
    )(tile_start, tile_count, tile_expert, in_src, out_dst, sorted_rows, wg, wu, wd)


def _combine_kernel(sub_ref, o_ref, post_ref, gate_ref, first_ref, last_ref, h_ref, x1_ref, mods_ref,
                    sg32_ref, su32_ref, sd32_ref, fg_ref, yp_ref, ys_ref, acc_ref, sg_ref, su_ref, sd_ref):
    i = pl.program_id(0)
    n_sub = sub_ref[i]

    @pl.when(i == 0)
    def _():
        sg_ref[...] = sg32_ref[...].astype(BF16)
        su_ref[...] = su32_ref[...].astype(BF16)
        sd_ref[...] = sd32_ref[...].astype(BF16)

    def locate(sb):
        rows = (lax.broadcasted_iota(jnp.int32, (LANE, SUB), 1) + sb * SUB).astype(F32)
        first = jnp.concatenate([first_ref[0]] * (SUB // LANE), axis=1)
        last = jnp.concatenate([last_ref[0]] * (SUB // LANE), axis=1)
        own = jnp.where(rows >= first, 1.0, 0.0) - jnp.where(rows >= last, 1.0, 0.0)
        start = jnp.sum(jnp.where(own > 0.0, first, 0.0), axis=0, keepdims=True)
        want = rows[0:1, :] + 1.0 - start
        ownb = own.astype(BF16)
        have = jnp.dot(post_ref[...], ownb, preferred_element_type=F32)
        own2 = jnp.concatenate([ownb[:N_EXPERTS], ownb[:N_EXPERTS]], axis=0)
        gate = jnp.dot(gate_ref[...], own2, preferred_element_type=F32)
        return have, want, gate

    def select(located):
        have, want, gate = located
        return jnp.where(have == want, gate, 0.0).astype(BF16)

    def weights_of(sb):
        return select(locate(sb))

    h = h_ref[...]
    a = jnp.dot(h, sg_ref[...], preferred_element_type=F32)
    u = jnp.dot(h, su_ref[...], preferred_element_type=F32)
    shared = jnp.dot((a * jax.nn.sigmoid(a) * u).astype(BF16), sd_ref[...],
                     preferred_element_type=F32)
    pws, ahead = [], locate(0)
    for sb in range(SUB_ALWAYS):
        cur, ahead = ahead, (locate(sb + 1) if sb + 1 < SUB_ALWAYS else None)
        pws.append(select(cur))
    pw_main = jnp.concatenate(pws, axis=1)
    acc_ref[...] = shared + jnp.dot(pw_main, o_ref[0:SUB_ALWAYS * SUB, :],
                                    preferred_element_type=F32)
    for sb in range(SUB_ALWAYS, N_SUB):
        @pl.when(sb < n_sub)
        def _():
            acc_ref[...] += jnp.dot(weights_of(sb), o_ref[sb * SUB:(sb + 1) * SUB, :],
                                    preferred_element_type=F32)

    m = mods_ref[0]
    y = _rms(x1_ref[...] + m[5:6] * acc_ref[...], fg_ref[...])

    @pl.when(i < NP_TILES)
    def _():
        yp_ref[...] = y

    @pl.when(i >= NP_TILES)
    def _():
        ys_ref[...] = y


def _combine(sub_used, expert_rows, pos_t, gate_t, seg, h2, x1, mods3, sg, su, sd, fg):
    tile = lambda i, *_: (i, 0)
    blk3 = lambda i, *_: (i, 0, 0)
    grid_spec = pltpu.PrefetchScalarGridSpec(
        num_scalar_prefetch=1,
        grid=(N_BLK,),
        in_specs=[pl.BlockSpec((BLK_ROWS, D_MODEL), tile),
                  pl.BlockSpec((TM, LANE), tile),
                  pl.BlockSpec((TM, LANE), tile),
                  pl.BlockSpec((1, LANE, LANE), blk3),
                  pl.BlockSpec((1, LANE, LANE), blk3),
                  pl.BlockSpec((TM, D_MODEL), tile),
                  pl.BlockSpec((TM, D_MODEL), tile),
                  pl.BlockSpec((1, N_MOD, D_MODEL), lambda i, *_: (_seg_of_tile(i), 0, 0)),
                  _const_spec((D_MODEL, EXPERT_FF)),
                  _const_spec((D_MODEL, EXPERT_FF)),
                  _const_spec((EXPERT_FF, D_MODEL)),
                  _const_spec((1, D_MODEL))],
        out_specs=[pl.BlockSpec((TM, D_MODEL), lambda i, *_: (jnp.minimum(i, NP_TILES - 1), 0)),
                   pl.BlockSpec((TM, D_MODEL), lambda i, *_: (jnp.maximum(i - NP_TILES, 0), 0))],
        scratch_shapes=[pltpu.VMEM((TM, D_MODEL), F32),
                        pltpu.VMEM((D_MODEL, EXPERT_FF), BF16),
                        pltpu.VMEM((D_MODEL, EXPERT_FF), BF16),
                        pltpu.VMEM((EXPERT_FF, D_MODEL), BF16)])
    return pl.pallas_call(
        _combine_kernel,
        grid_spec=grid_spec,
        out_shape=[jax.ShapeDtypeStruct((N_PROMPT, D_MODEL), F32),
                   jax.ShapeDtypeStruct((N_SAMPLE, D_MODEL), F32)],
        compiler_params=_cparams(("arbitrary",), 56),
        name="combine",
    )(sub_used, expert_rows, pos_t, gate_t, seg["first_col"], seg["last_col"], h2, x1, mods3,
      sg, su, sd, fg)


def _rope_tables():
    pos = np.arange(DEC_SEQ)
    row = (pos // GRID_W).astype(np.float64)
    col = (pos % GRID_W).astype(np.float64)
    n_freq = QK_ROPE_DIM // 4
    inv_freq = ROPE_BASE ** (-np.arange(n_freq, dtype=np.float64) / n_freq)
    ar, ac = row[:, None] * inv_freq, col[:, None] * inv_freq
    cos32 = np.concatenate([np.cos(ar), np.cos(ar), np.cos(ac), np.cos(ac)], axis=1)
    sin32 = np.concatenate([-np.sin(ar), np.sin(ar), -np.sin(ac), np.sin(ac)], axis=1)
    cos32 = np.concatenate([np.ones((TM, QK_ROPE_DIM)), cos32], axis=0)
    sin32 = np.concatenate([np.zeros((TM, QK_ROPE_DIM)), sin32], axis=0)
    n = cos32.shape[0]
    ones = np.ones((n, QK_NOPE_DIM))
    zeros = np.zeros((n, QK_NOPE_DIM))
    tail = np.zeros((n, HEAD_PAD - QK_NOPE_DIM - QK_ROPE_DIM))
    q_scale = ATTN_SCALE * math.log2(math.e)
    rest = np.zeros((n, HEAD_PAD - QK_ROPE_DIM))
    qc = np.concatenate([ones, cos32, tail], axis=1) * q_scale
    qs = np.concatenate([zeros, sin32, tail], axis=1) * q_scale
    kc = np.concatenate([cos32, rest], axis=1)
    ks = np.concatenate([sin32, rest], axis=1)
    return jnp.asarray(np.stack([qc, qs, kc, ks], axis=0).astype(np.float32))


def _pack_weights(w_in, w_uq, w_ukv):
    win = w_in.astype(BF16)
    gate_tail = w_in[:, S_GATE - PHASE + 2 * D_MODEL:]
    wgt = jnp.pad(gate_tail, ((0, 0), (0, LANE - PHASE))).astype(BF16)
    uq = w_uq.reshape(Q_LORA, N_HEADS, QK_NOPE_DIM + QK_ROPE_DIM)
    pad_q = HEAD_PAD - QK_NOPE_DIM - QK_ROPE_DIM
    wuqp = jnp.pad(uq, ((0, 0), (0, 0), (0, pad_q))).reshape(Q_LORA, QK_COLS).astype(BF16)
    ukv = w_ukv.reshape(KV_LORA, N_HEADS, QK_NOPE_DIM + V_DIM)
    wukp = jnp.pad(ukv[:, :, :QK_NOPE_DIM], ((0, 0), (0, 0), (0, HEAD_PAD - QK_NOPE_DIM)))
    wukp_t = wukp.reshape(KV_LORA, QK_COLS).T.astype(BF16)
    wuv = ukv[:, :, QK_NOPE_DIM:].reshape(KV_LORA, V_COLS).astype(BF16)
    return win, wgt, wuqp, wukp_t, wuv


def kernel(x_prompt, x_sample, cache_ckv, cache_krope, c, c_ctx, mod_w, mod_b, norm1_g, w_in, q_norm_g, w_uq, kv_norm_g, w_ukv, w_o_attn, conv_w, conv_b, conv_ln_g, conv_ln_b, w_pw2, w_out, norm2_g, router_w, router_bias, exp_w_gate, exp_w_up, exp_w_down, sh_w_gate, sh_w_up, sh_w_down, final_g):
    xp = x_prompt.reshape(N_PROMPT, D_MODEL)
    xs = x_sample.reshape(N_SAMPLE, D_MODEL)
    cond8 = jnp.concatenate([c_ctx[None, :], c, jnp.zeros((8 - 1 - DEC_BATCH, D_MODEL), F32)], axis=0)
    mods3 = _mods(cond8, mod_w[0], mod_b).reshape(8, N_MOD, D_MODEL)

    win, wgt, wuqp, wukp, wuv = _pack_weights(w_in[0], w_uq[0], w_ukv[0])
    q, k, v, z, gt, ckv_new, kr_new = _inproj(
        xp, xs, mods3, norm1_g, q_norm_g, kv_norm_g, win, wgt, wuqp, wukp, wuv, _rope_tables())

    krope_pad = jnp.pad(cache_krope[:, 0], ((0, 0), (0, 0),
                                            (QK_NOPE_DIM, HEAD_PAD - QK_NOPE_DIM - QK_ROPE_DIM)))
    kc, vc = _ctxkv(cache_ckv[:, 0], krope_pad, wukp, wuv)
    attn_p = _attn_prompt(q, k, v)
    attn_s = _attn_sample(q, k, v, kc, vc)

    rwt = router_w[0].T
    rwh = rwt.astype(BF16)
    rwl = (rwt - rwh.astype(F32)).astype(BF16)
    cw = conv_w[0].reshape(CONV_K, D_MODEL // LANE, LANE).transpose(1, 0, 2)
    x1, h2, gates_t = _mix(
        attn_p, attn_s, z, gt, xp, xs, mods3,
        w_o_attn[0], w_pw2[0], w_out[0],
        cw, conv_b, conv_ln_g, conv_ln_b, norm2_g, rwh, rwl, router_bias.reshape(N_EXPERTS, 1))

    pos, cnt = _plan(gates_t)
    tile_start, tile_count, tile_expert, in_src, out_dst, sub_used, seg = _tile_tables(cnt)
    expert_rows = _experts(tile_start, tile_count, tile_expert, in_src, out_dst,
                           _dispatch(sub_used, h2, pos, seg).reshape(-1, CHUNK, D_MODEL),
                           exp_w_gate[0], exp_w_up[0], exp_w_down[0])
    gates = gates_t.T
    gates_hi = gates.astype(BF16)
    gates_lo = (gates - gates_hi.astype(F32)).astype(BF16)
    y_p, y_s = _combine(sub_used, expert_rows.reshape(-1, D_MODEL), pos.T,
                        jnp.concatenate([gates_hi, gates_lo], axis=1), seg, h2, x1, mods3,
                        sh_w_gate[0], sh_w_up[0], sh_w_down[0], final_g[None, :])
    return (y_p.reshape(BATCH, SEQ, D_MODEL),
            y_s.reshape(DEC_BATCH, DEC_SEQ, D_MODEL),
            ckv_new.reshape(BATCH, 1, SEQ, KV_LORA),
            kr_new.reshape(BATCH, 1, SEQ, QK_ROPE_DIM))
```

```python
import functools
import math

import jax
import jax.numpy as jnp
import numpy as np
from jax import lax
from jax.experimental import pallas as pl
from jax.experimental.pallas import tpu as pltpu

F32 = jnp.float32
BF16 = jnp.bfloat16

D_MODEL = 1024
BATCH = 16
SEQ = 256
DEC_BATCH = 2
DEC_SEQ = 2048
PAST_LEN = 256
GRID_W = 64
N_HEADS = 16
QK_NOPE_DIM = 64
QK_ROPE_DIM = 32
V_DIM = 64
Q_LORA = 384
KV_LORA = 256
ROPE_BASE = 10000.0
ATTN_SCALE = 1.0 / math.sqrt(QK_NOPE_DIM + QK_ROPE_DIM)
CONV_K = 31
CONV_PAD = CONV_K // 2
N_EXPERTS = 64
TOP_K = 8
N_GROUPS = 8
GROUP_SIZE = N_EXPERTS // N_GROUPS
TOPK_GROUPS = 4
EXPERT_FF = 256
ROUTED_SCALE = 2.5
EPS = 1e-6
N_MOD = 6

LANE = 128
HEAD_PAD = LANE
QK_COLS = N_HEADS * HEAD_PAD
N_PROMPT = BATCH * SEQ
N_SAMPLE = DEC_BATCH * DEC_SEQ
N_TOK = N_PROMPT + N_SAMPLE

TM = 256
NP_TILES = N_PROMPT // TM
NS_TILES = N_SAMPLE // TM
N_TILES = NP_TILES + NS_TILES
TILES_PER_DEC = DEC_SEQ // TM
HALO = 16
S_Q = 0
S_KV = S_Q + Q_LORA
S_KR = S_KV + KV_LORA
S_CONV = S_KR + QK_ROPE_DIM
S_GATE = S_CONV + 2 * D_MODEL
PHASE = S_CONV % LANE
CONV_WIN = 2 * D_MODEL + LANE
GATE_WIN = 2 * D_MODEL + LANE
IN_COLS = S_GATE + 2 * D_MODEL
V_COLS = N_HEADS * V_DIM

VMEM_CAP = 56 * 1024 * 1024


def _cparams(sem, vmem_mb):
    return pltpu.CompilerParams(dimension_semantics=sem,
                                vmem_limit_bytes=min(vmem_mb * 1024 * 1024, VMEM_CAP))


def _const_spec(shape):
    n = len(shape)
    return pl.BlockSpec(shape, lambda *_: (0,) * n, pipeline_mode=pl.Buffered(1))


def _seg_of_tile(i):
    return jnp.where(i < NP_TILES, 0, 1 + (i - NP_TILES) // TILES_PER_DEC)


def _rms(x, g):
    return x * lax.rsqrt(jnp.mean(x * x, axis=-1, keepdims=True) + EPS) * g


_NT = (((1,), (1,)), ((), ()))


def _mods_kernel(cond_ref, w_ref, b_ref, o_ref):
    c = cond_ref[...]
    s = c * jax.nn.sigmoid(c)
    o_ref[...] = jnp.dot(s.astype(BF16), w_ref[...].astype(BF16),
                         preferred_element_type=F32) + b_ref[...]


def _mods(cond8, mod_w, mod_b):
    n = N_MOD * D_MODEL
    bn = 512
    return pl.pallas_call(
        _mods_kernel,
        grid=(n // bn,),
        in_specs=[pl.BlockSpec((8, D_MODEL), lambda j: (0, 0)),
                  pl.BlockSpec((D_MODEL, bn), lambda j: (0, j)),
                  pl.BlockSpec((1, bn), lambda j: (0, j))],
        out_specs=pl.BlockSpec((8, bn), lambda j: (0, j)),
        out_shape=jax.ShapeDtypeStruct((8, n), F32),
        compiler_params=_cparams(("arbitrary",), 16),
        name="mods",
    )(cond8, mod_w, mod_b)


def _keys_t(wukp_t, ckv, krope):
    eye = (lax.broadcasted_iota(jnp.int32, (HEAD_PAD, HEAD_PAD), 0)
           == lax.broadcasted_iota(jnp.int32, (HEAD_PAD, HEAD_PAD), 1))
    rope_t = lax.dot_general(jnp.where(eye, 1.0, 0.0).astype(BF16), krope, _NT,
                             preferred_element_type=F32)
    nope_t = lax.dot_general(wukp_t, ckv, _NT, preferred_element_type=F32)
    return (nope_t + jnp.concatenate([rope_t] * N_HEADS, axis=0)).astype(BF16)


def _rope_swap(x):
    n = x.shape[1]
    half = QK_ROPE_DIM // 4
    lane = lax.broadcasted_iota(jnp.int32, x.shape, 1)
    return jnp.where(lane % (2 * half) < half,
                     pltpu.roll(x, n - half, axis=1), pltpu.roll(x, half, axis=1))


def _inproj_kernel(xp_ref, xs_ref, mods_ref, g1_ref, qg_ref, kvg_ref, win_ref, wgt_ref, wuqp_ref,
                   wukp_ref, wuv_ref, tab_ref,
                   q_ref, k_ref, v_ref, z_ref, gt_ref, ckv_ref, kr_ref):
    i = pl.program_id(0)
    x = jnp.where(i < NP_TILES, xp_ref[...], xs_ref[...])
    m = mods_ref[0]
    h = _rms(x, g1_ref[...]) * (1.0 + m[1:2]) + m[0:1]
    hb = h.astype(BF16)

    def proj(c0, c1):
        return jnp.dot(hb, win_ref[:, c0:c1], preferred_element_type=F32)

    def lanes16(t):
        return jnp.concatenate([t] * N_HEADS, axis=1)

    qn = _rms(proj(S_Q, S_KV), qg_ref[...]).astype(BF16)
    q = jnp.dot(qn, wuqp_ref[...], preferred_element_type=F32)
    q_ref[...] = (q * lanes16(tab_ref[0]) + _rope_swap(q) * lanes16(tab_ref[1])).astype(BF16)

    ckv = _rms(proj(S_KV, S_KR), kvg_ref[...])
    ckvb = ckv.astype(BF16)
    u = proj(S_KR, S_KR + CONV_WIN)
    ua = u[:, :LANE]
    krot = pltpu.roll(ua * tab_ref[2] + _rope_swap(ua) * tab_ref[3], QK_NOPE_DIM, axis=1)
    k_ref[...] = _keys_t(wukp_ref[...], ckvb, krot.astype(BF16))
    v_ref[...] = jnp.dot(ckvb, wuv_ref[...], preferred_element_type=F32).astype(BF16)

    glu = u[:, :D_MODEL + LANE] * jax.nn.sigmoid(u[:, D_MODEL:])
    z_ref[...] = pltpu.roll(glu, D_MODEL + LANE - PHASE, axis=1)[:, :D_MODEL].astype(BF16)
    gates = jax.nn.sigmoid(jnp.concatenate(
        [proj(S_GATE - PHASE, S_GATE - PHASE + 2 * D_MODEL),
         jnp.dot(hb, wgt_ref[...], preferred_element_type=F32)], axis=1))
    gt_ref[...] = pltpu.roll(gates, GATE_WIN - PHASE, axis=1)[:, :2 * D_MODEL].astype(BF16)

    @pl.when(i < NP_TILES)
    def _():
        ckv_ref[...] = ckv
        kr_ref[...] = ua[:, :QK_ROPE_DIM]


def _inproj(xp, xs, mods3, g1, qg, kvg, win, wgt, wuqp, wukp, wuv, tab):
    tile = lambda i: (i, 0)
    ptile = lambda i: (jnp.minimum(i, NP_TILES - 1), 0)
    stile = lambda i: (jnp.maximum(i - NP_TILES, 0), 0)
    tab_blk = lambda i: (0, jnp.where(i < NP_TILES, 0, 1 + (i - NP_TILES) % TILES_PER_DEC), 0)
    return pl.pallas_call(
        _inproj_kernel,
        grid=(N_TILES,),
        in_specs=[pl.BlockSpec((TM, D_MODEL), ptile),
                  pl.BlockSpec((TM, D_MODEL), stile),
                  pl.BlockSpec((1, N_MOD, D_MODEL), lambda i: (_seg_of_tile(i), 0, 0)),
                  _const_spec((1, D_MODEL)),
                  _const_spec((1, Q_LORA)),
                  _const_spec((1, KV_LORA)),
                  _const_spec((D_MODEL, IN_COLS)),
                  _const_spec((D_MODEL, LANE)),
                  _const_spec((Q_LORA, QK_COLS)),
                  _const_spec((QK_COLS, KV_LORA)),
                  _const_spec((KV_LORA, V_COLS)),
                  pl.BlockSpec((4, TM, LANE), tab_blk)],
        out_specs=[pl.BlockSpec((TM, QK_COLS), tile),
                   pl.BlockSpec((QK_COLS, TM), lambda i: (0, i)),
                   pl.BlockSpec((TM, V_COLS), tile),
                   pl.BlockSpec((TM, D_MODEL), tile),
                   pl.BlockSpec((TM, 2 * D_MODEL), tile),
                   pl.BlockSpec((TM, KV_LORA), ptile),
                   pl.BlockSpec((TM, QK_ROPE_DIM), ptile)],
        out_shape=[jax.ShapeDtypeStruct((N_TOK, QK_COLS), BF16),
                   jax.ShapeDtypeStruct((QK_COLS, N_TOK), BF16),
                   jax.ShapeDtypeStruct((N_TOK, V_COLS), BF16),
                   jax.ShapeDtypeStruct((N_TOK, D_MODEL), BF16),
                   jax.ShapeDtypeStruct((N_TOK, 2 * D_MODEL), BF16),
                   jax.ShapeDtypeStruct((N_PROMPT, KV_LORA), F32),
                   jax.ShapeDtypeStruct((N_PROMPT, QK_ROPE_DIM), F32)],
        compiler_params=_cparams(("arbitrary",), 52),
        name="inproj",
    )(xp, xs, mods3, g1, qg, kvg, win, wgt, wuqp, wukp, wuv, tab)


def _ctxkv_kernel(ckv_ref, kr_ref, wukp_ref, wuv_ref, k_ref, v_ref):
    c = ckv_ref[0].astype(BF16)
    k_ref[0] = _keys_t(wukp_ref[...], c, kr_ref[0].astype(BF16))
    v_ref[0] = jnp.dot(c, wuv_ref[...], preferred_element_type=F32).astype(BF16)


def _ctxkv(cache_ckv, krope_pad, wukp, wuv):
    return pl.pallas_call(
        _ctxkv_kernel,
        grid=(DEC_BATCH,),
        in_specs=[pl.BlockSpec((1, PAST_LEN, KV_LORA), lambda b: (b, 0, 0)),
                  pl.BlockSpec((1, PAST_LEN, LANE), lambda b: (b, 0, 0)),
                  _const_spec((QK_COLS, KV_LORA)),
                  _const_spec((KV_LORA, V_COLS))],
        out_specs=[pl.BlockSpec((1, QK_COLS, PAST_LEN), lambda b: (b, 0, 0)),
                   pl.BlockSpec((1, PAST_LEN, V_COLS), lambda b: (b, 0, 0))],
        out_shape=[jax.ShapeDtypeStruct((DEC_BATCH, QK_COLS, PAST_LEN), BF16),
                   jax.ShapeDtypeStruct((DEC_BATCH, PAST_LEN, V_COLS), BF16)],
        compiler_params=_cparams(("arbitrary",), 16),
        name="ctxkv",
    )(cache_ckv, krope_pad, wukp, wuv)


def _attn_kernel(*refs, has_ctx):
    if has_ctx:
        q_ref, k_ref, v_ref, kc_ref, vc_ref, o_ref = refs
    else:
        q_ref, k_ref, v_ref, o_ref = refs
    tq = q_ref.shape[0]
    lane = lax.broadcasted_iota(jnp.int32, (tq, LANE), 1)

    def scores(h):
        hs = slice(h * HEAD_PAD, (h + 1) * HEAD_PAD)
        qh = q_ref[:, hs]
        s = jnp.dot(qh, k_ref[hs, :], preferred_element_type=F32)
        sc = jnp.dot(qh, kc_ref[0, hs, :], preferred_element_type=F32) if has_ctx else None
        return s, sc

    def softmax(s, sc):
        mx = jnp.max(s, axis=-1, keepdims=True)
        if has_ctx:
            mx = jnp.maximum(mx, jnp.max(sc, axis=-1, keepdims=True))
        p = jnp.exp2(s - mx)
        den = jnp.sum(p, axis=-1, keepdims=True)
        pc = None
        if has_ctx:
            pc = jnp.exp2(sc - mx)
            den = den + jnp.sum(pc, axis=-1, keepdims=True)
            pc = pc.astype(BF16)
        return p.astype(BF16), pc, den

    def values(j, heads):
        pair = slice(j * LANE, (j + 1) * LANE)
        (p0, pc0, den0), (p1, pc1, den1) = heads
        o = jnp.dot(jnp.concatenate([p0, p1], axis=0), v_ref[:, pair], preferred_element_type=F32)
        if has_ctx:
            o = o + jnp.dot(jnp.concatenate([pc0, pc1], axis=0), vc_ref[0, :, pair],
                            preferred_element_type=F32)
        o_ref[:, pair] = jnp.where(lane < V_DIM, o[:tq] / den0, o[tq:] / den1).astype(BF16)

    ahead = scores(0)
    done = []
    for h in range(N_HEADS):
        cur, ahead = ahead, (scores(h + 1) if h + 1 < N_HEADS else None)
        done.append(softmax(*cur))
        if h % 2 == 1:
            values(h // 2, done)
            done = []


def _attn_prompt(q, k, v):
    blk = lambda b: (b, 0)
    return pl.pallas_call(
        functools.partial(_attn_kernel, has_ctx=False),
        grid=(BATCH,),
        in_specs=[pl.BlockSpec((SEQ, QK_COLS), blk),
                  pl.BlockSpec((QK_COLS, SEQ), lambda b: (0, b)),
                  pl.BlockSpec((SEQ, V_COLS), blk)],
        out_specs=pl.BlockSpec((SEQ, N_HEADS * V_DIM), blk),
        out_shape=jax.ShapeDtypeStruct((N_PROMPT, N_HEADS * V_DIM), BF16),
        compiler_params=_cparams(("arbitrary",), 56),
        name="attn_prompt",
    )(q, k, v)


ATT_TQ = 256


def _attn_sample(q, k, v, kc, vc):
    nq = DEC_SEQ // ATT_TQ
    q0 = N_PROMPT // ATT_TQ
    s0 = N_PROMPT // DEC_SEQ
    once = pl.Buffered(1)
    return pl.pallas_call(
        functools.partial(_attn_kernel, has_ctx=True),
        grid=(DEC_BATCH, nq),
        in_specs=[pl.BlockSpec((ATT_TQ, QK_COLS), lambda b, t: (q0 + b * nq + t, 0)),
                  pl.BlockSpec((QK_COLS, DEC_SEQ), lambda b, t: (0, s0 + b), pipeline_mode=once),
                  pl.BlockSpec((DEC_SEQ, V_COLS), lambda b, t: (s0 + b, 0), pipeline_mode=once),
                  pl.BlockSpec((1, QK_COLS, PAST_LEN), lambda b, t: (b, 0, 0), pipeline_mode=once),
                  pl.BlockSpec((1, PAST_LEN, V_COLS), lambda b, t: (b, 0, 0), pipeline_mode=once)],
        out_specs=pl.BlockSpec((ATT_TQ, N_HEADS * V_DIM), lambda b, t: (b * nq + t, 0)),
        out_shape=jax.ShapeDtypeStruct((N_SAMPLE, N_HEADS * V_DIM), BF16),
        compiler_params=_cparams(("arbitrary", "arbitrary"), 54),
        name="attn_sample",
    )(q, k, v, kc, vc)


def _topk_rows(v, k, n):
    rows = lax.broadcasted_iota(jnp.int32, v.shape, 0).astype(F32)
    sel = jnp.zeros(v.shape, F32)
    for _ in range(k):
        mx = jnp.max(v, axis=0, keepdims=True)
        first = jnp.min(jnp.where(v == mx, rows, float(n)), axis=0, keepdims=True)
        hit = rows == first
        sel = jnp.where(hit, 1.0, sel)
        v = jnp.where(hit, -jnp.inf, v)
    return sel > 0.0


def _route(logits_t, bias):
    scores = jax.nn.sigmoid(logits_t)
    sel = scores + bias
    gscore = []
    for g in range(N_GROUPS):
        blk = sel[g * GROUP_SIZE:(g + 1) * GROUP_SIZE]
        top2 = _topk_rows(blk, 2, GROUP_SIZE)
        gscore.append(jnp.sum(jnp.where(top2, blk, 0.0), axis=0, keepdims=True))
    masked = []
    for g in range(N_GROUPS):
        beaten = jnp.zeros(gscore[g].shape, F32)
        for o in range(N_GROUPS):
            if o < g:
                beaten = beaten + jnp.where(gscore[o] >= gscore[g], 1.0, 0.0)
            elif o > g:
                beaten = beaten + jnp.where(gscore[o] > gscore[g], 1.0, 0.0)
        keep = beaten < float(TOPK_GROUPS)
        masked.append(jnp.where(keep, sel[g * GROUP_SIZE:(g + 1) * GROUP_SIZE], -jnp.inf))
    chosen = _topk_rows(jnp.concatenate(masked, axis=0), TOP_K, N_EXPERTS)
    w = jnp.where(chosen, scores, 0.0)
    return w / jnp.sum(w, axis=0, keepdims=True) * ROUTED_SCALE


def _mix_kernel(ap_ref, as_ref, z_ref, zp_ref, zn_ref, gt_ref, xp_ref, xs_ref, mods_ref,
                wo32_ref, wpw32_ref, wout32_ref, cw_ref, cb_ref, lng_ref, lnb_ref, g2_ref,
                rwh_ref, rwl_ref, rb_ref,
                x1_ref, h2_ref, gates_ref, win_ref, cz_ref, wo_ref, wpw_ref, wout_ref):
    i = pl.program_id(0)

    @pl.when(i == 0)
    def _():
        wo_ref[...] = wo32_ref[...].astype(BF16)
        wpw_ref[...] = wpw32_ref[...].astype(BF16)
        wout_ref[...] = wout32_ref[...].astype(BF16)

    is_prompt = i < NP_TILES
    pos = (i - NP_TILES) % TILES_PER_DEC
    first = jnp.logical_or(is_prompt, pos == 0)
    last = jnp.logical_or(is_prompt, pos == TILES_PER_DEC - 1)

    nchunk = D_MODEL // LANE
    zprev = jnp.where(first, 0.0, zp_ref[...].astype(F32))
    znext = jnp.where(last, 0.0, zn_ref[...].astype(F32))
    zc = z_ref[...].astype(F32)
    for c in range(nchunk):
        cs = slice(c * LANE, (c + 1) * LANE)
        win_ref[c, 0:HALO, :] = zprev[:, cs]
        win_ref[c, HALO:HALO + TM, :] = zc[:, cs]
        win_ref[c, HALO + TM:, :] = znext[:, cs]

    def conv_chunk(c, carry):
        w = cw_ref[c]
        acc = jnp.zeros((TM, LANE), F32)
        for kk in range(CONV_K):
            off = HALO - CONV_PAD + kk
            acc = acc + win_ref[c, off:off + TM, :] * w[kk:kk + 1, :]
        cz_ref[c] = acc
        return carry

    lax.fori_loop(0, nchunk, conv_chunk, 0)

    m = mods_ref[0]
    rwh = rwh_ref[...]
    rows = [slice(hf * (TM // 2), (hf + 1) * (TM // 2)) for hf in range(2)]

    def conv_act(rs):
        conv = jnp.concatenate([cz_ref[c, rs, :] for c in range(nchunk)], axis=1) + cb_ref[...]
        mu = jnp.mean(conv, axis=-1, keepdims=True)
        cen = conv - mu
        var = jnp.mean(cen * cen, axis=-1, keepdims=True)
        ln = cen * lax.rsqrt(var + EPS) * lng_ref[...] + lnb_ref[...]
        return (ln * jax.nn.sigmoid(ln)).astype(BF16)

    def branches(rs, act):
        attn = jnp.where(is_prompt, ap_ref[rs, :], as_ref[rs, :])
        return (jnp.dot(attn, wo_ref[...], preferred_element_type=F32),
                jnp.dot(act, wpw_ref[...], preferred_element_type=F32))

    def merge(rs, outs):
        gt = gt_ref[rs, :].astype(F32)
        return (gt[:, :D_MODEL] * outs[0] + gt[:, D_MODEL:] * outs[1]).astype(BF16)

    def project(merged):
        return jnp.dot(merged, wout_ref[...], preferred_element_type=F32)

    def residual(rs, mix):
        x = jnp.where(is_prompt, xp_ref[rs, :], xs_ref[rs, :])
        x1 = x + m[2:3] * mix
        x1_ref[rs, :] = x1
        h2 = _rms(x1, g2_ref[...]) * (1.0 + m[4:5]) + m[3:4]
        hh = h2.astype(BF16)
        h2_ref[rs, :] = hh
        return hh, (h2 - hh.astype(F32)).astype(BF16)

    def route(rs, hs):
        hh, hl = hs
        logits_t = (lax.dot_general(rwh, hh, _NT, preferred_element_type=F32)
                    + lax.dot_general(rwh, hl, _NT, preferred_element_type=F32)
                    + lax.dot_general(rwl_ref[...], hh, _NT, preferred_element_type=F32))
        gates_ref[:, rs] = _route(logits_t, rb_ref[...])

    r0, r1 = rows
    act0 = conv_act(r0)
    out0 = branches(r0, act0)
    act1 = conv_act(r1)
    mrg0 = merge(r0, out0)
    out1 = branches(r1, act1)
    mix0 = project(mrg0)
    mrg1 = merge(r1, out1)
    hs0 = residual(r0, mix0)
    mix1 = project(mrg1)
    route(r0, hs0)
    hs1 = residual(r1, mix1)
    route(r1, hs1)


def _mix(attn_p, attn_s, z, gt, xp, xs, mods3, wo, wpw, wout, cw, cb, lng, lnb, g2, rwh, rwl, rb):
    tile = lambda i: (i, 0)
    ptile = lambda i: (jnp.minimum(i, NP_TILES - 1), 0)
    stile = lambda i: (jnp.maximum(i - NP_TILES, 0), 0)
    hpt = TM // HALO
    nhb = N_TOK // HALO
    return pl.pallas_call(
        _mix_kernel,
        grid=(N_TILES,),
        in_specs=[pl.BlockSpec((TM, D_MODEL), ptile),
                  pl.BlockSpec((TM, D_MODEL), stile),
                  pl.BlockSpec((TM, D_MODEL), tile),
                  pl.BlockSpec((HALO, D_MODEL), lambda i: (jnp.maximum(i * hpt - 1, 0), 0)),
                  pl.BlockSpec((HALO, D_MODEL), lambda i: (jnp.minimum((i + 1) * hpt, nhb - 1), 0)),
                  pl.BlockSpec((TM, 2 * D_MODEL), tile),
                  pl.BlockSpec((TM, D_MODEL), ptile),
                  pl.BlockSpec((TM, D_MODEL), stile),
                  pl.BlockSpec((1, N_MOD, D_MODEL), lambda i: (_seg_of_tile(i), 0, 0)),
                  _const_spec((D_MODEL, D_MODEL)),
                  _const_spec((D_MODEL, D_MODEL)),
                  _const_spec((D_MODEL, D_MODEL)),
                  _const_spec((D_MODEL // LANE, CONV_K, LANE)),
                  _const_spec((1, D_MODEL)),
                  _const_spec((1, D_MODEL)),
                  _const_spec((1, D_MODEL)),
                  _const_spec((1, D_MODEL)),
                  _const_spec((N_EXPERTS, D_MODEL)),
                  _const_spec((N_EXPERTS, D_MODEL)),
                  _const_spec((N_EXPERTS, 1))],
        out_specs=[pl.BlockSpec((TM, D_MODEL), tile),
                   pl.BlockSpec((TM, D_MODEL), tile),
                   pl.BlockSpec((N_EXPERTS, TM), lambda i: (0, i))],
        out_shape=[jax.ShapeDtypeStruct((N_TOK, D_MODEL), F32),
                   jax.ShapeDtypeStruct((N_TOK, D_MODEL), BF16),
                   jax.ShapeDtypeStruct((N_EXPERTS, N_TOK), F32)],
        scratch_shapes=[pltpu.VMEM((D_MODEL // LANE, TM + 2 * HALO, LANE), F32),
                        pltpu.VMEM((D_MODEL // LANE, TM, LANE), F32)]
        + [pltpu.VMEM((D_MODEL, D_MODEL), BF16)] * 3,
        compiler_params=_cparams(("arbitrary",), 52),
        name="mix",
    )(attn_p, attn_s, z, z, z, gt, xp, xs, mods3, wo, wpw, wout, cw, cb, lng, lnb, g2, rwh, rwl, rb)


CHUNK = 16
BLK_ROWS = TM * TOP_K + N_EXPERTS * CHUNK
BLK_CHUNKS = BLK_ROWS // CHUNK
SUB = 256
N_SUB = BLK_ROWS // SUB
SUB_ALWAYS = TM * TOP_K // SUB + 2
N_BLK = N_TOK // TM
TILE_CHUNKS = 32
TILE_ROWS = TILE_CHUNKS * CHUNK
MAX_CHUNKS = N_TOK * TOP_K // CHUNK + N_BLK * N_EXPERTS
MAX_TILES = MAX_CHUNKS // TILE_CHUNKS + N_EXPERTS
EXP_TILES = N_TOK // TILE_ROWS + 3
EXP_SLOTS = EXP_TILES * TILE_CHUNKS
N_BUF = 3
ZERO_CHUNK = N_BLK * BLK_CHUNKS
TRASH_CHUNK = ZERO_CHUNK + 1


def _plan_kernel(g_ref, pos_ref, cnt_ref):
    g = g_ref[...]
    picked = jnp.where(g > 0.0, 1.0, 0.0)
    r = lax.broadcasted_iota(jnp.int32, (TM, TM), 0)
    c = lax.broadcasted_iota(jnp.int32, (TM, TM), 1)
    upto = jnp.where(r <= c, 1.0, 0.0).astype(BF16)
    pos = picked * jnp.dot(picked.astype(BF16), upto, preferred_element_type=F32)
    pos_ref[0:N_EXPERTS, :] = pos.astype(BF16)
    pos_ref[N_EXPERTS:, :] = jnp.zeros((LANE - N_EXPERTS, TM), BF16)
    n = jnp.sum(picked, axis=1, keepdims=True)
    nchunk = jnp.floor((n + float(CHUNK - 1)) * (1.0 / CHUNK))
    cnt_ref[0] = jnp.broadcast_to(nchunk, (N_EXPERTS, LANE))


def _plan(gates_t):
    return pl.pallas_call(
        _plan_kernel,
        grid=(N_BLK,),
        in_specs=[pl.BlockSpec((N_EXPERTS, TM), lambda b: (0, b))],
        out_specs=[pl.BlockSpec((LANE, TM), lambda b: (0, b)),
                   pl.BlockSpec((1, N_EXPERTS, LANE), lambda b: (b, 0, 0))],
        out_shape=[jax.ShapeDtypeStruct((LANE, N_TOK), BF16),
                   jax.ShapeDtypeStruct((N_BLK, N_EXPERTS, LANE), F32)],
        compiler_params=_cparams(("arbitrary",), 16),
        name="plan",
    )(gates_t)


def _tile_tables(cnt):
    c = cnt[:, :, 0].astype(jnp.int32)
    seg_chunk = jnp.cumsum(c, axis=1) - c
    ct = c.T
    seg_q = jnp.cumsum(ct, axis=1) - ct
    tiles_e = (ct.sum(axis=1) + TILE_CHUNKS - 1) // TILE_CHUNKS
    tile_end = jnp.cumsum(tiles_e)
    tile_start = tile_end - tiles_e
    q = jnp.arange(EXP_SLOTS, dtype=jnp.int32)
    inside = jnp.logical_and(q[None, :, None] >= seg_q[:, None, :],
                             q[None, :, None] < (seg_q + ct)[:, None, :])
    base = jnp.arange(N_BLK, dtype=jnp.int32)[None, :] * BLK_CHUNKS + seg_chunk.T - seg_q
    src = q[None, :] + jnp.sum(jnp.where(inside, base[:, None, :], 0), axis=2)
    valid = jnp.any(inside, axis=2)
    buf = (tile_start[:, None] + q[None, :] // TILE_CHUNKS) % N_BUF
    trash = TRASH_CHUNK + buf * TILE_CHUNKS + q[None, :] % TILE_CHUNKS
    in_src = jnp.where(valid, src, ZERO_CHUNK).astype(jnp.int32).reshape(-1)
    out_dst = jnp.where(valid, src, trash).astype(jnp.int32).reshape(-1)
    t = jnp.arange(MAX_TILES + N_BUF, dtype=jnp.int32)
    tile_expert = jnp.minimum(jnp.sum((t[:, None] >= tile_end[None, :]).astype(jnp.int32), axis=1),
                              N_EXPERTS - 1)
    tile_start = jnp.concatenate([tile_start, tile_end[-1:]]).astype(jnp.int32)
    sub_used = (jnp.sum(c, axis=1) * CHUNK + SUB - 1) // SUB
    pad = jnp.full((N_BLK, LANE - N_EXPERTS), float(2 * BLK_ROWS), F32)
    first = jnp.concatenate([(seg_chunk * CHUNK).astype(F32), pad], axis=1)
    last = jnp.concatenate([((seg_chunk + c) * CHUNK).astype(F32), pad], axis=1)
    first_chunk = jnp.concatenate([seg_chunk.astype(F32), jnp.zeros_like(pad)], axis=1)
    seg = dict(
        first_row=jnp.broadcast_to(first[:, None, :], (N_BLK, 8, LANE)),
        last_row=jnp.broadcast_to(last[:, None, :], (N_BLK, 8, LANE)),
        first_col=jnp.broadcast_to(first[:, :, None], (N_BLK, LANE, LANE)),
        last_col=jnp.broadcast_to(last[:, :, None], (N_BLK, LANE, LANE)),
        first_chunk_col=jnp.broadcast_to(first_chunk[:, :, None], (N_BLK, LANE, LANE)).astype(BF16))
    return (tile_start, tiles_e.astype(jnp.int32), tile_expert, in_src, out_dst,
            sub_used.astype(jnp.int32), seg)


def _dispatch_kernel(sub_ref, h_ref, pos_ref, first_ref, last_ref, fchunk_ref, s_ref):
    b = pl.program_id(0)
    n_sub = jnp.where(b < N_BLK, sub_ref[jnp.minimum(b, N_BLK - 1)], 0)

    def locate(sb):
        rows = (lax.broadcasted_iota(jnp.int32, (SUB, LANE), 0) + sb * SUB).astype(F32)
        own = (jnp.where(rows >= first_ref[0, 0:1, :], 1.0, 0.0)
               - jnp.where(rows >= last_ref[0, 0:1, :], 1.0, 0.0)).astype(BF16)
        both = jnp.dot(own, jnp.concatenate([pos_ref[...], fchunk_ref[0]], axis=1),
                       preferred_element_type=F32)
        return rows, both

    def gather(sb, located):
        rows, both = located
        have = both[:, :TM]
        want = rows + 1.0 - float(CHUNK) * both[:, TM:]
        p = jnp.where(have == jnp.concatenate([want] * (TM // LANE), axis=1), 1.0, 0.0)
        s_ref[sb * SUB:(sb + 1) * SUB, :] = jnp.dot(
            p.astype(BF16), h_ref[...], preferred_element_type=F32).astype(BF16)

    def fill(sb):
        gather(sb, locate(sb))

    def clear(sb):
        s_ref[sb * SUB:(sb + 1) * SUB, :] = jnp.zeros((SUB, D_MODEL), BF16)

    @pl.when(b < N_BLK)
    def _():
        ahead = locate(0)
        for sb in range(SUB_ALWAYS):
            cur, ahead = ahead, (locate(sb + 1) if sb + 1 < SUB_ALWAYS else None)
            gather(sb, cur)

    @pl.when(b == N_BLK)
    def _():
        for sb in range(SUB_ALWAYS):
            clear(sb)

    for sb in range(SUB_ALWAYS, N_SUB):
        pl.when(sb < n_sub)(functools.partial(fill, sb))
        pl.when(sb >= n_sub)(functools.partial(clear, sb))


def _dispatch(sub_used, h2, pos, seg):
    last = N_BLK - 1
    blk3 = lambda b, *_: (jnp.minimum(b, last), 0, 0)
    grid_spec = pltpu.PrefetchScalarGridSpec(
        num_scalar_prefetch=1,
        grid=(N_BLK + 1,),
        in_specs=[pl.BlockSpec((TM, D_MODEL), lambda b, *_: (jnp.minimum(b, last), 0)),
                  pl.BlockSpec((LANE, TM), lambda b, *_: (0, jnp.minimum(b, last))),
                  pl.BlockSpec((1, 8, LANE), blk3),
                  pl.BlockSpec((1, 8, LANE), blk3),
                  pl.BlockSpec((1, LANE, LANE), blk3)],
        out_specs=pl.BlockSpec((BLK_ROWS, D_MODEL), lambda b, *_: (b, 0)))
    return pl.pallas_call(
        _dispatch_kernel,
        grid_spec=grid_spec,
        out_shape=jax.ShapeDtypeStruct(((N_BLK + 1) * BLK_ROWS, D_MODEL), BF16),
        compiler_params=_cparams(("arbitrary",), 56),
        name="dispatch",
    )(sub_used, h2, pos, seg["first_row"], seg["last_row"], seg["first_chunk_col"])


def _experts_kernel(tstart_ref, tcount_ref, texp_ref, insrc_ref, outdst_ref,
                    s_hbm, wg_ref, wu_ref, wd_ref, o_hbm,
                    xb0, xb1, xb2, ob0, ob1, ob2, wgub, wdb, insem, outsem):
    xbufs = (xb0, xb1, xb2)
    obufs = (ob0, ob1, ob2)
    e = pl.program_id(0)
    n_tiles = tstart_ref[N_EXPERTS]
    t0 = tstart_ref[e]

    def slot_base(t):
        te = texp_ref[t]
        j = jnp.minimum(t - tstart_ref[te], EXP_TILES - 1)
        return te * EXP_SLOTS + j * TILE_CHUNKS

    def in_copy(src_chunk, buf, i, sem):
        return pltpu.make_async_copy(s_hbm.at[src_chunk], xbufs[buf].at[i], sem)

    def out_copy(dst_chunk, buf, i, sem):
        return pltpu.make_async_copy(obufs[buf].at[i], o_hbm.at[dst_chunk], sem)

    def start_in(t, buf):
        base = slot_base(t)
        for i in range(TILE_CHUNKS):
            in_copy(insrc_ref[base + i], buf, i, insem.at[buf]).start()

    def start_out(t, buf, real):
        base = slot_base(jnp.maximum(t, 0))
        for i in range(TILE_CHUNKS):
            dst = jnp.where(real, outdst_ref[base + i], TRASH_CHUNK + buf * TILE_CHUNKS + i)
            out_copy(dst, buf, i, outsem.at[buf]).start()

    def wait_in(sem_idx):
        for i in range(TILE_CHUNKS):
            in_copy(0, 0, i, insem.at[sem_idx]).wait()

    def wait_out(sem_idx):
        for i in range(TILE_CHUNKS):
            out_copy(0, 0, i, outsem.at[sem_idx]).wait()

    def on_buffer(buf_dyn, fn):
        for k in range(N_BUF):
            pl.when(buf_dyn == k)(functools.partial(fn, k))

    wgub[:, :EXPERT_FF] = wg_ref[0].astype(BF16)
    wgub[:, EXPERT_FF:] = wu_ref[0].astype(BF16)
    wdb[...] = wd_ref[0].astype(BF16)

    @pl.when(e == 0)
    def _():
        for ob in obufs:
            ob[...] = jnp.zeros(ob.shape, BF16)
        start_out(0, 0, False)
        start_out(0, 1, False)
        start_in(0, 0)
        start_in(1, 1)

    def tile_body(j, carry):
        t = t0 + j

        def run(k):
            nxt = (k + 2) % N_BUF
            start_in(t + 2, nxt)
            start_out(t - 1, nxt, t >= 1)
            wait_in(k)
            wait_out(k)
            x = xbufs[k][...].reshape(TILE_ROWS, D_MODEL)
            au = jnp.dot(x, wgub[...], preferred_element_type=F32)
            a, u = au[:, :EXPERT_FF], au[:, EXPERT_FF:]
            hid = (a * jax.nn.sigmoid(a) * u).astype(BF16)
            out = jnp.dot(hid, wdb[...], preferred_element_type=F32).astype(BF16)
            obufs[k][...] = out.reshape(TILE_CHUNKS, CHUNK, D_MODEL)

        on_buffer(t % N_BUF, run)
        return carry

    lax.fori_loop(0, tcount_ref[e], tile_body, 0)

    @pl.when(e == N_EXPERTS - 1)
    def _():
        on_buffer((n_tiles + 2) % N_BUF, lambda k: start_out(n_tiles - 1, k, n_tiles >= 1))
        wait_in(n_tiles % N_BUF)
        wait_in((n_tiles + 1) % N_BUF)
        for k in range(N_BUF):
            wait_out(k)


def _experts(tile_start, tile_count, tile_expert, in_src, out_dst, sorted_rows, wg, wu, wd):
    wspec = lambda shape: pl.BlockSpec(shape, lambda e, *_: (e, 0, 0))
    grid_spec = pltpu.PrefetchScalarGridSpec(
        num_scalar_prefetch=5,
        grid=(N_EXPERTS,),
        in_specs=[pl.BlockSpec(memory_space=pl.ANY),
                  wspec((1, D_MODEL, EXPERT_FF)),
                  wspec((1, D_MODEL, EXPERT_FF)),
                  wspec((1, EXPERT_FF, D_MODEL))],
        out_specs=pl.BlockSpec(memory_space=pl.ANY),
        scratch_shapes=[pltpu.VMEM((TILE_CHUNKS, CHUNK, D_MODEL), BF16)] * (2 * N_BUF) + [
                        pltpu.VMEM((D_MODEL, 2 * EXPERT_FF), BF16),
                        pltpu.VMEM((EXPERT_FF, D_MODEL), BF16),
                        pltpu.SemaphoreType.DMA((N_BUF,)),
                        pltpu.SemaphoreType.DMA((N_BUF,))])
    return pl.pallas_call(
        _experts_kernel,
        grid_spec=grid_spec,
        out_shape=jax.ShapeDtypeStruct(sorted_rows.shape, BF16),
        input_output_aliases={5: 0},
        compiler_params=_cparams(("arbitrary",), 32),
        name="experts",
    )(tile_start, tile_count, tile_expert, in_src, out_dst, sorted_rows, wg, wu, wd)


def _combine_kernel(sub_ref, o_ref, post_ref, gate_ref, first_ref, last_ref, h_ref, x1_ref, mods_ref,
                    sg32_ref, su32_ref, sd32_ref, fg_ref, yp_ref, ys_ref, acc_ref, sg_ref, su_ref, sd_ref):
    i = pl.program_id(0)
    n_sub = sub_ref[i]

    @pl.when(i == 0)
    def _():
        sg_ref[...] = sg32_ref[...].astype(BF16)
        su_ref[...] = su32_ref[...].astype(BF16)
        sd_ref[...] = sd32_ref[...].astype(BF16)

    def locate(sb):
        rows = (lax.broadcasted_iota(jnp.int32, (LANE, SUB), 1) + sb * SUB).astype(F32)
        first = jnp.concatenate([first_ref[0]] * (SUB // LANE), axis=1)
        last = jnp.concatenate([last_ref[0]] * (SUB // LANE), axis=1)
        own = jnp.where(rows >= first, 1.0, 0.0) - jnp.where(rows >= last, 1.0, 0.0)
        start = jnp.sum(jnp.where(own > 0.0, first, 0.0), axis=0, keepdims=True)
        want = rows[0:1, :] + 1.0 - start
        ownb = own.astype(BF16)
        have = jnp.dot(post_ref[...], ownb, preferred_element_type=F32)
        own2 = jnp.concatenate([ownb[:N_EXPERTS], ownb[:N_EXPERTS]], axis=0)
        gate = jnp.dot(gate_ref[...], own2, preferred_element_type=F32)
        return have, want, gate

    def select(located):
        have, want, gate = located
        return jnp.where(have == want, gate, 0.0).astype(BF16)

    def weights_of(sb):
        return select(locate(sb))

    h = h_ref[...]
    a = jnp.dot(h, sg_ref[...], preferred_element_type=F32)
    u = jnp.dot(h, su_ref[...], preferred_element_type=F32)
    shared = jnp.dot((a * jax.nn.sigmoid(a) * u).astype(BF16), sd_ref[...],
                     preferred_element_type=F32)
    pws, ahead = [], locate(0)
    for sb in range(SUB_ALWAYS):
        cur, ahead = ahead, (locate(sb + 1) if sb + 1 < SUB_ALWAYS else None)
        pws.append(select(cur))
    pw_main = jnp.concatenate(pws, axis=1)
    acc_ref[...] = shared + jnp.dot(pw_main, o_ref[0:SUB_ALWAYS * SUB, :],
                                    preferred_element_type=F32)
    for sb in range(SUB_ALWAYS, N_SUB):
        @pl.when(sb < n_sub)
        def _():
            acc_ref[...] += jnp.dot(weights_of(sb), o_ref[sb * SUB:(sb + 1) * SUB, :],
                                    preferred_element_type=F32)

    m = mods_ref[0]
    y = _rms(x1_ref[...] + m[5:6] * acc_ref[...], fg_ref[...])

    @pl.when(i < NP_TILES)
    def _():
        yp_ref[...] = y

    @pl.when(i >= NP_TILES)
    def _():
        ys_ref[...] = y


def _combine(sub_used, expert_rows, pos_t, gate_t, seg, h2, x1, mods3, sg, su, sd, fg):
    tile = lambda i, *_: (i, 0)
    blk3 = lambda i, *_: (i, 0, 0)
    grid_spec = pltpu.PrefetchScalarGridSpec(
        num_scalar_prefetch=1,
        grid=(N_BLK,),
        in_specs=[pl.BlockSpec((BLK_ROWS, D_MODEL), tile),
                  pl.BlockSpec((TM, LANE), tile),
                  pl.BlockSpec((TM, LANE), tile),
                  pl.BlockSpec((1, LANE, LANE), blk3),
                  pl.BlockSpec((1, LANE, LANE), blk3),
                  pl.BlockSpec((TM, D_MODEL), tile),
                  pl.BlockSpec((TM, D_MODEL), tile),
                  pl.BlockSpec((1, N_MOD, D_MODEL), lambda i, *_: (_seg_of_tile(i), 0, 0)),
                  _const_spec((D_MODEL, EXPERT_FF)),
                  _const_spec((D_MODEL, EXPERT_FF)),
                  _const_spec((EXPERT_FF, D_MODEL)),
                  _const_spec((1, D_MODEL))],
        out_specs=[pl.BlockSpec((TM, D_MODEL), lambda i, *_: (jnp.minimum(i, NP_TILES - 1), 0)),
                   pl.BlockSpec((TM, D_MODEL), lambda i, *_: (jnp.maximum(i - NP_TILES, 0), 0))],
        scratch_shapes=[pltpu.VMEM((TM, D_MODEL), F32),
                        pltpu.VMEM((D_MODEL, EXPERT_FF), BF16),
                        pltpu.VMEM((D_MODEL, EXPERT_FF), BF16),
                        pltpu.VMEM((EXPERT_FF, D_MODEL), BF16)])
    return pl.pallas_call(
        _combine_kernel,
        grid_spec=grid_spec,
        out_shape=[jax.ShapeDtypeStruct((N_PROMPT, D_MODEL), F32),
                   jax.ShapeDtypeStruct((N_SAMPLE, D_MODEL), F32)],
        compiler_params=_cparams(("arbitrary",), 56),
        name="combine",
    )(sub_used, expert_rows, pos_t, gate_t, seg["first_col"], seg["last_col"], h2, x1, mods3,
      sg, su, sd, fg)


def _rope_tables():
    pos = np.arange(DEC_SEQ)
    row = (pos // GRID_W).astype(np.float64)
    col = (pos % GRID_W).astype(np.float64)
    n_freq = QK_ROPE_DIM // 4
    inv_freq = ROPE_BASE ** (-np.arange(n_freq, dtype=np.float64) / n_freq)
    ar, ac = row[:, None] * inv_freq, col[:, None] * inv_freq
    cos32 = np.concatenate([np.cos(ar), np.cos(ar), np.cos(ac), np.cos(ac)], axis=1)
    sin32 = np.concatenate([-np.sin(ar), np.sin(ar), -np.sin(ac), np.sin(ac)], axis=1)
    cos32 = np.concatenate([np.ones((TM, QK_ROPE_DIM)), cos32], axis=0)
    sin32 = np.concatenate([np.zeros((TM, QK_ROPE_DIM)), sin32], axis=0)
    n = cos32.shape[0]
    ones = np.ones((n, QK_NOPE_DIM))
    zeros = np.zeros((n, QK_NOPE_DIM))
    tail = np.zeros((n, HEAD_PAD - QK_NOPE_DIM - QK_ROPE_DIM))
    q_scale = ATTN_SCALE * math.log2(math.e)
    rest = np.zeros((n, HEAD_PAD - QK_ROPE_DIM))
    qc = np.concatenate([ones, cos32, tail], axis=1) * q_scale
    qs = np.concatenate([zeros, sin32, tail], axis=1) * q_scale
    kc = np.concatenate([cos32, rest], axis=1)
    ks = np.concatenate([sin32, rest], axis=1)
    return jnp.asarray(np.stack([qc, qs, kc, ks], axis=0).astype(np.float32))


def _pack_weights(w_in, w_uq, w_ukv):
    win = w_in.astype(BF16)
    gate_tail = w_in[:, S_GATE - PHASE + 2 * D_MODEL:]
    wgt = jnp.pad(gate_tail, ((0, 0), (0, LANE - PHASE))).astype(BF16)
    uq = w_uq.reshape(Q_LORA, N_HEADS, QK_NOPE_DIM + QK_ROPE_DIM)
    pad_q = HEAD_PAD - QK_NOPE_DIM - QK_ROPE_DIM
    wuqp = jnp.pad(uq, ((0, 0), (0, 0), (0, pad_q))).reshape(Q_LORA, QK_COLS).astype(BF16)
    ukv = w_ukv.reshape(KV_LORA, N_HEADS, QK_NOPE_DIM + V_DIM)
    wukp = jnp.pad(ukv[:, :, :QK_NOPE_DIM], ((0, 0), (0, 0), (0, HEAD_PAD - QK_NOPE_DIM)))
    wukp_t = wukp.reshape(KV_LORA, QK_COLS).T.astype(BF16)
    wuv = ukv[:, :, QK_NOPE_DIM:].reshape(KV_LORA, V_COLS).astype(BF16)
    return win, wgt, wuqp, wukp_t, wuv


def kernel(x_prompt, x_sample, cache_ckv, cache_krope, c, c_ctx, mod_w, mod_b, norm1_g, w_in, q_norm_g, w_uq, kv_norm_g, w_ukv, w_o_attn, conv_w, conv_b, conv_ln_g, conv_ln_b, w_pw2, w_out, norm2_g, router_w, router_bias, exp_w_gate, exp_w_up, exp_w_down, sh_w_gate, sh_w_up, sh_w_down, final_g):
    xp = x_prompt.reshape(N_PROMPT, D_MODEL)
    xs = x_sample.reshape(N_SAMPLE, D_MODEL)
    cond8 = jnp.concatenate([c_ctx[None, :], c, jnp.zeros((8 - 1 - DEC_BATCH, D_MODEL), F32)], axis=0)
    mods3 = _mods(cond8, mod_w[0], mod_b).reshape(8, N_MOD, D_MODEL)

    win, wgt, wuqp, wukp, wuv = _pack_weights(w_in[0], w_uq[0], w_ukv[0])
    q, k, v, z, gt, ckv_new, kr_new = _inproj(
        xp, xs, mods3, norm1_g, q_norm_g, kv_norm_g, win, wgt, wuqp, wukp, wuv, _rope_tables())

    krope_pad = jnp.pad(cache_krope[:, 0], ((0, 0), (0, 0),
                                            (QK_NOPE_DIM, HEAD_PAD - QK_NOPE_DIM - QK_ROPE_DIM)))
    kc, vc = _ctxkv(cache_ckv[:, 0], krope_pad, wukp, wuv)
    attn_p = _attn_prompt(q, k, v)
    attn_s = _attn_sample(q, k, v, kc, vc)

    rwt = router_w[0].T
    rwh = rwt.astype(BF16)
    rwl = (rwt - rwh.astype(F32)).astype(BF16)
    cw = conv_w[0].reshape(CONV_K, D_MODEL // LANE, LANE).transpose(1, 0, 2)
    x1, h2, gates_t = _mix(
        attn_p, attn_s, z, gt, xp, xs, mods3,
        w_o_attn[0], w_pw2[0], w_out[0],
        cw, conv_b, conv_ln_g, conv_ln_b, norm2_g, rwh, rwl, router_bias.reshape(N_EXPERTS, 1))

    pos, cnt = _plan(gates_t)
    tile_start, tile_count, tile_expert, in_src, out_dst, sub_used, seg = _tile_tables(cnt)
    expert_rows = _experts(tile_start, tile_count, tile_expert, in_src, out_dst,
                           _dispatch(sub_used, h2, pos, seg).reshape(-1, CHUNK, D_MODEL),
                           exp_w_gate[0], exp_w_up[0], exp_w_down[0])
    gates = gates_t.T
    gates_hi = gates.astype(BF16)
    gates_lo = (gates - gates_hi.astype(F32)).astype(BF16)
    y_p, y_s = _combine(sub_used, expert_rows.reshape(-1, D_MODEL), pos.T,
                        jnp.concatenate([gates_hi, gates_lo], axis=1), seg, h2, x1, mods3,
                        sh_w_gate[0], sh_w_up[0], sh_w_down[0], final_g[None, :])
    return (y_p.reshape(BATCH, SEQ, D_MODEL),
            y_s.reshape(DEC_BATCH, DEC_SEQ, D_MODEL),
            ckv_new.reshape(BATCH, 1, SEQ, KV_LORA),
            kr_new.reshape(BATCH, 1, SEQ, QK_ROPE_DIM))
```
